```python
import math, functools
import jax, jax.numpy as jnp
from jax import lax
import numpy as np

D_MODEL = 1024
BATCH = 4
SEQ = 4096
DEPTH = 1
DEC_BATCH = 32
DEC_SEQ = 4
PAST_LEN = 8192
PAGE_SIZE = 128

D_CONV = D_MODEL
CONV_W = 3
N_HEADS = 16
HEAD_DIM = 64
N_KV = 4
GROUP = N_HEADS // N_KV
CMP_BLOCK = 32
CMP_STRIDE = 16
SEL_BLOCK = 64
N_SEL = 16
WINDOW = 512
D_PHI = 2 * HEAD_DIM
Q_BLOCK = 128
D_FF = ((8 * D_MODEL // 3 + 255) // 256) * 256
KV_W = N_KV * HEAD_DIM
IN_SIZES = (D_CONV, D_CONV, D_CONV, N_HEADS * HEAD_DIM, KV_W, KV_W, KV_W, KV_W, KV_W, KV_W, 3 * N_HEADS, 2 * D_MODEL)
IN_COLS = 3 * D_CONV + N_HEADS * HEAD_DIM + 6 * KV_W + 3 * N_HEADS + 2 * D_MODEL
EPS = 1e-6
NEG = -1e30
FORCE_BONUS = 1e3

kernel_name = 'hybrid_shortconv_nsa_alibi_adaln_step'


def rmsnorm(x, g):
    xf = x.astype(jnp.float32)
    inv = lax.rsqrt(jnp.mean(xf * xf, axis=-1, keepdims=True) + EPS)
    return (xf * inv).astype(x.dtype) * g


def alibi_slopes():
    s = 2.0 ** (-8.0 * jnp.arange(1, N_HEADS + 1, dtype=jnp.float32) / N_HEADS)
    return s.reshape(N_KV, GROUP)


def split_columns(z):
    parts, start = [], 0
    for size in IN_SIZES:
        parts.append(z[..., start:start + size])
        start += size
    return parts


def short_conv(u, u_prev, w, bias):
    t = u.shape[1]
    up = jnp.concatenate([u_prev, u], axis=1)
    y = bias + sum(w[k] * up[:, k:k + t] for k in range(CONV_W))
    return y, up[:, t:]


def compress(rows, pe, w1, w2):
    b, t, g, d = rows.shape
    nc = t // CMP_STRIDE
    halves = jnp.pad(rows, ((0, 0), (0, CMP_STRIDE), (0, 0), (0, 0))).reshape(b, nc + 1, CMP_STRIDE, g, d)
    blocks = jnp.concatenate([halves[:, :-1], halves[:, 1:]], axis=2) + pe[:, None, :]
    flat = blocks.transpose(0, 1, 3, 2, 4).reshape(b, nc, g, CMP_BLOCK * d)
    return jax.nn.gelu(flat @ w1) @ w2


def compress_kv(rows, pe, w1, w2):
    return (compress(rows[:, :, 0], pe[0], w1[0], w2[0]),
            compress(rows[:, :, 1], pe[1], w1[1], w2[1]))


def attend_block(q, gates, qpos, kc, vc, ks_t, vs_t, kw, vw, kwpos, slopes):
    b, nq = q.shape[:2]
    nc = kc.shape[1]
    ns = ks_t.shape[2]
    tq = qpos.astype(jnp.float32)
    sl = slopes[:, :, None, None]
    ci = jnp.arange(nc, dtype=jnp.int32)
    c_valid = (ci * CMP_STRIDE + CMP_BLOCK - 1)[None, :] <= qpos[:, None]
    c_dist = tq[:, None] - (ci * CMP_STRIDE).astype(jnp.float32)[None, :] - 0.5 * (CMP_BLOCK - 1)
    s_c = jnp.einsum('bqgrd,bcgd->bgrqc', q, kc).astype(jnp.float32) - sl * c_dist
    p_c = jnp.where(c_valid, jax.nn.softmax(jnp.where(c_valid, s_c, NEG), axis=-1), 0.0)
    o_c = jnp.einsum('bgrqc,bcgd->bqgrd', p_c.astype(vc.dtype), vc)
    r4 = p_c.sum(axis=2).reshape(b, N_KV, nq, ns, SEL_BLOCK // CMP_STRIDE)
    imp = r4.sum(-1) + jnp.pad(r4[..., :-1, -1], ((0, 0), (0, 0), (0, 0), (1, 0)))
    si = jnp.arange(ns, dtype=jnp.int32)
    cur = (qpos // SEL_BLOCK)[:, None]
    s_valid = (si * SEL_BLOCK)[None, :] <= qpos[:, None]
    forced = (si[None, :] == 0) | (si[None, :] == cur) | (si[None, :] == cur - 1)
    score = jnp.where(s_valid, imp + jnp.where(forced, FORCE_BONUS, 0.0), NEG)
    _, idx = lax.top_k(score, min(N_SEL, ns))
    bi = jnp.arange(b)[:, None, None, None]
    gi = jnp.arange(N_KV)[None, :, None, None]
    k_sel = ks_t[bi, gi, idx]
    v_sel = vs_t[bi, gi, idx]
    spos = idx[..., None] * SEL_BLOCK + jnp.arange(SEL_BLOCK, dtype=jnp.int32)
    tok_valid = spos <= qpos[None, None, :, None, None]
    s_dist = tq[None, None, :, None, None] - spos.astype(jnp.float32)
    s_s = jnp.einsum('bqgrd,bgqnkd->bgrqnk', q, k_sel).astype(jnp.float32) - slopes[None, :, :, None, None, None] * s_dist[:, :, None]
    p_s = jax.nn.softmax(jnp.where(tok_valid[:, :, None], s_s, NEG), axis=(-2, -1))
    o_s = jnp.einsum('bgrqnk,bgqnkd->bqgrd', p_s.astype(v_sel.dtype), v_sel)
    w_valid = (kwpos[None, :] <= qpos[:, None]) & (qpos[:, None] - kwpos[None, :] < WINDOW) & (kwpos[None, :] >= 0)
    w_dist = tq[:, None] - kwpos.astype(jnp.float32)[None, :]
    s_w = jnp.einsum('bqgrd,bkgd->bgrqk', q, kw).astype(jnp.float32) - sl * w_dist
    p_w = jax.nn.softmax(jnp.where(w_valid, s_w, NEG), axis=-1)
    o_w = jnp.einsum('bgrqk,bkgd->bqgrd', p_w.astype(vw.dtype), vw)
    o = gates[..., 0:1] * o_c + gates[..., 1:2] * o_s + gates[..., 2:3] * o_w
    return o.reshape(b, nq, N_HEADS * HEAD_DIM)


def nsa_prompt(q, gates, kv_cmp, kv_sel, kv_win, pe, w1, w2, slopes):
    b, t = q.shape[:2]
    kc, vc = compress_kv(kv_cmp, pe, w1, w2)
    sel_t = kv_sel.reshape(b, t // SEL_BLOCK, SEL_BLOCK, 2, N_KV, HEAD_DIM).transpose(3, 0, 4, 1, 2, 5)
    kw_pad = jnp.pad(kv_win, ((0, 0), (WINDOW, 0), (0, 0), (0, 0), (0, 0)))

    def one(i):
        q0 = i * Q_BLOCK
        qb = lax.dynamic_slice_in_dim(q, q0, Q_BLOCK, axis=1)
        gb = lax.dynamic_slice_in_dim(gates, q0, Q_BLOCK, axis=1)
        wb = lax.dynamic_slice_in_dim(kw_pad, q0, WINDOW + Q_BLOCK, axis=1)
        qpos = q0 + jnp.arange(Q_BLOCK, dtype=jnp.int32)
        kwpos = q0 - WINDOW + jnp.arange(WINDOW + Q_BLOCK, dtype=jnp.int32)
        return attend_block(qb, gb, qpos, kc, vc, sel_t[0], sel_t[1], wb[:, :, 0], wb[:, :, 1], kwpos, slopes)

    o = lax.map(one, jnp.arange(t // Q_BLOCK, dtype=jnp.int32))
    o = o.transpose(1, 0, 2, 3).reshape(b, t, N_HEADS * HEAD_DIM)
    keep = min(WINDOW, t)
    return o, (kv_cmp, kv_sel, kv_win[:, t - keep:])


def nsa_sample(q, gates, kv_cmp, kv_sel, kv_win, cache_cmp, cache_sel, cache_win, page_table, pe, w1, w2, slopes):
    b, s = q.shape[:2]
    past = page_table.shape[1] * cache_cmp.shape[1]
    t_all = past + s
    t_pad = -(-t_all // SEL_BLOCK) * SEL_BLOCK

    def full_rows(cache, new):
        rows = cache[page_table].reshape(b, past, 2, N_KV, HEAD_DIM)
        rows = jnp.concatenate([rows, new], axis=1)
        return jnp.pad(rows, ((0, 0), (0, t_pad - t_all), (0, 0), (0, 0), (0, 0)))

    kc, vc = compress_kv(full_rows(cache_cmp, kv_cmp), pe, w1, w2)
    sel_t = full_rows(cache_sel, kv_sel).reshape(b, t_pad // SEL_BLOCK, SEL_BLOCK, 2, N_KV, HEAD_DIM).transpose(3, 0, 4, 1, 2, 5)
    wbuf = cache_win.shape[1]
    win = jnp.concatenate([cache_win, kv_win], axis=1)
    qpos = past + jnp.arange(s, dtype=jnp.int32)
    kwpos = past - wbuf + jnp.arange(wbuf + s, dtype=jnp.int32)
    o = attend_block(q, gates, qpos, kc, vc, sel_t[0], sel_t[1], win[:, :, 0], win[:, :, 1], kwpos, slopes)
    return o, (kv_cmp, kv_sel, win[:, s:])


def layer(x, c, conv_prev, attend, w):
    b, t = x.shape[:2]
    mod = jax.nn.silu(c) @ w['w_ada'] + w['b_ada']
    sh1, sc1, g1, sh2, sc2, g2 = jnp.split(mod[:, None, :], 6, axis=-1)
    h = rmsnorm(x, w['norm1']) * (1 + sc1) + sh1
    bg, cg, xin, q, kc, vc, ks, vs, kw, vw, nsa_g, merge_g = split_columns(h @ w['w_in'])
    v, conv_new = short_conv(cg * xin, conv_prev, w['w_conv'], w['b_conv'])
    y_a = (bg * v) @ w['w_out_conv']
    qh = q.reshape(b, t, N_KV, GROUP, HEAD_DIM) * (HEAD_DIM ** -0.5)
    kvh = lambda k_, v_: jnp.stack([k_.reshape(b, t, N_KV, HEAD_DIM), v_.reshape(b, t, N_KV, HEAD_DIM)], axis=2)
    gates = jax.nn.sigmoid(nsa_g.reshape(b, t, N_KV, GROUP, 3))
    o_b, nsa_state = attend(qh, gates, kvh(kc, vc), kvh(ks, vs), kvh(kw, vw))
    y_b = o_b @ w['w_o_nsa']
    ga, gb = jnp.split(jax.nn.sigmoid(merge_g), 2, axis=-1)
    x = x + g1 * ((ga * y_a + gb * y_b) @ w['w_out'])
    h2 = rmsnorm(x, w['norm2']) * (1 + sc2) + sh2
    x = x + g2 * ((jax.nn.silu(h2 @ w['w_gate']) * (h2 @ w['w_up'])) @ w['w_down'])
    return x, nsa_state, conv_new


def setup_inputs(seed: int = 0) -> dict:
    key = jax.random.key(seed)
    k = jax.random.split(key, 28)
    n_pages = PAST_LEN // PAGE_SIZE
    n_used = DEC_BATCH * n_pages
    n_phys = n_used + (n_used + 3) // 4
    win_buf = min(WINDOW, PAST_LEN)
    kv_row = (2, N_KV, HEAD_DIM)

    def nrm(kk, shape, scale=1.0):
        return jax.random.normal(kk, shape, jnp.float32) * scale

    return {
        'x_prompt': nrm(k[0], (BATCH, SEQ, D_MODEL)),
        'x_sample': nrm(k[1], (DEC_BATCH, DEC_SEQ, D_MODEL)),
        'c_prompt': nrm(k[2], (BATCH, D_MODEL)),
        'c_sample': nrm(k[3], (DEC_BATCH, D_MODEL)),
        'cache_cmp': nrm(k[4], (DEPTH, n_phys, PAGE_SIZE) + kv_row),
        'cache_sel': nrm(k[5], (DEPTH, n_phys, PAGE_SIZE) + kv_row),
        'cache_win': nrm(k[6], (DEPTH, DEC_BATCH, win_buf) + kv_row),
        'state_conv': nrm(k[7], (DEPTH, DEC_BATCH, CONV_W - 1, D_CONV)),
        'page_table': jax.random.permutation(k[8], n_phys)[:n_used].reshape(DEC_BATCH, n_pages).astype(jnp.int32),
        'w_ada': nrm(k[9], (DEPTH, D_MODEL, 6 * D_MODEL), 0.5 * D_MODEL ** -0.5),
        'b_ada': nrm(k[10], (DEPTH, 6 * D_MODEL), 0.01),
        'norm1': 1.0 + nrm(k[11], (DEPTH, D_MODEL), 0.02),
        'w_in': nrm(k[12], (DEPTH, D_MODEL, IN_COLS), D_MODEL ** -0.5),
        'w_conv': nrm(k[13], (DEPTH, CONV_W, D_CONV), CONV_W ** -0.5),
        'b_conv': nrm(k[14], (DEPTH, D_CONV), 0.01),
        'w_out_conv': nrm(k[15], (DEPTH, D_CONV, D_MODEL), D_CONV ** -0.5),
        'pe_cmp': nrm(k[16], (DEPTH, 2, CMP_BLOCK, HEAD_DIM), 0.02),
        'w_phi1': nrm(k[17], (DEPTH, 2, CMP_BLOCK * HEAD_DIM, D_PHI), (CMP_BLOCK * HEAD_DIM) ** -0.5),
        'w_phi2': nrm(k[18], (DEPTH, 2, D_PHI, HEAD_DIM), D_PHI ** -0.5),
        'w_o_nsa': nrm(k[19], (DEPTH, N_HEADS * HEAD_DIM, D_MODEL), (N_HEADS * HEAD_DIM) ** -0.5),
        'w_out': nrm(k[20], (DEPTH, D_MODEL, D_MODEL), D_MODEL ** -0.5),
        'norm2': 1.0 + nrm(k[21], (DEPTH, D_MODEL), 0.02),
        'w_gate': nrm(k[22], (DEPTH, D_MODEL, D_FF), D_MODEL ** -0.5),
        'w_up': nrm(k[23], (DEPTH, D_MODEL, D_FF), D_MODEL ** -0.5),
        'w_down': nrm(k[24], (DEPTH, D_FF, D_MODEL), D_FF ** -0.5),
        'norm_f': 1.0 + nrm(k[25], (D_MODEL,), 0.02),
    }


def reference(x_prompt, x_sample, c_prompt, c_sample, cache_cmp, cache_sel, cache_win, state_conv, page_table,
              w_ada, b_ada, norm1, w_in, w_conv, b_conv, w_out_conv, pe_cmp, w_phi1, w_phi2, w_o_nsa, w_out,
              norm2, w_gate, w_up, w_down, norm_f):
    slopes = alibi_slopes()
    xp, xs = x_prompt, x_sample
    conv0 = jnp.zeros((xp.shape[0], CONV_W - 1, D_CONV), xp.dtype)
    st_p, st_s = [], []
    for l in range(DEPTH):
        w = dict(w_ada=w_ada[l], b_ada=b_ada[l], norm1=norm1[l], w_in=w_in[l], w_conv=w_conv[l], b_conv=b_conv[l],
                 w_out_conv=w_out_conv[l], w_o_nsa=w_o_nsa[l], w_out=w_out[l], norm2=norm2[l],
                 w_gate=w_gate[l], w_up=w_up[l], w_down=w_down[l])
        att_p = functools.partial(nsa_prompt, pe=pe_cmp[l], w1=w_phi1[l], w2=w_phi2[l], slopes=slopes)
        xp, nsa_p, conv_p = layer(xp, c_prompt, conv0, att_p, w)
        att_s = functools.partial(nsa_sample, cache_cmp=cache_cmp[l], cache_sel=cache_sel[l], cache_win=cache_win[l],
                                  page_table=page_table, pe=pe_cmp[l], w1=w_phi1[l], w2=w_phi2[l], slopes=slopes)
        xs, nsa_s, conv_s = layer(xs, c_sample, state_conv[l], att_s, w)
        st_p.append(nsa_p + (conv_p,))
        st_s.append(nsa_s + (conv_s,))
    y_prompt = rmsnorm(xp, norm_f)
    y_sample = rmsnorm(xs, norm_f)
    new_cmp_prompt = jnp.stack([s_[0] for s_ in st_p])
    new_sel_prompt = jnp.stack([s_[1] for s_ in st_p])
    new_win_prompt = jnp.stack([s_[2] for s_ in st_p])
    new_conv_prompt = jnp.stack([s_[3] for s_ in st_p])
    new_cmp_sample = jnp.stack([s_[0] for s_ in st_s])
    new_sel_sample = jnp.stack([s_[1] for s_ in st_s])
    new_win_sample = jnp.stack([s_[2] for s_ in st_s])
    new_conv_sample = jnp.stack([s_[3] for s_ in st_s])
    return (y_prompt, y_sample, new_cmp_prompt, new_sel_prompt, new_win_prompt, new_conv_prompt,
            new_cmp_sample, new_sel_sample, new_win_sample, new_conv_sample)
```

```python
import functools

import jax
import jax.numpy as jnp
import numpy as np
from jax import lax
from jax.experimental import pallas as pl
from jax.experimental.pallas import tpu as pltpu

F32 = jnp.float32
BF16 = jnp.bfloat16

D_MODEL = 1024
N_HEADS = 16
HEAD_DIM = 64
N_KV = 4
GROUP = N_HEADS // N_KV
KV_W = N_KV * HEAD_DIM
CMP_BLOCK = 32
CMP_STRIDE = 16
SEL_BLOCK = 64
N_SEL = 16
WINDOW = 512
D_PHI = 2 * HEAD_DIM
Q_BLOCK = 128
PAGE = 128
D_FF = ((8 * D_MODEL // 3 + 255) // 256) * 256
EPS = 1e-6
NEG = -1e30
FORCE_BONUS = 1e3

LANES = 128
AUG_W = 2 * LANES
HALVES_PER_PAGE = PAGE // CMP_STRIDE
ROW_TILES = 2 * KV_W // LANES
VMEM_LIMIT = 56 * 1024 * 1024

_C_CONV = 0
_C_QKV = 3 * D_MODEL
_C_GATE = _C_QKV + N_HEADS * HEAD_DIM + 6 * KV_W
_C_MERGE = _C_GATE + 3 * N_HEADS
_QKV_COLS = _C_MERGE - _C_QKV
_QKV_COLS_PAD = -(-_QKV_COLS // LANES) * LANES


def _dot(a, b):
    return jnp.dot(a, b, preferred_element_type=F32)


def _dot_nt(a, b):
    return lax.dot_general(a, b, (((1,), (1,)), ((), ())), preferred_element_type=F32)


def _params(n_axes):
    return pltpu.CompilerParams(dimension_semantics=("arbitrary",) * n_axes, vmem_limit_bytes=VMEM_LIMIT)


def _const_spec(shape):
    return pl.BlockSpec(shape, lambda *_: (0,) * len(shape))


def _mod_norm(x, nw, sc, sh):
    inv = lax.rsqrt(jnp.mean(x * x, axis=-1, keepdims=True) + EPS)
    return (x * inv) * nw * (1.0 + sc) + sh


def _split3(x):
    a = x.astype(BF16)
    r = x - a.astype(F32)
    b = r.astype(BF16)
    c = (r - b.astype(F32)).astype(BF16)
    return a, b, c


def _ada_kernel(c_ref, w_ref, b_ref, o_ref):
    c = c_ref[...]
    s = c * jax.nn.sigmoid(c)
    o_ref[...] = jnp.dot(s, w_ref[...], preferred_element_type=F32, precision=lax.Precision.HIGHEST) + b_ref[...]


def _ada(c, w_ada, b_ada):
    n = c.shape[0]
    tn = 1536
    return pl.pallas_call(
        _ada_kernel,
        grid=(6 * D_MODEL // tn,),
        in_specs=[_const_spec((n, D_MODEL)),
                  pl.BlockSpec((D_MODEL, tn), lambda j: (0, j)),
                  pl.BlockSpec((1, tn), lambda j: (0, j))],
        out_specs=pl.BlockSpec((n, tn), lambda j: (0, j)),
        out_shape=jax.ShapeDtypeStruct((n, 6 * D_MODEL), F32),
        compiler_params=_params(1),
        name="ada",
    )(c, w_ada, b_ada.reshape(1, -1))


class _Mod:
    def __init__(self, mod, per_token, seq_len, tm):
        self.per_token = per_token
        if per_token:
            self.rows = jnp.repeat(mod, seq_len, axis=0)
        else:
            self.rows = mod.reshape(mod.shape[0], 1, 6 * D_MODEL)
        self.tiles_per_seq = None if per_token else seq_len // tm
        self.tm = tm

    def spec(self, k):
        if self.per_token:
            return pl.BlockSpec((self.tm, D_MODEL), lambda i: (i, k))
        tps = self.tiles_per_seq
        return pl.BlockSpec((None, 1, D_MODEL), lambda i: (i // tps, 0, k))


def _conv_kernel(*refs, carry_rows, seq_len):
    if carry_rows:
        (x_ref, nw_ref, sc_ref, sh_ref, w_ref, wc_ref, bc_ref, a_ref, tail_ref, carry_ref) = refs
    else:
        (x_ref, nw_ref, sc_ref, sh_ref, w_ref, wc_ref, bc_ref, p1_ref, p2_ref, a_ref, tail_ref) = refs
    tm = x_ref.shape[0]
    h = _mod_norm(x_ref[...], nw_ref[...], sc_ref[...], sh_ref[...]).astype(BF16)
    z = _dot(h, w_ref[...])
    bg = z[:, 0:D_MODEL]
    u = z[:, D_MODEL:2 * D_MODEL] * z[:, 2 * D_MODEL:3 * D_MODEL]
    row = lax.broadcasted_iota(jnp.int32, (tm, 1), 0)
    u1 = pltpu.roll(u, 1, 0)
    u2 = pltpu.roll(u, 2, 0)
    if carry_rows:
        @pl.when(pl.program_id(0) % carry_rows == 0)
        def _():
            carry_ref[...] = jnp.zeros_like(carry_ref)
        c0 = carry_ref[0:1, :]
        c1 = carry_ref[1:2, :]
        u1 = jnp.where(row == 0, c1, u1)
        u2 = jnp.where(row == 0, c0, jnp.where(row == 1, c1, u2))
        carry_ref[0:2, :] = u[tm - 2:tm, :]
        tail_ref[...] = u[tm - 8:tm, :]
    else:
        pos = lax.rem(row, seq_len)
        u1 = jnp.where(pos >= 1, u1, p1_ref[...])
        u2 = jnp.where(pos >= 2, u2, p2_ref[...])
        tail_ref[...] = u
    v = bc_ref[...] + wc_ref[0:1, :] * u2 + wc_ref[1:2, :] * u1 + wc_ref[2:3, :] * u
    a_ref[...] = (bg * v).astype(BF16)


def _conv_path(x2d, mod, nw, w_conv_in, w_conv, b_conv, *, tm, seq_len, prev=None):
    n = x2d.shape[0]
    tok = pl.BlockSpec((tm, D_MODEL), lambda i: (i, 0))
    in_specs = [tok, _const_spec((1, D_MODEL)), mod.spec(1), mod.spec(0),
                _const_spec((D_MODEL, 3 * D_MODEL)), _const_spec((3, D_MODEL)), _const_spec((1, D_MODEL))]
    args = [x2d, nw, mod.rows, mod.rows, w_conv_in, w_conv, b_conv]
    if prev is None:
        tps = seq_len // tm
        out_specs = [tok, pl.BlockSpec((None, 8, D_MODEL), lambda i: (i // tps, 0, 0))]
        out_shape = [jax.ShapeDtypeStruct((n, D_MODEL), BF16), jax.ShapeDtypeStruct((n // seq_len, 8, D_MODEL), F32)]
        scratch = [pltpu.VMEM((8, D_MODEL), F32)]
        kern = functools.partial(_conv_kernel, carry_rows=tps, seq_len=seq_len)
    else:
        in_specs += [tok, tok]
        args += list(prev)
        out_specs = [tok, tok]
        out_shape = [jax.ShapeDtypeStruct((n, D_MODEL), BF16), jax.ShapeDtypeStruct((n, D_MODEL), F32)]
        scratch = []
        kern = functools.partial(_conv_kernel, carry_rows=0, seq_len=seq_len)
    return pl.pallas_call(kern, grid=(n // tm,), in_specs=in_specs, out_specs=out_specs, out_shape=out_shape,
                          scratch_shapes=scratch, compiler_params=_params(1), name="conv_path")(*args)


def _qkv_kernel(*refs, aug):
    if aug:
        (x_ref, nw_ref, sc_ref, sh_ref, w_ref, clo_ref, chi_ref, cv_ref,
         q_ref, kvc_ref, kvs_ref, kvw_ref, g_ref, ksa_ref, vsa_ref, kwa_ref, vwa_ref) = refs
    else:
        (x_ref, nw_ref, sc_ref, sh_ref, w_ref, q_ref, kvc_ref, kvs_ref, kvw_ref, g_ref) = refs
    tm = x_ref.shape[0]
    h = _mod_norm(x_ref[...], nw_ref[...], sc_ref[...], sh_ref[...]).astype(BF16)
    z = _dot(h, w_ref[...])
    low = lax.broadcasted_iota(jnp.int32, (tm, LANES), 1) < HEAD_DIM
    nq = N_HEADS * HEAD_DIM
    for c in range(N_HEADS // 2):
        t = z[:, c * LANES:(c + 1) * LANES] * (HEAD_DIM ** -0.5)
        q_ref[:, (2 * c) * LANES:(2 * c + 1) * LANES] = jnp.where(low, t, 0.0).astype(BF16)
        q_ref[:, (2 * c + 1) * LANES:(2 * c + 2) * LANES] = jnp.where(low, pltpu.roll(t, HEAD_DIM, 1), 0.0).astype(BF16)
    kvc_ref[...] = z[:, nq:nq + 2 * KV_W]
    kvs_ref[...] = z[:, nq + 2 * KV_W:nq + 4 * KV_W]
    kvw_ref[...] = z[:, nq + 4 * KV_W:nq + 6 * KV_W]
    g_ref[...] = jax.nn.sigmoid(z[:, nq + 6 * KV_W:nq + 6 * KV_W + LANES])
    if aug:
        chi = chi_ref[...]
        cv = cv_ref[...]
        for br, (ka_ref, va_ref) in enumerate(((ksa_ref, vsa_ref), (kwa_ref, vwa_ref))):
            kbase = nq + 2 * KV_W * (br + 1)
            clo = clo_ref[...] if br == 0 else 0.0
            for g in range(N_KV):
                kt = z[:, kbase + (g // 2) * LANES:kbase + (g // 2 + 1) * LANES]
                vt = z[:, kbase + KV_W + (g // 2) * LANES:kbase + KV_W + (g // 2 + 1) * LANES]
                if g % 2:
                    kt = pltpu.roll(kt, HEAD_DIM, 1)
                    vt = pltpu.roll(vt, HEAD_DIM, 1)
                ka_ref[g, :, 0:LANES] = jnp.where(low, kt, clo).astype(BF16)
                ka_ref[g, :, LANES:AUG_W] = chi
                va_ref[g, :, :] = jnp.where(low, vt, cv).astype(BF16)


def _qkv_path(x2d, mod, nw, w_qkv, *, tm, seq_len, consts=None):
    n = x2d.shape[0]
    aug = consts is not None
    tok = lambda w: pl.BlockSpec((tm, w), lambda i: (i, 0))
    in_specs = [tok(D_MODEL), _const_spec((1, D_MODEL)), mod.spec(1), mod.spec(0),
                _const_spec((D_MODEL, _QKV_COLS_PAD))]
    args = [x2d, nw, mod.rows, mod.rows, w_qkv]
    out_specs = [tok(N_HEADS * LANES), tok(2 * KV_W), tok(2 * KV_W), tok(2 * KV_W), tok(LANES)]
    out_shape = [jax.ShapeDtypeStruct((n, N_HEADS * LANES), BF16)] + \
                [jax.ShapeDtypeStruct((n, 2 * KV_W), F32)] * 3 + [jax.ShapeDtypeStruct((n, LANES), F32)]
    if aug:
        tps = seq_len // tm
        nb = n // seq_len
        pos = lambda w: pl.BlockSpec((tm, w), lambda i: (i % tps, 0))
        in_specs += [pos(LANES), pos(LANES), _const_spec((1, LANES))]
        args += list(consts)
        ka = pl.BlockSpec((None, N_KV, tm, AUG_W), lambda i: (i // tps, 0, i % tps, 0))
        va = pl.BlockSpec((None, N_KV, tm, LANES), lambda i: (i // tps, 0, i % tps, 0))
        out_specs += [ka, va, ka, va]
        ka_s = jax.ShapeDtypeStruct((nb, N_KV, seq_len, AUG_W), BF16)
        va_s = jax.ShapeDtypeStruct((nb, N_KV, seq_len, LANES), BF16)
        out_shape += [ka_s, va_s, ka_s, va_s]
    return pl.pallas_call(functools.partial(_qkv_kernel, aug=aug), grid=(n // tm,), in_specs=in_specs,
                          out_specs=out_specs, out_shape=out_shape, compiler_params=_params(1), name="qkv_path")(*args)


def _gather_half(ref, e, nrows):
    c, odd = e // 2, e % 2
    low = lax.broadcasted_iota(jnp.int32, (nrows, LANES), 1) < HEAD_DIM
    tiles = []
    for qq in range(CMP_STRIDE // 2):
        ra, rb = 2 * qq * ROW_TILES + c, (2 * qq + 1) * ROW_TILES + c
        if nrows == 1:
            a = ref[ra:ra + 1, :]
            b = ref[rb:rb + 1, :]
        else:
            a = ref[pl.ds(ra, nrows, stride=CMP_STRIDE * ROW_TILES), :]
            b = ref[pl.ds(rb, nrows, stride=CMP_STRIDE * ROW_TILES), :]
        if odd:
            tiles.append(jnp.where(low, pltpu.roll(a, HEAD_DIM, 1), b))
        else:
            tiles.append(jnp.where(low, a, pltpu.roll(b, HEAD_DIM, 1)))
    return jnp.concatenate(tiles, axis=1)


def _compress_kernel(pt_ref, *refs, pps):
    pages = refs[:pps]
    nxt_ref, pe_ref, w1_ref, w2_ref, chi_ref, cv_ref, kc_ref, vc_ref = refs[pps:]
    t = pl.program_id(1)
    last = t == pl.num_programs(1) - 1
    nhb = pps * HALVES_PER_PAGE
    n = N_KV * nhb
    row = lax.broadcasted_iota(jnp.int32, (n, 1), 0)
    chi = chi_ref[...]
    for kv in range(2):
        parts = []
        for g in range(N_KV):
            e = kv * N_KV + g
            parts += [_gather_half(p, e, HALVES_PER_PAGE) for p in pages]
        look = [_gather_half(nxt_ref, kv * N_KV + g, 1) for g in range(N_KV)]
        extra = jnp.concatenate([pe_ref[kv], jnp.zeros((2, CMP_STRIDE * HEAD_DIM), F32)] + look, axis=0)
        xmat = jnp.concatenate(parts + [extra], axis=0).astype(BF16)
        hab = _dot(xmat, w1_ref[kv])
        ha = hab[0:n, 0:D_PHI]
        hb = hab[0:n, D_PHI:2 * D_PHI]
        pbias = hab[n:n + 1, 0:D_PHI] + hab[n + 1:n + 2, D_PHI:2 * D_PHI]
        hbn = pltpu.roll(hb, n - 1, 0)
        for g in range(N_KV):
            la = jnp.where(last, 0.0, hab[n + 4 + g:n + 5 + g, D_PHI:2 * D_PHI])
            hbn = jnp.where(row == g * nhb + nhb - 1, la, hbn)
        act = jax.nn.gelu(ha + hbn + pbias).astype(BF16)
        out = _dot(act, w2_ref[kv])
        for g in range(N_KV):
            blk = out[g * nhb:(g + 1) * nhb, :]
            if kv == 0:
                kc_ref[g, :, 0:LANES] = blk.astype(BF16)
                kc_ref[g, :, LANES:AUG_W] = chi
            else:
                vc_ref[g, :, :] = (blk + cv_ref[...]).astype(BF16)


def _compress(pages, page_table, pe2, w1cat, w2pad, chi_c, cv, *, pps):
    nb, npages = page_table.shape
    nchunks = npages // pps
    nhb = pps * HALVES_PER_PAGE
    ncb = npages * HALVES_PER_PAGE

    def page_spec(k):
        return pl.BlockSpec((None, PAGE * ROW_TILES, LANES), lambda b, t, pt: (pt[b, t * pps + k], 0, 0))

    nxt_spec = pl.BlockSpec((None, CMP_STRIDE * ROW_TILES, LANES),
                            lambda b, t, pt: (pt[b, jnp.minimum((t + 1) * pps, npages - 1)], 0, 0))
    cs = lambda shape: pl.BlockSpec(shape, lambda b, t, pt: (0,) * len(shape))
    in_specs = [page_spec(k) for k in range(pps)] + [
        nxt_spec, cs((2, 2, CMP_STRIDE * HEAD_DIM)), cs((2, CMP_STRIDE * HEAD_DIM, 2 * D_PHI)),
        cs((2, D_PHI, LANES)), pl.BlockSpec((nhb, LANES), lambda b, t, pt: (t, 0)), cs((1, LANES))]
    out_specs = [pl.BlockSpec((None, N_KV, nhb, AUG_W), lambda b, t, pt: (b, 0, t, 0)),
                 pl.BlockSpec((None, N_KV, nhb, LANES), lambda b, t, pt: (b, 0, t, 0))]
    out_shape = [jax.ShapeDtypeStruct((nb, N_KV, ncb, AUG_W), BF16), jax.ShapeDtypeStruct((nb, N_KV, ncb, LANES), BF16)]
    gs = pltpu.PrefetchScalarGridSpec(num_scalar_prefetch=1, grid=(nb, nchunks), in_specs=in_specs, out_specs=out_specs)
    return pl.pallas_call(functools.partial(_compress_kernel, pps=pps), grid_spec=gs, out_shape=out_shape,
                          compiler_params=_params(2), name="compress")(
        page_table, *([pages] * (pps + 1)), pe2, w1cat, w2pad, chi_c, cv)


def _flash_update(s, v, m_ref, acc_ref):
    m_old = m_ref[...]
    m_new = jnp.maximum(m_old, jnp.max(s, axis=1, keepdims=True))
    alpha = jnp.exp(m_old - m_new)
    p = jnp.exp(s - m_new)
    acc_ref[...] = alpha * acc_ref[...] + _dot(p.astype(BF16), v)
    m_ref[...] = m_new


def _attn_prompt_kernel(q_ref, g_ref, qc_ref, mt_ref, kc_ref, vc_ref, ks_ref, vs_ref, kw_ref, vw_ref, o_ref,
                        qa_ref, m_ref, acc_ref, *, kt):
    nc = kc_ref.shape[0]
    ns = mt_ref.shape[0]
    nrow = GROUP * Q_BLOCK
    qb = pl.program_id(2)
    q0 = qb * Q_BLOCK
    lane = lax.broadcasted_iota(jnp.int32, (Q_BLOCK, LANES), 1)
    row = lax.broadcasted_iota(jnp.int32, (nrow, 1), 0)
    qpos = q0 + (row & (Q_BLOCK - 1))

    for r in range(GROUP):
        qa_ref[r * Q_BLOCK:(r + 1) * Q_BLOCK, 0:LANES] = q_ref[:, r * LANES:(r + 1) * LANES]
    qa_ref[:, LANES:AUG_W] = qc_ref[...]

    s = _dot_nt(qa_ref[...], kc_ref[...])
    ci = lax.broadcasted_iota(jnp.int32, (1, nc), 1)
    c_valid = ci * CMP_STRIDE + (CMP_BLOCK - 1) <= qpos
    s = jnp.where(c_valid, s, NEG)
    e = jnp.exp(s - jnp.max(s, axis=1, keepdims=True))
    p = jnp.where(c_valid, e / jnp.sum(e, axis=1, keepdims=True), 0.0)
    o_c = _dot(p.astype(BF16), vc_ref[...])

    psum = p[0:Q_BLOCK]
    for r in range(1, GROUP):
        psum = psum + p[r * Q_BLOCK:(r + 1) * Q_BLOCK]
    mt = mt_ref[...]
    imp = sum(_dot_nt(mt, part) for part in _split3(psum))
    si = lax.broadcasted_iota(jnp.int32, (ns, Q_BLOCK), 0)
    qpos_t = q0 + lax.broadcasted_iota(jnp.int32, (ns, Q_BLOCK), 1)
    cur = lax.shift_right_logical(qpos_t, 6)
    s_valid = si * SEL_BLOCK <= qpos_t
    forced = (si == 0) | (si == cur) | (si == cur - 1)
    score = jnp.where(s_valid, imp + jnp.where(forced, FORCE_BONUS, 0.0), NEG)
    rank = jnp.zeros((ns, Q_BLOCK), jnp.int32)
    for sp in range(ns):
        other = score[sp:sp + 1, :]
        beats = (other > score) | ((other == score) & (si > sp))
        rank = rank + beats.astype(jnp.int32)
    bias_t = jnp.where(rank < min(N_SEL, ns), 0.0, NEG)
    pad_lo = jnp.zeros((HEAD_DIM, Q_BLOCK), F32)
    pieces = [pad_lo, bias_t] + ([jnp.zeros((HEAD_DIM - ns, Q_BLOCK), F32)] if ns < HEAD_DIM else [])
    bias_q = jnp.concatenate(pieces, axis=0).T.astype(BF16)
    for r in range(GROUP):
        qa_ref[r * Q_BLOCK:(r + 1) * Q_BLOCK, 0:LANES] = jnp.where(lane < HEAD_DIM, q_ref[:, r * LANES:(r + 1) * LANES], bias_q)

    m_ref[...] = jnp.full(m_ref.shape, NEG, F32)
    acc_ref[...] = jnp.zeros(acc_ref.shape, F32)

    def sel_tile(t, masked):
        k0 = pl.multiple_of(t * kt, kt)
        s = _dot_nt(qa_ref[...], ks_ref[pl.ds(k0, kt), :])
        if masked:
            kpos = k0 + lax.broadcasted_iota(jnp.int32, (1, kt), 1)
            s = jnp.where(kpos <= qpos, s, NEG)
        _flash_update(s, vs_ref[pl.ds(k0, kt), :], m_ref, acc_ref)

    n_full = (q0 + 1) // kt

    def body(t, carry):
        sel_tile(t, False)
        return carry

    lax.fori_loop(0, n_full, body, 0)
    sel_tile(n_full, True)
    acc = acc_ref[...]
    o_s = acc / acc[:, HEAD_DIM:HEAD_DIM + 1]

    m_ref[...] = jnp.full(m_ref.shape, NEG, F32)
    acc_ref[...] = jnp.zeros(acc_ref.shape, F32)
    for j in range(WINDOW // Q_BLOCK + 1):
        tabs = qb - WINDOW // Q_BLOCK + j

        @pl.when(tabs >= 0)
        def _():
            k0 = pl.multiple_of(tabs * Q_BLOCK, Q_BLOCK)
            s = _dot_nt(qa_ref[...], kw_ref[pl.ds(k0, Q_BLOCK), :])
            kpos = k0 + lax.broadcasted_iota(jnp.int32, (1, Q_BLOCK), 1)
            w_valid = (kpos <= qpos) & (qpos - kpos < WINDOW)
            _flash_update(jnp.where(w_valid, s, NEG), vw_ref[pl.ds(k0, Q_BLOCK), :], m_ref, acc_ref)

    acc = acc_ref[...]
    o_w = acc / acc[:, HEAD_DIM:HEAD_DIM + 1]

    gts = g_ref[...]
    outs = []
    for r in range(GROUP):
        sl = slice(r * Q_BLOCK, (r + 1) * Q_BLOCK)
        outs.append(gts[:, 3 * r:3 * r + 1] * o_c[sl] + gts[:, 3 * r + 1:3 * r + 2] * o_s[sl]
                    + gts[:, 3 * r + 2:3 * r + 3] * o_w[sl])
    for c in range(GROUP // 2):
        tile = jnp.where(lane < HEAD_DIM, outs[2 * c], pltpu.roll(outs[2 * c + 1], HEAD_DIM, 1))
        o_ref[:, c * LANES:(c + 1) * LANES] = tile.astype(BF16)


def _attn_prompt(qpad, gates, qc, mt, kc, vc, ks, vs, kw, vw):
    nb, t = qpad.shape[0], qpad.shape[1]
    nc = kc.shape[2]
    ns = mt.shape[0]
    kt = min(2 * LANES, t)
    per_bg = lambda rows, w: pl.BlockSpec((None, None, rows, w), lambda b, g, i: (b, g, 0, 0))
    in_specs = [pl.BlockSpec((None, Q_BLOCK, GROUP * LANES), lambda b, g, i: (b, i, g)),
                pl.BlockSpec((None, None, Q_BLOCK, 3 * GROUP), lambda b, g, i: (b, g, i, 0)),
                pl.BlockSpec((None, GROUP * Q_BLOCK, LANES), lambda b, g, i: (g, 0, 0)),
                pl.BlockSpec((ns, nc), lambda b, g, i: (0, 0)),
                per_bg(nc, AUG_W), per_bg(nc, LANES), per_bg(t, AUG_W), per_bg(t, LANES), per_bg(t, AUG_W), per_bg(t, LANES)]
    out_spec = pl.BlockSpec((None, Q_BLOCK, GROUP * HEAD_DIM), lambda b, g, i: (b, i, g))
    scratch = [pltpu.VMEM((GROUP * Q_BLOCK, AUG_W), BF16), pltpu.VMEM((GROUP * Q_BLOCK, 1), F32),
               pltpu.VMEM((GROUP * Q_BLOCK, LANES), F32)]
    return pl.pallas_call(functools.partial(_attn_prompt_kernel, kt=kt), grid=(nb, N_KV, t // Q_BLOCK),
                          in_specs=in_specs, out_specs=out_spec,
                          out_shape=jax.ShapeDtypeStruct((nb, t, N_HEADS * HEAD_DIM), BF16),
                          scratch_shapes=scratch, compiler_params=_params(3), name="attn_prompt")(
        qpad, gates, qc, mt, kc, vc, ks, vs, kw, vw)


Q_PAD = 8
ROWS_G = GROUP * Q_PAD
ROWS_S = N_KV * ROWS_G


def _attn_sample_kernel(pt_ref, *refs, pps, past, wbuf, n_new):
    pages = refs[:pps]
    (qa_ref, qbd_ref, g_ref, slope_ref, kc_ref, vc_ref, ms_ref, e_ref, win_ref, ksn_ref, kwn_ref,
     osw_ref, oc_ref, bias_ref, m_ref, acc_ref, ow_ref) = refs[pps:]
    t = pl.program_id(1)
    nchunks = pl.num_programs(1)
    nc = kc_ref.shape[1]
    ns = past // SEL_BLOCK + 1
    row = lax.broadcasted_iota(jnp.int32, (ROWS_S, 1), 0)
    qpos = past + (row & (Q_PAD - 1))
    qposf = qpos.astype(F32)
    slope = slope_ref[...]
    qbd = qbd_ref[...]

    @pl.when(t == 0)
    def _():
        nsp = ms_ref.shape[1]
        rg = lax.broadcasted_iota(jnp.int32, (ROWS_G, 1), 0)
        qpos_g = past + (rg & (Q_PAD - 1))
        r8 = lax.broadcasted_iota(jnp.int32, (Q_PAD, 1), 0)
        qpos8 = past + r8
        si = lax.broadcasted_iota(jnp.int32, (Q_PAD, nsp), 1)
        for g in range(N_KV):
            s = _dot_nt(qa_ref[g], kc_ref[g])
            ci = lax.broadcasted_iota(jnp.int32, (1, nc), 1)
            c_valid = ci * CMP_STRIDE + (CMP_BLOCK - 1) <= qpos_g
            s = jnp.where(c_valid, s, NEG)
            e = jnp.exp(s - jnp.max(s, axis=1, keepdims=True))
            p = jnp.where(c_valid, e / jnp.sum(e, axis=1, keepdims=True), 0.0)
            oc_ref[g * ROWS_G:(g + 1) * ROWS_G, :] = _dot(p.astype(BF16), vc_ref[g])
            psum = p[0:Q_PAD]
            for r in range(1, GROUP):
                psum = psum + p[r * Q_PAD:(r + 1) * Q_PAD]
            ms = ms_ref[...]
            imp = sum(_dot(part, ms) for part in _split3(psum))
            cur = lax.shift_right_logical(qpos8, 6)
            s_valid = (si * SEL_BLOCK <= qpos8) & (si < ns)
            forced = (si == 0) | (si == cur) | (si == cur - 1)
            score = jnp.where(s_valid, imp + jnp.where(forced, FORCE_BONUS, 0.0), NEG)
            rank = jnp.zeros((Q_PAD, nsp), jnp.int32)
            for sp in range(ns):
                other = score[:, sp:sp + 1]
                beats = (other > score) | ((other == score) & (si > sp))
                rank = rank + beats.astype(jnp.int32)
            bias_ref[g] = jnp.where((rank < min(N_SEL, ns)) & (si < ns), 0.0, NEG).astype(BF16)

        kw = win_ref[:, 0:KV_W].astype(BF16)
        vw = win_ref[:, KV_W:2 * KV_W].astype(BF16)
        kpos = past - wbuf + lax.broadcasted_iota(jnp.int32, (1, wbuf), 1)
        s1 = _dot_nt(qbd, kw) - slope * (qposf - kpos.astype(F32))
        s1 = jnp.where((kpos <= qpos) & (qpos - kpos < WINDOW) & (kpos >= 0), s1, NEG)
        kn = kwn_ref[:, 0:KV_W].astype(BF16)
        vn = kwn_ref[:, KV_W:2 * KV_W].astype(BF16)
        li = lax.broadcasted_iota(jnp.int32, (1, LANES), 1)
        kposn = past + li
        s2 = _dot_nt(qbd, kn) - slope * (qposf - kposn.astype(F32))
        s2 = jnp.where((li < n_new) & (kposn <= qpos) & (qpos - kposn < WINDOW), s2, NEG)
        m = jnp.maximum(jnp.max(s1, axis=1, keepdims=True), jnp.max(s2, axis=1, keepdims=True))
        p1 = jnp.exp(s1 - m)
        p2 = jnp.exp(s2 - m)
        den = jnp.sum(p1, axis=1, keepdims=True) + jnp.sum(p2, axis=1, keepdims=True)
        ow_ref[...] = (_dot((p1 / den).astype(BF16), vw) + _dot((p2 / den).astype(BF16), vn))
        m_ref[...] = jnp.full(m_ref.shape, NEG, F32)
        acc_ref[...] = jnp.zeros(acc_ref.shape, F32)

    def block_bias(emat):
        rows = []
        for g in range(N_KV):
            bt = _dot(bias_ref[g], emat)
            rows += [bt] * GROUP
        return jnp.concatenate(rows, axis=0)

    def update(s, v):
        m_old = m_ref[...]
        m_new = jnp.maximum(m_old, jnp.max(s, axis=1, keepdims=True))
        alpha = jnp.exp(m_old - m_new)
        p = jnp.exp(s - m_new)
        ones = jnp.sum(p, axis=1, keepdims=True)
        acc_ref[:, 0:KV_W] = alpha * acc_ref[:, 0:KV_W] + _dot(p.astype(BF16), v)
        acc_ref[:, KV_W:KV_W + LANES] = alpha * acc_ref[:, KV_W:KV_W + LANES] + ones
        m_ref[...] = m_new

    bias_chunk = block_bias(e_ref[...])
    li = lax.broadcasted_iota(jnp.int32, (1, PAGE), 1)
    for k in range(pps):
        pg = pages[k]
        kpos = (t * pps + k) * PAGE + li
        s = _dot_nt(qbd, pg[:, 0:KV_W].astype(BF16)) - slope * (qposf - kpos.astype(F32))
        update(s + bias_chunk[:, k * PAGE:(k + 1) * PAGE], pg[:, KV_W:2 * KV_W].astype(BF16))

    @pl.when(t == nchunks - 1)
    def _():
        nsp = ms_ref.shape[1]
        kposn = past + li
        e_new = (lax.broadcasted_iota(jnp.int32, (nsp, PAGE), 0) == past // SEL_BLOCK).astype(BF16)
        s = _dot_nt(qbd, ksn_ref[:, 0:KV_W].astype(BF16)) - slope * (qposf - kposn.astype(F32)) + block_bias(e_new)
        s = jnp.where((li < n_new) & (kposn <= qpos), s, NEG)
        update(s, ksn_ref[:, KV_W:2 * KV_W].astype(BF16))
        o_s = acc_ref[:, 0:KV_W] / acc_ref[:, KV_W:KV_W + 1]
        gts = g_ref[...]
        osw_ref[...] = gts[:, 1:2] * o_s + gts[:, 2:3] * ow_ref[...]
        oc_ref[...] = gts[:, 0:1] * oc_ref[...]


def _attn_sample(page_table, pages, qa, qbd, gates, slope_rows, kc, vc, ms, emat, win, ksn, kwn, *, pps, n_new):
    nb, npages = page_table.shape
    nchunks = npages // pps
    past = npages * PAGE
    wbuf = win.shape[1]
    nc = kc.shape[2]
    nsp = ms.shape[1]

    def page_spec(k):
        return pl.BlockSpec((None, PAGE, 2 * KV_W), lambda b, t, pt: (pt[b, t * pps + k], 0, 0))

    per_b = lambda *shape: pl.BlockSpec((None,) + shape, lambda b, t, pt: (b,) + (0,) * len(shape))
    cs = lambda *shape: pl.BlockSpec(shape, lambda b, t, pt: (0,) * len(shape))
    in_specs = [page_spec(k) for k in range(pps)] + [
        per_b(N_KV, ROWS_G, AUG_W), per_b(ROWS_S, KV_W), per_b(ROWS_S, 3), cs(ROWS_S, 1),
        per_b(N_KV, nc, AUG_W), per_b(N_KV, nc, LANES), cs(nc, nsp),
        pl.BlockSpec((nsp, pps * PAGE), lambda b, t, pt: (0, t)),
        per_b(wbuf, 2 * KV_W), per_b(LANES, 2 * KV_W), per_b(LANES, 2 * KV_W)]
    out_specs = [per_b(ROWS_S, KV_W), per_b(ROWS_S, LANES)]
    out_shape = [jax.ShapeDtypeStruct((nb, ROWS_S, KV_W), F32), jax.ShapeDtypeStruct((nb, ROWS_S, LANES), F32)]
    scratch = [pltpu.VMEM((N_KV, Q_PAD, nsp), BF16), pltpu.VMEM((ROWS_S, 1), F32),
               pltpu.VMEM((ROWS_S, KV_W + LANES), F32), pltpu.VMEM((ROWS_S, KV_W), F32)]
    gs = pltpu.PrefetchScalarGridSpec(num_scalar_prefetch=1, grid=(nb, nchunks), in_specs=in_specs,
                                      out_specs=out_specs, scratch_shapes=scratch)
    return pl.pallas_call(functools.partial(_attn_sample_kernel, pps=pps, past=past, wbuf=wbuf, n_new=n_new), grid_spec=gs,
                          out_shape=out_shape, compiler_params=_params(2), name="attn_sample")(
        page_table, *([pages] * pps), qa, qbd, gates, slope_rows, kc, vc, ms, emat, win, ksn, kwn)


def _outproj_kernel(x_ref, a_ref, o_ref, nw_ref, sc_ref, sh_ref, g1_ref, wm_ref, woc_ref, won_ref, wo_ref, x1_ref):
    x = x_ref[...]
    h = _mod_norm(x, nw_ref[...], sc_ref[...], sh_ref[...]).astype(BF16)
    mg = jax.nn.sigmoid(_dot(h, wm_ref[...]))
    y_a = _dot(a_ref[...], woc_ref[...])
    y_b = _dot(o_ref[...], won_ref[...])
    mix = (mg[:, 0:D_MODEL] * y_a + mg[:, D_MODEL:2 * D_MODEL] * y_b).astype(BF16)
    x1_ref[...] = x + g1_ref[...] * _dot(mix, wo_ref[...])


def _outproj(x2d, a, o, mod, nw, w_merge, w_oc, w_on, w_o, *, tm):
    n = x2d.shape[0]
    tok = pl.BlockSpec((tm, D_MODEL), lambda i: (i, 0))
    sq = _const_spec((D_MODEL, D_MODEL))
    return pl.pallas_call(
        _outproj_kernel, grid=(n // tm,),
        in_specs=[tok, tok, tok, _const_spec((1, D_MODEL)), mod.spec(1), mod.spec(0), mod.spec(2),
                  _const_spec((D_MODEL, 2 * D_MODEL)), sq, sq, sq],
        out_specs=tok, out_shape=jax.ShapeDtypeStruct((n, D_MODEL), F32),
        compiler_params=_params(1), name="outproj",
    )(x2d, a, o, nw, mod.rows, mod.rows, mod.rows, w_merge, w_oc, w_on, w_o)


FF_CHUNK = D_FF // 2


def _ffn_kernel(x_ref, nw_ref, sc_ref, sh_ref, g2_ref, nf_ref, wg_ref, wu_ref, wd_ref, y_ref):
    x = x_ref[...]
    h = _mod_norm(x, nw_ref[...], sc_ref[...], sh_ref[...]).astype(BF16)
    acc = jnp.zeros(x.shape, F32)
    for c in range(D_FF // FF_CHUNK):
        sl = slice(c * FF_CHUNK, (c + 1) * FF_CHUNK)
        gate = _dot(h, wg_ref[:, sl])
        up = _dot(h, wu_ref[:, sl])
        act = (gate * jax.nn.sigmoid(gate) * up).astype(BF16)
        acc = acc + _dot(act, wd_ref[sl, :])
    x2 = x + g2_ref[...] * acc
    inv = lax.rsqrt(jnp.mean(x2 * x2, axis=-1, keepdims=True) + EPS)
    y_ref[...] = (x2 * inv) * nf_ref[...]


def _ffn(x1, mod, nw2, nf, w_gate, w_up, w_down, *, tm):
    n = x1.shape[0]
    tok = pl.BlockSpec((tm, D_MODEL), lambda i: (i, 0))
    vec = _const_spec((1, D_MODEL))
    return pl.pallas_call(
        _ffn_kernel, grid=(n // tm,),
        in_specs=[tok, vec, mod.spec(4), mod.spec(3), mod.spec(5), vec,
                  _const_spec((D_MODEL, D_FF)), _const_spec((D_MODEL, D_FF)), _const_spec((D_FF, D_MODEL))],
        out_specs=tok, out_shape=jax.ShapeDtypeStruct((n, D_MODEL), F32),
        compiler_params=_params(1), name="ffn",
    )(x1, nw2, mod.rows, mod.rows, mod.rows, nf, w_gate, w_up, w_down)


def _slopes():
    return 2.0 ** (-8.0 * jnp.arange(1, N_HEADS + 1, dtype=F32) / N_HEADS)


def _slope_lanes(slopes):
    parts = _split3(slopes)
    cols = jnp.stack([parts[0], parts[0], parts[1], parts[1], parts[2], parts[2]], axis=1)
    return jnp.pad(cols, ((0, 0), (0, LANES - 6)))


def _pos_lanes(pos_hi, pos_lo):
    cols = jnp.stack([pos_hi, pos_lo] * 3, axis=1).astype(F32)
    return jnp.pad(cols, ((0, 0), (0, LANES - 6))).astype(BF16)


def _token_consts(t):
    pos = jnp.arange(t, dtype=jnp.int32)
    onehot = (pos[:, None] // SEL_BLOCK == jnp.arange(HEAD_DIM, dtype=jnp.int32)[None, :]).astype(F32)
    clo = jnp.concatenate([jnp.zeros((t, HEAD_DIM), F32), onehot], axis=1)
    chi = _pos_lanes((pos // SEL_BLOCK) * SEL_BLOCK, pos % SEL_BLOCK)
    return clo, chi


def _cmp_consts(ncb):
    ci = jnp.arange(ncb, dtype=jnp.int32) * CMP_STRIDE
    return _pos_lanes((ci // SEL_BLOCK) * SEL_BLOCK, ci % SEL_BLOCK)


def _ones_lane():
    return (jnp.arange(LANES) == HEAD_DIM).astype(F32).reshape(1, LANES)


def _imp_matrix(nc, ns_pad):
    c = jnp.arange(nc, dtype=jnp.int32)[:, None]
    s = jnp.arange(ns_pad, dtype=jnp.int32)[None, :]
    per = SEL_BLOCK // CMP_STRIDE
    return ((c // per == s) | ((c % per == per - 1) & (c // per == s - 1))).astype(BF16)


def _prep_weights(w_in, w_phi1, w_phi2, pe_cmp):
    wb = w_in.astype(BF16)
    w_conv_in = wb[:, _C_CONV:_C_QKV]
    w_qkv = jnp.pad(wb[:, _C_QKV:_C_MERGE], ((0, 0), (0, _QKV_COLS_PAD - _QKV_COLS)))
    w_merge = wb[:, _C_MERGE:]
    half = CMP_STRIDE * HEAD_DIM
    w1cat = jnp.concatenate([w_phi1[:, :half], w_phi1[:, half:]], axis=2).astype(BF16)
    w2pad = jnp.pad(w_phi2, ((0, 0), (0, 0), (0, LANES - HEAD_DIM))).astype(BF16)
    pe2 = pe_cmp.reshape(2, 2, half)
    return w_conv_in, w_qkv, w_merge, w1cat, w2pad, pe2


def _prompt_layer(x, mod_p, wts):
    (nw1, nw2, nf, w_conv_in, w_qkv, w_merge, w1cat, w2pad, pe2, w_conv, b_conv, w_oc, w_on, w_o,
     w_gate, w_up, w_down, slopes) = wts
    nb, t, _ = x.shape
    tm = min(512, t)
    x2d = x.reshape(nb * t, D_MODEL)
    mod = _Mod(mod_p, False, t, tm)
    a, tail = _conv_path(x2d, mod, nw1, w_conv_in, w_conv, b_conv, tm=tm, seq_len=t)
    clo, chi = _token_consts(t)
    cv = _ones_lane()
    qpad, kvc, kvs, kvw, gts, ksa, vsa, kwa, vwa = _qkv_path(x2d, mod, nw1, w_qkv, tm=tm, seq_len=t, consts=(clo, chi, cv))
    npages = t // PAGE
    pt = jnp.arange(nb * npages, dtype=jnp.int32).reshape(nb, npages)
    nc = t // CMP_STRIDE
    ns = t // SEL_BLOCK
    kca, vca = _compress(kvc.reshape(nb * npages, PAGE * ROW_TILES, LANES), pt, pe2, w1cat, w2pad, _cmp_consts(nc), cv,
                         pps=min(8, npages))
    gates = gts[:, :3 * N_HEADS].reshape(nb, t, N_KV, 3 * GROUP).transpose(0, 2, 1, 3)
    qc = jnp.repeat(_slope_lanes(slopes).reshape(N_KV, GROUP, 1, LANES), Q_BLOCK, axis=2).reshape(N_KV, GROUP * Q_BLOCK, LANES)
    mt = _imp_matrix(nc, ns).T
    o = _attn_prompt(qpad.reshape(nb, t, N_HEADS * LANES), gates, qc, mt, kca, vca, ksa, vsa, kwa, vwa)
    x1 = _outproj(x2d, a, o.reshape(nb * t, D_MODEL), mod, nw1, w_merge, w_oc, w_on, w_o, tm=tm)
    y = _ffn(x1, mod, nw2, nf, w_gate, w_up, w_down, tm=tm)
    keep = min(WINDOW, t)
    state = (kvc.reshape(nb, t, 2, N_KV, HEAD_DIM), kvs.reshape(nb, t, 2, N_KV, HEAD_DIM),
             kvw.reshape(nb, t, 2 * KV_W)[:, t - keep:].reshape(nb, keep, 2, N_KV, HEAD_DIM),
             tail[:, 8 - 2:, :])
    return y.reshape(nb, t, D_MODEL), state


def _sample_layer(x, mod_s, wts, cache_cmp, cache_sel, cache_win, state_conv, page_table):
    (nw1, nw2, nf, w_conv_in, w_qkv, w_merge, w1cat, w2pad, pe2, w_conv, b_conv, w_oc, w_on, w_o,
     w_gate, w_up, w_down, slopes) = wts
    nb, s, _ = x.shape
    n = nb * s
    x2d = x.reshape(n, D_MODEL)
    mod = _Mod(mod_s, True, s, n)
    tpos = jnp.arange(s)
    p1 = jnp.broadcast_to(state_conv[:, 1:2, :], (nb, s, D_MODEL)).reshape(n, D_MODEL)
    p2 = state_conv[:, jnp.minimum(tpos, 1), :].reshape(n, D_MODEL)
    a, u = _conv_path(x2d, mod, nw1, w_conv_in, w_conv, b_conv, tm=n, seq_len=s, prev=(p1, p2))
    qpad, kvc, kvs, kvw, gts = _qkv_path(x2d, mod, nw1, w_qkv, tm=n, seq_len=s)

    npages = page_table.shape[1]
    past = npages * PAGE
    cv = _ones_lane()
    nc = past // CMP_STRIDE
    kca, vca = _compress(cache_cmp.reshape(-1, PAGE * ROW_TILES, LANES), page_table, pe2, w1cat, w2pad, _cmp_consts(nc), cv,
                         pps=min(16, npages))

    qh = qpad.reshape(nb, s, N_KV, GROUP, LANES).transpose(0, 2, 3, 1, 4)
    qh = jnp.pad(qh, ((0, 0), (0, 0), (0, 0), (0, Q_PAD - s), (0, 0)))
    sl = jnp.broadcast_to(_slope_lanes(slopes).reshape(1, N_KV, GROUP, 1, LANES), qh.shape)
    qa = jnp.concatenate([qh, sl], axis=-1).reshape(nb, N_KV, ROWS_G, AUG_W)
    eye = jnp.eye(N_KV, dtype=BF16)
    qbd = (qh[..., None, :HEAD_DIM] * eye[None, :, None, None, :, None]).reshape(nb, ROWS_S, KV_W)
    gates = gts[:, :3 * N_HEADS].reshape(nb, s, N_KV, GROUP, 3).transpose(0, 2, 3, 1, 4)
    gates = jnp.pad(gates, ((0, 0), (0, 0), (0, 0), (0, Q_PAD - s), (0, 0))).reshape(nb, ROWS_S, 3)
    slope_rows = jnp.repeat(slopes, Q_PAD).reshape(ROWS_S, 1)
    ns = past // SEL_BLOCK + 1
    nsp = -(-ns // LANES) * LANES
    ms = _imp_matrix(nc, nsp)
    tok = jnp.arange(past, dtype=jnp.int32)
    emat = (jnp.arange(nsp, dtype=jnp.int32)[:, None] == tok[None, :] // SEL_BLOCK).astype(BF16)
    pad_rows = lambda r: jnp.pad(r.reshape(nb, s, 2 * KV_W), ((0, 0), (0, LANES - s), (0, 0)))
    wbuf = cache_win.shape[1]
    osw, ocg = _attn_sample(page_table, cache_sel.reshape(-1, PAGE, 2 * KV_W), qa, qbd, gates, slope_rows, kca, vca,
                            ms, emat, cache_win.reshape(nb, wbuf, 2 * KV_W), pad_rows(kvs), pad_rows(kvw),
                            pps=min(8, npages), n_new=s)
    osw = osw.reshape(nb, N_KV, GROUP, Q_PAD, N_KV, HEAD_DIM)
    o_sw = jnp.einsum('bgrqgd->bqgrd', osw)
    o_c = ocg.reshape(nb, N_KV, GROUP, Q_PAD, LANES)[..., :HEAD_DIM].transpose(0, 3, 1, 2, 4)
    o = (o_sw + o_c)[:, :s].reshape(n, N_HEADS * HEAD_DIM).astype(BF16)

    x1 = _outproj(x2d, a, o, mod, nw1, w_merge, w_oc, w_on, w_o, tm=n)
    y = _ffn(x1, mod, nw2, nf, w_gate, w_up, w_down, tm=n)
    kv5 = lambda r: r.reshape(nb, s, 2, N_KV, HEAD_DIM)
    win = jnp.concatenate([cache_win, kv5(kvw)], axis=1)[:, s:]
    state = (kv5(kvc), kv5(kvs), win, u.reshape(nb, s, D_MODEL)[:, s - 2:])
    return y.reshape(nb, s, D_MODEL), state


def kernel(x_prompt, x_sample, c_prompt, c_sample, cache_cmp, cache_sel, cache_win, state_conv, page_table,
           w_ada, b_ada, norm1, w_in, w_conv, b_conv, w_out_conv, pe_cmp, w_phi1, w_phi2, w_o_nsa, w_out,
           norm2, w_gate, w_up, w_down, norm_f):
    depth = w_ada.shape[0]
    assert depth == 1, "single-layer trunk"
    nbp, nbs = c_prompt.shape[0], c_sample.shape[0]
    slopes = _slopes()
    l = 0
    c_all = jnp.concatenate([c_prompt, c_sample], axis=0)
    c_all = jnp.pad(c_all, ((0, -c_all.shape[0] % 8), (0, 0)))
    mod = _ada(c_all, w_ada[l], b_ada[l])
    w_conv_in, w_qkv, w_merge, w1cat, w2pad, pe2 = _prep_weights(w_in[l], w_phi1[l], w_phi2[l], pe_cmp[l])
    row = lambda v: v.reshape(1, -1)
    wts = (row(norm1[l]), row(norm2[l]), row(norm_f), w_conv_in, w_qkv, w_merge, w1cat, w2pad, pe2,
           w_conv[l], row(b_conv[l]), w_out_conv[l].astype(BF16), w_o_nsa[l].astype(BF16), w_out[l].astype(BF16),
           w_gate[l].astype(BF16), w_up[l].astype(BF16), w_down[l].astype(BF16), slopes)
    yp, st_p = _prompt_layer(x_prompt, mod[:nbp], wts)
    ys, st_s = _sample_layer(x_sample, mod[nbp:nbp + nbs], wts, cache_cmp[l], cache_sel[l], cache_win[l],
                             state_conv[l], page_table)
    return (yp, ys, st_p[0][None], st_p[1][None], st_p[2][None], st_p[3][None],
            st_s[0][None], st_s[1][None], st_s[2][None], st_s[3][None])
```

```python
import functools

import jax
import jax.numpy as jnp
import numpy as np
from jax import lax
from jax.experimental import pallas as pl
from jax.experimental.pallas import tpu as pltpu

F32 = jnp.float32
BF16 = jnp.bfloat16

D_MODEL = 1024
N_HEADS = 16
HEAD_DIM = 64
N_KV = 4
GROUP = N_HEADS // N_KV
KV_W = N_KV * HEAD_DIM
CMP_BLOCK = 32
CMP_STRIDE = 16
SEL_BLOCK = 64
N_SEL = 16
WINDOW = 512
D_PHI = 2 * HEAD_DIM
Q_BLOCK = 128
PAGE = 128
D_FF = ((8 * D_MODEL // 3 + 255) // 256) * 256
EPS = 1e-6
NEG = -1e30
FORCE_BONUS = 1e3

LANES = 128
AUG_W = 2 * LANES
HALVES_PER_PAGE = PAGE // CMP_STRIDE
ROW_TILES = 2 * KV_W // LANES
VMEM_LIMIT = 56 * 1024 * 1024
KEY_TILE = 2 * LANES
QC_ROWS = 16
SEL_TILES_PER_ITER = 2

_C_CONV = 0
_C_QKV = 3 * D_MODEL
_C_GATE = _C_QKV + N_HEADS * HEAD_DIM + 6 * KV_W
_C_MERGE = _C_GATE + 3 * N_HEADS
_QKV_COLS = _C_MERGE - _C_QKV
_QKV_COLS_PAD = -(-_QKV_COLS // LANES) * LANES


def _dot(a, b):
    return jnp.dot(a, b, preferred_element_type=F32)


def _dot_nt(a, b):
    return lax.dot_general(a, b, (((1,), (1,)), ((), ())), preferred_element_type=F32)


def _params(n_axes):
    return pltpu.CompilerParams(dimension_semantics=("arbitrary",) * n_axes, vmem_limit_bytes=VMEM_LIMIT)


def _const_spec(shape):
    return pl.BlockSpec(shape, lambda *_: (0,) * len(shape))


def _mod_norm(x, nw, sc, sh):
    inv = lax.rsqrt(jnp.mean(x * x, axis=-1, keepdims=True) + EPS)
    return (x * inv) * nw * (1.0 + sc) + sh


def _split3(x):
    a = x.astype(BF16)
    r = x - a.astype(F32)
    b = r.astype(BF16)
    c = (r - b.astype(F32)).astype(BF16)
    return a, b, c


def _ada_kernel(c_ref, w_ref, b_ref, o_ref):
    c = c_ref[...]
    s = c * jax.nn.sigmoid(c)
    o_ref[...] = jnp.dot(s, w_ref[...], preferred_element_type=F32, precision=lax.Precision.HIGHEST) + b_ref[...]


def _ada(c, w_ada, b_ada):
    n = c.shape[0]
    tn = 1536
    return pl.pallas_call(
        _ada_kernel,
        grid=(6 * D_MODEL // tn,),
        in_specs=[_const_spec((n, D_MODEL)),
                  pl.BlockSpec((D_MODEL, tn), lambda j: (0, j)),
                  pl.BlockSpec((1, tn), lambda j: (0, j))],
        out_specs=pl.BlockSpec((n, tn), lambda j: (0, j)),
        out_shape=jax.ShapeDtypeStruct((n, 6 * D_MODEL), F32),
        compiler_params=_params(1),
        name="ada",
    )(c, w_ada, b_ada.reshape(1, -1))


class _Mod:
    def __init__(self, mod, per_token, seq_len, tm):
        self.per_token = per_token
        if per_token:
            self.rows = jnp.repeat(mod, seq_len, axis=0)
        else:
            self.rows = mod.reshape(mod.shape[0], 1, 6 * D_MODEL)
        self.tiles_per_seq = None if per_token else seq_len // tm
        self.tm = tm

    def spec(self, k):
        if self.per_token:
            return pl.BlockSpec((self.tm, D_MODEL), lambda i: (i, k))
        tps = self.tiles_per_seq
        return pl.BlockSpec((None, 1, D_MODEL), lambda i: (i // tps, 0, k))


def _conv_kernel(*refs, carry_rows, seq_len):
    if carry_rows:
        (x_ref, nw_ref, sc_ref, sh_ref, w_ref, wc_ref, bc_ref, a_ref, tail_ref, carry_ref) = refs
    else:
        (x_ref, nw_ref, sc_ref, sh_ref, w_ref, wc_ref, bc_ref, p1_ref, p2_ref, a_ref, tail_ref) = refs
    tm = x_ref.shape[0]
    h = _mod_norm(x_ref[...], nw_ref[...], sc_ref[...], sh_ref[...]).astype(BF16)
    z = _dot(h, w_ref[...])
    bg = z[:, 0:D_MODEL]
    u = z[:, D_MODEL:2 * D_MODEL] * z[:, 2 * D_MODEL:3 * D_MODEL]
    row = lax.broadcasted_iota(jnp.int32, (tm, 1), 0)
    u1 = pltpu.roll(u, 1, 0)
    u2 = pltpu.roll(u, 2, 0)
    if carry_rows:
        @pl.when(pl.program_id(0) % carry_rows == 0)
        def _():
            carry_ref[...] = jnp.zeros_like(carry_ref)
        c0 = carry_ref[0:1, :]
        c1 = carry_ref[1:2, :]
        u1 = jnp.where(row == 0, c1, u1)
        u2 = jnp.where(row == 0, c0, jnp.where(row == 1, c1, u2))
        carry_ref[0:2, :] = u[tm - 2:tm, :]
        tail_ref[...] = u[tm - 8:tm, :]
    else:
        pos = lax.rem(row, seq_len)
        u1 = jnp.where(pos >= 1, u1, p1_ref[...])
        u2 = jnp.where(pos >= 2, u2, p2_ref[...])
        tail_ref[...] = u
    v = bc_ref[...] + wc_ref[0:1, :] * u2 + wc_ref[1:2, :] * u1 + wc_ref[2:3, :] * u
    a_ref[...] = (bg * v).astype(BF16)


def _conv_path(x2d, mod, nw, w_conv_in, w_conv, b_conv, *, tm, seq_len, prev=None):
    n = x2d.shape[0]
    tok = pl.BlockSpec((tm, D_MODEL), lambda i: (i, 0))
    in_specs = [tok, _const_spec((1, D_MODEL)), mod.spec(1), mod.spec(0),
                _const_spec((D_MODEL, 3 * D_MODEL)), _const_spec((3, D_MODEL)), _const_spec((1, D_MODEL))]
    args = [x2d, nw, mod.rows, mod.rows, w_conv_in, w_conv, b_conv]
    if prev is None:
        tps = seq_len // tm
        out_specs = [tok, pl.BlockSpec((None, 8, D_MODEL), lambda i: (i // tps, 0, 0))]
        out_shape = [jax.ShapeDtypeStruct((n, D_MODEL), BF16), jax.ShapeDtypeStruct((n // seq_len, 8, D_MODEL), F32)]
        scratch = [pltpu.VMEM((8, D_MODEL), F32)]
        kern = functools.partial(_conv_kernel, carry_rows=tps, seq_len=seq_len)
    else:
        in_specs += [tok, tok]
        args += list(prev)
        out_specs = [tok, tok]
        out_shape = [jax.ShapeDtypeStruct((n, D_MODEL), BF16), jax.ShapeDtypeStruct((n, D_MODEL), F32)]
        scratch = []
        kern = functools.partial(_conv_kernel, carry_rows=0, seq_len=seq_len)
    return pl.pallas_call(kern, grid=(n // tm,), in_specs=in_specs, out_specs=out_specs, out_shape=out_shape,
                          scratch_shapes=scratch, compiler_params=_params(1), name="conv_path")(*args)


def _qkv_kernel(*refs, aug):
    if aug:
        (x_ref, nw_ref, sc_ref, sh_ref, w_ref, clo_ref, chi_ref,
         q_ref, kvc_ref, kvs_ref, kvw_ref, g_ref, ksa_ref, vsa_ref, kwa_ref, vwa_ref) = refs
    else:
        (x_ref, nw_ref, sc_ref, sh_ref, w_ref, q_ref, kvc_ref, kvs_ref, kvw_ref, g_ref) = refs
    tm = x_ref.shape[0]
    h = _mod_norm(x_ref[...], nw_ref[...], sc_ref[...], sh_ref[...]).astype(BF16)
    z = _dot(h, w_ref[...])
    low = lax.broadcasted_iota(jnp.int32, (tm, LANES), 1) < HEAD_DIM
    nq = N_HEADS * HEAD_DIM
    for c in range(N_HEADS // 2):
        t = z[:, c * LANES:(c + 1) * LANES] * (HEAD_DIM ** -0.5)
        if aug:
            tt = t.T.astype(BF16)
            q_ref[2 * c] = tt[0:HEAD_DIM]
            q_ref[2 * c + 1] = tt[HEAD_DIM:2 * HEAD_DIM]
        else:
            q_ref[:, (2 * c) * LANES:(2 * c + 1) * LANES] = jnp.where(low, t, 0.0).astype(BF16)
            q_ref[:, (2 * c + 1) * LANES:(2 * c + 2) * LANES] = jnp.where(low, pltpu.roll(t, HEAD_DIM, 1), 0.0).astype(BF16)
    kvc_ref[...] = z[:, nq:nq + 2 * KV_W]
    kvs_ref[...] = z[:, nq + 2 * KV_W:nq + 4 * KV_W]
    kvw_ref[...] = z[:, nq + 4 * KV_W:nq + 6 * KV_W]
    g_ref[...] = jax.nn.sigmoid(z[:, nq + 6 * KV_W:nq + 6 * KV_W + LANES])
    if aug:
        chi = chi_ref[...]
        ones_row = (lax.broadcasted_iota(jnp.int32, (LANES - HEAD_DIM, KEY_TILE), 0) == 0).astype(BF16)
        for br, (ka_ref, va_ref) in enumerate(((ksa_ref, vsa_ref), (kwa_ref, vwa_ref))):
            kbase = nq + 2 * KV_W * (br + 1)
            clo = clo_ref[...] if br == 0 else 0.0
            for g in range(N_KV):
                kt = z[:, kbase + (g // 2) * LANES:kbase + (g // 2 + 1) * LANES]
                if g % 2:
                    kt = pltpu.roll(kt, HEAD_DIM, 1)
                ka_ref[g, :, 0:LANES] = jnp.where(low, kt, clo).astype(BF16)
                ka_ref[g, :, LANES:AUG_W] = chi
            for c in range(N_KV // 2):
                vt = z[:, kbase + KV_W + c * LANES:kbase + KV_W + (c + 1) * LANES].T.astype(BF16)
                for gg in range(2):
                    for j in range(tm // KEY_TILE):
                        va_ref[2 * c + gg, j, 0:HEAD_DIM, :] = vt[gg * HEAD_DIM:(gg + 1) * HEAD_DIM, j * KEY_TILE:(j + 1) * KEY_TILE]
                        va_ref[2 * c + gg, j, HEAD_DIM:LANES, :] = ones_row


def _qkv_path(x2d, mod, nw, w_qkv, *, tm, seq_len, consts=None):
    n = x2d.shape[0]
    aug = consts is not None
    tok = lambda w: pl.BlockSpec((tm, w), lambda i: (i, 0))
    in_specs = [tok(D_MODEL), _const_spec((1, D_MODEL)), mod.spec(1), mod.spec(0),
                _const_spec((D_MODEL, _QKV_COLS_PAD))]
    args = [x2d, nw, mod.rows, mod.rows, w_qkv]
    out_specs = [tok(N_HEADS * LANES), tok(2 * KV_W), tok(2 * KV_W), tok(2 * KV_W), tok(LANES)]
    out_shape = [jax.ShapeDtypeStruct((n, N_HEADS * LANES), BF16)] + \
                [jax.ShapeDtypeStruct((n, 2 * KV_W), F32)] * 3 + [jax.ShapeDtypeStruct((n, LANES), F32)]
    if aug:
        tps = seq_len // tm
        nb = n // seq_len
        pos = lambda w: pl.BlockSpec((tm, w), lambda i: (i % tps, 0))
        in_specs += [pos(LANES), pos(LANES)]
        args += list(consts)
        out_specs[0] = pl.BlockSpec((None, N_HEADS, HEAD_DIM, tm), lambda i: (i // tps, 0, 0, i % tps))
        out_shape[0] = jax.ShapeDtypeStruct((nb, N_HEADS, HEAD_DIM, seq_len), BF16)
        ka = pl.BlockSpec((None, N_KV, tm, AUG_W), lambda i: (i // tps, 0, i % tps, 0))
        va = pl.BlockSpec((None, N_KV, tm // KEY_TILE, LANES, KEY_TILE), lambda i: (i // tps, 0, i % tps, 0, 0))
        out_specs += [ka, va, ka, va]
        ka_s = jax.ShapeDtypeStruct((nb, N_KV, seq_len, AUG_W), BF16)
        va_s = jax.ShapeDtypeStruct((nb, N_KV, seq_len // KEY_TILE, LANES, KEY_TILE), BF16)
        out_shape += [ka_s, va_s, ka_s, va_s]
    return pl.pallas_call(functools.partial(_qkv_kernel, aug=aug), grid=(n // tm,), in_specs=in_specs,
                          out_specs=out_specs, out_shape=out_shape, compiler_params=_params(1), name="qkv_path")(*args)


def _gather_half(ref, e, nrows):
    c, odd = e // 2, e % 2
    low = lax.broadcasted_iota(jnp.int32, (nrows, LANES), 1) < HEAD_DIM
    tiles = []
    for qq in range(CMP_STRIDE // 2):
        ra, rb = 2 * qq * ROW_TILES + c, (2 * qq + 1) * ROW_TILES + c
        if nrows == 1:
            a = ref[ra:ra + 1, :]
            b = ref[rb:rb + 1, :]
        else:
            a = ref[pl.ds(ra, nrows, stride=CMP_STRIDE * ROW_TILES), :]
            b = ref[pl.ds(rb, nrows, stride=CMP_STRIDE * ROW_TILES), :]
        if odd:
            tiles.append(jnp.where(low, pltpu.roll(a, HEAD_DIM, 1), b))
        else:
            tiles.append(jnp.where(low, a, pltpu.roll(b, HEAD_DIM, 1)))
    return jnp.concatenate(tiles, axis=1)


def _compress_kernel(pt_ref, *refs, pps, transpose_v):
    pages = refs[:pps]
    nxt_ref, pe_ref, w1_ref, w2_ref, chi_ref, cv_ref, kc_ref, vc_ref = refs[pps:]
    t = pl.program_id(1)
    last = t == pl.num_programs(1) - 1
    nhb = pps * HALVES_PER_PAGE
    n = N_KV * nhb
    row = lax.broadcasted_iota(jnp.int32, (n, 1), 0)
    chi = chi_ref[...]
    for kv in range(2):
        parts = []
        for g in range(N_KV):
            e = kv * N_KV + g
            parts += [_gather_half(p, e, HALVES_PER_PAGE) for p in pages]
        look = [_gather_half(nxt_ref, kv * N_KV + g, 1) for g in range(N_KV)]
        extra = jnp.concatenate([pe_ref[kv], jnp.zeros((2, CMP_STRIDE * HEAD_DIM), F32)] + look, axis=0)
        xmat = jnp.concatenate(parts + [extra], axis=0).astype(BF16)
        hab = _dot(xmat, w1_ref[kv])
        ha = hab[0:n, 0:D_PHI]
        hb = hab[0:n, D_PHI:2 * D_PHI]
        pbias = hab[n:n + 1, 0:D_PHI] + hab[n + 1:n + 2, D_PHI:2 * D_PHI]
        hbn = pltpu.roll(hb, n - 1, 0)
        for g in range(N_KV):
            la = jnp.where(last, 0.0, hab[n + 4 + g:n + 5 + g, D_PHI:2 * D_PHI])
            hbn = jnp.where(row == g * nhb + nhb - 1, la, hbn)
        act = jax.nn.gelu(ha + hbn + pbias).astype(BF16)
        out = _dot(act, w2_ref[kv])
        for g in range(N_KV):
            blk = out[g * nhb:(g + 1) * nhb, :]
            if kv == 0:
                kc_ref[g, :, 0:LANES] = blk.astype(BF16)
                kc_ref[g, :, LANES:AUG_W] = chi
            elif transpose_v:
                vc_ref[g, :, :] = (blk + cv_ref[...]).T.astype(BF16)
            else:
                vc_ref[g, :, :] = (blk + cv_ref[...]).astype(BF16)


def _compress(pages, page_table, pe2, w1cat, w2pad, chi_c, cv, *, pps, transpose_v):
    nb, npages = page_table.shape
    nchunks = npages // pps
    nhb = pps * HALVES_PER_PAGE
    ncb = npages * HALVES_PER_PAGE

    def page_spec(k):
        return pl.BlockSpec((None, PAGE * ROW_TILES, LANES), lambda b, t, pt: (pt[b, t * pps + k], 0, 0))

    nxt_spec = pl.BlockSpec((None, CMP_STRIDE * ROW_TILES, LANES),
                            lambda b, t, pt: (pt[b, jnp.minimum((t + 1) * pps, npages - 1)], 0, 0))
    cs = lambda shape: pl.BlockSpec(shape, lambda b, t, pt: (0,) * len(shape))
    in_specs = [page_spec(k) for k in range(pps)] + [
        nxt_spec, cs((2, 2, CMP_STRIDE * HEAD_DIM)), cs((2, CMP_STRIDE * HEAD_DIM, 2 * D_PHI)),
        cs((2, D_PHI, LANES)), pl.BlockSpec((nhb, LANES), lambda b, t, pt: (t, 0)), cs((1, LANES))]
    if transpose_v:
        v_spec = pl.BlockSpec((None, N_KV, LANES, nhb), lambda b, t, pt: (b, 0, 0, t))
        v_shape = jax.ShapeDtypeStruct((nb, N_KV, LANES, ncb), BF16)
    else:
        v_spec = pl.BlockSpec((None, N_KV, nhb, LANES), lambda b, t, pt: (b, 0, t, 0))
        v_shape = jax.ShapeDtypeStruct((nb, N_KV, ncb, LANES), BF16)
    out_specs = [pl.BlockSpec((None, N_KV, nhb, AUG_W), lambda b, t, pt: (b, 0, t, 0)), v_spec]
    out_shape = [jax.ShapeDtypeStruct((nb, N_KV, ncb, AUG_W), BF16), v_shape]
    gs = pltpu.PrefetchScalarGridSpec(num_scalar_prefetch=1, grid=(nb, nchunks), in_specs=in_specs, out_specs=out_specs)
    return pl.pallas_call(functools.partial(_compress_kernel, pps=pps, transpose_v=transpose_v), grid_spec=gs,
                          out_shape=out_shape,
                          compiler_params=_params(2), name="compress")(
        page_table, *([pages] * (pps + 1)), pe2, w1cat, w2pad, chi_c, cv)


def _attn_prompt_t_kernel(q_ref, g_ref, qc_ref, mt_ref, kc_ref, vc_ref, ks_ref, vs_ref, kw_ref, vw_ref, o_ref,
                          qa_ref, qw_ref, m_ref, acc_ref):
    nc = kc_ref.shape[0]
    ns = mt_ref.shape[0]
    kt = vs_ref.shape[2]
    ncol = GROUP * Q_BLOCK
    qb = pl.program_id(2)
    q0 = qb * Q_BLOCK
    qpos = q0 + (lax.broadcasted_iota(jnp.int32, (1, ncol), 1) & (Q_BLOCK - 1))

    for ref in (qw_ref, qa_ref):
        for r in range(GROUP):
            ref[0:HEAD_DIM, r * Q_BLOCK:(r + 1) * Q_BLOCK] = q_ref[r]
        ref[HEAD_DIM:2 * HEAD_DIM, :] = jnp.zeros((HEAD_DIM, ncol), BF16)
        ref[2 * HEAD_DIM:2 * HEAD_DIM + QC_ROWS, :] = qc_ref[...]
        ref[2 * HEAD_DIM + QC_ROWS:AUG_W, :] = jnp.zeros((AUG_W - 2 * HEAD_DIM - QC_ROWS, ncol), BF16)

    s = _dot(kc_ref[...], qw_ref[...])
    ci = lax.broadcasted_iota(jnp.int32, (nc, 1), 0)
    c_valid = ci * CMP_STRIDE + (CMP_BLOCK - 1) <= qpos
    s = jnp.where(c_valid, s, NEG)
    e = jnp.exp(s - jnp.max(s, axis=0, keepdims=True))
    p = jnp.where(c_valid, e / jnp.sum(e, axis=0, keepdims=True), 0.0)
    o_c = _dot(vc_ref[...], p.astype(BF16))[0:HEAD_DIM]

    psum = p[:, 0:Q_BLOCK]
    for r in range(1, GROUP):
        psum = psum + p[:, r * Q_BLOCK:(r + 1) * Q_BLOCK]
    mt = mt_ref[...]
    imp = sum(_dot(mt, part) for part in _split3(psum))
    si = lax.broadcasted_iota(jnp.int32, (ns, Q_BLOCK), 0)
    qpos_t = q0 + lax.broadcasted_iota(jnp.int32, (ns, Q_BLOCK), 1)
    cur = lax.shift_right_logical(qpos_t, 6)
    s_valid = si * SEL_BLOCK <= qpos_t
    forced = (si == 0) | (si == cur) | (si == cur - 1)
    score = jnp.where(s_valid, imp + jnp.where(forced, FORCE_BONUS, 0.0), NEG)
    rank = jnp.zeros((ns, Q_BLOCK), jnp.int32)
    for sp in range(ns):
        other = score[sp:sp + 1, :]
        beats = (other > score) | ((other == score) & (si > sp))
        rank = rank + beats.astype(jnp.int32)
    bias_t = jnp.where(rank < min(N_SEL, ns), 0.0, NEG).astype(BF16)
    qa_ref[HEAD_DIM:HEAD_DIM + ns, :] = jnp.concatenate([bias_t] * GROUP, axis=1)

    t_hi = lax.div(q0, kt)
    key_iota = lax.broadcasted_iota(jnp.int32, (kt, 1), 0)

    n_back = WINDOW // kt
    s_w, v_w = [], []
    for j in range(n_back + 1):
        tw = t_hi - n_back + j
        tc = jnp.maximum(tw, 0)
        kpos = tw * kt + key_iota
        valid = kpos >= 0
        if j == 0:
            valid = valid & (qpos - kpos < WINDOW)
        if j == n_back:
            valid = valid & (kpos <= qpos)
        s = _dot(kw_ref[pl.ds(pl.multiple_of(tc * kt, kt), kt), :], qw_ref[...])
        s_w.append(jnp.where(valid, s, NEG))
        v_w.append(vw_ref[tc])
    m_w = functools.reduce(jnp.maximum, [jnp.max(s, axis=0, keepdims=True) for s in s_w])
    acc_w = sum(_dot(v, jnp.exp(s - m_w).astype(BF16)) for s, v in zip(s_w, v_w))
    o_w = acc_w[0:HEAD_DIM] / acc_w[HEAD_DIM:HEAD_DIM + 1]

    m_ref[...] = jnp.full(m_ref.shape, NEG, F32)
    acc_ref[...] = jnp.zeros(acc_ref.shape, F32)

    def sel_tiles(t0, n, causal_last):
        qa = qa_ref[...]
        ss = []
        for i in range(n):
            k0 = pl.multiple_of((t0 + i) * kt, kt)
            s = _dot(ks_ref[pl.ds(k0, kt), :], qa)
            if causal_last and i == n - 1:
                s = jnp.where(k0 + key_iota <= qpos, s, NEG)
            ss.append(s)
        m_old = m_ref[...]
        m_new = functools.reduce(jnp.maximum, [m_old] + [jnp.max(s, axis=0, keepdims=True) for s in ss])
        pv = sum(_dot(vs_ref[t0 + i], jnp.exp(s - m_new).astype(BF16)) for i, s in enumerate(ss))
        acc_ref[...] = jnp.exp(m_old - m_new) * acc_ref[...] + pv
        m_ref[...] = m_new

    n_group = lax.div(t_hi, SEL_TILES_PER_ITER)

    def body(i, carry):
        sel_tiles(i * SEL_TILES_PER_ITER, SEL_TILES_PER_ITER, False)
        return carry

    lax.fori_loop(0, n_group, body, 0)
    rem = t_hi - n_group * SEL_TILES_PER_ITER
    for left in range(SEL_TILES_PER_ITER):
        @pl.when(rem == left)
        def _():
            sel_tiles(t_hi - left, left + 1, True)

    acc = acc_ref[...]
    o_s = acc[0:HEAD_DIM] / acc[HEAD_DIM:HEAD_DIM + 1]

    gts = g_ref[...]
    o = gts[0:1] * o_c + gts[1:2] * o_s + gts[2:3] * o_w
    for c in range(GROUP // 2):
        pair = jnp.concatenate([o[:, (2 * c) * Q_BLOCK:(2 * c + 1) * Q_BLOCK],
                                o[:, (2 * c + 1) * Q_BLOCK:(2 * c + 2) * Q_BLOCK]], axis=0)
        o_ref[:, c * LANES:(c + 1) * LANES] = pair.T.astype(BF16)


def _attn_prompt_t(qt, gates_t, qc_t, mt, kc, vct, ks, vst, kw, vwt):
    nb, t = qt.shape[0], qt.shape[3]
    nc = kc.shape[2]
    ns = mt.shape[0]
    ntile, kt = vst.shape[2], vst.shape[4]
    ncol = GROUP * Q_BLOCK
    per_bg = lambda *shape: pl.BlockSpec((None, None) + shape, lambda b, g, i: (b, g) + (0,) * len(shape))
    in_specs = [pl.BlockSpec((None, GROUP, HEAD_DIM, Q_BLOCK), lambda b, g, i: (b, g, 0, i)),
                pl.BlockSpec((None, None, None, 3, ncol), lambda b, g, i: (b, g, i, 0, 0)),
                pl.BlockSpec((None, QC_ROWS, ncol), lambda b, g, i: (g, 0, 0)),
                pl.BlockSpec((ns, nc), lambda b, g, i: (0, 0)),
                per_bg(nc, AUG_W), per_bg(LANES, nc), per_bg(t, AUG_W), per_bg(ntile, LANES, kt),
                per_bg(t, AUG_W), per_bg(ntile, LANES, kt)]
    out_spec = pl.BlockSpec((None, Q_BLOCK, GROUP * HEAD_DIM), lambda b, g, i: (b, i, g))
    scratch = [pltpu.VMEM((AUG_W, ncol), BF16), pltpu.VMEM((AUG_W, ncol), BF16),
               pltpu.VMEM((1, ncol), F32), pltpu.VMEM((LANES, ncol), F32)]
    return pl.pallas_call(_attn_prompt_t_kernel, grid=(nb, N_KV, t // Q_BLOCK),
                          in_specs=in_specs, out_specs=out_spec,
                          out_shape=jax.ShapeDtypeStruct((nb, t, N_HEADS * HEAD_DIM), BF16),
                          scratch_shapes=scratch, compiler_params=_params(3), name="attn_prompt")(
        qt, gates_t, qc_t, mt, kc, vct, ks, vst, kw, vwt)


Q_PAD = 8
ROWS_G = GROUP * Q_PAD
ROWS_S = N_KV * ROWS_G


def _attn_sample_kernel(pt_ref, *refs, pps, past, wbuf, n_new):
    pages = refs[:pps]
    (qa_ref, qbd_ref, g_ref, slope_ref, kc_ref, vc_ref, ms_ref, e_ref, win_ref, ksn_ref, kwn_ref,
     osw_ref, oc_ref, bias_ref, m_ref, acc_ref, ow_ref) = refs[pps:]
    t = pl.program_id(1)
    nchunks = pl.num_programs(1)
    nc = kc_ref.shape[1]
    ns = past // SEL_BLOCK + 1
    row = lax.broadcasted_iota(jnp.int32, (ROWS_S, 1), 0)
    qpos = past + (row & (Q_PAD - 1))
    qposf = qpos.astype(F32)
    slope = slope_ref[...]
    qbd = qbd_ref[...]

    @pl.when(t == 0)
    def _():
        nsp = ms_ref.shape[1]
        rg = lax.broadcasted_iota(jnp.int32, (ROWS_G, 1), 0)
        qpos_g = past + (rg & (Q_PAD - 1))
        r8 = lax.broadcasted_iota(jnp.int32, (Q_PAD, 1), 0)
        qpos8 = past + r8
        si = lax.broadcasted_iota(jnp.int32, (Q_PAD, nsp), 1)
        for g in range(N_KV):
            s = _dot_nt(qa_ref[g], kc_ref[g])
            ci = lax.broadcasted_iota(jnp.int32, (1, nc), 1)
            c_valid = ci * CMP_STRIDE + (CMP_BLOCK - 1) <= qpos_g
            s = jnp.where(c_valid, s, NEG)
            e = jnp.exp(s - jnp.max(s, axis=1, keepdims=True))
            p = jnp.where(c_valid, e / jnp.sum(e, axis=1, keepdims=True), 0.0)
            oc_ref[g * ROWS_G:(g + 1) * ROWS_G, :] = _dot(p.astype(BF16), vc_ref[g])
            psum = p[0:Q_PAD]
            for r in range(1, GROUP):
                psum = psum + p[r * Q_PAD:(r + 1) * Q_PAD]
            ms = ms_ref[...]
            imp = sum(_dot(part, ms) for part in _split3(psum))
            cur = lax.shift_right_logical(qpos8, 6)
            s_valid = (si * SEL_BLOCK <= qpos8) & (si < ns)
            forced = (si == 0) | (si == cur) | (si == cur - 1)
            score = jnp.where(s_valid, imp + jnp.where(forced, FORCE_BONUS, 0.0), NEG)
            rank = jnp.zeros((Q_PAD, nsp), jnp.int32)
            for sp in range(ns):
                other = score[:, sp:sp + 1]
                beats = (other > score) | ((other == score) & (si > sp))
                rank = rank + beats.astype(jnp.int32)
            bias_ref[g] = jnp.where((rank < min(N_SEL, ns)) & (si < ns), 0.0, NEG).astype(BF16)

        kw = win_ref[:, 0:KV_W].astype(BF16)
        vw = win_ref[:, KV_W:2 * KV_W].astype(BF16)
        kpos = past - wbuf + lax.broadcasted_iota(jnp.int32, (1, wbuf), 1)
        s1 = _dot_nt(qbd, kw) - slope * (qposf - kpos.astype(F32))
        s1 = jnp.where((kpos <= qpos) & (qpos - kpos < WINDOW) & (kpos >= 0), s1, NEG)
        kn = kwn_ref[:, 0:KV_W].astype(BF16)
        vn = kwn_ref[:, KV_W:2 * KV_W].astype(BF16)
        li = lax.broadcasted_iota(jnp.int32, (1, LANES), 1)
        kposn = past + li
        s2 = _dot_nt(qbd, kn) - slope * (qposf - kposn.astype(F32))
        s2 = jnp.where((li < n_new) & (kposn <= qpos) & (qpos - kposn < WINDOW), s2, NEG)
        m = jnp.maximum(jnp.max(s1, axis=1, keepdims=True), jnp.max(s2, axis=1, keepdims=True))
        p1 = jnp.exp(s1 - m)
        p2 = jnp.exp(s2 - m)
        den = jnp.sum(p1, axis=1, keepdims=True) + jnp.sum(p2, axis=1, keepdims=True)
        ow_ref[...] = (_dot((p1 / den).astype(BF16), vw) + _dot((p2 / den).astype(BF16), vn))
        m_ref[...] = jnp.full(m_ref.shape, NEG, F32)
        acc_ref[...] = jnp.zeros(acc_ref.shape, F32)

    def block_bias(emat):
        rows = []
        for g in range(N_KV):
            bt = _dot(bias_ref[g], emat)
            rows += [bt] * GROUP
        return jnp.concatenate(rows, axis=0)

    def update(s, v):
        m_old = m_ref[...]
        m_new = jnp.maximum(m_old, jnp.max(s, axis=1, keepdims=True))
        alpha = jnp.exp(m_old - m_new)
        p = jnp.exp(s - m_new)
        ones = jnp.sum(p, axis=1, keepdims=True)
        acc_ref[:, 0:KV_W] = alpha * acc_ref[:, 0:KV_W] + _dot(p.astype(BF16), v)
        acc_ref[:, KV_W:KV_W + LANES] = alpha * acc_ref[:, KV_W:KV_W + LANES] + ones
        m_ref[...] = m_new

    bias_chunk = block_bias(e_ref[...])
    li = lax.broadcasted_iota(jnp.int32, (1, PAGE), 1)
    for k in range(pps):
        pg = pages[k]
        kpos = (t * pps + k) * PAGE + li
        s = _dot_nt(qbd, pg[:, 0:KV_W].astype(BF16)) - slope * (qposf - kpos.astype(F32))
        update(s + bias_chunk[:, k * PAGE:(k + 1) * PAGE], pg[:, KV_W:2 * KV_W].astype(BF16))

    @pl.when(t == nchunks - 1)
    def _():
        nsp = ms_ref.shape[1]
        kposn = past + li
        e_new = (lax.broadcasted_iota(jnp.int32, (nsp, PAGE), 0) == past // SEL_BLOCK).astype(BF16)
        s = _dot_nt(qbd, ksn_ref[:, 0:KV_W].astype(BF16)) - slope * (qposf - kposn.astype(F32)) + block_bias(e_new)
        s = jnp.where((li < n_new) & (kposn <= qpos), s, NEG)
        update(s, ksn_ref[:, KV_W:2 * KV_W].astype(BF16))
        o_s = acc_ref[:, 0:KV_W] / acc_ref[:, KV_W:KV_W + 1]
        gts = g_ref[...]
        osw_ref[...] = gts[:, 1:2] * o_s + gts[:, 2:3] * ow_ref[...]
        oc_ref[...] = gts[:, 0:1] * oc_ref[...]


def _attn_sample(page_table, pages, qa, qbd, gates, slope_rows, kc, vc, ms, emat, win, ksn, kwn, *, pps, n_new):
    nb, npages = page_table.shape
    nchunks = npages // pps
    past = npages * PAGE
    wbuf = win.shape[1]
    nc = kc.shape[2]
    nsp = ms.shape[1]

    def page_spec(k):
        return pl.BlockSpec((None, PAGE, 2 * KV_W), lambda b, t, pt: (pt[b, t * pps + k], 0, 0))

    per_b = lambda *shape: pl.BlockSpec((None,) + shape, lambda b, t, pt: (b,) + (0,) * len(shape))
    cs = lambda *shape: pl.BlockSpec(shape, lambda b, t, pt: (0,) * len(shape))
    in_specs = [page_spec(k) for k in range(pps)] + [
        per_b(N_KV, ROWS_G, AUG_W), per_b(ROWS_S, KV_W), per_b(ROWS_S, 3), cs(ROWS_S, 1),
        per_b(N_KV, nc, AUG_W), per_b(N_KV, nc, LANES), cs(nc, nsp),
        pl.BlockSpec((nsp, pps * PAGE), lambda b, t, pt: (0, t)),
        per_b(wbuf, 2 * KV_W), per_b(LANES, 2 * KV_W), per_b(LANES, 2 * KV_W)]
    out_specs = [per_b(ROWS_S, KV_W), per_b(ROWS_S, LANES)]
    out_shape = [jax.ShapeDtypeStruct((nb, ROWS_S, KV_W), F32), jax.ShapeDtypeStruct((nb, ROWS_S, LANES), F32)]
    scratch = [pltpu.VMEM((N_KV, Q_PAD, nsp), BF16), pltpu.VMEM((ROWS_S, 1), F32),
               pltpu.VMEM((ROWS_S, KV_W + LANES), F32), pltpu.VMEM((ROWS_S, KV_W), F32)]
    gs = pltpu.PrefetchScalarGridSpec(num_scalar_prefetch=1, grid=(nb, nchunks), in_specs=in_specs,
                                      out_specs=out_specs, scratch_shapes=scratch)
    return pl.pallas_call(functools.partial(_attn_sample_kernel, pps=pps, past=past, wbuf=wbuf, n_new=n_new), grid_spec=gs,
                          out_shape=out_shape, compiler_params=_params(2), name="attn_sample")(
        page_table, *([pages] * pps), qa, qbd, gates, slope_rows, kc, vc, ms, emat, win, ksn, kwn)


def _outproj_kernel(x_ref, a_ref, o_ref, nw_ref, sc_ref, sh_ref, g1_ref, wm_ref, woc_ref, won_ref, wo_ref, x1_ref):
    x = x_ref[...]
    h = _mod_norm(x, nw_ref[...], sc_ref[...], sh_ref[...]).astype(BF16)
    mg = jax.nn.sigmoid(_dot(h, wm_ref[...]))
    y_a = _dot(a_ref[...], woc_ref[...])
    y_b = _dot(o_ref[...], won_ref[...])
    mix = (mg[:, 0:D_MODEL] * y_a + mg[:, D_MODEL:2 * D_MODEL] * y_b).astype(BF16)
    x1_ref[...] = x + g1_ref[...] * _dot(mix, wo_ref[...])


def _outproj(x2d, a, o, mod, nw, w_merge, w_oc, w_on, w_o, *, tm):
    n = x2d.shape[0]
    tok = pl.BlockSpec((tm, D_MODEL), lambda i: (i, 0))
    sq = _const_spec((D_MODEL, D_MODEL))
    return pl.pallas_call(
        _outproj_kernel, grid=(n // tm,),
        in_specs=[tok, tok, tok, _const_spec((1, D_MODEL)), mod.spec(1), mod.spec(0), mod.spec(2),
                  _const_spec((D_MODEL, 2 * D_MODEL)), sq, sq, sq],
        out_specs=tok, out_shape=jax.ShapeDtypeStruct((n, D_MODEL), F32),
        compiler_params=_params(1), name="outproj",
    )(x2d, a, o, nw, mod.rows, mod.rows, mod.rows, w_merge, w_oc, w_on, w_o)


FF_CHUNK = D_FF // 2


def _ffn_kernel(x_ref, nw_ref, sc_ref, sh_ref, g2_ref, nf_ref, wg_ref, wu_ref, wd_ref, y_ref):
    x = x_ref[...]
    h = _mod_norm(x, nw_ref[...], sc_ref[...], sh_ref[...]).astype(BF16)
    acc = jnp.zeros(x.shape, F32)
    for c in range(D_FF // FF_CHUNK):
        sl = slice(c * FF_CHUNK, (c + 1) * FF_CHUNK)
        gate = _dot(h, wg_ref[:, sl])
        up = _dot(h, wu_ref[:, sl])
        act = (gate * jax.nn.sigmoid(gate) * up).astype(BF16)
        acc = acc + _dot(act, wd_ref[sl, :])
    x2 = x + g2_ref[...] * acc
    inv = lax.rsqrt(jnp.mean(x2 * x2, axis=-1, keepdims=True) + EPS)
    y_ref[...] = (x2 * inv) * nf_ref[...]


def _ffn(x1, mod, nw2, nf, w_gate, w_up, w_down, *, tm):
    n = x1.shape[0]
    tok = pl.BlockSpec((tm, D_MODEL), lambda i: (i, 0))
    vec = _const_spec((1, D_MODEL))
    return pl.pallas_call(
        _ffn_kernel, grid=(n // tm,),
        in_specs=[tok, vec, mod.spec(4), mod.spec(3), mod.spec(5), vec,
                  _const_spec((D_MODEL, D_FF)), _const_spec((D_MODEL, D_FF)), _const_spec((D_FF, D_MODEL))],
        out_specs=tok, out_shape=jax.ShapeDtypeStruct((n, D_MODEL), F32),
        compiler_params=_params(1), name="ffn",
    )(x1, nw2, mod.rows, mod.rows, mod.rows, nf, w_gate, w_up, w_down)


def _slopes():
    return 2.0 ** (-8.0 * jnp.arange(1, N_HEADS + 1, dtype=F32) / N_HEADS)


def _slope_lanes(slopes):
    parts = _split3(slopes)
    cols = jnp.stack([parts[0], parts[0], parts[1], parts[1], parts[2], parts[2]], axis=1)
    return jnp.pad(cols, ((0, 0), (0, LANES - 6)))


def _pos_lanes(pos_hi, pos_lo):
    cols = jnp.stack([pos_hi, pos_lo] * 3, axis=1).astype(F32)
    return jnp.pad(cols, ((0, 0), (0, LANES - 6))).astype(BF16)


def _token_consts(t):
    pos = jnp.arange(t, dtype=jnp.int32)
    onehot = (pos[:, None] // SEL_BLOCK == jnp.arange(HEAD_DIM, dtype=jnp.int32)[None, :]).astype(F32)
    clo = jnp.concatenate([jnp.zeros((t, HEAD_DIM), F32), onehot], axis=1)
    chi = _pos_lanes((pos // SEL_BLOCK) * SEL_BLOCK, pos % SEL_BLOCK)
    return clo, chi


def _cmp_consts(ncb):
    ci = jnp.arange(ncb, dtype=jnp.int32) * CMP_STRIDE
    return _pos_lanes((ci // SEL_BLOCK) * SEL_BLOCK, ci % SEL_BLOCK)


def _ones_lane():
    return (jnp.arange(LANES) == HEAD_DIM).astype(F32).reshape(1, LANES)


def _imp_matrix(nc, ns_pad):
    c = jnp.arange(nc, dtype=jnp.int32)[:, None]
    s = jnp.arange(ns_pad, dtype=jnp.int32)[None, :]
    per = SEL_BLOCK // CMP_STRIDE
    return ((c // per == s) | ((c % per == per - 1) & (c // per == s - 1))).astype(BF16)


def _prep_weights(w_in, w_phi1, w_phi2, pe_cmp):
    wb = w_in.astype(BF16)
    w_conv_in = wb[:, _C_CONV:_C_QKV]
    w_qkv = jnp.pad(wb[:, _C_QKV:_C_MERGE], ((0, 0), (0, _QKV_COLS_PAD - _QKV_COLS)))
    w_merge = wb[:, _C_MERGE:]
    half = CMP_STRIDE * HEAD_DIM
    w1cat = jnp.concatenate([w_phi1[:, :half], w_phi1[:, half:]], axis=2).astype(BF16)
    w2pad = jnp.pad(w_phi2, ((0, 0), (0, 0), (0, LANES - HEAD_DIM))).astype(BF16)
    pe2 = pe_cmp.reshape(2, 2, half)
    return w_conv_in, w_qkv, w_merge, w1cat, w2pad, pe2


def _prompt_layer(x, mod_p, wts):
    (nw1, nw2, nf, w_conv_in, w_qkv, w_merge, w1cat, w2pad, pe2, w_conv, b_conv, w_oc, w_on, w_o,
     w_gate, w_up, w_down, slopes) = wts
    nb, t, _ = x.shape
    tm = min(512, t)
    x2d = x.reshape(nb * t, D_MODEL)
    mod = _Mod(mod_p, False, t, tm)
    a, tail = _conv_path(x2d, mod, nw1, w_conv_in, w_conv, b_conv, tm=tm, seq_len=t)
    clo, chi = _token_consts(t)
    cv = _ones_lane()
    qt, kvc, kvs, kvw, gts, ksa, vst, kwa, vwt = _qkv_path(x2d, mod, nw1, w_qkv, tm=tm, seq_len=t, consts=(clo, chi))
    npages = t // PAGE
    pt = jnp.arange(nb * npages, dtype=jnp.int32).reshape(nb, npages)
    nc = t // CMP_STRIDE
    ns = t // SEL_BLOCK
    nqb = t // Q_BLOCK
    kca, vct = _compress(kvc.reshape(nb * npages, PAGE * ROW_TILES, LANES), pt, pe2, w1cat, w2pad, _cmp_consts(nc), cv,
                         pps=min(16, npages), transpose_v=True)
    gates_t = gts[:, :3 * N_HEADS].reshape(nb, nqb, Q_BLOCK, N_KV, GROUP, 3).transpose(0, 3, 1, 5, 4, 2)
    gates_t = gates_t.reshape(nb, N_KV, nqb, 3, GROUP * Q_BLOCK)
    qc_t = _slope_lanes(slopes)[:, :QC_ROWS].reshape(N_KV, GROUP, QC_ROWS).transpose(0, 2, 1)
    qc_t = jnp.repeat(qc_t, Q_BLOCK, axis=2)
    mt = _imp_matrix(nc, ns).T
    o = _attn_prompt_t(qt, gates_t, qc_t, mt, kca, vct, ksa, vst, kwa, vwt)
    x1 = _outproj(x2d, a, o.reshape(nb * t, D_MODEL), mod, nw1, w_merge, w_oc, w_on, w_o, tm=tm)
    y = _ffn(x1, mod, nw2, nf, w_gate, w_up, w_down, tm=tm)
    keep = min(WINDOW, t)
    state = (kvc.reshape(nb, t, 2, N_KV, HEAD_DIM), kvs.reshape(nb, t, 2, N_KV, HEAD_DIM),
             kvw.reshape(nb, t, 2 * KV_W)[:, t - keep:].reshape(nb, keep, 2, N_KV, HEAD_DIM),
             tail[:, 8 - 2:, :])
    return y.reshape(nb, t, D_MODEL), state


def _sample_layer(x, mod_s, wts, cache_cmp, cache_sel, cache_win, state_conv, page_table):
    (nw1, nw2, nf, w_conv_in, w_qkv, w_merge, w1cat, w2pad, pe2, w_conv, b_conv, w_oc, w_on, w_o,
     w_gate, w_up, w_down, slopes) = wts
    nb, s, _ = x.shape
    n = nb * s
    x2d = x.reshape(n, D_MODEL)
    mod = _Mod(mod_s, True, s, n)
    tpos = jnp.arange(s)
    p1 = jnp.broadcast_to(state_conv[:, 1:2, :], (nb, s, D_MODEL)).reshape(n, D_MODEL)
    p2 = state_conv[:, jnp.minimum(tpos, 1), :].reshape(n, D_MODEL)
    a, u = _conv_path(x2d, mod, nw1, w_conv_in, w_conv, b_conv, tm=n, seq_len=s, prev=(p1, p2))
    qpad, kvc, kvs, kvw, gts = _qkv_path(x2d, mod, nw1, w_qkv, tm=n, seq_len=s)

    npages = page_table.shape[1]
    past = npages * PAGE
    cv = _ones_lane()
    nc = past // CMP_STRIDE
    kca, vca = _compress(cache_cmp.reshape(-1, PAGE * ROW_TILES, LANES), page_table, pe2, w1cat, w2pad, _cmp_consts(nc), cv,
                         pps=min(16, npages), transpose_v=False)

    qh = qpad.reshape(nb, s, N_KV, GROUP, LANES).transpose(0, 2, 3, 1, 4)
    qh = jnp.pad(qh, ((0, 0), (0, 0), (0, 0), (0, Q_PAD - s), (0, 0)))
    sl = jnp.broadcast_to(_slope_lanes(slopes).reshape(1, N_KV, GROUP, 1, LANES), qh.shape)
    qa = jnp.concatenate([qh, sl], axis=-1).reshape(nb, N_KV, ROWS_G, AUG_W)
    eye = jnp.eye(N_KV, dtype=BF16)
    qbd = (qh[..., None, :HEAD_DIM] * eye[None, :, None, None, :, None]).reshape(nb, ROWS_S, KV_W)
    gates = gts[:, :3 * N_HEADS].reshape(nb, s, N_KV, GROUP, 3).transpose(0, 2, 3, 1, 4)
    gates = jnp.pad(gates, ((0, 0), (0, 0), (0, 0), (0, Q_PAD - s), (0, 0))).reshape(nb, ROWS_S, 3)
    slope_rows = jnp.repeat(slopes, Q_PAD).reshape(ROWS_S, 1)
    ns = past // SEL_BLOCK + 1
    nsp = -(-ns // LANES) * LANES
    ms = _imp_matrix(nc, nsp)
    tok = jnp.arange(past, dtype=jnp.int32)
    emat = (jnp.arange(nsp, dtype=jnp.int32)[:, None] == tok[None, :] // SEL_BLOCK).astype(BF16)
    pad_rows = lambda r: jnp.pad(r.reshape(nb, s, 2 * KV_W), ((0, 0), (0, LANES - s), (0, 0)))
    wbuf = cache_win.shape[1]
    osw, ocg = _attn_sample(page_table, cache_sel.reshape(-1, PAGE, 2 * KV_W), qa, qbd, gates, slope_rows, kca, vca,
                            ms, emat, cache_win.reshape(nb, wbuf, 2 * KV_W), pad_rows(kvs), pad_rows(kvw),
                            pps=min(8, npages), n_new=s)
    osw = osw.reshape(nb, N_KV, GROUP, Q_PAD, N_KV, HEAD_DIM)
    o_sw = jnp.einsum('bgrqgd->bqgrd', osw)
    o_c = ocg.reshape(nb, N_KV, GROUP, Q_PAD, LANES)[..., :HEAD_DIM].transpose(0, 3, 1, 2, 4)
    o = (o_sw + o_c)[:, :s].reshape(n, N_HEADS * HEAD_DIM).astype(BF16)

    x1 = _outproj(x2d, a, o, mod, nw1, w_merge, w_oc, w_on, w_o, tm=n)
    y = _ffn(x1, mod, nw2, nf, w_gate, w_up, w_down, tm=n)
    kv5 = lambda r: r.reshape(nb, s, 2, N_KV, HEAD_DIM)
    win = jnp.concatenate([cache_win, kv5(kvw)], axis=1)[:, s:]
    state = (kv5(kvc), kv5(kvs), win, u.reshape(nb, s, D_MODEL)[:, s - 2:])
    return y.reshape(nb, s, D_MODEL), state


def kernel(x_prompt, x_sample, c_prompt, c_sample, cache_cmp, cache_sel, cache_win, state_conv, page_table,
           w_ada, b_ada, norm1, w_in, w_conv, b_conv, w_out_conv, pe_cmp, w_phi1, w_phi2, w_o_nsa, w_out,
           norm2, w_gate, w_up, w_down, norm_f):
    depth = w_ada.shape[0]
    assert depth == 1, "single-layer trunk"
    nbp, nbs = c_prompt.shape[0], c_sample.shape[0]
    slopes = _slopes()
    l = 0
    c_all = jnp.concatenate([c_prompt, c_sample], axis=0)
    c_all = jnp.pad(c_all, ((0, -c_all.shape[0] % 8), (0, 0)))
    mod = _ada(c_all, w_ada[l], b_ada[l])
    w_conv_in, w_qkv, w_merge, w1cat, w2pad, pe2 = _prep_weights(w_in[l], w_phi1[l], w_phi2[l], pe_cmp[l])
    row = lambda v: v.reshape(1, -1)
    wts = (row(norm1[l]), row(norm2[l]), row(norm_f), w_conv_in, w_qkv, w_merge, w1cat, w2pad, pe2,
           w_conv[l], row(b_conv[l]), w_out_conv[l].astype(BF16), w_o_nsa[l].astype(BF16), w_out[l].astype(BF16),
           w_gate[l].astype(BF16), w_up[l].astype(BF16), w_down[l].astype(BF16), slopes)
    yp, st_p = _prompt_layer(x_prompt, mod[:nbp], wts)
    ys, st_s = _sample_layer(x_sample, mod[nbp:nbp + nbs], wts, cache_cmp[l], cache_sel[l], cache_win[l],
                             state_conv[l], page_table)
    return (yp, ys, st_p[0][None], st_p[1][None], st_p[2][None], st_p[3][None],
            st_s[0][None], st_s[1][None], st_s[2][None], st_s[3][None])
```

```python
import functools

import jax
import jax.numpy as jnp
import numpy as np
from jax import lax
from jax.experimental import pallas as pl
from jax.experimental.pallas import tpu as pltpu

F32 = jnp.float32
BF16 = jnp.bfloat16

D_MODEL = 1024
N_HEADS = 16
HEAD_DIM = 64
N_KV = 4
GROUP = N_HEADS // N_KV
KV_W = N_KV * HEAD_DIM
CMP_BLOCK = 32
CMP_STRIDE = 16
SEL_BLOCK = 64
N_SEL = 16
WINDOW = 512
D_PHI = 2 * HEAD_DIM
Q_BLOCK = 128
PAGE = 128
D_FF = ((8 * D_MODEL // 3 + 255) // 256) * 256
EPS = 1e-6
NEG = -1e30
FORCE_BONUS = 1e3

LANES = 128
AUG_W = 2 * LANES
HALVES_PER_PAGE = PAGE // CMP_STRIDE
ROW_TILES = 2 * KV_W // LANES
VMEM_LIMIT = 56 * 1024 * 1024
KEY_TILE = 2 * LANES
QC_ROWS = 16
SEL_TILES_PER_ITER = 2

_C_CONV = 0
_C_QKV = 3 * D_MODEL
_C_GATE = _C_QKV + N_HEADS * HEAD_DIM + 6 * KV_W
_C_MERGE = _C_GATE + 3 * N_HEADS
_QKV_COLS = _C_MERGE - _C_QKV
_QKV_COLS_PAD = -(-_QKV_COLS // LANES) * LANES


def _dot(a, b):
    return jnp.dot(a, b, preferred_element_type=F32)


def _dot_nt(a, b):
    return lax.dot_general(a, b, (((1,), (1,)), ((), ())), preferred_element_type=F32)


def _params(n_axes):
    return pltpu.CompilerParams(dimension_semantics=("arbitrary",) * n_axes, vmem_limit_bytes=VMEM_LIMIT)


def _const_spec(shape):
    return pl.BlockSpec(shape, lambda *_: (0,) * len(shape))


def _mod_norm(x, nw, sc, sh):
    inv = lax.rsqrt(jnp.mean(x * x, axis=-1, keepdims=True) + EPS)
    return (x * inv) * nw * (1.0 + sc) + sh


def _split3(x):
    a = x.astype(BF16)
    r = x - a.astype(F32)
    b = r.astype(BF16)
    c = (r - b.astype(F32)).astype(BF16)
    return a, b, c


def _ada_kernel(c_ref, w_ref, b_ref, o_ref):
    c = c_ref[...]
    s = c * jax.nn.sigmoid(c)
    o_ref[...] = jnp.dot(s, w_ref[...], preferred_element_type=F32, precision=lax.Precision.HIGHEST) + b_ref[...]


def _ada(c, w_ada, b_ada):
    n = c.shape[0]
    tn = 1536
    return pl.pallas_call(
        _ada_kernel,
        grid=(6 * D_MODEL // tn,),
        in_specs=[_const_spec((n, D_MODEL)),
                  pl.BlockSpec((D_MODEL, tn), lambda j: (0, j)),
                  pl.BlockSpec((1, tn), lambda j: (0, j))],
        out_specs=pl.BlockSpec((n, tn), lambda j: (0, j)),
        out_shape=jax.ShapeDtypeStruct((n, 6 * D_MODEL), F32),
        compiler_params=_params(1),
        name="ada",
    )(c, w_ada, b_ada.reshape(1, -1))


class _Mod:
    def __init__(self, mod, per_token, seq_len, tm):
        self.per_token = per_token
        if per_token:
            self.rows = jnp.repeat(mod, seq_len, axis=0)
        else:
            self.rows = mod.reshape(mod.shape[0], 1, 6 * D_MODEL)
        self.tiles_per_seq = None if per_token else seq_len // tm
        self.tm = tm

    def spec(self, k):
        if self.per_token:
            return pl.BlockSpec((self.tm, D_MODEL), lambda i: (i, k))
        tps = self.tiles_per_seq
        return pl.BlockSpec((None, 1, D_MODEL), lambda i: (i // tps, 0, k))


def _conv_kernel(*refs, carry_rows, seq_len):
    if carry_rows:
        (x_ref, nw_ref, sc_ref, sh_ref, w_ref, wc_ref, bc_ref, a_ref, tail_ref, carry_ref) = refs
    else:
        (x_ref, nw_ref, sc_ref, sh_ref, w_ref, wc_ref, bc_ref, p1_ref, p2_ref, a_ref, tail_ref) = refs
    tm = x_ref.shape[0]
    h = _mod_norm(x_ref[...], nw_ref[...], sc_ref[...], sh_ref[...]).astype(BF16)
    z = _dot(h, w_ref[...])
    bg = z[:, 0:D_MODEL]
    u = z[:, D_MODEL:2 * D_MODEL] * z[:, 2 * D_MODEL:3 * D_MODEL]
    row = lax.broadcasted_iota(jnp.int32, (tm, 1), 0)
    u1 = pltpu.roll(u, 1, 0)
    u2 = pltpu.roll(u, 2, 0)
    if carry_rows:
        @pl.when(pl.program_id(0) % carry_rows == 0)
        def _():
            carry_ref[...] = jnp.zeros_like(carry_ref)
        c0 = carry_ref[0:1, :]
        c1 = carry_ref[1:2, :]
        u1 = jnp.where(row == 0, c1, u1)
        u2 = jnp.where(row == 0, c0, jnp.where(row == 1, c1, u2))
        carry_ref[0:2, :] = u[tm - 2:tm, :]
        tail_ref[...] = u[tm - 8:tm, :]
    else:
        pos = lax.rem(row, seq_len)
        u1 = jnp.where(pos >= 1, u1, p1_ref[...])
        u2 = jnp.where(pos >= 2, u2, p2_ref[...])
        tail_ref[...] = u
    v = bc_ref[...] + wc_ref[0:1, :] * u2 + wc_ref[1:2, :] * u1 + wc_ref[2:3, :] * u
    a_ref[...] = (bg * v).astype(BF16)


def _conv_path(x2d, mod, nw, w_conv_in, w_conv, b_conv, *, tm, seq_len, prev=None):
    n = x2d.shape[0]
    tok = pl.BlockSpec((tm, D_MODEL), lambda i: (i, 0))
    in_specs = [tok, _const_spec((1, D_MODEL)), mod.spec(1), mod.spec(0),
                _const_spec((D_MODEL, 3 * D_MODEL)), _const_spec((3, D_MODEL)), _const_spec((1, D_MODEL))]
    args = [x2d, nw, mod.rows, mod.rows, w_conv_in, w_conv, b_conv]
    if prev is None:
        tps = seq_len // tm
        out_specs = [tok, pl.BlockSpec((None, 8, D_MODEL), lambda i: (i // tps, 0, 0))]
        out_shape = [jax.ShapeDtypeStruct((n, D_MODEL), BF16), jax.ShapeDtypeStruct((n // seq_len, 8, D_MODEL), F32)]
        scratch = [pltpu.VMEM((8, D_MODEL), F32)]
        kern = functools.partial(_conv_kernel, carry_rows=tps, seq_len=seq_len)
    else:
        in_specs += [tok, tok]
        args += list(prev)
        out_specs = [tok, tok]
        out_shape = [jax.ShapeDtypeStruct((n, D_MODEL), BF16), jax.ShapeDtypeStruct((n, D_MODEL), F32)]
        scratch = []
        kern = functools.partial(_conv_kernel, carry_rows=0, seq_len=seq_len)
    return pl.pallas_call(kern, grid=(n // tm,), in_specs=in_specs, out_specs=out_specs, out_shape=out_shape,
                          scratch_shapes=scratch, compiler_params=_params(1), name="conv_path")(*args)


def _qkv_kernel(*refs, aug):
    if aug:
        (x_ref, nw_ref, sc_ref, sh_ref, w_ref, clo_ref, chi_ref,
         q_ref, kvc_ref, kvs_ref, kvw_ref, g_ref, ksa_ref, vsa_ref, kwa_ref, vwa_ref) = refs
    else:
        (x_ref, nw_ref, sc_ref, sh_ref, w_ref, q_ref, kvc_ref, kvs_ref, kvw_ref, g_ref) = refs
    tm = x_ref.shape[0]
    h = _mod_norm(x_ref[...], nw_ref[...], sc_ref[...], sh_ref[...]).astype(BF16)
    z = _dot(h, w_ref[...])
    low = lax.broadcasted_iota(jnp.int32, (tm, LANES), 1) < HEAD_DIM
    nq = N_HEADS * HEAD_DIM
    for c in range(N_HEADS // 2):
        t = z[:, c * LANES:(c + 1) * LANES] * (HEAD_DIM ** -0.5)
        if aug:
            tt = t.T.astype(BF16)
            q_ref[2 * c] = tt[0:HEAD_DIM]
            q_ref[2 * c + 1] = tt[HEAD_DIM:2 * HEAD_DIM]
        else:
            q_ref[:, (2 * c) * LANES:(2 * c + 1) * LANES] = jnp.where(low, t, 0.0).astype(BF16)
            q_ref[:, (2 * c + 1) * LANES:(2 * c + 2) * LANES] = jnp.where(low, pltpu.roll(t, HEAD_DIM, 1), 0.0).astype(BF16)
    kvc_ref[...] = z[:, nq:nq + 2 * KV_W]
    kvs_ref[...] = z[:, nq + 2 * KV_W:nq + 4 * KV_W]
    kvw_ref[...] = z[:, nq + 4 * KV_W:nq + 6 * KV_W]
    g_ref[...] = jax.nn.sigmoid(z[:, nq + 6 * KV_W:nq + 6 * KV_W + LANES])
    if aug:
        chi = chi_ref[...]
        ones_row = (lax.broadcasted_iota(jnp.int32, (LANES - HEAD_DIM, KEY_TILE), 0) == 0).astype(BF16)
        for br, (ka_ref, va_ref) in enumerate(((ksa_ref, vsa_ref), (kwa_ref, vwa_ref))):
            kbase = nq + 2 * KV_W * (br + 1)
            clo = clo_ref[...] if br == 0 else 0.0
            for g in range(N_KV):
                kt = z[:, kbase + (g // 2) * LANES:kbase + (g // 2 + 1) * LANES]
                if g % 2:
                    kt = pltpu.roll(kt, HEAD_DIM, 1)
                ka_ref[g, :, 0:LANES] = jnp.where(low, kt, clo).astype(BF16)
                ka_ref[g, :, LANES:AUG_W] = chi
            for c in range(N_KV // 2):
                vt = z[:, kbase + KV_W + c * LANES:kbase + KV_W + (c + 1) * LANES].T.astype(BF16)
                for gg in range(2):
                    for j in range(tm // KEY_TILE):
                        va_ref[2 * c + gg, j, 0:HEAD_DIM, :] = vt[gg * HEAD_DIM:(gg + 1) * HEAD_DIM, j * KEY_TILE:(j + 1) * KEY_TILE]
                        va_ref[2 * c + gg, j, HEAD_DIM:LANES, :] = ones_row


def _qkv_path(x2d, mod, nw, w_qkv, *, tm, seq_len, consts=None):
    n = x2d.shape[0]
    aug = consts is not None
    tok = lambda w: pl.BlockSpec((tm, w), lambda i: (i, 0))
    in_specs = [tok(D_MODEL), _const_spec((1, D_MODEL)), mod.spec(1), mod.spec(0),
                _const_spec((D_MODEL, _QKV_COLS_PAD))]
    args = [x2d, nw, mod.rows, mod.rows, w_qkv]
    out_specs = [tok(N_HEADS * LANES), tok(2 * KV_W), tok(2 * KV_W), tok(2 * KV_W), tok(LANES)]
    out_shape = [jax.ShapeDtypeStruct((n, N_HEADS * LANES), BF16)] + \
                [jax.ShapeDtypeStruct((n, 2 * KV_W), F32)] * 3 + [jax.ShapeDtypeStruct((n, LANES), F32)]
    if aug:
        tps = seq_len // tm
        nb = n // seq_len
        pos = lambda w: pl.BlockSpec((tm, w), lambda i: (i % tps, 0))
        in_specs += [pos(LANES), pos(LANES)]
        args += list(consts)
        out_specs[0] = pl.BlockSpec((None, N_HEADS, HEAD_DIM, tm), lambda i: (i // tps, 0, 0, i % tps))
        out_shape[0] = jax.ShapeDtypeStruct((nb, N_HEADS, HEAD_DIM, seq_len), BF16)
        ka = pl.BlockSpec((None, N_KV, tm, AUG_W), lambda i: (i // tps, 0, i % tps, 0))
        va = pl.BlockSpec((None, N_KV, tm // KEY_TILE, LANES, KEY_TILE), lambda i: (i // tps, 0, i % tps, 0, 0))
        out_specs += [ka, va, ka, va]
        ka_s = jax.ShapeDtypeStruct((nb, N_KV, seq_len, AUG_W), BF16)
        va_s = jax.ShapeDtypeStruct((nb, N_KV, seq_len // KEY_TILE, LANES, KEY_TILE), BF16)
        out_shape += [ka_s, va_s, ka_s, va_s]
    return pl.pallas_call(functools.partial(_qkv_kernel, aug=aug), grid=(n // tm,), in_specs=in_specs,
                          out_specs=out_specs, out_shape=out_shape, compiler_params=_params(1), name="qkv_path")(*args)


def _row_view_loader(ref, e):
    c = e // 2

    def load(p, nrows):
        r0 = p * ROW_TILES + c
        if nrows == 1:
            return ref[r0:r0 + 1, :]
        return ref[pl.ds(r0, nrows, stride=CMP_STRIDE * ROW_TILES), :]
    return load


def _token_major_loader(ref, k, e):
    c = e // 2

    def load(p, nrows):
        if nrows == 1:
            return ref[k, c, p:p + 1, :]
        return ref[k, c, pl.ds(p, nrows, stride=CMP_STRIDE), :]
    return load


def _gather_half(load, e, nrows):
    odd = e % 2
    low = lax.broadcasted_iota(jnp.int32, (nrows, LANES), 1) < HEAD_DIM
    tiles = []
    for qq in range(CMP_STRIDE // 2):
        a = load(2 * qq, nrows)
        b = load(2 * qq + 1, nrows)
        if odd:
            tiles.append(jnp.where(low, pltpu.roll(a, HEAD_DIM, 1), b))
        else:
            tiles.append(jnp.where(low, a, pltpu.roll(b, HEAD_DIM, 1)))
    return jnp.concatenate(tiles, axis=1)


def _compress_kernel(pt_ref, *refs, pps, transpose_v, token_minor):
    pages = refs[:pps]
    if token_minor:
        nxt_ref, pe_ref, w1_ref, w2_ref, chi_ref, cv_ref, kc_ref, vc_ref, xt_ref = refs[pps:]
        for k, pg in enumerate(list(pages) + [nxt_ref]):
            for c in range(ROW_TILES):
                xt_ref[k, c] = pg[c * LANES:(c + 1) * LANES, :].T
        loaders = lambda e: [_token_major_loader(xt_ref, k, e) for k in range(pps)]
        look_loader = lambda e: _token_major_loader(xt_ref, pps, e)
    else:
        nxt_ref, pe_ref, w1_ref, w2_ref, chi_ref, cv_ref, kc_ref, vc_ref = refs[pps:]
        loaders = lambda e: [_row_view_loader(p, e) for p in pages]
        look_loader = lambda e: _row_view_loader(nxt_ref, e)
    t = pl.program_id(1)
    last = t == pl.num_programs(1) - 1
    nhb = pps * HALVES_PER_PAGE
    n = N_KV * nhb
    row = lax.broadcasted_iota(jnp.int32, (n, 1), 0)
    chi = chi_ref[...]
    for kv in range(2):
        parts = []
        for g in range(N_KV):
            e = kv * N_KV + g
            parts += [_gather_half(ld, e, HALVES_PER_PAGE) for ld in loaders(e)]
        look = [_gather_half(look_loader(kv * N_KV + g), kv * N_KV + g, 1) for g in range(N_KV)]
        extra = jnp.concatenate([pe_ref[kv], jnp.zeros((2, CMP_STRIDE * HEAD_DIM), F32)] + look, axis=0)
        xmat = jnp.concatenate(parts + [extra], axis=0).astype(BF16)
        hab = _dot(xmat, w1_ref[kv])
        ha = hab[0:n, 0:D_PHI]
        hb = hab[0:n, D_PHI:2 * D_PHI]
        pbias = hab[n:n + 1, 0:D_PHI] + hab[n + 1:n + 2, D_PHI:2 * D_PHI]
        hbn = pltpu.roll(hb, n - 1, 0)
        for g in range(N_KV):
            la = jnp.where(last, 0.0, hab[n + 4 + g:n + 5 + g, D_PHI:2 * D_PHI])
            hbn = jnp.where(row == g * nhb + nhb - 1, la, hbn)
        act = jax.nn.gelu(ha + hbn + pbias).astype(BF16)
        out = _dot(act, w2_ref[kv])
        for g in range(N_KV):
            blk = out[g * nhb:(g + 1) * nhb, :]
            if kv == 0:
                kc_ref[g, :, 0:LANES] = blk.astype(BF16)
                kc_ref[g, :, LANES:AUG_W] = chi
            elif transpose_v:
                vc_ref[g, :, :] = (blk + cv_ref[...]).T.astype(BF16)
            else:
                vc_ref[g, :, :] = (blk + cv_ref[...]).astype(BF16)


def _compress(pages, page_table, pe2, w1cat, w2pad, chi_c, cv, *, pps, transpose_v, token_minor):
    nb, npages = page_table.shape
    nchunks = npages // pps
    nhb = pps * HALVES_PER_PAGE
    ncb = npages * HALVES_PER_PAGE

    def page_spec(k):
        return pl.BlockSpec((None, PAGE * ROW_TILES, LANES), lambda b, t, pt: (pt[b, t * pps + k], 0, 0))

    nxt_rows = PAGE * ROW_TILES if token_minor else CMP_STRIDE * ROW_TILES
    nxt_spec = pl.BlockSpec((None, nxt_rows, LANES),
                            lambda b, t, pt: (pt[b, jnp.minimum((t + 1) * pps, npages - 1)], 0, 0))
    scratch = [pltpu.VMEM((pps + 1, ROW_TILES, PAGE, LANES), F32)] if token_minor else []
    cs = lambda shape: pl.BlockSpec(shape, lambda b, t, pt: (0,) * len(shape))
    in_specs = [page_spec(k) for k in range(pps)] + [
        nxt_spec, cs((2, 2, CMP_STRIDE * HEAD_DIM)), cs((2, CMP_STRIDE * HEAD_DIM, 2 * D_PHI)),
        cs((2, D_PHI, LANES)), pl.BlockSpec((nhb, LANES), lambda b, t, pt: (t, 0)), cs((1, LANES))]
    if transpose_v:
        v_spec = pl.BlockSpec((None, N_KV, LANES, nhb), lambda b, t, pt: (b, 0, 0, t))
        v_shape = jax.ShapeDtypeStruct((nb, N_KV, LANES, ncb), BF16)
    else:
        v_spec = pl.BlockSpec((None, N_KV, nhb, LANES), lambda b, t, pt: (b, 0, t, 0))
        v_shape = jax.ShapeDtypeStruct((nb, N_KV, ncb, LANES), BF16)
    out_specs = [pl.BlockSpec((None, N_KV, nhb, AUG_W), lambda b, t, pt: (b, 0, t, 0)), v_spec]
    out_shape = [jax.ShapeDtypeStruct((nb, N_KV, ncb, AUG_W), BF16), v_shape]
    gs = pltpu.PrefetchScalarGridSpec(num_scalar_prefetch=1, grid=(nb, nchunks), in_specs=in_specs, out_specs=out_specs,
                                      scratch_shapes=scratch)
    return pl.pallas_call(functools.partial(_compress_kernel, pps=pps, transpose_v=transpose_v, token_minor=token_minor),
                          grid_spec=gs, out_shape=out_shape,
                          compiler_params=_params(2), name="compress")(
        page_table, *([pages] * (pps + 1)), pe2, w1cat, w2pad, chi_c, cv)


def _attn_prompt_t_kernel(q_ref, g_ref, qc_ref, mt_ref, kc_ref, vc_ref, ks_ref, vs_ref, kw_ref, vw_ref, o_ref,
                          qa_ref, qw_ref, m_ref, acc_ref):
    nc = kc_ref.shape[0]
    ns = mt_ref.shape[0]
    kt = vs_ref.shape[2]
    ncol = GROUP * Q_BLOCK
    qb = pl.program_id(2)
    q0 = qb * Q_BLOCK
    qpos = q0 + (lax.broadcasted_iota(jnp.int32, (1, ncol), 1) & (Q_BLOCK - 1))

    for ref in (qw_ref, qa_ref):
        for r in range(GROUP):
            ref[0:HEAD_DIM, r * Q_BLOCK:(r + 1) * Q_BLOCK] = q_ref[r]
        ref[HEAD_DIM:2 * HEAD_DIM, :] = jnp.zeros((HEAD_DIM, ncol), BF16)
        ref[2 * HEAD_DIM:2 * HEAD_DIM + QC_ROWS, :] = qc_ref[...]
        ref[2 * HEAD_DIM + QC_ROWS:AUG_W, :] = jnp.zeros((AUG_W - 2 * HEAD_DIM - QC_ROWS, ncol), BF16)

    s = _dot(kc_ref[...], qw_ref[...])
    ci = lax.broadcasted_iota(jnp.int32, (nc, 1), 0)
    c_valid = ci * CMP_STRIDE + (CMP_BLOCK - 1) <= qpos
    s = jnp.where(c_valid, s, NEG)
    e = jnp.exp(s - jnp.max(s, axis=0, keepdims=True))
    p = jnp.where(c_valid, e / jnp.sum(e, axis=0, keepdims=True), 0.0)
    o_c = _dot(vc_ref[...], p.astype(BF16))[0:HEAD_DIM]

    psum = p[:, 0:Q_BLOCK]
    for r in range(1, GROUP):
        psum = psum + p[:, r * Q_BLOCK:(r + 1) * Q_BLOCK]
    mt = mt_ref[...]
    imp = sum(_dot(mt, part) for part in _split3(psum))
    si = lax.broadcasted_iota(jnp.int32, (ns, Q_BLOCK), 0)
    qpos_t = q0 + lax.broadcasted_iota(jnp.int32, (ns, Q_BLOCK), 1)
    cur = lax.shift_right_logical(qpos_t, 6)
    s_valid = si * SEL_BLOCK <= qpos_t
    forced = (si == 0) | (si == cur) | (si == cur - 1)
    score = jnp.where(s_valid, imp + jnp.where(forced, FORCE_BONUS, 0.0), NEG)
    rank = jnp.zeros((ns, Q_BLOCK), jnp.int32)
    for sp in range(ns):
        other = score[sp:sp + 1, :]
        beats = (other > score) | ((other == score) & (si > sp))
        rank = rank + beats.astype(jnp.int32)
    bias_t = jnp.where(rank < min(N_SEL, ns), 0.0, NEG).astype(BF16)
    qa_ref[HEAD_DIM:HEAD_DIM + ns, :] = jnp.concatenate([bias_t] * GROUP, axis=1)

    t_hi = lax.div(q0, kt)
    key_iota = lax.broadcasted_iota(jnp.int32, (kt, 1), 0)

    n_back = WINDOW // kt
    s_w, v_w = [], []
    for j in range(n_back + 1):
        tw = t_hi - n_back + j
        tc = jnp.maximum(tw, 0)
        kpos = tw * kt + key_iota
        valid = kpos >= 0
        if j == 0:
            valid = valid & (qpos - kpos < WINDOW)
        if j == n_back:
            valid = valid & (kpos <= qpos)
        s = _dot(kw_ref[pl.ds(pl.multiple_of(tc * kt, kt), kt), :], qw_ref[...])
        s_w.append(jnp.where(valid, s, NEG))
        v_w.append(vw_ref[tc])
    m_w = functools.reduce(jnp.maximum, [jnp.max(s, axis=0, keepdims=True) for s in s_w])
    acc_w = sum(_dot(v, jnp.exp(s - m_w).astype(BF16)) for s, v in zip(s_w, v_w))
    o_w = acc_w[0:HEAD_DIM] / acc_w[HEAD_DIM:HEAD_DIM + 1]

    m_ref[...] = jnp.full(m_ref.shape, NEG, F32)
    acc_ref[...] = jnp.zeros(acc_ref.shape, F32)

    def sel_tiles(t0, n, causal_last):
        qa = qa_ref[...]
        ss = []
        for i in range(n):
            k0 = pl.multiple_of((t0 + i) * kt, kt)
            s = _dot(ks_ref[pl.ds(k0, kt), :], qa)
            if causal_last and i == n - 1:
                s = jnp.where(k0 + key_iota <= qpos, s, NEG)
            ss.append(s)
        m_old = m_ref[...]
        m_new = functools.reduce(jnp.maximum, [m_old] + [jnp.max(s, axis=0, keepdims=True) for s in ss])
        pv = sum(_dot(vs_ref[t0 + i], jnp.exp(s - m_new).astype(BF16)) for i, s in enumerate(ss))
        acc_ref[...] = jnp.exp(m_old - m_new) * acc_ref[...] + pv
        m_ref[...] = m_new

    n_group = lax.div(t_hi, SEL_TILES_PER_ITER)

    def body(i, carry):
        sel_tiles(i * SEL_TILES_PER_ITER, SEL_TILES_PER_ITER, False)
        return carry

    lax.fori_loop(0, n_group, body, 0)
    rem = t_hi - n_group * SEL_TILES_PER_ITER
    for left in range(SEL_TILES_PER_ITER):
        @pl.when(rem == left)
        def _():
            sel_tiles(t_hi - left, left + 1, True)

    acc = acc_ref[...]
    o_s = acc[0:HEAD_DIM] / acc[HEAD_DIM:HEAD_DIM + 1]

    gts = g_ref[...]
    o = gts[0:1] * o_c + gts[1:2] * o_s + gts[2:3] * o_w
    for c in range(GROUP // 2):
        pair = jnp.concatenate([o[:, (2 * c) * Q_BLOCK:(2 * c + 1) * Q_BLOCK],
                                o[:, (2 * c + 1) * Q_BLOCK:(2 * c + 2) * Q_BLOCK]], axis=0)
        o_ref[:, c * LANES:(c + 1) * LANES] = pair.T.astype(BF16)


def _attn_prompt_t(qt, gates_t, qc_t, mt, kc, vct, ks, vst, kw, vwt):
    nb, t = qt.shape[0], qt.shape[3]
    nc = kc.shape[2]
    ns = mt.shape[0]
    ntile, kt = vst.shape[2], vst.shape[4]
    ncol = GROUP * Q_BLOCK
    per_bg = lambda *shape: pl.BlockSpec((None, None) + shape, lambda b, g, i: (b, g) + (0,) * len(shape))
    in_specs = [pl.BlockSpec((None, GROUP, HEAD_DIM, Q_BLOCK), lambda b, g, i: (b, g, 0, i)),
                pl.BlockSpec((None, None, None, 3, ncol), lambda b, g, i: (b, g, i, 0, 0)),
                pl.BlockSpec((None, QC_ROWS, ncol), lambda b, g, i: (g, 0, 0)),
                pl.BlockSpec((ns, nc), lambda b, g, i: (0, 0)),
                per_bg(nc, AUG_W), per_bg(LANES, nc), per_bg(t, AUG_W), per_bg(ntile, LANES, kt),
                per_bg(t, AUG_W), per_bg(ntile, LANES, kt)]
    out_spec = pl.BlockSpec((None, Q_BLOCK, GROUP * HEAD_DIM), lambda b, g, i: (b, i, g))
    scratch = [pltpu.VMEM((AUG_W, ncol), BF16), pltpu.VMEM((AUG_W, ncol), BF16),
               pltpu.VMEM((1, ncol), F32), pltpu.VMEM((LANES, ncol), F32)]
    return pl.pallas_call(_attn_prompt_t_kernel, grid=(nb, N_KV, t // Q_BLOCK),
                          in_specs=in_specs, out_specs=out_spec,
                          out_shape=jax.ShapeDtypeStruct((nb, t, N_HEADS * HEAD_DIM), BF16),
                          scratch_shapes=scratch, compiler_params=_params(3), name="attn_prompt")(
        qt, gates_t, qc_t, mt, kc, vct, ks, vst, kw, vwt)


Q_PAD = 8
ROWS_G = GROUP * Q_PAD
ROWS_S = N_KV * ROWS_G


def _attn_sample_kernel(pt_ref, *refs, pps, past, wbuf, n_new):
    pages = refs[:pps]
    (qa_ref, qbd_ref, g_ref, slope_ref, kc_ref, vc_ref, ms_ref, e_ref, win_ref, ksn_ref, kwn_ref,
     osw_ref, oc_ref, bias_ref, m_ref, acc_ref, ow_ref) = refs[pps:]
    t = pl.program_id(1)
    nchunks = pl.num_programs(1)
    nc = kc_ref.shape[1]
    ns = past // SEL_BLOCK + 1
    row = lax.broadcasted_iota(jnp.int32, (ROWS_S, 1), 0)
    qpos = past + (row & (Q_PAD - 1))
    qposf = qpos.astype(F32)
    slope = slope_ref[...]
    qbd = qbd_ref[...]

    @pl.when(t == 0)
    def _():
        nsp = ms_ref.shape[1]
        rg = lax.broadcasted_iota(jnp.int32, (ROWS_G, 1), 0)
        qpos_g = past + (rg & (Q_PAD - 1))
        r8 = lax.broadcasted_iota(jnp.int32, (Q_PAD, 1), 0)
        qpos8 = past + r8
        si = lax.broadcasted_iota(jnp.int32, (Q_PAD, nsp), 1)
        for g in range(N_KV):
            s = _dot_nt(qa_ref[g], kc_ref[g])
            ci = lax.broadcasted_iota(jnp.int32, (1, nc), 1)
            c_valid = ci * CMP_STRIDE + (CMP_BLOCK - 1) <= qpos_g
            s = jnp.where(c_valid, s, NEG)
            e = jnp.exp(s - jnp.max(s, axis=1, keepdims=True))
            p = jnp.where(c_valid, e / jnp.sum(e, axis=1, keepdims=True), 0.0)
            oc_ref[g * ROWS_G:(g + 1) * ROWS_G, :] = _dot(p.astype(BF16), vc_ref[g])
            psum = p[0:Q_PAD]
            for r in range(1, GROUP):
                psum = psum + p[r * Q_PAD:(r + 1) * Q_PAD]
            ms = ms_ref[...]
            imp = sum(_dot(part, ms) for part in _split3(psum))
            cur = lax.shift_right_logical(qpos8, 6)
            s_valid = (si * SEL_BLOCK <= qpos8) & (si < ns)
            forced = (si == 0) | (si == cur) | (si == cur - 1)
            score = jnp.where(s_valid, imp + jnp.where(forced, FORCE_BONUS, 0.0), NEG)
            rank = jnp.zeros((Q_PAD, nsp), jnp.int32)
            for sp in range(ns):
                other = score[:, sp:sp + 1]
                beats = (other > score) | ((other == score) & (si > sp))
                rank = rank + beats.astype(jnp.int32)
            bias_ref[g] = jnp.where((rank < min(N_SEL, ns)) & (si < ns), 0.0, NEG).astype(BF16)

        kw_t = win_ref[0:KV_W, :].astype(BF16)
        vw_t = win_ref[KV_W:2 * KV_W, :].astype(BF16)
        kpos = past - wbuf + lax.broadcasted_iota(jnp.int32, (1, wbuf), 1)
        s1 = _dot(qbd, kw_t) - slope * (qposf - kpos.astype(F32))
        s1 = jnp.where((kpos <= qpos) & (qpos - kpos < WINDOW) & (kpos >= 0), s1, NEG)
        kn = kwn_ref[:, 0:KV_W].astype(BF16)
        vn = kwn_ref[:, KV_W:2 * KV_W].astype(BF16)
        li = lax.broadcasted_iota(jnp.int32, (1, LANES), 1)
        kposn = past + li
        s2 = _dot_nt(qbd, kn) - slope * (qposf - kposn.astype(F32))
        s2 = jnp.where((li < n_new) & (kposn <= qpos) & (qpos - kposn < WINDOW), s2, NEG)
        m = jnp.maximum(jnp.max(s1, axis=1, keepdims=True), jnp.max(s2, axis=1, keepdims=True))
        p1 = jnp.exp(s1 - m)
        p2 = jnp.exp(s2 - m)
        den = jnp.sum(p1, axis=1, keepdims=True) + jnp.sum(p2, axis=1, keepdims=True)
        ow_ref[...] = (_dot_nt((p1 / den).astype(BF16), vw_t) + _dot((p2 / den).astype(BF16), vn))
        m_ref[...] = jnp.full(m_ref.shape, NEG, F32)
        acc_ref[...] = jnp.zeros(acc_ref.shape, F32)

    def block_bias(emat):
        rows = []
        for g in range(N_KV):
            bt = _dot(bias_ref[g], emat)
            rows += [bt] * GROUP
        return jnp.concatenate(rows, axis=0)

    def update(scores, pv_fns):
        m_old = m_ref[...]
        m_new = functools.reduce(jnp.maximum, [m_old] + [jnp.max(s, axis=1, keepdims=True) for s in scores])
        alpha = jnp.exp(m_old - m_new)
        ps = [jnp.exp(s - m_new) for s in scores]
        ones = functools.reduce(jnp.add, [jnp.sum(p, axis=1, keepdims=True) for p in ps])
        pv = functools.reduce(jnp.add, [f(p.astype(BF16)) for f, p in zip(pv_fns, ps)])
        acc_ref[:, 0:KV_W] = alpha * acc_ref[:, 0:KV_W] + pv
        acc_ref[:, KV_W:KV_W + LANES] = alpha * acc_ref[:, KV_W:KV_W + LANES] + ones
        m_ref[...] = m_new

    bias_chunk = block_bias(e_ref[...])
    li = lax.broadcasted_iota(jnp.int32, (1, PAGE), 1)
    scores, pv_fns = [], []
    for k in range(pps):
        pg = pages[k]
        kpos = (t * pps + k) * PAGE + li
        s = _dot(qbd, pg[0:KV_W, :].astype(BF16)) - slope * (qposf - kpos.astype(F32))
        scores.append(s + bias_chunk[:, k * PAGE:(k + 1) * PAGE])
        pv_fns.append(lambda p, pg=pg: _dot_nt(p, pg[KV_W:2 * KV_W, :].astype(BF16)))
    update(scores, pv_fns)

    @pl.when(t == nchunks - 1)
    def _():
        nsp = ms_ref.shape[1]
        kposn = past + li
        e_new = (lax.broadcasted_iota(jnp.int32, (nsp, PAGE), 0) == past // SEL_BLOCK).astype(BF16)
        s = _dot_nt(qbd, ksn_ref[:, 0:KV_W].astype(BF16)) - slope * (qposf - kposn.astype(F32)) + block_bias(e_new)
        s = jnp.where((li < n_new) & (kposn <= qpos), s, NEG)
        update([s], [lambda p: _dot(p, ksn_ref[:, KV_W:2 * KV_W].astype(BF16))])
        o_s = acc_ref[:, 0:KV_W] / acc_ref[:, KV_W:KV_W + 1]
        gts = g_ref[...]
        osw_ref[...] = gts[:, 1:2] * o_s + gts[:, 2:3] * ow_ref[...]
        oc_ref[...] = gts[:, 0:1] * oc_ref[...]


def _attn_sample(page_table, pages, qa, qbd, gates, slope_rows, kc, vc, ms, emat, win, ksn, kwn, *, pps, n_new):
    nb, npages = page_table.shape
    nchunks = npages // pps
    past = npages * PAGE
    wbuf = win.shape[2]
    nc = kc.shape[2]
    nsp = ms.shape[1]

    def page_spec(k):
        return pl.BlockSpec((None, 2 * KV_W, PAGE), lambda b, t, pt: (pt[b, t * pps + k], 0, 0))

    per_b = lambda *shape: pl.BlockSpec((None,) + shape, lambda b, t, pt: (b,) + (0,) * len(shape))
    cs = lambda *shape: pl.BlockSpec(shape, lambda b, t, pt: (0,) * len(shape))
    in_specs = [page_spec(k) for k in range(pps)] + [
        per_b(N_KV, ROWS_G, AUG_W), per_b(ROWS_S, KV_W), per_b(ROWS_S, 3), cs(ROWS_S, 1),
        per_b(N_KV, nc, AUG_W), per_b(N_KV, nc, LANES), cs(nc, nsp),
        pl.BlockSpec((nsp, pps * PAGE), lambda b, t, pt: (0, t)),
        per_b(2 * KV_W, wbuf), per_b(LANES, 2 * KV_W), per_b(LANES, 2 * KV_W)]
    out_specs = [per_b(ROWS_S, KV_W), per_b(ROWS_S, LANES)]
    out_shape = [jax.ShapeDtypeStruct((nb, ROWS_S, KV_W), F32), jax.ShapeDtypeStruct((nb, ROWS_S, LANES), F32)]
    scratch = [pltpu.VMEM((N_KV, Q_PAD, nsp), BF16), pltpu.VMEM((ROWS_S, 1), F32),
               pltpu.VMEM((ROWS_S, KV_W + LANES), F32), pltpu.VMEM((ROWS_S, KV_W), F32)]
    gs = pltpu.PrefetchScalarGridSpec(num_scalar_prefetch=1, grid=(nb, nchunks), in_specs=in_specs,
                                      out_specs=out_specs, scratch_shapes=scratch)
    return pl.pallas_call(functools.partial(_attn_sample_kernel, pps=pps, past=past, wbuf=wbuf, n_new=n_new), grid_spec=gs,
                          out_shape=out_shape, compiler_params=_params(2), name="attn_sample")(
        page_table, *([pages] * pps), qa, qbd, gates, slope_rows, kc, vc, ms, emat, win, ksn, kwn)


def _outproj_kernel(x_ref, a_ref, o_ref, nw_ref, sc_ref, sh_ref, g1_ref, wm_ref, woc_ref, won_ref, wo_ref, x1_ref):
    x = x_ref[...]
    h = _mod_norm(x, nw_ref[...], sc_ref[...], sh_ref[...]).astype(BF16)
    mg = jax.nn.sigmoid(_dot(h, wm_ref[...]))
    y_a = _dot(a_ref[...], woc_ref[...])
    y_b = _dot(o_ref[...], won_ref[...])
    mix = (mg[:, 0:D_MODEL] * y_a + mg[:, D_MODEL:2 * D_MODEL] * y_b).astype(BF16)
    x1_ref[...] = x + g1_ref[...] * _dot(mix, wo_ref[...])


def _outproj(x2d, a, o, mod, nw, w_merge, w_oc, w_on, w_o, *, tm):
    n = x2d.shape[0]
    tok = pl.BlockSpec((tm, D_MODEL), lambda i: (i, 0))
    sq = _const_spec((D_MODEL, D_MODEL))
    return pl.pallas_call(
        _outproj_kernel, grid=(n // tm,),
        in_specs=[tok, tok, tok, _const_spec((1, D_MODEL)), mod.spec(1), mod.spec(0), mod.spec(2),
                  _const_spec((D_MODEL, 2 * D_MODEL)), sq, sq, sq],
        out_specs=tok, out_shape=jax.ShapeDtypeStruct((n, D_MODEL), F32),
        compiler_params=_params(1), name="outproj",
    )(x2d, a, o, nw, mod.rows, mod.rows, mod.rows, w_merge, w_oc, w_on, w_o)


FF_CHUNK = D_FF // 2


def _ffn_kernel(x_ref, nw_ref, sc_ref, sh_ref, g2_ref, nf_ref, wg_ref, wu_ref, wd_ref, y_ref):
    x = x_ref[...]
    h = _mod_norm(x, nw_ref[...], sc_ref[...], sh_ref[...]).astype(BF16)
    acc = jnp.zeros(x.shape, F32)
    for c in range(D_FF // FF_CHUNK):
        sl = slice(c * FF_CHUNK, (c + 1) * FF_CHUNK)
        gate = _dot(h, wg_ref[:, sl])
        up = _dot(h, wu_ref[:, sl])
        act = (gate * jax.nn.sigmoid(gate) * up).astype(BF16)
        acc = acc + _dot(act, wd_ref[sl, :])
    x2 = x + g2_ref[...] * acc
    inv = lax.rsqrt(jnp.mean(x2 * x2, axis=-1, keepdims=True) + EPS)
    y_ref[...] = (x2 * inv) * nf_ref[...]


def _ffn(x1, mod, nw2, nf, w_gate, w_up, w_down, *, tm):
    n = x1.shape[0]
    tok = pl.BlockSpec((tm, D_MODEL), lambda i: (i, 0))
    vec = _const_spec((1, D_MODEL))
    return pl.pallas_call(
        _ffn_kernel, grid=(n // tm,),
        in_specs=[tok, vec, mod.spec(4), mod.spec(3), mod.spec(5), vec,
                  _const_spec((D_MODEL, D_FF)), _const_spec((D_MODEL, D_FF)), _const_spec((D_FF, D_MODEL))],
        out_specs=tok, out_shape=jax.ShapeDtypeStruct((n, D_MODEL), F32),
        compiler_params=_params(1), name="ffn",
    )(x1, nw2, mod.rows, mod.rows, mod.rows, nf, w_gate, w_up, w_down)


def _slopes():
    return 2.0 ** (-8.0 * jnp.arange(1, N_HEADS + 1, dtype=F32) / N_HEADS)


def _slope_lanes(slopes):
    parts = _split3(slopes)
    cols = jnp.stack([parts[0], parts[0], parts[1], parts[1], parts[2], parts[2]], axis=1)
    return jnp.pad(cols, ((0, 0), (0, LANES - 6)))


def _pos_lanes(pos_hi, pos_lo):
    cols = jnp.stack([pos_hi, pos_lo] * 3, axis=1).astype(F32)
    return jnp.pad(cols, ((0, 0), (0, LANES - 6))).astype(BF16)


def _token_consts(t):
    pos = jnp.arange(t, dtype=jnp.int32)
    onehot = (pos[:, None] // SEL_BLOCK == jnp.arange(HEAD_DIM, dtype=jnp.int32)[None, :]).astype(F32)
    clo = jnp.concatenate([jnp.zeros((t, HEAD_DIM), F32), onehot], axis=1)
    chi = _pos_lanes((pos // SEL_BLOCK) * SEL_BLOCK, pos % SEL_BLOCK)
    return clo, chi


def _cmp_consts(ncb):
    ci = jnp.arange(ncb, dtype=jnp.int32) * CMP_STRIDE
    return _pos_lanes((ci // SEL_BLOCK) * SEL_BLOCK, ci % SEL_BLOCK)


def _ones_lane():
    return (jnp.arange(LANES) == HEAD_DIM).astype(F32).reshape(1, LANES)


def _imp_matrix(nc, ns_pad):
    c = jnp.arange(nc, dtype=jnp.int32)[:, None]
    s = jnp.arange(ns_pad, dtype=jnp.int32)[None, :]
    per = SEL_BLOCK // CMP_STRIDE
    return ((c // per == s) | ((c % per == per - 1) & (c // per == s - 1))).astype(BF16)


def _prep_weights(w_in, w_phi1, w_phi2, pe_cmp):
    wb = w_in.astype(BF16)
    w_conv_in = wb[:, _C_CONV:_C_QKV]
    w_qkv = jnp.pad(wb[:, _C_QKV:_C_MERGE], ((0, 0), (0, _QKV_COLS_PAD - _QKV_COLS)))
    w_merge = wb[:, _C_MERGE:]
    half = CMP_STRIDE * HEAD_DIM
    w1cat = jnp.concatenate([w_phi1[:, :half], w_phi1[:, half:]], axis=2).astype(BF16)
    w2pad = jnp.pad(w_phi2, ((0, 0), (0, 0), (0, LANES - HEAD_DIM))).astype(BF16)
    pe2 = pe_cmp.reshape(2, 2, half)
    return w_conv_in, w_qkv, w_merge, w1cat, w2pad, pe2


def _prompt_layer(x, mod_p, wts):
    (nw1, nw2, nf, w_conv_in, w_qkv, w_merge, w1cat, w2pad, pe2, w_conv, b_conv, w_oc, w_on, w_o,
     w_gate, w_up, w_down, slopes) = wts
    nb, t, _ = x.shape
    tm = min(512, t)
    x2d = x.reshape(nb * t, D_MODEL)
    mod = _Mod(mod_p, False, t, tm)
    a, tail = _conv_path(x2d, mod, nw1, w_conv_in, w_conv, b_conv, tm=tm, seq_len=t)
    clo, chi = _token_consts(t)
    cv = _ones_lane()
    qt, kvc, kvs, kvw, gts, ksa, vst, kwa, vwt = _qkv_path(x2d, mod, nw1, w_qkv, tm=tm, seq_len=t, consts=(clo, chi))
    npages = t // PAGE
    pt = jnp.arange(nb * npages, dtype=jnp.int32).reshape(nb, npages)
    nc = t // CMP_STRIDE
    ns = t // SEL_BLOCK
    nqb = t // Q_BLOCK
    kca, vct = _compress(kvc.reshape(nb * npages, PAGE * ROW_TILES, LANES), pt, pe2, w1cat, w2pad, _cmp_consts(nc), cv,
                         pps=min(16, npages), transpose_v=True, token_minor=False)
    gates_t = gts[:, :3 * N_HEADS].reshape(nb, nqb, Q_BLOCK, N_KV, GROUP, 3).transpose(0, 3, 1, 5, 4, 2)
    gates_t = gates_t.reshape(nb, N_KV, nqb, 3, GROUP * Q_BLOCK)
    qc_t = _slope_lanes(slopes)[:, :QC_ROWS].reshape(N_KV, GROUP, QC_ROWS).transpose(0, 2, 1)
    qc_t = jnp.repeat(qc_t, Q_BLOCK, axis=2)
    mt = _imp_matrix(nc, ns).T
    o = _attn_prompt_t(qt, gates_t, qc_t, mt, kca, vct, ksa, vst, kwa, vwt)
    x1 = _outproj(x2d, a, o.reshape(nb * t, D_MODEL), mod, nw1, w_merge, w_oc, w_on, w_o, tm=tm)
    y = _ffn(x1, mod, nw2, nf, w_gate, w_up, w_down, tm=tm)
    keep = min(WINDOW, t)
    state = (kvc.reshape(nb, t, 2, N_KV, HEAD_DIM), kvs.reshape(nb, t, 2, N_KV, HEAD_DIM),
             kvw.reshape(nb, t, 2 * KV_W)[:, t - keep:].reshape(nb, keep, 2, N_KV, HEAD_DIM),
             tail[:, 8 - 2:, :])
    return y.reshape(nb, t, D_MODEL), state


def _sample_layer(x, mod_s, wts, cache_cmp, cache_sel, cache_win, state_conv, page_table):
    (nw1, nw2, nf, w_conv_in, w_qkv, w_merge, w1cat, w2pad, pe2, w_conv, b_conv, w_oc, w_on, w_o,
     w_gate, w_up, w_down, slopes) = wts
    nb, s, _ = x.shape
    n = nb * s
    x2d = x.reshape(n, D_MODEL)
    mod = _Mod(mod_s, True, s, n)
    tpos = jnp.arange(s)
    p1 = jnp.broadcast_to(state_conv[:, 1:2, :], (nb, s, D_MODEL)).reshape(n, D_MODEL)
    p2 = state_conv[:, jnp.minimum(tpos, 1), :].reshape(n, D_MODEL)
    a, u = _conv_path(x2d, mod, nw1, w_conv_in, w_conv, b_conv, tm=n, seq_len=s, prev=(p1, p2))
    qpad, kvc, kvs, kvw, gts = _qkv_path(x2d, mod, nw1, w_qkv, tm=n, seq_len=s)

    npages = page_table.shape[1]
    past = npages * PAGE
    cv = _ones_lane()
    nc = past // CMP_STRIDE
    token_minor = lambda c: c.transpose(0, 2, 3, 4, 1).reshape(c.shape[0], 2 * KV_W, c.shape[1])
    kca, vca = _compress(token_minor(cache_cmp), page_table, pe2, w1cat, w2pad, _cmp_consts(nc), cv,
                         pps=min(16, npages), transpose_v=False, token_minor=True)

    qh = qpad.reshape(nb, s, N_KV, GROUP, LANES).transpose(0, 2, 3, 1, 4)
    qh = jnp.pad(qh, ((0, 0), (0, 0), (0, 0), (0, Q_PAD - s), (0, 0)))
    sl = jnp.broadcast_to(_slope_lanes(slopes).reshape(1, N_KV, GROUP, 1, LANES), qh.shape)
    qa = jnp.concatenate([qh, sl], axis=-1).reshape(nb, N_KV, ROWS_G, AUG_W)
    eye = jnp.eye(N_KV, dtype=BF16)
    qbd = (qh[..., None, :HEAD_DIM] * eye[None, :, None, None, :, None]).reshape(nb, ROWS_S, KV_W)
    gates = gts[:, :3 * N_HEADS].reshape(nb, s, N_KV, GROUP, 3).transpose(0, 2, 3, 1, 4)
    gates = jnp.pad(gates, ((0, 0), (0, 0), (0, 0), (0, Q_PAD - s), (0, 0))).reshape(nb, ROWS_S, 3)
    slope_rows = jnp.repeat(slopes, Q_PAD).reshape(ROWS_S, 1)
    ns = past // SEL_BLOCK + 1
    nsp = -(-ns // LANES) * LANES
    ms = _imp_matrix(nc, nsp)
    tok = jnp.arange(past, dtype=jnp.int32)
    emat = (jnp.arange(nsp, dtype=jnp.int32)[:, None] == tok[None, :] // SEL_BLOCK).astype(BF16)
    pad_rows = lambda r: jnp.pad(r.reshape(nb, s, 2 * KV_W), ((0, 0), (0, LANES - s), (0, 0)))
    osw, ocg = _attn_sample(page_table, token_minor(cache_sel), qa, qbd, gates, slope_rows, kca, vca,
                            ms, emat, token_minor(cache_win), pad_rows(kvs), pad_rows(kvw),
                            pps=min(8, npages), n_new=s)
    osw = osw.reshape(nb, N_KV, GROUP, Q_PAD, N_KV, HEAD_DIM)
    o_sw = jnp.einsum('bgrqgd->bqgrd', osw)
    o_c = ocg.reshape(nb, N_KV, GROUP, Q_PAD, LANES)[..., :HEAD_DIM].transpose(0, 3, 1, 2, 4)
    o = (o_sw + o_c)[:, :s].reshape(n, N_HEADS * HEAD_DIM).astype(BF16)

    x1 = _outproj(x2d, a, o, mod, nw1, w_merge, w_oc, w_on, w_o, tm=n)
    y = _ffn(x1, mod, nw2, nf, w_gate, w_up, w_down, tm=n)
    kv5 = lambda r: r.reshape(nb, s, 2, N_KV, HEAD_DIM)
    win = jnp.concatenate([cache_win, kv5(kvw)], axis=1)[:, s:]
    state = (kv5(kvc), kv5(kvs), win, u.reshape(nb, s, D_MODEL)[:, s - 2:])
    return y.reshape(nb, s, D_MODEL), state


def kernel(x_prompt, x_sample, c_prompt, c_sample, cache_cmp, cache_sel, cache_win, state_conv, page_table,
           w_ada, b_ada, norm1, w_in, w_conv, b_conv, w_out_conv, pe_cmp, w_phi1, w_phi2, w_o_nsa, w_out,
           norm2, w_gate, w_up, w_down, norm_f):
    depth = w_ada.shape[0]
    assert depth == 1, "single-layer trunk"
    nbp, nbs = c_prompt.shape[0], c_sample.shape[0]
    slopes = _slopes()
    l = 0
    c_all = jnp.concatenate([c_prompt, c_sample], axis=0)
    c_all = jnp.pad(c_all, ((0, -c_all.shape[0] % 8), (0, 0)))
    mod = _ada(c_all, w_ada[l], b_ada[l])
    w_conv_in, w_qkv, w_merge, w1cat, w2pad, pe2 = _prep_weights(w_in[l], w_phi1[l], w_phi2[l], pe_cmp[l])
    row = lambda v: v.reshape(1, -1)
    wts = (row(norm1[l]), row(norm2[l]), row(norm_f), w_conv_in, w_qkv, w_merge, w1cat, w2pad, pe2,
           w_conv[l], row(b_conv[l]), w_out_conv[l].astype(BF16), w_o_nsa[l].astype(BF16), w_out[l].astype(BF16),
           w_gate[l].astype(BF16), w_up[l].astype(BF16), w_down[l].astype(BF16), slopes)
    yp, st_p = _prompt_layer(x_prompt, mod[:nbp], wts)
    ys, st_s = _sample_layer(x_sample, mod[nbp:nbp + nbs], wts, cache_cmp[l], cache_sel[l], cache_win[l],
                             state_conv[l], page_table)
    return (yp, ys, st_p[0][None], st_p[1][None], st_p[2][None], st_p[3][None],
            st_s[0][None], st_s[1][None], st_s[2][None], st_s[3][None])
```

```python
import functools

import jax
import jax.numpy as jnp
import numpy as np
from jax import lax
from jax.experimental import pallas as pl
from jax.experimental.pallas import tpu as pltpu

F32 = jnp.float32
BF16 = jnp.bfloat16

D_MODEL = 1024
N_HEADS = 16
HEAD_DIM = 64
N_KV = 4
GROUP = N_HEADS // N_KV
KV_W = N_KV * HEAD_DIM
CMP_BLOCK = 32
CMP_STRIDE = 16
SEL_BLOCK = 64
N_SEL = 16
WINDOW = 512
D_PHI = 2 * HEAD_DIM
Q_BLOCK = 128
PAGE = 128
D_FF = ((8 * D_MODEL // 3 + 255) // 256) * 256
EPS = 1e-6
NEG = -1e30
FORCE_BONUS = 1e3

LANES = 128
AUG_W = 2 * LANES
HALVES_PER_PAGE = PAGE // CMP_STRIDE
ROW_TILES = 2 * KV_W // LANES
VMEM_LIMIT = 56 * 1024 * 1024
KEY_TILE = 2 * LANES
QC_ROWS = 16
SEL_TILES_PER_ITER = 4

_C_CONV = 0
_C_QKV = 3 * D_MODEL
_C_GATE = _C_QKV + N_HEADS * HEAD_DIM + 6 * KV_W
_C_MERGE = _C_GATE + 3 * N_HEADS
_QKV_COLS = _C_MERGE - _C_QKV
_QKV_COLS_PAD = -(-_QKV_COLS // LANES) * LANES


def _dot(a, b):
    return jnp.dot(a, b, preferred_element_type=F32)


def _dot_nt(a, b):
    return lax.dot_general(a, b, (((1,), (1,)), ((), ())), preferred_element_type=F32)


def _params(n_axes):
    return pltpu.CompilerParams(dimension_semantics=("arbitrary",) * n_axes, vmem_limit_bytes=VMEM_LIMIT)


def _const_spec(shape):
    return pl.BlockSpec(shape, lambda *_: (0,) * len(shape))


def _mod_norm(x, nw, sc, sh):
    inv = lax.rsqrt(jnp.mean(x * x, axis=-1, keepdims=True) + EPS)
    return (x * inv) * nw * (1.0 + sc) + sh


def _split3(x):
    a = x.astype(BF16)
    r = x - a.astype(F32)
    b = r.astype(BF16)
    c = (r - b.astype(F32)).astype(BF16)
    return a, b, c


def _ada_kernel(c_ref, w_ref, b_ref, o_ref):
    c = c_ref[...]
    s = c * jax.nn.sigmoid(c)
    o_ref[...] = jnp.dot(s, w_ref[...], preferred_element_type=F32, precision=lax.Precision.HIGHEST) + b_ref[...]


def _ada(c, w_ada, b_ada):
    n = c.shape[0]
    tn = 1536
    return pl.pallas_call(
        _ada_kernel,
        grid=(6 * D_MODEL // tn,),
        in_specs=[_const_spec((n, D_MODEL)),
                  pl.BlockSpec((D_MODEL, tn), lambda j: (0, j)),
                  pl.BlockSpec((1, tn), lambda j: (0, j))],
        out_specs=pl.BlockSpec((n, tn), lambda j: (0, j)),
        out_shape=jax.ShapeDtypeStruct((n, 6 * D_MODEL), F32),
        compiler_params=_params(1),
        name="ada",
    )(c, w_ada, b_ada.reshape(1, -1))


class _Mod:
    def __init__(self, mod, per_token, seq_len, tm):
        self.per_token = per_token
        if per_token:
            self.rows = jnp.repeat(mod, seq_len, axis=0)
        else:
            self.rows = mod.reshape(mod.shape[0], 1, 6 * D_MODEL)
        self.tiles_per_seq = None if per_token else seq_len // tm
        self.tm = tm

    def spec(self, k):
        if self.per_token:
            return pl.BlockSpec((self.tm, D_MODEL), lambda i: (i, k))
        tps = self.tiles_per_seq
        return pl.BlockSpec((None, 1, D_MODEL), lambda i: (i // tps, 0, k))


def _conv_kernel(*refs, carry_rows, seq_len):
    if carry_rows:
        (x_ref, nw_ref, sc_ref, sh_ref, w_ref, wc_ref, bc_ref, a_ref, tail_ref, carry_ref) = refs
    else:
        (x_ref, nw_ref, sc_ref, sh_ref, w_ref, wc_ref, bc_ref, p1_ref, p2_ref, a_ref, tail_ref) = refs
    tm = x_ref.shape[0]
    h = _mod_norm(x_ref[...], nw_ref[...], sc_ref[...], sh_ref[...]).astype(BF16)
    z = _dot(h, w_ref[...])
    bg = z[:, 0:D_MODEL]
    u = z[:, D_MODEL:2 * D_MODEL] * z[:, 2 * D_MODEL:3 * D_MODEL]
    row = lax.broadcasted_iota(jnp.int32, (tm, 1), 0)
    u1 = pltpu.roll(u, 1, 0)
    u2 = pltpu.roll(u, 2, 0)
    if carry_rows:
        @pl.when(pl.program_id(0) % carry_rows == 0)
        def _():
            carry_ref[...] = jnp.zeros_like(carry_ref)
        c0 = carry_ref[0:1, :]
        c1 = carry_ref[1:2, :]
        u1 = jnp.where(row == 0, c1, u1)
        u2 = jnp.where(row == 0, c0, jnp.where(row == 1, c1, u2))
        carry_ref[0:2, :] = u[tm - 2:tm, :]
        tail_ref[...] = u[tm - 8:tm, :]
    else:
        pos = lax.rem(row, seq_len)
        u1 = jnp.where(pos >= 1, u1, p1_ref[...])
        u2 = jnp.where(pos >= 2, u2, p2_ref[...])
        tail_ref[...] = u
    v = bc_ref[...] + wc_ref[0:1, :] * u2 + wc_ref[1:2, :] * u1 + wc_ref[2:3, :] * u
    a_ref[...] = (bg * v).astype(BF16)


def _conv_path(x2d, mod, nw, w_conv_in, w_conv, b_conv, *, tm, seq_len, prev=None):
    n = x2d.shape[0]
    tok = pl.BlockSpec((tm, D_MODEL), lambda i: (i, 0))
    in_specs = [tok, _const_spec((1, D_MODEL)), mod.spec(1), mod.spec(0),
                _const_spec((D_MODEL, 3 * D_MODEL)), _const_spec((3, D_MODEL)), _const_spec((1, D_MODEL))]
    args = [x2d, nw, mod.rows, mod.rows, w_conv_in, w_conv, b_conv]
    if prev is None:
        tps = seq_len // tm
        out_specs = [tok, pl.BlockSpec((None, 8, D_MODEL), lambda i: (i // tps, 0, 0))]
        out_shape = [jax.ShapeDtypeStruct((n, D_MODEL), BF16), jax.ShapeDtypeStruct((n // seq_len, 8, D_MODEL), F32)]
        scratch = [pltpu.VMEM((8, D_MODEL), F32)]
        kern = functools.partial(_conv_kernel, carry_rows=tps, seq_len=seq_len)
    else:
        in_specs += [tok, tok]
        args += list(prev)
        out_specs = [tok, tok]
        out_shape = [jax.ShapeDtypeStruct((n, D_MODEL), BF16), jax.ShapeDtypeStruct((n, D_MODEL), F32)]
        scratch = []
        kern = functools.partial(_conv_kernel, carry_rows=0, seq_len=seq_len)
    return pl.pallas_call(kern, grid=(n // tm,), in_specs=in_specs, out_specs=out_specs, out_shape=out_shape,
                          scratch_shapes=scratch, compiler_params=_params(1), name="conv_path")(*args)


def _qkv_kernel(*refs, aug):
    if aug:
        (x_ref, nw_ref, sc_ref, sh_ref, w_ref, clo_ref, chi_ref,
         q_ref, kvc_ref, kvs_ref, kvw_ref, g_ref, ksa_ref, vsa_ref, kwa_ref, vwa_ref) = refs
    else:
        (x_ref, nw_ref, sc_ref, sh_ref, w_ref, q_ref, kvc_ref, kvs_ref, kvw_ref, g_ref) = refs
    tm = x_ref.shape[0]
    h = _mod_norm(x_ref[...], nw_ref[...], sc_ref[...], sh_ref[...]).astype(BF16)
    z = _dot(h, w_ref[...])
    low = lax.broadcasted_iota(jnp.int32, (tm, LANES), 1) < HEAD_DIM
    nq = N_HEADS * HEAD_DIM
    for c in range(N_HEADS // 2):
        t = z[:, c * LANES:(c + 1) * LANES] * (HEAD_DIM ** -0.5)
        if aug:
            tt = t.T.astype(BF16)
            q_ref[2 * c] = tt[0:HEAD_DIM]
            q_ref[2 * c + 1] = tt[HEAD_DIM:2 * HEAD_DIM]
        else:
            q_ref[:, (2 * c) * LANES:(2 * c + 1) * LANES] = jnp.where(low, t, 0.0).astype(BF16)
            q_ref[:, (2 * c + 1) * LANES:(2 * c + 2) * LANES] = jnp.where(low, pltpu.roll(t, HEAD_DIM, 1), 0.0).astype(BF16)
    kvc_ref[...] = z[:, nq:nq + 2 * KV_W]
    kvs_ref[...] = z[:, nq + 2 * KV_W:nq + 4 * KV_W]
    kvw_ref[...] = z[:, nq + 4 * KV_W:nq + 6 * KV_W]
    g_ref[...] = jax.nn.sigmoid(z[:, nq + 6 * KV_W:nq + 6 * KV_W + LANES])
    if aug:
        chi = chi_ref[...]
        ones_row = (lax.broadcasted_iota(jnp.int32, (LANES - HEAD_DIM, KEY_TILE), 0) == 0).astype(BF16)
        for br, (ka_ref, va_ref) in enumerate(((ksa_ref, vsa_ref), (kwa_ref, vwa_ref))):
            kbase = nq + 2 * KV_W * (br + 1)
            clo = clo_ref[...] if br == 0 else 0.0
            for g in range(N_KV):
                kt = z[:, kbase + (g // 2) * LANES:kbase + (g // 2 + 1) * LANES]
                if g % 2:
                    kt = pltpu.roll(kt, HEAD_DIM, 1)
                ka_ref[g, :, 0:LANES] = jnp.where(low, kt, clo).astype(BF16)
                ka_ref[g, :, LANES:AUG_W] = chi
            for c in range(N_KV // 2):
                vt = z[:, kbase + KV_W + c * LANES:kbase + KV_W + (c + 1) * LANES].T.astype(BF16)
                for gg in range(2):
                    for j in range(tm // KEY_TILE):
                        va_ref[2 * c + gg, j, 0:HEAD_DIM, :] = vt[gg * HEAD_DIM:(gg + 1) * HEAD_DIM, j * KEY_TILE:(j + 1) * KEY_TILE]
                        va_ref[2 * c + gg, j, HEAD_DIM:LANES, :] = ones_row


def _qkv_path(x2d, mod, nw, w_qkv, *, tm, seq_len, consts=None):
    n = x2d.shape[0]
    aug = consts is not None
    tok = lambda w: pl.BlockSpec((tm, w), lambda i: (i, 0))
    in_specs = [tok(D_MODEL), _const_spec((1, D_MODEL)), mod.spec(1), mod.spec(0),
                _const_spec((D_MODEL, _QKV_COLS_PAD))]
    args = [x2d, nw, mod.rows, mod.rows, w_qkv]
    out_specs = [tok(N_HEADS * LANES), tok(2 * KV_W), tok(2 * KV_W), tok(2 * KV_W), tok(LANES)]
    out_shape = [jax.ShapeDtypeStruct((n, N_HEADS * LANES), BF16)] + \
                [jax.ShapeDtypeStruct((n, 2 * KV_W), F32)] * 3 + [jax.ShapeDtypeStruct((n, LANES), F32)]
    if aug:
        tps = seq_len // tm
        nb = n // seq_len
        pos = lambda w: pl.BlockSpec((tm, w), lambda i: (i % tps, 0))
        in_specs += [pos(LANES), pos(LANES)]
        args += list(consts)
        out_specs[0] = pl.BlockSpec((None, N_HEADS, HEAD_DIM, tm), lambda i: (i // tps, 0, 0, i % tps))
        out_shape[0] = jax.ShapeDtypeStruct((nb, N_HEADS, HEAD_DIM, seq_len), BF16)
        ka = pl.BlockSpec((None, N_KV, tm, AUG_W), lambda i: (i // tps, 0, i % tps, 0))
        va = pl.BlockSpec((None, N_KV, tm // KEY_TILE, LANES, KEY_TILE), lambda i: (i // tps, 0, i % tps, 0, 0))
        out_specs += [ka, va, ka, va]
        ka_s = jax.ShapeDtypeStruct((nb, N_KV, seq_len, AUG_W), BF16)
        va_s = jax.ShapeDtypeStruct((nb, N_KV, seq_len // KEY_TILE, LANES, KEY_TILE), BF16)
        out_shape += [ka_s, va_s, ka_s, va_s]
    return pl.pallas_call(functools.partial(_qkv_kernel, aug=aug), grid=(n // tm,), in_specs=in_specs,
                          out_specs=out_specs, out_shape=out_shape, compiler_params=_params(1), name="qkv_path")(*args)


def _row_view_loader(ref, e):
    c = e // 2

    def load(p, nrows):
        r0 = p * ROW_TILES + c
        if nrows == 1:
            return ref[r0:r0 + 1, :]
        return ref[pl.ds(r0, nrows, stride=CMP_STRIDE * ROW_TILES), :]
    return load


def _token_major_loader(ref, k, e):
    c = e // 2

    def load(p, nrows):
        if nrows == 1:
            return ref[k, c, p:p + 1, :]
        return ref[k, c, pl.ds(p, nrows, stride=CMP_STRIDE), :]
    return load


def _gather_half(load, e, nrows):
    odd = e % 2
    low = lax.broadcasted_iota(jnp.int32, (nrows, LANES), 1) < HEAD_DIM
    tiles = []
    for qq in range(CMP_STRIDE // 2):
        a = load(2 * qq, nrows)
        b = load(2 * qq + 1, nrows)
        if odd:
            tiles.append(jnp.where(low, pltpu.roll(a, HEAD_DIM, 1), b))
        else:
            tiles.append(jnp.where(low, a, pltpu.roll(b, HEAD_DIM, 1)))
    return jnp.concatenate(tiles, axis=1)


def _compress_kernel(pt_ref, *refs, pps, transpose_v, token_minor):
    pages = refs[:pps]
    if token_minor:
        nxt_ref, pe_ref, w1_ref, w2_ref, chi_ref, cv_ref, kc_ref, vc_ref, xt_ref = refs[pps:]
        for k, pg in enumerate(list(pages) + [nxt_ref]):
            for c in range(ROW_TILES):
                xt_ref[k, c] = pg[c * LANES:(c + 1) * LANES, :].T
        loaders = lambda e: [_token_major_loader(xt_ref, k, e) for k in range(pps)]
        look_loader = lambda e: _token_major_loader(xt_ref, pps, e)
    else:
        nxt_ref, pe_ref, w1_ref, w2_ref, chi_ref, cv_ref, kc_ref, vc_ref = refs[pps:]
        loaders = lambda e: [_row_view_loader(p, e) for p in pages]
        look_loader = lambda e: _row_view_loader(nxt_ref, e)
    t = pl.program_id(1)
    last = t == pl.num_programs(1) - 1
    nhb = pps * HALVES_PER_PAGE
    n = N_KV * nhb
    row = lax.broadcasted_iota(jnp.int32, (n, 1), 0)
    chi = chi_ref[...]
    for kv in range(2):
        parts = []
        for g in range(N_KV):
            e = kv * N_KV + g
            parts += [_gather_half(ld, e, HALVES_PER_PAGE) for ld in loaders(e)]
        look = [_gather_half(look_loader(kv * N_KV + g), kv * N_KV + g, 1) for g in range(N_KV)]
        extra = jnp.concatenate([pe_ref[kv], jnp.zeros((2, CMP_STRIDE * HEAD_DIM), F32)] + look, axis=0)
        xmat = jnp.concatenate(parts + [extra], axis=0).astype(BF16)
        hab = _dot(xmat, w1_ref[kv])
        ha = hab[0:n, 0:D_PHI]
        hb = hab[0:n, D_PHI:2 * D_PHI]
        pbias = hab[n:n + 1, 0:D_PHI] + hab[n + 1:n + 2, D_PHI:2 * D_PHI]
        hbn = pltpu.roll(hb, n - 1, 0)
        for g in range(N_KV):
            la = jnp.where(last, 0.0, hab[n + 4 + g:n + 5 + g, D_PHI:2 * D_PHI])
            hbn = jnp.where(row == g * nhb + nhb - 1, la, hbn)
        act = jax.nn.gelu(ha + hbn + pbias).astype(BF16)
        out = _dot(act, w2_ref[kv])
        for g in range(N_KV):
            blk = out[g * nhb:(g + 1) * nhb, :]
            if kv == 0:
                kc_ref[g, :, 0:LANES] = blk.astype(BF16)
                kc_ref[g, :, LANES:AUG_W] = chi
            elif transpose_v:
                vc_ref[g, :, :] = (blk + cv_ref[...]).T.astype(BF16)
            else:
                vc_ref[g, :, :] = (blk + cv_ref[...]).astype(BF16)


def _compress(pages, page_table, pe2, w1cat, w2pad, chi_c, cv, *, pps, transpose_v, token_minor):
    nb, npages = page_table.shape
    nchunks = npages // pps
    nhb = pps * HALVES_PER_PAGE
    ncb = npages * HALVES_PER_PAGE

    def page_spec(k):
        return pl.BlockSpec((None, PAGE * ROW_TILES, LANES), lambda b, t, pt: (pt[b, t * pps + k], 0, 0))

    nxt_rows = PAGE * ROW_TILES if token_minor else CMP_STRIDE * ROW_TILES
    nxt_spec = pl.BlockSpec((None, nxt_rows, LANES),
                            lambda b, t, pt: (pt[b, jnp.minimum((t + 1) * pps, npages - 1)], 0, 0))
    scratch = [pltpu.VMEM((pps + 1, ROW_TILES, PAGE, LANES), F32)] if token_minor else []
    cs = lambda shape: pl.BlockSpec(shape, lambda b, t, pt: (0,) * len(shape))
    in_specs = [page_spec(k) for k in range(pps)] + [
        nxt_spec, cs((2, 2, CMP_STRIDE * HEAD_DIM)), cs((2, CMP_STRIDE * HEAD_DIM, 2 * D_PHI)),
        cs((2, D_PHI, LANES)), pl.BlockSpec((nhb, LANES), lambda b, t, pt: (t, 0)), cs((1, LANES))]
    if transpose_v:
        v_spec = pl.BlockSpec((None, N_KV, LANES, nhb), lambda b, t, pt: (b, 0, 0, t))
        v_shape = jax.ShapeDtypeStruct((nb, N_KV, LANES, ncb), BF16)
    else:
        v_spec = pl.BlockSpec((None, N_KV, nhb, LANES), lambda b, t, pt: (b, 0, t, 0))
        v_shape = jax.ShapeDtypeStruct((nb, N_KV, ncb, LANES), BF16)
    out_specs = [pl.BlockSpec((None, N_KV, nhb, AUG_W), lambda b, t, pt: (b, 0, t, 0)), v_spec]
    out_shape = [jax.ShapeDtypeStruct((nb, N_KV, ncb, AUG_W), BF16), v_shape]
    gs = pltpu.PrefetchScalarGridSpec(num_scalar_prefetch=1, grid=(nb, nchunks), in_specs=in_specs, out_specs=out_specs,
                                      scratch_shapes=scratch)
    return pl.pallas_call(functools.partial(_compress_kernel, pps=pps, transpose_v=transpose_v, token_minor=token_minor),
                          grid_spec=gs, out_shape=out_shape,
                          compiler_params=_params(2), name="compress")(
        page_table, *([pages] * (pps + 1)), pe2, w1cat, w2pad, chi_c, cv)


def _attn_prompt_t_kernel(q_ref, g_ref, qc_ref, mt_ref, kc_ref, vc_ref, ks_ref, vs_ref, kw_ref, vw_ref, o_ref,
                          qa_ref, qw_ref, m_ref, acc_ref):
    nc = kc_ref.shape[0]
    ns = mt_ref.shape[0]
    kt = vs_ref.shape[2]
    ncol = GROUP * Q_BLOCK
    qb = pl.program_id(2)
    q0 = qb * Q_BLOCK
    qpos = q0 + (lax.broadcasted_iota(jnp.int32, (1, ncol), 1) & (Q_BLOCK - 1))

    for ref in (qw_ref, qa_ref):
        for r in range(GROUP):
            ref[0:HEAD_DIM, r * Q_BLOCK:(r + 1) * Q_BLOCK] = q_ref[r]
        ref[HEAD_DIM:2 * HEAD_DIM, :] = jnp.zeros((HEAD_DIM, ncol), BF16)
        ref[2 * HEAD_DIM:2 * HEAD_DIM + QC_ROWS, :] = qc_ref[...]
        ref[2 * HEAD_DIM + QC_ROWS:AUG_W, :] = jnp.zeros((AUG_W - 2 * HEAD_DIM - QC_ROWS, ncol), BF16)

    s = _dot(kc_ref[...], qw_ref[...])
    ci = lax.broadcasted_iota(jnp.int32, (nc, 1), 0)
    c_valid = ci * CMP_STRIDE + (CMP_BLOCK - 1) <= qpos
    s = jnp.where(c_valid, s, NEG)
    e = jnp.exp(s - jnp.max(s, axis=0, keepdims=True))
    p = jnp.where(c_valid, e / jnp.sum(e, axis=0, keepdims=True), 0.0)
    o_c = _dot(vc_ref[...], p.astype(BF16))[0:HEAD_DIM]

    psum = p[:, 0:Q_BLOCK]
    for r in range(1, GROUP):
        psum = psum + p[:, r * Q_BLOCK:(r + 1) * Q_BLOCK]
    mt = mt_ref[...]
    imp = sum(_dot(mt, part) for part in _split3(psum))
    si = lax.broadcasted_iota(jnp.int32, (ns, Q_BLOCK), 0)
    qpos_t = q0 + lax.broadcasted_iota(jnp.int32, (ns, Q_BLOCK), 1)
    cur = lax.shift_right_logical(qpos_t, 6)
    s_valid = si * SEL_BLOCK <= qpos_t
    forced = (si == 0) | (si == cur) | (si == cur - 1)
    score = jnp.where(s_valid, imp + jnp.where(forced, FORCE_BONUS, 0.0), NEG)
    rank = jnp.zeros((ns, Q_BLOCK), jnp.int32)
    for sp in range(ns):
        other = score[sp:sp + 1, :]
        beats = (other > score) | ((other == score) & (si > sp))
        rank = rank + beats.astype(jnp.int32)
    selected = rank < min(N_SEL, ns)
    bias_t = jnp.where(selected, 0.0, NEG).astype(BF16)
    qa_ref[HEAD_DIM:HEAD_DIM + ns, :] = jnp.concatenate([bias_t] * GROUP, axis=1)

    t_hi = lax.div(q0, kt)
    key_iota = lax.broadcasted_iota(jnp.int32, (kt, 1), 0)

    n_back = WINDOW // kt
    s_w, v_w = [], []
    for j in range(n_back + 1):
        tw = t_hi - n_back + j
        tc = jnp.maximum(tw, 0)
        kpos = tw * kt + key_iota
        valid = kpos >= 0
        if j == 0:
            valid = valid & (qpos - kpos < WINDOW)
        if j == n_back:
            valid = valid & (kpos <= qpos)
        s = _dot(kw_ref[pl.ds(pl.multiple_of(tc * kt, kt), kt), :], qw_ref[...])
        s_w.append(jnp.where(valid, s, NEG))
        v_w.append(vw_ref[tc])
    m_w = functools.reduce(jnp.maximum, [jnp.max(s, axis=0, keepdims=True) for s in s_w])
    acc_w = sum(_dot(v, jnp.exp(s - m_w).astype(BF16)) for s, v in zip(s_w, v_w))
    o_w = acc_w[0:HEAD_DIM] / acc_w[HEAD_DIM:HEAD_DIM + 1]

    m_ref[...] = jnp.full(m_ref.shape, NEG, F32)
    acc_ref[...] = jnp.zeros(acc_ref.shape, F32)

    def sel_tiles(tiles, causal_last):
        qa = qa_ref[...]
        ss = []
        for i, tl in enumerate(tiles):
            k0 = pl.multiple_of(tl * kt, kt)
            s = _dot(ks_ref[pl.ds(k0, kt), :], qa)
            if causal_last and i == len(tiles) - 1:
                s = jnp.where(k0 + key_iota <= qpos, s, NEG)
            ss.append(s)
        m_old = m_ref[...]
        m_new = functools.reduce(jnp.maximum, [m_old] + [jnp.max(s, axis=0, keepdims=True) for s in ss])
        pv = sum(_dot(vs_ref[tl], jnp.exp(s - m_new).astype(BF16)) for tl, s in zip(tiles, ss))
        acc_ref[...] = jnp.exp(m_old - m_new) * acc_ref[...] + pv
        m_ref[...] = m_new

    per_tile = kt // SEL_BLOCK
    lo_blk = jnp.min(jnp.where(selected & (si >= per_tile), si, ns))
    start = jnp.clip(lax.div(lo_blk, per_tile), 1, jnp.maximum(t_hi, 1))
    n_plain = jnp.where(t_hi >= 1, 1 + t_hi - start, 0)
    tile_at = lambda j: jnp.where(j == 0, 0, start + j - 1)
    n_group = lax.div(n_plain, SEL_TILES_PER_ITER)

    def body(i, carry):
        sel_tiles([tile_at(i * SEL_TILES_PER_ITER + k) for k in range(SEL_TILES_PER_ITER)], False)
        return carry

    lax.fori_loop(0, n_group, body, 0)
    rem = n_plain - n_group * SEL_TILES_PER_ITER
    for left in range(SEL_TILES_PER_ITER):
        @pl.when(rem == left)
        def _():
            sel_tiles([tile_at(n_plain - left + k) for k in range(left)] + [t_hi], True)

    acc = acc_ref[...]
    o_s = acc[0:HEAD_DIM] / acc[HEAD_DIM:HEAD_DIM + 1]

    gts = g_ref[...]
    o = gts[0:1] * o_c + gts[1:2] * o_s + gts[2:3] * o_w
    for c in range(GROUP // 2):
        pair = jnp.concatenate([o[:, (2 * c) * Q_BLOCK:(2 * c + 1) * Q_BLOCK],
                                o[:, (2 * c + 1) * Q_BLOCK:(2 * c + 2) * Q_BLOCK]], axis=0)
        o_ref[:, c * LANES:(c + 1) * LANES] = pair.T.astype(BF16)


def _attn_prompt_t(qt, gates_t, qc_t, mt, kc, vct, ks, vst, kw, vwt):
    nb, t = qt.shape[0], qt.shape[3]
    nc = kc.shape[2]
    ns = mt.shape[0]
    ntile, kt = vst.shape[2], vst.shape[4]
    ncol = GROUP * Q_BLOCK
    per_bg = lambda *shape: pl.BlockSpec((None, None) + shape, lambda b, g, i: (b, g) + (0,) * len(shape))
    in_specs = [pl.BlockSpec((None, GROUP, HEAD_DIM, Q_BLOCK), lambda b, g, i: (b, g, 0, i)),
                pl.BlockSpec((None, None, None, 3, ncol), lambda b, g, i: (b, g, i, 0, 0)),
                pl.BlockSpec((None, QC_ROWS, ncol), lambda b, g, i: (g, 0, 0)),
                pl.BlockSpec((ns, nc), lambda b, g, i: (0, 0)),
                per_bg(nc, AUG_W), per_bg(LANES, nc), per_bg(t, AUG_W), per_bg(ntile, LANES, kt),
                per_bg(t, AUG_W), per_bg(ntile, LANES, kt)]
    out_spec = pl.BlockSpec((None, Q_BLOCK, GROUP * HEAD_DIM), lambda b, g, i: (b, i, g))
    scratch = [pltpu.VMEM((AUG_W, ncol), BF16), pltpu.VMEM((AUG_W, ncol), BF16),
               pltpu.VMEM((1, ncol), F32), pltpu.VMEM((LANES, ncol), F32)]
    return pl.pallas_call(_attn_prompt_t_kernel, grid=(nb, N_KV, t // Q_BLOCK),
                          in_specs=in_specs, out_specs=out_spec,
                          out_shape=jax.ShapeDtypeStruct((nb, t, N_HEADS * HEAD_DIM), BF16),
                          scratch_shapes=scratch, compiler_params=_params(3), name="attn_prompt")(
        qt, gates_t, qc_t, mt, kc, vct, ks, vst, kw, vwt)


Q_PAD = 8
ROWS_G = GROUP * Q_PAD
ROWS_S = N_KV * ROWS_G


def _attn_sample_kernel(pt_ref, *refs, pps, past, wbuf, n_new):
    pages = refs[:pps]
    (qa_ref, qbd_ref, g_ref, slope_ref, kc_ref, vc_ref, ms_ref, e_ref, win_ref, ksn_ref, kwn_ref,
     osw_ref, oc_ref, bias_ref, m_ref, acc_ref, ow_ref) = refs[pps:]
    t = pl.program_id(1)
    nchunks = pl.num_programs(1)
    nc = kc_ref.shape[1]
    ns = past // SEL_BLOCK + 1
    row = lax.broadcasted_iota(jnp.int32, (ROWS_S, 1), 0)
    qpos = past + (row & (Q_PAD - 1))
    qposf = qpos.astype(F32)
    slope = slope_ref[...]
    qbd = qbd_ref[...]

    @pl.when(t == 0)
    def _():
        nsp = ms_ref.shape[1]
        rg = lax.broadcasted_iota(jnp.int32, (ROWS_G, 1), 0)
        qpos_g = past + (rg & (Q_PAD - 1))
        r8 = lax.broadcasted_iota(jnp.int32, (Q_PAD, 1), 0)
        qpos8 = past + r8
        si = lax.broadcasted_iota(jnp.int32, (Q_PAD, nsp), 1)
        for g in range(N_KV):
            s = _dot_nt(qa_ref[g], kc_ref[g])
            ci = lax.broadcasted_iota(jnp.int32, (1, nc), 1)
            c_valid = ci * CMP_STRIDE + (CMP_BLOCK - 1) <= qpos_g
            s = jnp.where(c_valid, s, NEG)
            e = jnp.exp(s - jnp.max(s, axis=1, keepdims=True))
            p = jnp.where(c_valid, e / jnp.sum(e, axis=1, keepdims=True), 0.0)
            oc_ref[g * ROWS_G:(g + 1) * ROWS_G, :] = _dot(p.astype(BF16), vc_ref[g])
            psum = p[0:Q_PAD]
            for r in range(1, GROUP):
                psum = psum + p[r * Q_PAD:(r + 1) * Q_PAD]
            ms = ms_ref[...]
            imp = sum(_dot(part, ms) for part in _split3(psum))
            cur = lax.shift_right_logical(qpos8, 6)
            s_valid = (si * SEL_BLOCK <= qpos8) & (si < ns)
            forced = (si == 0) | (si == cur) | (si == cur - 1)
            score = jnp.where(s_valid, imp + jnp.where(forced, FORCE_BONUS, 0.0), NEG)
            rank = jnp.zeros((Q_PAD, nsp), jnp.int32)
            for sp in range(ns):
                other = score[:, sp:sp + 1]
                beats = (other > score) | ((other == score) & (si > sp))
                rank = rank + beats.astype(jnp.int32)
            bias_ref[g] = jnp.where((rank < min(N_SEL, ns)) & (si < ns), 0.0, NEG).astype(BF16)

        kw_t = win_ref[0:KV_W, :].astype(BF16)
        vw_t = win_ref[KV_W:2 * KV_W, :].astype(BF16)
        kpos = past - wbuf + lax.broadcasted_iota(jnp.int32, (1, wbuf), 1)
        s1 = _dot(qbd, kw_t) - slope * (qposf - kpos.astype(F32))
        s1 = jnp.where((kpos <= qpos) & (qpos - kpos < WINDOW) & (kpos >= 0), s1, NEG)
        kn = kwn_ref[:, 0:KV_W].astype(BF16)
        vn = kwn_ref[:, KV_W:2 * KV_W].astype(BF16)
        li = lax.broadcasted_iota(jnp.int32, (1, LANES), 1)
        kposn = past + li
        s2 = _dot_nt(qbd, kn) - slope * (qposf - kposn.astype(F32))
        s2 = jnp.where((li < n_new) & (kposn <= qpos) & (qpos - kposn < WINDOW), s2, NEG)
        m = jnp.maximum(jnp.max(s1, axis=1, keepdims=True), jnp.max(s2, axis=1, keepdims=True))
        p1 = jnp.exp(s1 - m)
        p2 = jnp.exp(s2 - m)
        den = jnp.sum(p1, axis=1, keepdims=True) + jnp.sum(p2, axis=1, keepdims=True)
        ow_ref[...] = (_dot_nt((p1 / den).astype(BF16), vw_t) + _dot((p2 / den).astype(BF16), vn))
        m_ref[...] = jnp.full(m_ref.shape, NEG, F32)
        acc_ref[...] = jnp.zeros(acc_ref.shape, F32)

    def block_bias(emat):
        rows = []
        for g in range(N_KV):
            bt = _dot(bias_ref[g], emat)
            rows += [bt] * GROUP
        return jnp.concatenate(rows, axis=0)

    def update(scores, pv_fns):
        m_old = m_ref[...]
        m_new = functools.reduce(jnp.maximum, [m_old] + [jnp.max(s, axis=1, keepdims=True) for s in scores])
        alpha = jnp.exp(m_old - m_new)
        ps = [jnp.exp(s - m_new) for s in scores]
        ones = functools.reduce(jnp.add, [jnp.sum(p, axis=1, keepdims=True) for p in ps])
        pv = functools.reduce(jnp.add, [f(p.astype(BF16)) for f, p in zip(pv_fns, ps)])
        acc_ref[:, 0:KV_W] = alpha * acc_ref[:, 0:KV_W] + pv
        acc_ref[:, KV_W:KV_W + LANES] = alpha * acc_ref[:, KV_W:KV_W + LANES] + ones
        m_ref[...] = m_new

    bias_chunk = block_bias(e_ref[...])
    li = lax.broadcasted_iota(jnp.int32, (1, PAGE), 1)
    scores, pv_fns = [], []
    for k in range(pps):
        pg = pages[k]
        kpos = (t * pps + k) * PAGE + li
        s = _dot(qbd, pg[0:KV_W, :].astype(BF16)) - slope * (qposf - kpos.astype(F32))
        scores.append(s + bias_chunk[:, k * PAGE:(k + 1) * PAGE])
        pv_fns.append(lambda p, pg=pg: _dot_nt(p, pg[KV_W:2 * KV_W, :].astype(BF16)))
    update(scores, pv_fns)

    @pl.when(t == nchunks - 1)
    def _():
        nsp = ms_ref.shape[1]
        kposn = past + li
        e_new = (lax.broadcasted_iota(jnp.int32, (nsp, PAGE), 0) == past // SEL_BLOCK).astype(BF16)
        s = _dot_nt(qbd, ksn_ref[:, 0:KV_W].astype(BF16)) - slope * (qposf - kposn.astype(F32)) + block_bias(e_new)
        s = jnp.where((li < n_new) & (kposn <= qpos), s, NEG)
        update([s], [lambda p: _dot(p, ksn_ref[:, KV_W:2 * KV_W].astype(BF16))])
        o_s = acc_ref[:, 0:KV_W] / acc_ref[:, KV_W:KV_W + 1]
        gts = g_ref[...]
        osw_ref[...] = gts[:, 1:2] * o_s + gts[:, 2:3] * ow_ref[...]
        oc_ref[...] = gts[:, 0:1] * oc_ref[...]


def _attn_sample(page_table, pages, qa, qbd, gates, slope_rows, kc, vc, ms, emat, win, ksn, kwn, *, pps, n_new):
    nb, npages = page_table.shape
    nchunks = npages // pps
    past = npages * PAGE
    wbuf = win.shape[2]
    nc = kc.shape[2]
    nsp = ms.shape[1]

    def page_spec(k):
        return pl.BlockSpec((None, 2 * KV_W, PAGE), lambda b, t, pt: (pt[b, t * pps + k], 0, 0))

    per_b = lambda *shape: pl.BlockSpec((None,) + shape, lambda b, t, pt: (b,) + (0,) * len(shape))
    cs = lambda *shape: pl.BlockSpec(shape, lambda b, t, pt: (0,) * len(shape))
    in_specs = [page_spec(k) for k in range(pps)] + [
        per_b(N_KV, ROWS_G, AUG_W), per_b(ROWS_S, KV_W), per_b(ROWS_S, 3), cs(ROWS_S, 1),
        per_b(N_KV, nc, AUG_W), per_b(N_KV, nc, LANES), cs(nc, nsp),
        pl.BlockSpec((nsp, pps * PAGE), lambda b, t, pt: (0, t)),
        per_b(2 * KV_W, wbuf), per_b(LANES, 2 * KV_W), per_b(LANES, 2 * KV_W)]
    out_specs = [per_b(ROWS_S, KV_W), per_b(ROWS_S, LANES)]
    out_shape = [jax.ShapeDtypeStruct((nb, ROWS_S, KV_W), F32), jax.ShapeDtypeStruct((nb, ROWS_S, LANES), F32)]
    scratch = [pltpu.VMEM((N_KV, Q_PAD, nsp), BF16), pltpu.VMEM((ROWS_S, 1), F32),
               pltpu.VMEM((ROWS_S, KV_W + LANES), F32), pltpu.VMEM((ROWS_S, KV_W), F32)]
    gs = pltpu.PrefetchScalarGridSpec(num_scalar_prefetch=1, grid=(nb, nchunks), in_specs=in_specs,
                                      out_specs=out_specs, scratch_shapes=scratch)
    return pl.pallas_call(functools.partial(_attn_sample_kernel, pps=pps, past=past, wbuf=wbuf, n_new=n_new), grid_spec=gs,
                          out_shape=out_shape, compiler_params=_params(2), name="attn_sample")(
        page_table, *([pages] * pps), qa, qbd, gates, slope_rows, kc, vc, ms, emat, win, ksn, kwn)


def _outproj_kernel(x_ref, a_ref, o_ref, nw_ref, sc_ref, sh_ref, g1_ref, wm_ref, woc_ref, won_ref, wo_ref, x1_ref):
    x = x_ref[...]
    h = _mod_norm(x, nw_ref[...], sc_ref[...], sh_ref[...]).astype(BF16)
    mg = jax.nn.sigmoid(_dot(h, wm_ref[...]))
    y_a = _dot(a_ref[...], woc_ref[...])
    y_b = _dot(o_ref[...], won_ref[...])
    mix = (mg[:, 0:D_MODEL] * y_a + mg[:, D_MODEL:2 * D_MODEL] * y_b).astype(BF16)
    x1_ref[...] = x + g1_ref[...] * _dot(mix, wo_ref[...])


def _outproj(x2d, a, o, mod, nw, w_merge, w_oc, w_on, w_o, *, tm):
    n = x2d.shape[0]
    tok = pl.BlockSpec((tm, D_MODEL), lambda i: (i, 0))
    sq = _const_spec((D_MODEL, D_MODEL))
    return pl.pallas_call(
        _outproj_kernel, grid=(n // tm,),
        in_specs=[tok, tok, tok, _const_spec((1, D_MODEL)), mod.spec(1), mod.spec(0), mod.spec(2),
                  _const_spec((D_MODEL, 2 * D_MODEL)), sq, sq, sq],
        out_specs=tok, out_shape=jax.ShapeDtypeStruct((n, D_MODEL), F32),
        compiler_params=_params(1), name="outproj",
    )(x2d, a, o, nw, mod.rows, mod.rows, mod.rows, w_merge, w_oc, w_on, w_o)


FF_CHUNK = D_FF // 2


def _ffn_kernel(x_ref, nw_ref, sc_ref, sh_ref, g2_ref, nf_ref, wg_ref, wu_ref, wd_ref, y_ref):
    x = x_ref[...]
    h = _mod_norm(x, nw_ref[...], sc_ref[...], sh_ref[...]).astype(BF16)
    acc = jnp.zeros(x.shape, F32)
    for c in range(D_FF // FF_CHUNK):
        sl = slice(c * FF_CHUNK, (c + 1) * FF_CHUNK)
        gate = _dot(h, wg_ref[:, sl])
        up = _dot(h, wu_ref[:, sl])
        act = (gate * jax.nn.sigmoid(gate) * up).astype(BF16)
        acc = acc + _dot(act, wd_ref[sl, :])
    x2 = x + g2_ref[...] * acc
    inv = lax.rsqrt(jnp.mean(x2 * x2, axis=-1, keepdims=True) + EPS)
    y_ref[...] = (x2 * inv) * nf_ref[...]


def _ffn(x1, mod, nw2, nf, w_gate, w_up, w_down, *, tm):
    n = x1.shape[0]
    tok = pl.BlockSpec((tm, D_MODEL), lambda i: (i, 0))
    vec = _const_spec((1, D_MODEL))
    return pl.pallas_call(
        _ffn_kernel, grid=(n // tm,),
        in_specs=[tok, vec, mod.spec(4), mod.spec(3), mod.spec(5), vec,
                  _const_spec((D_MODEL, D_FF)), _const_spec((D_MODEL, D_FF)), _const_spec((D_FF, D_MODEL))],
        out_specs=tok, out_shape=jax.ShapeDtypeStruct((n, D_MODEL), F32),
        compiler_params=_params(1), name="ffn",
    )(x1, nw2, mod.rows, mod.rows, mod.rows, nf, w_gate, w_up, w_down)


def _slopes():
    return 2.0 ** (-8.0 * jnp.arange(1, N_HEADS + 1, dtype=F32) / N_HEADS)


def _slope_lanes(slopes):
    parts = _split3(slopes)
    cols = jnp.stack([parts[0], parts[0], parts[1], parts[1], parts[2], parts[2]], axis=1)
    return jnp.pad(cols, ((0, 0), (0, LANES - 6)))


def _pos_lanes(pos_hi, pos_lo):
    cols = jnp.stack([pos_hi, pos_lo] * 3, axis=1).astype(F32)
    return jnp.pad(cols, ((0, 0), (0, LANES - 6))).astype(BF16)


def _token_consts(t):
    pos = jnp.arange(t, dtype=jnp.int32)
    onehot = (pos[:, None] // SEL_BLOCK == jnp.arange(HEAD_DIM, dtype=jnp.int32)[None, :]).astype(F32)
    clo = jnp.concatenate([jnp.zeros((t, HEAD_DIM), F32), onehot], axis=1)
    chi = _pos_lanes((pos // SEL_BLOCK) * SEL_BLOCK, pos % SEL_BLOCK)
    return clo, chi


def _cmp_consts(ncb):
    ci = jnp.arange(ncb, dtype=jnp.int32) * CMP_STRIDE
    return _pos_lanes((ci // SEL_BLOCK) * SEL_BLOCK, ci % SEL_BLOCK)


def _ones_lane():
    return (jnp.arange(LANES) == HEAD_DIM).astype(F32).reshape(1, LANES)


def _imp_matrix(nc, ns_pad):
    c = jnp.arange(nc, dtype=jnp.int32)[:, None]
    s = jnp.arange(ns_pad, dtype=jnp.int32)[None, :]
    per = SEL_BLOCK // CMP_STRIDE
    return ((c // per == s) | ((c % per == per - 1) & (c // per == s - 1))).astype(BF16)


def _prep_weights(w_in, w_phi1, w_phi2, pe_cmp):
    wb = w_in.astype(BF16)
    w_conv_in = wb[:, _C_CONV:_C_QKV]
    w_qkv = jnp.pad(wb[:, _C_QKV:_C_MERGE], ((0, 0), (0, _QKV_COLS_PAD - _QKV_COLS)))
    w_merge = wb[:, _C_MERGE:]
    half = CMP_STRIDE * HEAD_DIM
    w1cat = jnp.concatenate([w_phi1[:, :half], w_phi1[:, half:]], axis=2).astype(BF16)
    w2pad = jnp.pad(w_phi2, ((0, 0), (0, 0), (0, LANES - HEAD_DIM))).astype(BF16)
    pe2 = pe_cmp.reshape(2, 2, half)
    return w_conv_in, w_qkv, w_merge, w1cat, w2pad, pe2


def _prompt_layer(x, mod_p, wts):
    (nw1, nw2, nf, w_conv_in, w_qkv, w_merge, w1cat, w2pad, pe2, w_conv, b_conv, w_oc, w_on, w_o,
     w_gate, w_up, w_down, slopes) = wts
    nb, t, _ = x.shape
    tm = min(512, t)
    x2d = x.reshape(nb * t, D_MODEL)
    mod = _Mod(mod_p, False, t, tm)
    a, tail = _conv_path(x2d, mod, nw1, w_conv_in, w_conv, b_conv, tm=tm, seq_len=t)
    clo, chi = _token_consts(t)
    cv = _ones_lane()
    qt, kvc, kvs, kvw, gts, ksa, vst, kwa, vwt = _qkv_path(x2d, mod, nw1, w_qkv, tm=tm, seq_len=t, consts=(clo, chi))
    npages = t // PAGE
    pt = jnp.arange(nb * npages, dtype=jnp.int32).reshape(nb, npages)
    nc = t // CMP_STRIDE
    ns = t // SEL_BLOCK
    nqb = t // Q_BLOCK
    kca, vct = _compress(kvc.reshape(nb * npages, PAGE * ROW_TILES, LANES), pt, pe2, w1cat, w2pad, _cmp_consts(nc), cv,
                         pps=min(16, npages), transpose_v=True, token_minor=False)
    gates_t = gts[:, :3 * N_HEADS].reshape(nb, nqb, Q_BLOCK, N_KV, GROUP, 3).transpose(0, 3, 1, 5, 4, 2)
    gates_t = gates_t.reshape(nb, N_KV, nqb, 3, GROUP * Q_BLOCK)
    qc_t = _slope_lanes(slopes)[:, :QC_ROWS].reshape(N_KV, GROUP, QC_ROWS).transpose(0, 2, 1)
    qc_t = jnp.repeat(qc_t, Q_BLOCK, axis=2)
    mt = _imp_matrix(nc, ns).T
    o = _attn_prompt_t(qt, gates_t, qc_t, mt, kca, vct, ksa, vst, kwa, vwt)
    x1 = _outproj(x2d, a, o.reshape(nb * t, D_MODEL), mod, nw1, w_merge, w_oc, w_on, w_o, tm=tm)
    y = _ffn(x1, mod, nw2, nf, w_gate, w_up, w_down, tm=tm)
    keep = min(WINDOW, t)
    state = (kvc.reshape(nb, t, 2, N_KV, HEAD_DIM), kvs.reshape(nb, t, 2, N_KV, HEAD_DIM),
             kvw.reshape(nb, t, 2 * KV_W)[:, t - keep:].reshape(nb, keep, 2, N_KV, HEAD_DIM),
             tail[:, 8 - 2:, :])
    return y.reshape(nb, t, D_MODEL), state


def _sample_layer(x, mod_s, wts, cache_cmp, cache_sel, cache_win, state_conv, page_table):
    (nw1, nw2, nf, w_conv_in, w_qkv, w_merge, w1cat, w2pad, pe2, w_conv, b_conv, w_oc, w_on, w_o,
     w_gate, w_up, w_down, slopes) = wts
    nb, s, _ = x.shape
    n = nb * s
    x2d = x.reshape(n, D_MODEL)
    mod = _Mod(mod_s, True, s, n)
    tpos = jnp.arange(s)
    p1 = jnp.broadcast_to(state_conv[:, 1:2, :], (nb, s, D_MODEL)).reshape(n, D_MODEL)
    p2 = state_conv[:, jnp.minimum(tpos, 1), :].reshape(n, D_MODEL)
    a, u = _conv_path(x2d, mod, nw1, w_conv_in, w_conv, b_conv, tm=n, seq_len=s, prev=(p1, p2))
    qpad, kvc, kvs, kvw, gts = _qkv_path(x2d, mod, nw1, w_qkv, tm=n, seq_len=s)

    npages = page_table.shape[1]
    past = npages * PAGE
    cv = _ones_lane()
    nc = past // CMP_STRIDE
    token_minor = lambda c: c.transpose(0, 2, 3, 4, 1).reshape(c.shape[0], 2 * KV_W, c.shape[1])
    kca, vca = _compress(token_minor(cache_cmp), page_table, pe2, w1cat, w2pad, _cmp_consts(nc), cv,
                         pps=min(16, npages), transpose_v=False, token_minor=True)

    qh = qpad.reshape(nb, s, N_KV, GROUP, LANES).transpose(0, 2, 3, 1, 4)
    qh = jnp.pad(qh, ((0, 0), (0, 0), (0, 0), (0, Q_PAD - s), (0, 0)))
    sl = jnp.broadcast_to(_slope_lanes(slopes).reshape(1, N_KV, GROUP, 1, LANES), qh.shape)
    qa = jnp.concatenate([qh, sl], axis=-1).reshape(nb, N_KV, ROWS_G, AUG_W)
    eye = jnp.eye(N_KV, dtype=BF16)
    qbd = (qh[..., None, :HEAD_DIM] * eye[None, :, None, None, :, None]).reshape(nb, ROWS_S, KV_W)
    gates = gts[:, :3 * N_HEADS].reshape(nb, s, N_KV, GROUP, 3).transpose(0, 2, 3, 1, 4)
    gates = jnp.pad(gates, ((0, 0), (0, 0), (0, 0), (0, Q_PAD - s), (0, 0))).reshape(nb, ROWS_S, 3)
    slope_rows = jnp.repeat(slopes, Q_PAD).reshape(ROWS_S, 1)
    ns = past // SEL_BLOCK + 1
    nsp = -(-ns // LANES) * LANES
    ms = _imp_matrix(nc, nsp)
    tok = jnp.arange(past, dtype=jnp.int32)
    emat = (jnp.arange(nsp, dtype=jnp.int32)[:, None] == tok[None, :] // SEL_BLOCK).astype(BF16)
    pad_rows = lambda r: jnp.pad(r.reshape(nb, s, 2 * KV_W), ((0, 0), (0, LANES - s), (0, 0)))
    osw, ocg = _attn_sample(page_table, token_minor(cache_sel), qa, qbd, gates, slope_rows, kca, vca,
                            ms, emat, token_minor(cache_win), pad_rows(kvs), pad_rows(kvw),
                            pps=min(8, npages), n_new=s)
    osw = osw.reshape(nb, N_KV, GROUP, Q_PAD, N_KV, HEAD_DIM)
    o_sw = jnp.einsum('bgrqgd->bqgrd', osw)
    o_c = ocg.reshape(nb, N_KV, GROUP, Q_PAD, LANES)[..., :HEAD_DIM].transpose(0, 3, 1, 2, 4)
    o = (o_sw + o_c)[:, :s].reshape(n, N_HEADS * HEAD_DIM).astype(BF16)

    x1 = _outproj(x2d, a, o, mod, nw1, w_merge, w_oc, w_on, w_o, tm=n)
    y = _ffn(x1, mod, nw2, nf, w_gate, w_up, w_down, tm=n)
    kv5 = lambda r: r.reshape(nb, s, 2, N_KV, HEAD_DIM)
    win = jnp.concatenate([cache_win, kv5(kvw)], axis=1)[:, s:]
    state = (kv5(kvc), kv5(kvs), win, u.reshape(nb, s, D_MODEL)[:, s - 2:])
    return y.reshape(nb, s, D_MODEL), state


def kernel(x_prompt, x_sample, c_prompt, c_sample, cache_cmp, cache_sel, cache_win, state_conv, page_table,
           w_ada, b_ada, norm1, w_in, w_conv, b_conv, w_out_conv, pe_cmp, w_phi1, w_phi2, w_o_nsa, w_out,
           norm2, w_gate, w_up, w_down, norm_f):
    depth = w_ada.shape[0]
    assert depth == 1, "single-layer trunk"
    nbp, nbs = c_prompt.shape[0], c_sample.shape[0]
    slopes = _slopes()
    l = 0
    c_all = jnp.concatenate([c_prompt, c_sample], axis=0)
    c_all = jnp.pad(c_all, ((0, -c_all.shape[0] % 8), (0, 0)))
    mod = _ada(c_all, w_ada[l], b_ada[l])
    w_conv_in, w_qkv, w_merge, w1cat, w2pad, pe2 = _prep_weights(w_in[l], w_phi1[l], w_phi2[l], pe_cmp[l])
    row = lambda v: v.reshape(1, -1)
    wts = (row(norm1[l]), row(norm2[l]), row(norm_f), w_conv_in, w_qkv, w_merge, w1cat, w2pad, pe2,
           w_conv[l], row(b_conv[l]), w_out_conv[l].astype(BF16), w_o_nsa[l].astype(BF16), w_out[l].astype(BF16),
           w_gate[l].astype(BF16), w_up[l].astype(BF16), w_down[l].astype(BF16), slopes)
    yp, st_p = _prompt_layer(x_prompt, mod[:nbp], wts)
    ys, st_s = _sample_layer(x_sample, mod[nbp:nbp + nbs], wts, cache_cmp[l], cache_sel[l], cache_win[l],
                             state_conv[l], page_table)
    return (yp, ys, st_p[0][None], st_p[1][None], st_p[2][None], st_p[3][None],
            st_s[0][None], st_s[1][None], st_s[2][None], st_s[3][None])
```

```python
import functools

import jax
import jax.numpy as jnp
import numpy as np
from jax import lax
from jax.experimental import pallas as pl
from jax.experimental.pallas import tpu as pltpu

F32 = jnp.float32
BF16 = jnp.bfloat16

D_MODEL = 1024
N_HEADS = 16
HEAD_DIM = 64
N_KV = 4
GROUP = N_HEADS // N_KV
KV_W = N_KV * HEAD_DIM
CMP_BLOCK = 32
CMP_STRIDE = 16
SEL_BLOCK = 64
N_SEL = 16
WINDOW = 512
D_PHI = 2 * HEAD_DIM
Q_BLOCK = 256
PAGE = 128
D_FF = ((8 * D_MODEL // 3 + 255) // 256) * 256
EPS = 1e-6
NEG = -1e30
FORCE_BONUS = 1e3

LANES = 128
AUG_W = 2 * LANES
HALVES_PER_PAGE = PAGE // CMP_STRIDE
ROW_TILES = 2 * KV_W // LANES
VMEM_LIMIT = 56 * 1024 * 1024
KEY_TILE = 2 * LANES
QC_ROWS = 16
SEL_TILES_PER_ITER = 4

_C_CONV = 0
_C_QKV = 3 * D_MODEL
_C_GATE = _C_QKV + N_HEADS * HEAD_DIM + 6 * KV_W
_C_MERGE = _C_GATE + 3 * N_HEADS
_QKV_COLS = _C_MERGE - _C_QKV
_QKV_COLS_PAD = -(-_QKV_COLS // LANES) * LANES


def _dot(a, b):
    return jnp.dot(a, b, preferred_element_type=F32)


def _dot_nt(a, b):
    return lax.dot_general(a, b, (((1,), (1,)), ((), ())), preferred_element_type=F32)


def _params(n_axes):
    return pltpu.CompilerParams(dimension_semantics=("arbitrary",) * n_axes, vmem_limit_bytes=VMEM_LIMIT)


def _const_spec(shape):
    return pl.BlockSpec(shape, lambda *_: (0,) * len(shape))


def _mod_norm(x, nw, sc, sh):
    inv = lax.rsqrt(jnp.mean(x * x, axis=-1, keepdims=True) + EPS)
    return (x * inv) * nw * (1.0 + sc) + sh


def _split3(x):
    a = x.astype(BF16)
    r = x - a.astype(F32)
    b = r.astype(BF16)
    c = (r - b.astype(F32)).astype(BF16)
    return a, b, c


def _ada_kernel(c_ref, w_ref, b_ref, o_ref):
    c = c_ref[...]
    s = c * jax.nn.sigmoid(c)
    o_ref[...] = jnp.dot(s, w_ref[...], preferred_element_type=F32, precision=lax.Precision.HIGHEST) + b_ref[...]


def _ada(c, w_ada, b_ada):
    n = c.shape[0]
    tn = 1536
    return pl.pallas_call(
        _ada_kernel,
        grid=(6 * D_MODEL // tn,),
        in_specs=[_const_spec((n, D_MODEL)),
                  pl.BlockSpec((D_MODEL, tn), lambda j: (0, j)),
                  pl.BlockSpec((1, tn), lambda j: (0, j))],
        out_specs=pl.BlockSpec((n, tn), lambda j: (0, j)),
        out_shape=jax.ShapeDtypeStruct((n, 6 * D_MODEL), F32),
        compiler_params=_params(1),
        name="ada",
    )(c, w_ada, b_ada.reshape(1, -1))


class _Mod:
    def __init__(self, mod, per_token, seq_len, tm):
        self.per_token = per_token
        if per_token:
            self.rows = jnp.repeat(mod, seq_len, axis=0)
        else:
            self.rows = mod.reshape(mod.shape[0], 1, 6 * D_MODEL)
        self.tiles_per_seq = None if per_token else seq_len // tm
        self.tm = tm

    def spec(self, k):
        if self.per_token:
            return pl.BlockSpec((self.tm, D_MODEL), lambda i: (i, k))
        tps = self.tiles_per_seq
        return pl.BlockSpec((None, 1, D_MODEL), lambda i: (i // tps, 0, k))


def _conv_kernel(*refs, carry_rows, seq_len):
    if carry_rows:
        (x_ref, nw_ref, sc_ref, sh_ref, w_ref, wc_ref, bc_ref, a_ref, tail_ref, carry_ref) = refs
    else:
        (x_ref, nw_ref, sc_ref, sh_ref, w_ref, wc_ref, bc_ref, p1_ref, p2_ref, a_ref, tail_ref) = refs
    tm = x_ref.shape[0]
    h = _mod_norm(x_ref[...], nw_ref[...], sc_ref[...], sh_ref[...]).astype(BF16)
    z = _dot(h, w_ref[...])
    bg = z[:, 0:D_MODEL]
    u = z[:, D_MODEL:2 * D_MODEL] * z[:, 2 * D_MODEL:3 * D_MODEL]
    row = lax.broadcasted_iota(jnp.int32, (tm, 1), 0)
    u1 = pltpu.roll(u, 1, 0)
    u2 = pltpu.roll(u, 2, 0)
    if carry_rows:
        @pl.when(pl.program_id(0) % carry_rows == 0)
        def _():
            carry_ref[...] = jnp.zeros_like(carry_ref)
        c0 = carry_ref[0:1, :]
        c1 = carry_ref[1:2, :]
        u1 = jnp.where(row == 0, c1, u1)
        u2 = jnp.where(row == 0, c0, jnp.where(row == 1, c1, u2))
        carry_ref[0:2, :] = u[tm - 2:tm, :]
        tail_ref[...] = u[tm - 8:tm, :]
    else:
        pos = lax.rem(row, seq_len)
        u1 = jnp.where(pos >= 1, u1, p1_ref[...])
        u2 = jnp.where(pos >= 2, u2, p2_ref[...])
        tail_ref[...] = u
    v = bc_ref[...] + wc_ref[0:1, :] * u2 + wc_ref[1:2, :] * u1 + wc_ref[2:3, :] * u
    a_ref[...] = (bg * v).astype(BF16)


def _conv_path(x2d, mod, nw, w_conv_in, w_conv, b_conv, *, tm, seq_len, prev=None):
    n = x2d.shape[0]
    tok = pl.BlockSpec((tm, D_MODEL), lambda i: (i, 0))
    in_specs = [tok, _const_spec((1, D_MODEL)), mod.spec(1), mod.spec(0),
                _const_spec((D_MODEL, 3 * D_MODEL)), _const_spec((3, D_MODEL)), _const_spec((1, D_MODEL))]
    args = [x2d, nw, mod.rows, mod.rows, w_conv_in, w_conv, b_conv]
    if prev is None:
        tps = seq_len // tm
        out_specs = [tok, pl.BlockSpec((None, 8, D_MODEL), lambda i: (i // tps, 0, 0))]
        out_shape = [jax.ShapeDtypeStruct((n, D_MODEL), BF16), jax.ShapeDtypeStruct((n // seq_len, 8, D_MODEL), F32)]
        scratch = [pltpu.VMEM((8, D_MODEL), F32)]
        kern = functools.partial(_conv_kernel, carry_rows=tps, seq_len=seq_len)
    else:
        in_specs += [tok, tok]
        args += list(prev)
        out_specs = [tok, tok]
        out_shape = [jax.ShapeDtypeStruct((n, D_MODEL), BF16), jax.ShapeDtypeStruct((n, D_MODEL), F32)]
        scratch = []
        kern = functools.partial(_conv_kernel, carry_rows=0, seq_len=seq_len)
    return pl.pallas_call(kern, grid=(n // tm,), in_specs=in_specs, out_specs=out_specs, out_shape=out_shape,
                          scratch_shapes=scratch, compiler_params=_params(1), name="conv_path")(*args)


def _qkv_kernel(*refs, aug):
    if aug:
        (x_ref, nw_ref, sc_ref, sh_ref, w_ref, clo_ref, chi_ref,
         q_ref, kvc_ref, kvs_ref, kvw_ref, g_ref, ksa_ref, vsa_ref, kwa_ref, vwa_ref) = refs
    else:
        (x_ref, nw_ref, sc_ref, sh_ref, w_ref, q_ref, kvc_ref, kvs_ref, kvw_ref, g_ref) = refs
    tm = x_ref.shape[0]
    h = _mod_norm(x_ref[...], nw_ref[...], sc_ref[...], sh_ref[...]).astype(BF16)
    z = _dot(h, w_ref[...])
    low = lax.broadcasted_iota(jnp.int32, (tm, LANES), 1) < HEAD_DIM
    nq = N_HEADS * HEAD_DIM
    for c in range(N_HEADS // 2):
        t = z[:, c * LANES:(c + 1) * LANES] * (HEAD_DIM ** -0.5)
        if aug:
            tt = t.T.astype(BF16)
            q_ref[2 * c] = tt[0:HEAD_DIM]
            q_ref[2 * c + 1] = tt[HEAD_DIM:2 * HEAD_DIM]
        else:
            q_ref[:, (2 * c) * LANES:(2 * c + 1) * LANES] = jnp.where(low, t, 0.0).astype(BF16)
            q_ref[:, (2 * c + 1) * LANES:(2 * c + 2) * LANES] = jnp.where(low, pltpu.roll(t, HEAD_DIM, 1), 0.0).astype(BF16)
    kvc_ref[...] = z[:, nq:nq + 2 * KV_W]
    kvs_ref[...] = z[:, nq + 2 * KV_W:nq + 4 * KV_W]
    kvw_ref[...] = z[:, nq + 4 * KV_W:nq + 6 * KV_W]
    g_ref[...] = jax.nn.sigmoid(z[:, nq + 6 * KV_W:nq + 6 * KV_W + LANES])
    if aug:
        chi = chi_ref[...]
        ones_row = (lax.broadcasted_iota(jnp.int32, (LANES - HEAD_DIM, KEY_TILE), 0) == 0).astype(BF16)
        for br, (ka_ref, va_ref) in enumerate(((ksa_ref, vsa_ref), (kwa_ref, vwa_ref))):
            kbase = nq + 2 * KV_W * (br + 1)
            clo = clo_ref[...] if br == 0 else 0.0
            for g in range(N_KV):
                kt = z[:, kbase + (g // 2) * LANES:kbase + (g // 2 + 1) * LANES]
                if g % 2:
                    kt = pltpu.roll(kt, HEAD_DIM, 1)
                ka_ref[g, :, 0:LANES] = jnp.where(low, kt, clo).astype(BF16)
                ka_ref[g, :, LANES:AUG_W] = chi
            for c in range(N_KV // 2):
                vt = z[:, kbase + KV_W + c * LANES:kbase + KV_W + (c + 1) * LANES].T.astype(BF16)
                for gg in range(2):
                    for j in range(tm // KEY_TILE):
                        va_ref[2 * c + gg, j, 0:HEAD_DIM, :] = vt[gg * HEAD_DIM:(gg + 1) * HEAD_DIM, j * KEY_TILE:(j + 1) * KEY_TILE]
                        va_ref[2 * c + gg, j, HEAD_DIM:LANES, :] = ones_row


def _qkv_path(x2d, mod, nw, w_qkv, *, tm, seq_len, consts=None):
    n = x2d.shape[0]
    aug = consts is not None
    tok = lambda w: pl.BlockSpec((tm, w), lambda i: (i, 0))
    in_specs = [tok(D_MODEL), _const_spec((1, D_MODEL)), mod.spec(1), mod.spec(0),
                _const_spec((D_MODEL, _QKV_COLS_PAD))]
    args = [x2d, nw, mod.rows, mod.rows, w_qkv]
    out_specs = [tok(N_HEADS * LANES), tok(2 * KV_W), tok(2 * KV_W), tok(2 * KV_W), tok(LANES)]
    out_shape = [jax.ShapeDtypeStruct((n, N_HEADS * LANES), BF16)] + \
                [jax.ShapeDtypeStruct((n, 2 * KV_W), F32)] * 3 + [jax.ShapeDtypeStruct((n, LANES), F32)]
    if aug:
        tps = seq_len // tm
        nb = n // seq_len
        pos = lambda w: pl.BlockSpec((tm, w), lambda i: (i % tps, 0))
        in_specs += [pos(LANES), pos(LANES)]
        args += list(consts)
        out_specs[0] = pl.BlockSpec((None, N_HEADS, HEAD_DIM, tm), lambda i: (i // tps, 0, 0, i % tps))
        out_shape[0] = jax.ShapeDtypeStruct((nb, N_HEADS, HEAD_DIM, seq_len), BF16)
        ka = pl.BlockSpec((None, N_KV, tm, AUG_W), lambda i: (i // tps, 0, i % tps, 0))
        va = pl.BlockSpec((None, N_KV, tm // KEY_TILE, LANES, KEY_TILE), lambda i: (i // tps, 0, i % tps, 0, 0))
        out_specs += [ka, va, ka, va]
        ka_s = jax.ShapeDtypeStruct((nb, N_KV, seq_len, AUG_W), BF16)
        va_s = jax.ShapeDtypeStruct((nb, N_KV, seq_len // KEY_TILE, LANES, KEY_TILE), BF16)
        out_shape += [ka_s, va_s, ka_s, va_s]
    return pl.pallas_call(functools.partial(_qkv_kernel, aug=aug), grid=(n // tm,), in_specs=in_specs,
                          out_specs=out_specs, out_shape=out_shape, compiler_params=_params(1), name="qkv_path")(*args)


def _row_view_loader(ref, c):
    def load(p, nrows):
        r0 = p * ROW_TILES + c
        if nrows == 1:
            return ref[r0:r0 + 1, :]
        return ref[pl.ds(r0, nrows, stride=CMP_STRIDE * ROW_TILES), :]
    return load


def _block_major_loader(ref, k, c):
    def load(p, nrows):
        return ref[k, c, p * HALVES_PER_PAGE:p * HALVES_PER_PAGE + nrows, :]
    return load


def _gather_pair(load, nrows):
    low = lax.broadcasted_iota(jnp.int32, (nrows, LANES), 1) < HEAD_DIM
    even, odd = [], []
    for qq in range(CMP_STRIDE // 2):
        a = load(2 * qq, nrows)
        b = load(2 * qq + 1, nrows)
        even.append(jnp.where(low, a, pltpu.roll(b, HEAD_DIM, 1)))
        odd.append(jnp.where(low, pltpu.roll(a, HEAD_DIM, 1), b))
    return jnp.concatenate(even, axis=1), jnp.concatenate(odd, axis=1)


def _compress_kernel(pt_ref, *refs, pps, transpose_v, token_minor):
    pages = refs[:pps]
    if token_minor:
        nxt_ref, pe_ref, w1_ref, w2_ref, chi_ref, cv_ref, perm_ref, kc_ref, vc_ref, xt_ref = refs[pps:]
        perm = perm_ref[...]
        for k, pg in enumerate(list(pages) + [nxt_ref]):
            for c in range(ROW_TILES):
                xt_ref[k, c] = _dot_nt(perm, pg[c * LANES:(c + 1) * LANES, :].astype(BF16))
        loaders = lambda c: [_block_major_loader(xt_ref, k, c) for k in range(pps)]
        look_loader = lambda c: _block_major_loader(xt_ref, pps, c)
    else:
        nxt_ref, pe_ref, w1_ref, w2_ref, chi_ref, cv_ref, kc_ref, vc_ref = refs[pps:]
        loaders = lambda c: [_row_view_loader(p, c) for p in pages]
        look_loader = lambda c: _row_view_loader(nxt_ref, c)
    t = pl.program_id(1)
    last = t == pl.num_programs(1) - 1
    nhb = pps * HALVES_PER_PAGE
    n = N_KV * nhb
    row = lax.broadcasted_iota(jnp.int32, (n, 1), 0)
    chi = chi_ref[...]
    for kv in range(2):
        by_group, look = [], []
        for cc in range(N_KV // 2):
            c = kv * (N_KV // 2) + cc
            pairs = [_gather_pair(ld, HALVES_PER_PAGE) for ld in loaders(c)]
            by_group += [[ev for ev, _ in pairs], [od for _, od in pairs]]
            look += list(_gather_pair(look_loader(c), 1))
        parts = [x for group in by_group for x in group]
        extra = jnp.concatenate([pe_ref[kv], jnp.zeros((2, CMP_STRIDE * HEAD_DIM), F32)] + look, axis=0)
        xmat = jnp.concatenate(parts + [extra], axis=0).astype(BF16)
        hab = _dot(xmat, w1_ref[kv])
        ha = hab[0:n, 0:D_PHI]
        hb = hab[0:n, D_PHI:2 * D_PHI]
        pbias = hab[n:n + 1, 0:D_PHI] + hab[n + 1:n + 2, D_PHI:2 * D_PHI]
        hbn = pltpu.roll(hb, n - 1, 0)
        for g in range(N_KV):
            la = jnp.where(last, 0.0, hab[n + 4 + g:n + 5 + g, D_PHI:2 * D_PHI])
            hbn = jnp.where(row == g * nhb + nhb - 1, la, hbn)
        act = jax.nn.gelu(ha + hbn + pbias).astype(BF16)
        out = _dot(act, w2_ref[kv])
        for g in range(N_KV):
            blk = out[g * nhb:(g + 1) * nhb, :]
            if kv == 0:
                kc_ref[g, :, 0:LANES] = blk.astype(BF16)
                kc_ref[g, :, LANES:AUG_W] = chi
            elif transpose_v:
                vc_ref[g, :, :] = (blk + cv_ref[...]).T.astype(BF16)
            else:
                vc_ref[g, :, :] = (blk + cv_ref[...]).astype(BF16)


def _compress(pages, page_table, pe2, w1cat, w2pad, chi_c, cv, *, pps, transpose_v, token_minor):
    nb, npages = page_table.shape
    nchunks = npages // pps
    nhb = pps * HALVES_PER_PAGE
    ncb = npages * HALVES_PER_PAGE

    def page_spec(k):
        return pl.BlockSpec((None, PAGE * ROW_TILES, LANES), lambda b, t, pt: (pt[b, t * pps + k], 0, 0))

    nxt_rows = PAGE * ROW_TILES if token_minor else CMP_STRIDE * ROW_TILES
    nxt_spec = pl.BlockSpec((None, nxt_rows, LANES),
                            lambda b, t, pt: (pt[b, jnp.minimum((t + 1) * pps, npages - 1)], 0, 0))
    scratch = [pltpu.VMEM((pps + 1, ROW_TILES, PAGE, LANES), F32)] if token_minor else []
    r = jnp.arange(PAGE, dtype=jnp.int32)
    perm = (r[None, :] == (CMP_STRIDE * (r % HALVES_PER_PAGE) + r // HALVES_PER_PAGE)[:, None]).astype(BF16)
    cs = lambda shape: pl.BlockSpec(shape, lambda b, t, pt: (0,) * len(shape))
    in_specs = [page_spec(k) for k in range(pps)] + [
        nxt_spec, cs((2, 2, CMP_STRIDE * HEAD_DIM)), cs((2, CMP_STRIDE * HEAD_DIM, 2 * D_PHI)),
        cs((2, D_PHI, LANES)), pl.BlockSpec((nhb, LANES), lambda b, t, pt: (t, 0)), cs((1, LANES))]
    extra_args = []
    if token_minor:
        in_specs.append(cs((PAGE, PAGE)))
        extra_args.append(perm)
    if transpose_v:
        v_spec = pl.BlockSpec((None, N_KV, LANES, nhb), lambda b, t, pt: (b, 0, 0, t))
        v_shape = jax.ShapeDtypeStruct((nb, N_KV, LANES, ncb), BF16)
    else:
        v_spec = pl.BlockSpec((None, N_KV, nhb, LANES), lambda b, t, pt: (b, 0, t, 0))
        v_shape = jax.ShapeDtypeStruct((nb, N_KV, ncb, LANES), BF16)
    out_specs = [pl.BlockSpec((None, N_KV, nhb, AUG_W), lambda b, t, pt: (b, 0, t, 0)), v_spec]
    out_shape = [jax.ShapeDtypeStruct((nb, N_KV, ncb, AUG_W), BF16), v_shape]
    gs = pltpu.PrefetchScalarGridSpec(num_scalar_prefetch=1, grid=(nb, nchunks), in_specs=in_specs, out_specs=out_specs,
                                      scratch_shapes=scratch)
    return pl.pallas_call(functools.partial(_compress_kernel, pps=pps, transpose_v=transpose_v, token_minor=token_minor),
                          grid_spec=gs, out_shape=out_shape,
                          compiler_params=_params(2), name="compress")(
        page_table, *([pages] * (pps + 1)), pe2, w1cat, w2pad, chi_c, cv, *extra_args)


def _attn_prompt_t_kernel(q_ref, g_ref, qc_ref, mt_ref, kc_ref, vc_ref, ks_ref, vs_ref, kw_ref, vw_ref, o_ref,
                          qa_ref, qw_ref, m_ref, acc_ref):
    nc = kc_ref.shape[0]
    ns = mt_ref.shape[0]
    kt = vs_ref.shape[2]
    ncol = GROUP * Q_BLOCK
    qb = pl.program_id(2)
    q0 = qb * Q_BLOCK
    qpos = q0 + (lax.broadcasted_iota(jnp.int32, (1, ncol), 1) & (Q_BLOCK - 1))

    for ref in (qw_ref, qa_ref):
        for r in range(GROUP):
            ref[0:HEAD_DIM, r * Q_BLOCK:(r + 1) * Q_BLOCK] = q_ref[r]
        ref[HEAD_DIM:2 * HEAD_DIM, :] = jnp.zeros((HEAD_DIM, ncol), BF16)
        ref[2 * HEAD_DIM:2 * HEAD_DIM + QC_ROWS, :] = qc_ref[...]
        ref[2 * HEAD_DIM + QC_ROWS:AUG_W, :] = jnp.zeros((AUG_W - 2 * HEAD_DIM - QC_ROWS, ncol), BF16)

    s = _dot(kc_ref[...], qw_ref[...])
    ci = lax.broadcasted_iota(jnp.int32, (nc, 1), 0)
    c_valid = ci * CMP_STRIDE + (CMP_BLOCK - 1) <= qpos
    s = jnp.where(c_valid, s, NEG)
    e = jnp.exp(s - jnp.max(s, axis=0, keepdims=True))
    p = jnp.where(c_valid, e * (1.0 / jnp.sum(e, axis=0, keepdims=True)), 0.0)
    o_c = _dot(vc_ref[...], p.astype(BF16))[0:HEAD_DIM]

    psum = p[:, 0:Q_BLOCK]
    for r in range(1, GROUP):
        psum = psum + p[:, r * Q_BLOCK:(r + 1) * Q_BLOCK]
    mt = mt_ref[...]
    imp = sum(_dot(mt, part) for part in _split3(psum))
    si = lax.broadcasted_iota(jnp.int32, (ns, Q_BLOCK), 0)
    qpos_t = q0 + lax.broadcasted_iota(jnp.int32, (ns, Q_BLOCK), 1)
    cur = lax.shift_right_logical(qpos_t, 6)
    s_valid = si * SEL_BLOCK <= qpos_t
    forced = (si == 0) | (si == cur) | (si == cur - 1)
    score = jnp.where(s_valid, imp + jnp.where(forced, FORCE_BONUS, 0.0), NEG)
    rank = jnp.zeros((ns, Q_BLOCK), jnp.int32)
    for sp in range(ns):
        other = score[sp:sp + 1, :]
        beats = (other > score) | ((other == score) & (si > sp))
        rank = rank + beats.astype(jnp.int32)
    selected = rank < min(N_SEL, ns)
    bias_t = jnp.where(selected, 0.0, NEG).astype(BF16)
    qa_ref[HEAD_DIM:HEAD_DIM + ns, :] = jnp.concatenate([bias_t] * GROUP, axis=1)

    t_hi = lax.div(q0, kt)
    key_iota = lax.broadcasted_iota(jnp.int32, (kt, 1), 0)

    n_back = WINDOW // kt
    s_w, v_w = [], []
    for j in range(n_back + 1):
        tw = t_hi - n_back + j
        tc = jnp.maximum(tw, 0)
        kpos = tw * kt + key_iota
        s = _dot(kw_ref[pl.ds(pl.multiple_of(tc * kt, kt), kt), :], qw_ref[...])
        if j == 0:
            oldest = jnp.where(tw >= 0, qpos - WINDOW, jnp.iinfo(jnp.int32).max)
            s = jnp.where(kpos > oldest, s, NEG)
        elif j == n_back:
            s = jnp.where(kpos <= qpos, s, NEG)
        else:
            s = jnp.where(tw >= 0, s, NEG)
        s_w.append(s)
        v_w.append(vw_ref[tc])
    m_w = functools.reduce(jnp.maximum, [jnp.max(s, axis=0, keepdims=True) for s in s_w])
    acc_w = sum(_dot(v, jnp.exp(s - m_w).astype(BF16)) for s, v in zip(s_w, v_w))
    o_w = acc_w[0:HEAD_DIM] * (1.0 / acc_w[HEAD_DIM:HEAD_DIM + 1])

    m_ref[...] = jnp.full(m_ref.shape, NEG, F32)
    acc_ref[...] = jnp.zeros(acc_ref.shape, F32)

    def sel_tiles(tiles, causal_last):
        qa = qa_ref[...]
        ss = []
        for i, tl in enumerate(tiles):
            k0 = pl.multiple_of(tl * kt, kt)
            s = _dot(ks_ref[pl.ds(k0, kt), :], qa)
            if causal_last and i == len(tiles) - 1:
                s = jnp.where(k0 + key_iota <= qpos, s, NEG)
            ss.append(s)
        m_old = m_ref[...]
        m_new = functools.reduce(jnp.maximum, [m_old] + [jnp.max(s, axis=0, keepdims=True) for s in ss])
        pv = sum(_dot(vs_ref[tl], jnp.exp(s - m_new).astype(BF16)) for tl, s in zip(tiles, ss))
        acc_ref[...] = jnp.exp(m_old - m_new) * acc_ref[...] + pv
        m_ref[...] = m_new

    per_tile = kt // SEL_BLOCK
    lo_blk = jnp.min(jnp.where(selected & (si >= per_tile), si, ns))
    start = jnp.clip(lax.div(lo_blk, per_tile), 1, jnp.maximum(t_hi, 1))
    n_plain = jnp.where(t_hi >= 1, 1 + t_hi - start, 0)
    tile_at = lambda j: jnp.where(j == 0, 0, start + j - 1)
    n_group = lax.div(n_plain, SEL_TILES_PER_ITER)

    def body(i, carry):
        sel_tiles([tile_at(i * SEL_TILES_PER_ITER + k) for k in range(SEL_TILES_PER_ITER)], False)
        return carry

    lax.fori_loop(0, n_group, body, 0)
    rem = n_plain - n_group * SEL_TILES_PER_ITER
    for left in range(SEL_TILES_PER_ITER):
        @pl.when(rem == left)
        def _():
            sel_tiles([tile_at(n_plain - left + k) for k in range(left)] + [t_hi], True)

    acc = acc_ref[...]
    o_s = acc[0:HEAD_DIM] * (1.0 / acc[HEAD_DIM:HEAD_DIM + 1])

    gts = g_ref[...]
    o = gts[0:1] * o_c + gts[1:2] * o_s + gts[2:3] * o_w
    for c in range(GROUP // 2):
        pair = jnp.concatenate([o[:, (2 * c) * Q_BLOCK:(2 * c + 1) * Q_BLOCK],
                                o[:, (2 * c + 1) * Q_BLOCK:(2 * c + 2) * Q_BLOCK]], axis=0)
        o_ref[:, c * LANES:(c + 1) * LANES] = pair.T.astype(BF16)


def _attn_prompt_t(qt, gates_t, qc_t, mt, kc, vct, ks, vst, kw, vwt):
    nb, t = qt.shape[0], qt.shape[3]
    nc = kc.shape[2]
    ns = mt.shape[0]
    ntile, kt = vst.shape[2], vst.shape[4]
    ncol = GROUP * Q_BLOCK
    per_bg = lambda *shape: pl.BlockSpec((None, None) + shape, lambda b, g, i: (b, g) + (0,) * len(shape))
    in_specs = [pl.BlockSpec((None, GROUP, HEAD_DIM, Q_BLOCK), lambda b, g, i: (b, g, 0, i)),
                pl.BlockSpec((None, None, None, 3, ncol), lambda b, g, i: (b, g, i, 0, 0)),
                pl.BlockSpec((None, QC_ROWS, ncol), lambda b, g, i: (g, 0, 0)),
                pl.BlockSpec((ns, nc), lambda b, g, i: (0, 0)),
                per_bg(nc, AUG_W), per_bg(LANES, nc), per_bg(t, AUG_W), per_bg(ntile, LANES, kt),
                per_bg(t, AUG_W), per_bg(ntile, LANES, kt)]
    out_spec = pl.BlockSpec((None, Q_BLOCK, GROUP * HEAD_DIM), lambda b, g, i: (b, i, g))
    scratch = [pltpu.VMEM((AUG_W, ncol), BF16), pltpu.VMEM((AUG_W, ncol), BF16),
               pltpu.VMEM((1, ncol), F32), pltpu.VMEM((LANES, ncol), F32)]
    return pl.pallas_call(_attn_prompt_t_kernel, grid=(nb, N_KV, t // Q_BLOCK),
                          in_specs=in_specs, out_specs=out_spec,
                          out_shape=jax.ShapeDtypeStruct((nb, t, N_HEADS * HEAD_DIM), BF16),
                          scratch_shapes=scratch, compiler_params=_params(3), name="attn_prompt")(
        qt, gates_t, qc_t, mt, kc, vct, ks, vst, kw, vwt)


Q_PAD = 8
ROWS_G = GROUP * Q_PAD
ROWS_S = N_KV * ROWS_G


def _attn_sample_kernel(pt_ref, *refs, pps, past, wbuf, n_new):
    pages = refs[:pps]
    (qa_ref, qbd_ref, g_ref, slope_ref, kc_ref, vc_ref, ms_ref, e_ref, win_ref, ksn_ref, kwn_ref,
     osw_ref, oc_ref, bias_ref, m_ref, acc_ref, ow_ref) = refs[pps:]
    t = pl.program_id(1)
    nchunks = pl.num_programs(1)
    nc = kc_ref.shape[1]
    ns = past // SEL_BLOCK + 1
    row = lax.broadcasted_iota(jnp.int32, (ROWS_S, 1), 0)
    qpos = past + (row & (Q_PAD - 1))
    qposf = qpos.astype(F32)
    slope = slope_ref[...]
    qbd = qbd_ref[...]

    @pl.when(t == 0)
    def _():
        nsp = ms_ref.shape[1]
        rg = lax.broadcasted_iota(jnp.int32, (ROWS_G, 1), 0)
        qpos_g = past + (rg & (Q_PAD - 1))
        r8 = lax.broadcasted_iota(jnp.int32, (Q_PAD, 1), 0)
        qpos8 = past + r8
        si = lax.broadcasted_iota(jnp.int32, (Q_PAD, nsp), 1)
        for g in range(N_KV):
            s = _dot_nt(qa_ref[g], kc_ref[g])
            ci = lax.broadcasted_iota(jnp.int32, (1, nc), 1)
            c_valid = ci * CMP_STRIDE + (CMP_BLOCK - 1) <= qpos_g
            s = jnp.where(c_valid, s, NEG)
            e = jnp.exp(s - jnp.max(s, axis=1, keepdims=True))
            p = jnp.where(c_valid, e * (1.0 / jnp.sum(e, axis=1, keepdims=True)), 0.0)
            oc_ref[g * ROWS_G:(g + 1) * ROWS_G, :] = _dot(p.astype(BF16), vc_ref[g])
            psum = p[0:Q_PAD]
            for r in range(1, GROUP):
                psum = psum + p[r * Q_PAD:(r + 1) * Q_PAD]
            ms = ms_ref[...]
            imp = sum(_dot(part, ms) for part in _split3(psum))
            cur = lax.shift_right_logical(qpos8, 6)
            s_valid = (si * SEL_BLOCK <= qpos8) & (si < ns)
            forced = (si == 0) | (si == cur) | (si == cur - 1)
            score = jnp.where(s_valid, imp + jnp.where(forced, FORCE_BONUS, 0.0), NEG)
            rank = jnp.zeros((Q_PAD, nsp), jnp.int32)
            for sp in range(ns):
                other = score[:, sp:sp + 1]
                beats = (other > score) | ((other == score) & (si > sp))
                rank = rank + beats.astype(jnp.int32)
            bias_ref[g] = jnp.where((rank < min(N_SEL, ns)) & (si < ns), 0.0, NEG).astype(BF16)

        kw_t = win_ref[0:KV_W, :].astype(BF16)
        vw_t = win_ref[KV_W:2 * KV_W, :].astype(BF16)
        kpos = past - wbuf + lax.broadcasted_iota(jnp.int32, (1, wbuf), 1)
        s1 = _dot(qbd, kw_t) - slope * (qposf - kpos.astype(F32))
        s1 = jnp.where((kpos <= qpos) & (qpos - kpos < WINDOW) & (kpos >= 0), s1, NEG)
        kn = kwn_ref[:, 0:KV_W].astype(BF16)
        vn = kwn_ref[:, KV_W:2 * KV_W].astype(BF16)
        li = lax.broadcasted_iota(jnp.int32, (1, LANES), 1)
        kposn = past + li
        s2 = _dot_nt(qbd, kn) - slope * (qposf - kposn.astype(F32))
        s2 = jnp.where((li < n_new) & (kposn <= qpos) & (qpos - kposn < WINDOW), s2, NEG)
        m = jnp.maximum(jnp.max(s1, axis=1, keepdims=True), jnp.max(s2, axis=1, keepdims=True))
        p1 = jnp.exp(s1 - m)
        p2 = jnp.exp(s2 - m)
        den = jnp.sum(p1, axis=1, keepdims=True) + jnp.sum(p2, axis=1, keepdims=True)
        inv = 1.0 / den
        ow_ref[...] = (_dot_nt((p1 * inv).astype(BF16), vw_t) + _dot((p2 * inv).astype(BF16), vn))
        m_ref[...] = jnp.full(m_ref.shape, NEG, F32)
        acc_ref[...] = jnp.zeros(acc_ref.shape, F32)

    def block_bias(emat):
        rows = []
        for g in range(N_KV):
            bt = _dot(bias_ref[g], emat)
            rows += [bt] * GROUP
        return jnp.concatenate(rows, axis=0)

    def update(scores, pv_fns):
        m_old = m_ref[...]
        m_new = functools.reduce(jnp.maximum, [m_old] + [jnp.max(s, axis=1, keepdims=True) for s in scores])
        alpha = jnp.exp(m_old - m_new)
        ps = [jnp.exp(s - m_new) for s in scores]
        ones = functools.reduce(jnp.add, [jnp.sum(p, axis=1, keepdims=True) for p in ps])
        pv = functools.reduce(jnp.add, [f(p.astype(BF16)) for f, p in zip(pv_fns, ps)])
        acc_ref[:, 0:KV_W] = alpha * acc_ref[:, 0:KV_W] + pv
        acc_ref[:, KV_W:KV_W + LANES] = alpha * acc_ref[:, KV_W:KV_W + LANES] + ones
        m_ref[...] = m_new

    bias_chunk = block_bias(e_ref[...])
    li = lax.broadcasted_iota(jnp.int32, (1, PAGE), 1)
    scores, pv_fns = [], []
    for k in range(pps):
        pg = pages[k]
        kpos = (t * pps + k) * PAGE + li
        s = _dot(qbd, pg[0:KV_W, :].astype(BF16)) - slope * (qposf - kpos.astype(F32))
        scores.append(s + bias_chunk[:, k * PAGE:(k + 1) * PAGE])
        pv_fns.append(lambda p, pg=pg: _dot_nt(p, pg[KV_W:2 * KV_W, :].astype(BF16)))
    update(scores, pv_fns)

    @pl.when(t == nchunks - 1)
    def _():
        nsp = ms_ref.shape[1]
        kposn = past + li
        e_new = (lax.broadcasted_iota(jnp.int32, (nsp, PAGE), 0) == past // SEL_BLOCK).astype(BF16)
        s = _dot_nt(qbd, ksn_ref[:, 0:KV_W].astype(BF16)) - slope * (qposf - kposn.astype(F32)) + block_bias(e_new)
        s = jnp.where((li < n_new) & (kposn <= qpos), s, NEG)
        update([s], [lambda p: _dot(p, ksn_ref[:, KV_W:2 * KV_W].astype(BF16))])
        o_s = acc_ref[:, 0:KV_W] * (1.0 / acc_ref[:, KV_W:KV_W + 1])
        gts = g_ref[...]
        osw_ref[...] = gts[:, 1:2] * o_s + gts[:, 2:3] * ow_ref[...]
        oc_ref[...] = gts[:, 0:1] * oc_ref[...]


def _attn_sample(page_table, pages, qa, qbd, gates, slope_rows, kc, vc, ms, emat, win, ksn, kwn, *, pps, n_new):
    nb, npages = page_table.shape
    nchunks = npages // pps
    past = npages * PAGE
    wbuf = win.shape[2]
    nc = kc.shape[2]
    nsp = ms.shape[1]

    def page_spec(k):
        return pl.BlockSpec((None, 2 * KV_W, PAGE), lambda b, t, pt: (pt[b, t * pps + k], 0, 0))

    per_b = lambda *shape: pl.BlockSpec((None,) + shape, lambda b, t, pt: (b,) + (0,) * len(shape))
    cs = lambda *shape: pl.BlockSpec(shape, lambda b, t, pt: (0,) * len(shape))
    in_specs = [page_spec(k) for k in range(pps)] + [
        per_b(N_KV, ROWS_G, AUG_W), per_b(ROWS_S, KV_W), per_b(ROWS_S, 3), cs(ROWS_S, 1),
        per_b(N_KV, nc, AUG_W), per_b(N_KV, nc, LANES), cs(nc, nsp),
        pl.BlockSpec((nsp, pps * PAGE), lambda b, t, pt: (0, t)),
        per_b(2 * KV_W, wbuf), per_b(LANES, 2 * KV_W), per_b(LANES, 2 * KV_W)]
    out_specs = [per_b(ROWS_S, KV_W), per_b(ROWS_S, LANES)]
    out_shape = [jax.ShapeDtypeStruct((nb, ROWS_S, KV_W), F32), jax.ShapeDtypeStruct((nb, ROWS_S, LANES), F32)]
    scratch = [pltpu.VMEM((N_KV, Q_PAD, nsp), BF16), pltpu.VMEM((ROWS_S, 1), F32),
               pltpu.VMEM((ROWS_S, KV_W + LANES), F32), pltpu.VMEM((ROWS_S, KV_W), F32)]
    gs = pltpu.PrefetchScalarGridSpec(num_scalar_prefetch=1, grid=(nb, nchunks), in_specs=in_specs,
                                      out_specs=out_specs, scratch_shapes=scratch)
    return pl.pallas_call(functools.partial(_attn_sample_kernel, pps=pps, past=past, wbuf=wbuf, n_new=n_new), grid_spec=gs,
                          out_shape=out_shape, compiler_params=_params(2), name="attn_sample")(
        page_table, *([pages] * pps), qa, qbd, gates, slope_rows, kc, vc, ms, emat, win, ksn, kwn)


def _outproj_kernel(x_ref, a_ref, o_ref, nw_ref, sc_ref, sh_ref, g1_ref, wm_ref, woc_ref, won_ref, wo_ref, x1_ref):
    x = x_ref[...]
    h = _mod_norm(x, nw_ref[...], sc_ref[...], sh_ref[...]).astype(BF16)
    mg = jax.nn.sigmoid(_dot(h, wm_ref[...]))
    y_a = _dot(a_ref[...], woc_ref[...])
    y_b = _dot(o_ref[...], won_ref[...])
    mix = (mg[:, 0:D_MODEL] * y_a + mg[:, D_MODEL:2 * D_MODEL] * y_b).astype(BF16)
    x1_ref[...] = x + g1_ref[...] * _dot(mix, wo_ref[...])


def _outproj(x2d, a, o, mod, nw, w_merge, w_oc, w_on, w_o, *, tm):
    n = x2d.shape[0]
    tok = pl.BlockSpec((tm, D_MODEL), lambda i: (i, 0))
    sq = _const_spec((D_MODEL, D_MODEL))
    return pl.pallas_call(
        _outproj_kernel, grid=(n // tm,),
        in_specs=[tok, tok, tok, _const_spec((1, D_MODEL)), mod.spec(1), mod.spec(0), mod.spec(2),
                  _const_spec((D_MODEL, 2 * D_MODEL)), sq, sq, sq],
        out_specs=tok, out_shape=jax.ShapeDtypeStruct((n, D_MODEL), F32),
        compiler_params=_params(1), name="outproj",
    )(x2d, a, o, nw, mod.rows, mod.rows, mod.rows, w_merge, w_oc, w_on, w_o)


FF_CHUNK = D_FF // 2


def _ffn_kernel(x_ref, nw_ref, sc_ref, sh_ref, g2_ref, nf_ref, wg_ref, wu_ref, wd_ref, y_ref):
    x = x_ref[...]
    h = _mod_norm(x, nw_ref[...], sc_ref[...], sh_ref[...]).astype(BF16)
    acc = jnp.zeros(x.shape, F32)
    for c in range(D_FF // FF_CHUNK):
        sl = slice(c * FF_CHUNK, (c + 1) * FF_CHUNK)
        gate = _dot(h, wg_ref[:, sl])
        up = _dot(h, wu_ref[:, sl])
        act = (gate * jax.nn.sigmoid(gate) * up).astype(BF16)
        acc = acc + _dot(act, wd_ref[sl, :])
    x2 = x + g2_ref[...] * acc
    inv = lax.rsqrt(jnp.mean(x2 * x2, axis=-1, keepdims=True) + EPS)
    y_ref[...] = (x2 * inv) * nf_ref[...]


def _ffn(x1, mod, nw2, nf, w_gate, w_up, w_down, *, tm):
    n = x1.shape[0]
    tok = pl.BlockSpec((tm, D_MODEL), lambda i: (i, 0))
    vec = _const_spec((1, D_MODEL))
    return pl.pallas_call(
        _ffn_kernel, grid=(n // tm,),
        in_specs=[tok, vec, mod.spec(4), mod.spec(3), mod.spec(5), vec,
                  _const_spec((D_MODEL, D_FF)), _const_spec((D_MODEL, D_FF)), _const_spec((D_FF, D_MODEL))],
        out_specs=tok, out_shape=jax.ShapeDtypeStruct((n, D_MODEL), F32),
        compiler_params=_params(1), name="ffn",
    )(x1, nw2, mod.rows, mod.rows, mod.rows, nf, w_gate, w_up, w_down)


def _slopes():
    return 2.0 ** (-8.0 * jnp.arange(1, N_HEADS + 1, dtype=F32) / N_HEADS)


def _slope_lanes(slopes):
    parts = _split3(slopes)
    cols = jnp.stack([parts[0], parts[0], parts[1], parts[1], parts[2], parts[2]], axis=1)
    return jnp.pad(cols, ((0, 0), (0, LANES - 6)))


def _pos_lanes(pos_hi, pos_lo):
    cols = jnp.stack([pos_hi, pos_lo] * 3, axis=1).astype(F32)
    return jnp.pad(cols, ((0, 0), (0, LANES - 6))).astype(BF16)


def _token_consts(t):
    pos = jnp.arange(t, dtype=jnp.int32)
    onehot = (pos[:, None] // SEL_BLOCK == jnp.arange(HEAD_DIM, dtype=jnp.int32)[None, :]).astype(F32)
    clo = jnp.concatenate([jnp.zeros((t, HEAD_DIM), F32), onehot], axis=1)
    chi = _pos_lanes((pos // SEL_BLOCK) * SEL_BLOCK, pos % SEL_BLOCK)
    return clo, chi


def _cmp_consts(ncb):
    ci = jnp.arange(ncb, dtype=jnp.int32) * CMP_STRIDE
    return _pos_lanes((ci // SEL_BLOCK) * SEL_BLOCK, ci % SEL_BLOCK)


def _ones_lane():
    return (jnp.arange(LANES) == HEAD_DIM).astype(F32).reshape(1, LANES)


def _imp_matrix(nc, ns_pad):
    c = jnp.arange(nc, dtype=jnp.int32)[:, None]
    s = jnp.arange(ns_pad, dtype=jnp.int32)[None, :]
    per = SEL_BLOCK // CMP_STRIDE
    return ((c // per == s) | ((c % per == per - 1) & (c // per == s - 1))).astype(BF16)


def _prep_weights(w_in, w_phi1, w_phi2, pe_cmp):
    wb = w_in.astype(BF16)
    w_conv_in = wb[:, _C_CONV:_C_QKV]
    w_qkv = jnp.pad(wb[:, _C_QKV:_C_MERGE], ((0, 0), (0, _QKV_COLS_PAD - _QKV_COLS)))
    w_merge = wb[:, _C_MERGE:]
    half = CMP_STRIDE * HEAD_DIM
    w1cat = jnp.concatenate([w_phi1[:, :half], w_phi1[:, half:]], axis=2).astype(BF16)
    w2pad = jnp.pad(w_phi2, ((0, 0), (0, 0), (0, LANES - HEAD_DIM))).astype(BF16)
    pe2 = pe_cmp.reshape(2, 2, half)
    return w_conv_in, w_qkv, w_merge, w1cat, w2pad, pe2


def _prompt_layer(x, mod_p, wts):
    (nw1, nw2, nf, w_conv_in, w_qkv, w_merge, w1cat, w2pad, pe2, w_conv, b_conv, w_oc, w_on, w_o,
     w_gate, w_up, w_down, slopes) = wts
    nb, t, _ = x.shape
    tm = min(512, t)
    x2d = x.reshape(nb * t, D_MODEL)
    mod = _Mod(mod_p, False, t, tm)
    a, tail = _conv_path(x2d, mod, nw1, w_conv_in, w_conv, b_conv, tm=tm, seq_len=t)
    clo, chi = _token_consts(t)
    cv = _ones_lane()
    qt, kvc, kvs, kvw, gts, ksa, vst, kwa, vwt = _qkv_path(x2d, mod, nw1, w_qkv, tm=tm, seq_len=t, consts=(clo, chi))
    npages = t // PAGE
    pt = jnp.arange(nb * npages, dtype=jnp.int32).reshape(nb, npages)
    nc = t // CMP_STRIDE
    ns = t // SEL_BLOCK
    nqb = t // Q_BLOCK
    kca, vct = _compress(kvc.reshape(nb * npages, PAGE * ROW_TILES, LANES), pt, pe2, w1cat, w2pad, _cmp_consts(nc), cv,
                         pps=min(16, npages), transpose_v=True, token_minor=False)
    gates_t = gts[:, :3 * N_HEADS].reshape(nb, nqb, Q_BLOCK, N_KV, GROUP, 3).transpose(0, 3, 1, 5, 4, 2)
    gates_t = gates_t.reshape(nb, N_KV, nqb, 3, GROUP * Q_BLOCK)
    qc_t = _slope_lanes(slopes)[:, :QC_ROWS].reshape(N_KV, GROUP, QC_ROWS).transpose(0, 2, 1)
    qc_t = jnp.repeat(qc_t, Q_BLOCK, axis=2)
    mt = _imp_matrix(nc, ns).T
    o = _attn_prompt_t(qt, gates_t, qc_t, mt, kca, vct, ksa, vst, kwa, vwt)
    x1 = _outproj(x2d, a, o.reshape(nb * t, D_MODEL), mod, nw1, w_merge, w_oc, w_on, w_o, tm=tm)
    y = _ffn(x1, mod, nw2, nf, w_gate, w_up, w_down, tm=tm)
    keep = min(WINDOW, t)
    state = (kvc.reshape(nb, t, 2, N_KV, HEAD_DIM), kvs.reshape(nb, t, 2, N_KV, HEAD_DIM),
             kvw.reshape(nb, t, 2 * KV_W)[:, t - keep:].reshape(nb, keep, 2, N_KV, HEAD_DIM),
             tail[:, 8 - 2:, :])
    return y.reshape(nb, t, D_MODEL), state


def _sample_layer(x, mod_s, wts, cache_cmp, cache_sel, cache_win, state_conv, page_table):
    (nw1, nw2, nf, w_conv_in, w_qkv, w_merge, w1cat, w2pad, pe2, w_conv, b_conv, w_oc, w_on, w_o,
     w_gate, w_up, w_down, slopes) = wts
    nb, s, _ = x.shape
    n = nb * s
    x2d = x.reshape(n, D_MODEL)
    mod = _Mod(mod_s, True, s, n)
    tpos = jnp.arange(s)
    p1 = jnp.broadcast_to(state_conv[:, 1:2, :], (nb, s, D_MODEL)).reshape(n, D_MODEL)
    p2 = state_conv[:, jnp.minimum(tpos, 1), :].reshape(n, D_MODEL)
    a, u = _conv_path(x2d, mod, nw1, w_conv_in, w_conv, b_conv, tm=n, seq_len=s, prev=(p1, p2))
    qpad, kvc, kvs, kvw, gts = _qkv_path(x2d, mod, nw1, w_qkv, tm=n, seq_len=s)

    npages = page_table.shape[1]
    past = npages * PAGE
    cv = _ones_lane()
    nc = past // CMP_STRIDE
    token_minor = lambda c: c.transpose(0, 2, 3, 4, 1).reshape(c.shape[0], 2 * KV_W, c.shape[1])
    kca, vca = _compress(token_minor(cache_cmp), page_table, pe2, w1cat, w2pad, _cmp_consts(nc), cv,
                         pps=min(16, npages), transpose_v=False, token_minor=True)

    qh = qpad.reshape(nb, s, N_KV, GROUP, LANES).transpose(0, 2, 3, 1, 4)
    qh = jnp.pad(qh, ((0, 0), (0, 0), (0, 0), (0, Q_PAD - s), (0, 0)))
    sl = jnp.broadcast_to(_slope_lanes(slopes).reshape(1, N_KV, GROUP, 1, LANES), qh.shape)
    qa = jnp.concatenate([qh, sl], axis=-1).reshape(nb, N_KV, ROWS_G, AUG_W)
    eye = jnp.eye(N_KV, dtype=BF16)
    qbd = (qh[..., None, :HEAD_DIM] * eye[None, :, None, None, :, None]).reshape(nb, ROWS_S, KV_W)
    gates = gts[:, :3 * N_HEADS].reshape(nb, s, N_KV, GROUP, 3).transpose(0, 2, 3, 1, 4)
    gates = jnp.pad(gates, ((0, 0), (0, 0), (0, 0), (0, Q_PAD - s), (0, 0))).reshape(nb, ROWS_S, 3)
    slope_rows = jnp.repeat(slopes, Q_PAD).reshape(ROWS_S, 1)
    ns = past // SEL_BLOCK + 1
    nsp = -(-ns // LANES) * LANES
    ms = _imp_matrix(nc, nsp)
    tok = jnp.arange(past, dtype=jnp.int32)
    emat = (jnp.arange(nsp, dtype=jnp.int32)[:, None] == tok[None, :] // SEL_BLOCK).astype(BF16)
    pad_rows = lambda r: jnp.pad(r.reshape(nb, s, 2 * KV_W), ((0, 0), (0, LANES - s), (0, 0)))
    osw, ocg = _attn_sample(page_table, token_minor(cache_sel), qa, qbd, gates, slope_rows, kca, vca,
                            ms, emat, token_minor(cache_win), pad_rows(kvs), pad_rows(kvw),
                            pps=min(8, npages), n_new=s)
    osw = osw.reshape(nb, N_KV, GROUP, Q_PAD, N_KV, HEAD_DIM)
    o_sw = jnp.einsum('bgrqgd->bqgrd', osw)
    o_c = ocg.reshape(nb, N_KV, GROUP, Q_PAD, LANES)[..., :HEAD_DIM].transpose(0, 3, 1, 2, 4)
    o = (o_sw + o_c)[:, :s].reshape(n, N_HEADS * HEAD_DIM).astype(BF16)

    x1 = _outproj(x2d, a, o, mod, nw1, w_merge, w_oc, w_on, w_o, tm=n)
    y = _ffn(x1, mod, nw2, nf, w_gate, w_up, w_down, tm=n)
    kv5 = lambda r: r.reshape(nb, s, 2, N_KV, HEAD_DIM)
    win = jnp.concatenate([cache_win, kv5(kvw)], axis=1)[:, s:]
    state = (kv5(kvc), kv5(kvs), win, u.reshape(nb, s, D_MODEL)[:, s - 2:])
    return y.reshape(nb, s, D_MODEL), state


def kernel(x_prompt, x_sample, c_prompt, c_sample, cache_cmp, cache_sel, cache_win, state_conv, page_table,
           w_ada, b_ada, norm1, w_in, w_conv, b_conv, w_out_conv, pe_cmp, w_phi1, w_phi2, w_o_nsa, w_out,
           norm2, w_gate, w_up, w_down, norm_f):
    depth = w_ada.shape[0]
    assert depth == 1, "single-layer trunk"
    nbp, nbs = c_prompt.shape[0], c_sample.shape[0]
    slopes = _slopes()
    l = 0
    c_all = jnp.concatenate([c_prompt, c_sample], axis=0)
    c_all = jnp.pad(c_all, ((0, -c_all.shape[0] % 8), (0, 0)))
    mod = _ada(c_all, w_ada[l], b_ada[l])
    w_conv_in, w_qkv, w_merge, w1cat, w2pad, pe2 = _prep_weights(w_in[l], w_phi1[l], w_phi2[l], pe_cmp[l])
    row = lambda v: v.reshape(1, -1)
    wts = (row(norm1[l]), row(norm2[l]), row(norm_f), w_conv_in, w_qkv, w_merge, w1cat, w2pad, pe2,
           w_conv[l], row(b_conv[l]), w_out_conv[l].astype(BF16), w_o_nsa[l].astype(BF16), w_out[l].astype(BF16),
           w_gate[l].astype(BF16), w_up[l].astype(BF16), w_down[l].astype(BF16), slopes)
    yp, st_p = _prompt_layer(x_prompt, mod[:nbp], wts)
    ys, st_s = _sample_layer(x_sample, mod[nbp:nbp + nbs], wts, cache_cmp[l], cache_sel[l], cache_win[l],
                             state_conv[l], page_table)
    return (yp, ys, st_p[0][None], st_p[1][None], st_p[2][None], st_p[3][None],
            st_s[0][None], st_s[1][None], st_s[2][None], st_s[3][None])
```

```python
import functools

import jax
import jax.numpy as jnp
import numpy as np
from jax import lax
from jax.experimental import pallas as pl
from jax.experimental.pallas import tpu as pltpu

F32 = jnp.float32
BF16 = jnp.bfloat16

D_MODEL = 1024
N_HEADS = 16
HEAD_DIM = 64
N_KV = 4
GROUP = N_HEADS // N_KV
KV_W = N_KV * HEAD_DIM
CMP_BLOCK = 32
CMP_STRIDE = 16
SEL_BLOCK = 64
N_SEL = 16
WINDOW = 512
D_PHI = 2 * HEAD_DIM
Q_BLOCK = 256
PAGE = 128
D_FF = ((8 * D_MODEL // 3 + 255) // 256) * 256
EPS = 1e-6
NEG = -1e30
FORCE_BONUS = 1e3

LANES = 128
AUG_W = 2 * LANES
HALVES_PER_PAGE = PAGE // CMP_STRIDE
ROW_TILES = 2 * KV_W // LANES
VMEM_LIMIT = 56 * 1024 * 1024
KEY_TILE = 2 * LANES
QC_ROWS = 16
SEL_TILES_PER_ITER = 4

_C_CONV = 0
_C_QKV = 3 * D_MODEL
_C_GATE = _C_QKV + N_HEADS * HEAD_DIM + 6 * KV_W
_C_MERGE = _C_GATE + 3 * N_HEADS
_QKV_COLS = _C_MERGE - _C_QKV
_QKV_COLS_PAD = -(-_QKV_COLS // LANES) * LANES


def _dot(a, b):
    return jnp.dot(a, b, preferred_element_type=F32)


def _dot_nt(a, b):
    return lax.dot_general(a, b, (((1,), (1,)), ((), ())), preferred_element_type=F32)


def _params(n_axes):
    return pltpu.CompilerParams(dimension_semantics=("arbitrary",) * n_axes, vmem_limit_bytes=VMEM_LIMIT)


def _const_spec(shape):
    return pl.BlockSpec(shape, lambda *_: (0,) * len(shape))


def _mod_norm(x, nw, sc, sh):
    inv = lax.rsqrt(jnp.mean(x * x, axis=-1, keepdims=True) + EPS)
    return (x * inv) * nw * (1.0 + sc) + sh


def _split3(x):
    a = x.astype(BF16)
    r = x - a.astype(F32)
    b = r.astype(BF16)
    c = (r - b.astype(F32)).astype(BF16)
    return a, b, c


def _ada_kernel(c_ref, w_ref, b_ref, o_ref):
    c = c_ref[...]
    s = c * jax.nn.sigmoid(c)
    o_ref[...] = jnp.dot(s, w_ref[...], preferred_element_type=F32, precision=lax.Precision.HIGHEST) + b_ref[...]


def _ada(c, w_ada, b_ada):
    n = c.shape[0]
    tn = 1536
    return pl.pallas_call(
        _ada_kernel,
        grid=(6 * D_MODEL // tn,),
        in_specs=[_const_spec((n, D_MODEL)),
                  pl.BlockSpec((D_MODEL, tn), lambda j: (0, j)),
                  pl.BlockSpec((1, tn), lambda j: (0, j))],
        out_specs=pl.BlockSpec((n, tn), lambda j: (0, j)),
        out_shape=jax.ShapeDtypeStruct((n, 6 * D_MODEL), F32),
        compiler_params=_params(1),
        name="ada",
    )(c, w_ada, b_ada.reshape(1, -1))


class _Mod:
    def __init__(self, mod, per_token, seq_len, tm):
        self.per_token = per_token
        if per_token:
            self.rows = jnp.repeat(mod, seq_len, axis=0)
        else:
            self.rows = mod.reshape(mod.shape[0], 1, 6 * D_MODEL)
        self.tiles_per_seq = None if per_token else seq_len // tm
        self.tm = tm

    def spec(self, k):
        if self.per_token:
            return pl.BlockSpec((self.tm, D_MODEL), lambda i: (i, k))
        tps = self.tiles_per_seq
        return pl.BlockSpec((None, 1, D_MODEL), lambda i: (i // tps, 0, k))


def _conv_kernel(*refs, carry_rows, seq_len):
    if carry_rows:
        (x_ref, nw_ref, sc_ref, sh_ref, w_ref, wc_ref, bc_ref, a_ref, tail_ref, carry_ref) = refs
    else:
        (x_ref, nw_ref, sc_ref, sh_ref, w_ref, wc_ref, bc_ref, p1_ref, p2_ref, a_ref, tail_ref) = refs
    tm = x_ref.shape[0]
    h = _mod_norm(x_ref[...], nw_ref[...], sc_ref[...], sh_ref[...]).astype(BF16)
    z = _dot(h, w_ref[...])
    bg = z[:, 0:D_MODEL]
    u = z[:, D_MODEL:2 * D_MODEL] * z[:, 2 * D_MODEL:3 * D_MODEL]
    row = lax.broadcasted_iota(jnp.int32, (tm, 1), 0)
    u1 = pltpu.roll(u, 1, 0)
    u2 = pltpu.roll(u, 2, 0)
    if carry_rows:
        @pl.when(pl.program_id(0) % carry_rows == 0)
        def _():
            carry_ref[...] = jnp.zeros_like(carry_ref)
        c0 = carry_ref[0:1, :]
        c1 = carry_ref[1:2, :]
        u1 = jnp.where(row == 0, c1, u1)
        u2 = jnp.where(row == 0, c0, jnp.where(row == 1, c1, u2))
        carry_ref[0:2, :] = u[tm - 2:tm, :]
        tail_ref[...] = u[tm - 8:tm, :]
    else:
        pos = lax.rem(row, seq_len)
        u1 = jnp.where(pos >= 1, u1, p1_ref[...])
        u2 = jnp.where(pos >= 2, u2, p2_ref[...])
        tail_ref[...] = u
    v = bc_ref[...] + wc_ref[0:1, :] * u2 + wc_ref[1:2, :] * u1 + wc_ref[2:3, :] * u
    a_ref[...] = (bg * v).astype(BF16)


def _conv_path(x2d, mod, nw, w_conv_in, w_conv, b_conv, *, tm, seq_len, prev=None):
    n = x2d.shape[0]
    tok = pl.BlockSpec((tm, D_MODEL), lambda i: (i, 0))
    in_specs = [tok, _const_spec((1, D_MODEL)), mod.spec(1), mod.spec(0),
                _const_spec((D_MODEL, 3 * D_MODEL)), _const_spec((3, D_MODEL)), _const_spec((1, D_MODEL))]
    args = [x2d, nw, mod.rows, mod.rows, w_conv_in, w_conv, b_conv]
    if prev is None:
        tps = seq_len // tm
        out_specs = [tok, pl.BlockSpec((None, 8, D_MODEL), lambda i: (i // tps, 0, 0))]
        out_shape = [jax.ShapeDtypeStruct((n, D_MODEL), BF16), jax.ShapeDtypeStruct((n // seq_len, 8, D_MODEL), F32)]
        scratch = [pltpu.VMEM((8, D_MODEL), F32)]
        kern = functools.partial(_conv_kernel, carry_rows=tps, seq_len=seq_len)
    else:
        in_specs += [tok, tok]
        args += list(prev)
        out_specs = [tok, tok]
        out_shape = [jax.ShapeDtypeStruct((n, D_MODEL), BF16), jax.ShapeDtypeStruct((n, D_MODEL), F32)]
        scratch = []
        kern = functools.partial(_conv_kernel, carry_rows=0, seq_len=seq_len)
    return pl.pallas_call(kern, grid=(n // tm,), in_specs=in_specs, out_specs=out_specs, out_shape=out_shape,
                          scratch_shapes=scratch, compiler_params=_params(1), name="conv_path")(*args)


def _qkv_kernel(*refs, aug):
    if aug:
        (x_ref, nw_ref, sc_ref, sh_ref, w_ref, clo_ref, chi_ref,
         q_ref, kvc_ref, kvs_ref, kvw_ref, g_ref, ksa_ref, vsa_ref, kwa_ref, vwa_ref) = refs
    else:
        (x_ref, nw_ref, sc_ref, sh_ref, w_ref, q_ref, kvc_ref, kvs_ref, kvw_ref, g_ref) = refs
    tm = x_ref.shape[0]
    h = _mod_norm(x_ref[...], nw_ref[...], sc_ref[...], sh_ref[...]).astype(BF16)
    z = _dot(h, w_ref[...])
    low = lax.broadcasted_iota(jnp.int32, (tm, LANES), 1) < HEAD_DIM
    nq = N_HEADS * HEAD_DIM
    for c in range(N_HEADS // 2):
        t = z[:, c * LANES:(c + 1) * LANES] * (HEAD_DIM ** -0.5)
        if aug:
            tt = t.T.astype(BF16)
            q_ref[2 * c] = tt[0:HEAD_DIM]
            q_ref[2 * c + 1] = tt[HEAD_DIM:2 * HEAD_DIM]
        else:
            q_ref[:, (2 * c) * LANES:(2 * c + 1) * LANES] = jnp.where(low, t, 0.0).astype(BF16)
            q_ref[:, (2 * c + 1) * LANES:(2 * c + 2) * LANES] = jnp.where(low, pltpu.roll(t, HEAD_DIM, 1), 0.0).astype(BF16)
    if aug:
        kvc_ref[...] = z[:, nq:nq + 2 * KV_W].T
        kvs_ref[...] = z[:, nq + 2 * KV_W:nq + 4 * KV_W].T
    else:
        kvc_ref[...] = z[:, nq:nq + 2 * KV_W]
        kvs_ref[...] = z[:, nq + 2 * KV_W:nq + 4 * KV_W]
    kvw_ref[...] = z[:, nq + 4 * KV_W:nq + 6 * KV_W]
    g_ref[...] = jax.nn.sigmoid(z[:, nq + 6 * KV_W:nq + 6 * KV_W + LANES])
    if aug:
        chi = chi_ref[...]
        ones_row = (lax.broadcasted_iota(jnp.int32, (LANES - HEAD_DIM, KEY_TILE), 0) == 0).astype(BF16)
        for br, (ka_ref, va_ref) in enumerate(((ksa_ref, vsa_ref), (kwa_ref, vwa_ref))):
            kbase = nq + 2 * KV_W * (br + 1)
            clo = clo_ref[...] if br == 0 else 0.0
            for g in range(N_KV):
                kt = z[:, kbase + (g // 2) * LANES:kbase + (g // 2 + 1) * LANES]
                if g % 2:
                    kt = pltpu.roll(kt, HEAD_DIM, 1)
                ka_ref[g, :, 0:LANES] = jnp.where(low, kt, clo).astype(BF16)
                ka_ref[g, :, LANES:AUG_W] = chi
            for c in range(N_KV // 2):
                vt = z[:, kbase + KV_W + c * LANES:kbase + KV_W + (c + 1) * LANES].T.astype(BF16)
                for gg in range(2):
                    for j in range(tm // KEY_TILE):
                        va_ref[2 * c + gg, j, 0:HEAD_DIM, :] = vt[gg * HEAD_DIM:(gg + 1) * HEAD_DIM, j * KEY_TILE:(j + 1) * KEY_TILE]
                        va_ref[2 * c + gg, j, HEAD_DIM:LANES, :] = ones_row


def _qkv_path(x2d, mod, nw, w_qkv, *, tm, seq_len, consts=None):
    n = x2d.shape[0]
    aug = consts is not None
    tok = lambda w: pl.BlockSpec((tm, w), lambda i: (i, 0))
    in_specs = [tok(D_MODEL), _const_spec((1, D_MODEL)), mod.spec(1), mod.spec(0),
                _const_spec((D_MODEL, _QKV_COLS_PAD))]
    args = [x2d, nw, mod.rows, mod.rows, w_qkv]
    out_specs = [tok(N_HEADS * LANES), tok(2 * KV_W), tok(2 * KV_W), tok(2 * KV_W), tok(LANES)]
    out_shape = [jax.ShapeDtypeStruct((n, N_HEADS * LANES), BF16)] + \
                [jax.ShapeDtypeStruct((n, 2 * KV_W), F32)] * 3 + [jax.ShapeDtypeStruct((n, LANES), F32)]
    if aug:
        tps = seq_len // tm
        nb = n // seq_len
        pos = lambda w: pl.BlockSpec((tm, w), lambda i: (i % tps, 0))
        in_specs += [pos(LANES), pos(LANES)]
        args += list(consts)
        out_specs[0] = pl.BlockSpec((None, N_HEADS, HEAD_DIM, tm), lambda i: (i // tps, 0, 0, i % tps))
        out_shape[0] = jax.ShapeDtypeStruct((nb, N_HEADS, HEAD_DIM, seq_len), BF16)
        for k in (1, 2):
            out_specs[k] = pl.BlockSpec((None, 2 * KV_W, tm), lambda i: (i // tps, 0, i % tps))
            out_shape[k] = jax.ShapeDtypeStruct((nb, 2 * KV_W, seq_len), F32)
        ka = pl.BlockSpec((None, N_KV, tm, AUG_W), lambda i: (i // tps, 0, i % tps, 0))
        va = pl.BlockSpec((None, N_KV, tm // KEY_TILE, LANES, KEY_TILE), lambda i: (i // tps, 0, i % tps, 0, 0))
        out_specs += [ka, va, ka, va]
        ka_s = jax.ShapeDtypeStruct((nb, N_KV, seq_len, AUG_W), BF16)
        va_s = jax.ShapeDtypeStruct((nb, N_KV, seq_len // KEY_TILE, LANES, KEY_TILE), BF16)
        out_shape += [ka_s, va_s, ka_s, va_s]
    return pl.pallas_call(functools.partial(_qkv_kernel, aug=aug), grid=(n // tm,), in_specs=in_specs,
                          out_specs=out_specs, out_shape=out_shape, compiler_params=_params(1), name="qkv_path")(*args)


def _block_major_loader(ref, k, c):
    def load(p, nrows):
        return ref[k, c, p * HALVES_PER_PAGE:p * HALVES_PER_PAGE + nrows, :]
    return load


def _gather_pair(load, nrows):
    low = lax.broadcasted_iota(jnp.int32, (nrows, LANES), 1) < HEAD_DIM
    even, odd = [], []
    for qq in range(CMP_STRIDE // 2):
        a = load(2 * qq, nrows)
        b = load(2 * qq + 1, nrows)
        even.append(jnp.where(low, a, pltpu.roll(b, HEAD_DIM, 1)))
        odd.append(jnp.where(low, pltpu.roll(a, HEAD_DIM, 1), b))
    return jnp.concatenate(even, axis=1), jnp.concatenate(odd, axis=1)


def _compress_kernel(pt_ref, *refs, pps, transpose_v):
    pages = refs[:pps]
    nxt_ref, pe_ref, w1_ref, w2_ref, chi_ref, cv_ref, perm_ref, kc_ref, vc_ref, xt_ref = refs[pps:]
    perm = perm_ref[...]
    for k, pg in enumerate(list(pages) + [nxt_ref]):
        for c in range(ROW_TILES):
            xt_ref[k, c] = _dot_nt(perm, pg[c * LANES:(c + 1) * LANES, :].astype(BF16))
    loaders = lambda c: [_block_major_loader(xt_ref, k, c) for k in range(pps)]
    look_loader = lambda c: _block_major_loader(xt_ref, pps, c)
    t = pl.program_id(1)
    last = t == pl.num_programs(1) - 1
    nhb = pps * HALVES_PER_PAGE
    n = N_KV * nhb
    row = lax.broadcasted_iota(jnp.int32, (n, 1), 0)
    chi = chi_ref[...]
    for kv in range(2):
        by_group, look = [], []
        for cc in range(N_KV // 2):
            c = kv * (N_KV // 2) + cc
            pairs = [_gather_pair(ld, HALVES_PER_PAGE) for ld in loaders(c)]
            by_group += [[ev for ev, _ in pairs], [od for _, od in pairs]]
            look += list(_gather_pair(look_loader(c), 1))
        parts = [x for group in by_group for x in group]
        extra = jnp.concatenate([pe_ref[kv], jnp.zeros((2, CMP_STRIDE * HEAD_DIM), F32)] + look, axis=0)
        xmat = jnp.concatenate(parts + [extra], axis=0).astype(BF16)
        hab = _dot(xmat, w1_ref[kv])
        ha = hab[0:n, 0:D_PHI]
        hb = hab[0:n, D_PHI:2 * D_PHI]
        pbias = hab[n:n + 1, 0:D_PHI] + hab[n + 1:n + 2, D_PHI:2 * D_PHI]
        hbn = pltpu.roll(hb, n - 1, 0)
        for g in range(N_KV):
            la = jnp.where(last, 0.0, hab[n + 4 + g:n + 5 + g, D_PHI:2 * D_PHI])
            hbn = jnp.where(row == g * nhb + nhb - 1, la, hbn)
        act = jax.nn.gelu(ha + hbn + pbias).astype(BF16)
        out = _dot(act, w2_ref[kv])
        for g in range(N_KV):
            blk = out[g * nhb:(g + 1) * nhb, :]
            if kv == 0:
                kc_ref[g, :, 0:LANES] = blk.astype(BF16)
                kc_ref[g, :, LANES:AUG_W] = chi
            elif transpose_v:
                vc_ref[g, :, :] = (blk + cv_ref[...]).T.astype(BF16)
            else:
                vc_ref[g, :, :] = (blk + cv_ref[...]).astype(BF16)


def _compress(pages, page_table, pe2, w1cat, w2pad, chi_c, cv, *, pps, transpose_v, paged):
    nb, npages = page_table.shape
    nchunks = npages // pps
    nhb = pps * HALVES_PER_PAGE
    ncb = npages * HALVES_PER_PAGE
    where = (lambda b, j: (j, 0, 0)) if paged else (lambda b, j: (b, 0, j))

    def page_spec(k):
        return pl.BlockSpec((None, 2 * KV_W, PAGE), lambda b, t, pt: where(b, pt[b, t * pps + k]))

    nxt_spec = pl.BlockSpec((None, 2 * KV_W, PAGE),
                            lambda b, t, pt: where(b, pt[b, jnp.minimum((t + 1) * pps, npages - 1)]))
    scratch = [pltpu.VMEM((pps + 1, ROW_TILES, PAGE, LANES), F32)]
    r = jnp.arange(PAGE, dtype=jnp.int32)
    perm = (r[None, :] == (CMP_STRIDE * (r % HALVES_PER_PAGE) + r // HALVES_PER_PAGE)[:, None]).astype(BF16)
    cs = lambda shape: pl.BlockSpec(shape, lambda b, t, pt: (0,) * len(shape))
    in_specs = [page_spec(k) for k in range(pps)] + [
        nxt_spec, cs((2, 2, CMP_STRIDE * HEAD_DIM)), cs((2, CMP_STRIDE * HEAD_DIM, 2 * D_PHI)),
        cs((2, D_PHI, LANES)), pl.BlockSpec((nhb, LANES), lambda b, t, pt: (t, 0)), cs((1, LANES)),
        cs((PAGE, PAGE))]
    if transpose_v:
        v_spec = pl.BlockSpec((None, N_KV, LANES, nhb), lambda b, t, pt: (b, 0, 0, t))
        v_shape = jax.ShapeDtypeStruct((nb, N_KV, LANES, ncb), BF16)
    else:
        v_spec = pl.BlockSpec((None, N_KV, nhb, LANES), lambda b, t, pt: (b, 0, t, 0))
        v_shape = jax.ShapeDtypeStruct((nb, N_KV, ncb, LANES), BF16)
    out_specs = [pl.BlockSpec((None, N_KV, nhb, AUG_W), lambda b, t, pt: (b, 0, t, 0)), v_spec]
    out_shape = [jax.ShapeDtypeStruct((nb, N_KV, ncb, AUG_W), BF16), v_shape]
    gs = pltpu.PrefetchScalarGridSpec(num_scalar_prefetch=1, grid=(nb, nchunks), in_specs=in_specs, out_specs=out_specs,
                                      scratch_shapes=scratch)
    return pl.pallas_call(functools.partial(_compress_kernel, pps=pps, transpose_v=transpose_v),
                          grid_spec=gs, out_shape=out_shape,
                          compiler_params=_params(2), name="compress")(
        page_table, *([pages] * (pps + 1)), pe2, w1cat, w2pad, chi_c, cv, perm)


def _attn_prompt_t_kernel(q_ref, g_ref, qc_ref, mt_ref, kc_ref, vc_ref, ks_ref, vs_ref, kw_ref, vw_ref, o_ref,
                          qa_ref, qw_ref, m_ref, acc_ref):
    nc = kc_ref.shape[0]
    ns = mt_ref.shape[0]
    kt = vs_ref.shape[2]
    ncol = GROUP * Q_BLOCK
    qb = pl.program_id(2)
    q0 = qb * Q_BLOCK
    qpos = q0 + (lax.broadcasted_iota(jnp.int32, (1, ncol), 1) & (Q_BLOCK - 1))

    for ref in (qw_ref, qa_ref):
        for r in range(GROUP):
            ref[0:HEAD_DIM, r * Q_BLOCK:(r + 1) * Q_BLOCK] = q_ref[r]
        ref[HEAD_DIM:2 * HEAD_DIM, :] = jnp.zeros((HEAD_DIM, ncol), BF16)
        ref[2 * HEAD_DIM:2 * HEAD_DIM + QC_ROWS, :] = qc_ref[...]
        ref[2 * HEAD_DIM + QC_ROWS:AUG_W, :] = jnp.zeros((AUG_W - 2 * HEAD_DIM - QC_ROWS, ncol), BF16)

    s = _dot(kc_ref[...], qw_ref[...])
    ci = lax.broadcasted_iota(jnp.int32, (nc, 1), 0)
    c_valid = ci * CMP_STRIDE + (CMP_BLOCK - 1) <= qpos
    s = jnp.where(c_valid, s, NEG)
    e = jnp.exp(s - jnp.max(s, axis=0, keepdims=True))
    p = jnp.where(c_valid, e * (1.0 / jnp.sum(e, axis=0, keepdims=True)), 0.0)
    o_c = _dot(vc_ref[...], p.astype(BF16))[0:HEAD_DIM]

    psum = p[:, 0:Q_BLOCK]
    for r in range(1, GROUP):
        psum = psum + p[:, r * Q_BLOCK:(r + 1) * Q_BLOCK]
    mt = mt_ref[...]
    imp = sum(_dot(mt, part) for part in _split3(psum))
    si = lax.broadcasted_iota(jnp.int32, (ns, Q_BLOCK), 0)
    qpos_t = q0 + lax.broadcasted_iota(jnp.int32, (ns, Q_BLOCK), 1)
    cur = lax.shift_right_logical(qpos_t, 6)
    s_valid = si * SEL_BLOCK <= qpos_t
    forced = (si == 0) | (si == cur) | (si == cur - 1)
    score = jnp.where(s_valid, imp + jnp.where(forced, FORCE_BONUS, 0.0), NEG)
    rank = jnp.zeros((ns, Q_BLOCK), jnp.int32)
    for sp in range(ns):
        other = score[sp:sp + 1, :]
        beats = (other > score) | ((other == score) & (si > sp))
        rank = rank + beats.astype(jnp.int32)
    selected = rank < min(N_SEL, ns)
    bias_t = jnp.where(selected, 0.0, NEG).astype(BF16)
    qa_ref[HEAD_DIM:HEAD_DIM + ns, :] = jnp.concatenate([bias_t] * GROUP, axis=1)

    t_hi = lax.div(q0, kt)
    key_iota = lax.broadcasted_iota(jnp.int32, (kt, 1), 0)

    n_back = WINDOW // kt
    s_w, v_w = [], []
    for j in range(n_back + 1):
        tw = t_hi - n_back + j
        tc = jnp.maximum(tw, 0)
        kpos = tw * kt + key_iota
        s = _dot(kw_ref[pl.ds(pl.multiple_of(tc * kt, kt), kt), :], qw_ref[...])
        if j == 0:
            oldest = jnp.where(tw >= 0, qpos - WINDOW, jnp.iinfo(jnp.int32).max)
            s = jnp.where(kpos > oldest, s, NEG)
        elif j == n_back:
            s = jnp.where(kpos <= qpos, s, NEG)
        else:
            s = jnp.where(tw >= 0, s, NEG)
        s_w.append(s)
        v_w.append(vw_ref[tc])
    m_w = functools.reduce(jnp.maximum, [jnp.max(s, axis=0, keepdims=True) for s in s_w])
    acc_w = sum(_dot(v, jnp.exp(s - m_w).astype(BF16)) for s, v in zip(s_w, v_w))
    o_w = acc_w[0:HEAD_DIM] * (1.0 / acc_w[HEAD_DIM:HEAD_DIM + 1])

    m_ref[...] = jnp.full(m_ref.shape, NEG, F32)
    acc_ref[...] = jnp.zeros(acc_ref.shape, F32)

    def sel_tiles(tiles, causal_last):
        qa = qa_ref[...]
        ss = []
        for i, tl in enumerate(tiles):
            k0 = pl.multiple_of(tl * kt, kt)
            s = _dot(ks_ref[pl.ds(k0, kt), :], qa)
            if causal_last and i == len(tiles) - 1:
                s = jnp.where(k0 + key_iota <= qpos, s, NEG)
            ss.append(s)
        m_old = m_ref[...]
        m_new = functools.reduce(jnp.maximum, [m_old] + [jnp.max(s, axis=0, keepdims=True) for s in ss])
        pv = sum(_dot(vs_ref[tl], jnp.exp(s - m_new).astype(BF16)) for tl, s in zip(tiles, ss))
        acc_ref[...] = jnp.exp(m_old - m_new) * acc_ref[...] + pv
        m_ref[...] = m_new

    per_tile = kt // SEL_BLOCK
    lo_blk = jnp.min(jnp.where(selected & (si >= per_tile), si, ns))
    start = jnp.clip(lax.div(lo_blk, per_tile), 1, jnp.maximum(t_hi, 1))
    n_plain = jnp.where(t_hi >= 1, 1 + t_hi - start, 0)
    tile_at = lambda j: jnp.where(j == 0, 0, start + j - 1)
    n_group = lax.div(n_plain, SEL_TILES_PER_ITER)

    def body(i, carry):
        sel_tiles([tile_at(i * SEL_TILES_PER_ITER + k) for k in range(SEL_TILES_PER_ITER)], False)
        return carry

    lax.fori_loop(0, n_group, body, 0)
    rem = n_plain - n_group * SEL_TILES_PER_ITER
    for left in range(SEL_TILES_PER_ITER):
        @pl.when(rem == left)
        def _():
            sel_tiles([tile_at(n_plain - left + k) for k in range(left)] + [t_hi], True)

    acc = acc_ref[...]
    o_s = acc[0:HEAD_DIM] * (1.0 / acc[HEAD_DIM:HEAD_DIM + 1])

    gts = g_ref[...]
    o = gts[0:1] * o_c + gts[1:2] * o_s + gts[2:3] * o_w
    for c in range(GROUP // 2):
        pair = jnp.concatenate([o[:, (2 * c) * Q_BLOCK:(2 * c + 1) * Q_BLOCK],
                                o[:, (2 * c + 1) * Q_BLOCK:(2 * c + 2) * Q_BLOCK]], axis=0)
        o_ref[:, c * LANES:(c + 1) * LANES] = pair.T.astype(BF16)


def _attn_prompt_t(qt, gates_t, qc_t, mt, kc, vct, ks, vst, kw, vwt):
    nb, t = qt.shape[0], qt.shape[3]
    nc = kc.shape[2]
    ns = mt.shape[0]
    ntile, kt = vst.shape[2], vst.shape[4]
    ncol = GROUP * Q_BLOCK
    per_bg = lambda *shape: pl.BlockSpec((None, None) + shape, lambda b, g, i: (b, g) + (0,) * len(shape))
    in_specs = [pl.BlockSpec((None, GROUP, HEAD_DIM, Q_BLOCK), lambda b, g, i: (b, g, 0, i)),
                pl.BlockSpec((None, None, None, 3, ncol), lambda b, g, i: (b, g, i, 0, 0)),
                pl.BlockSpec((None, QC_ROWS, ncol), lambda b, g, i: (g, 0, 0)),
                pl.BlockSpec((ns, nc), lambda b, g, i: (0, 0)),
                per_bg(nc, AUG_W), per_bg(LANES, nc), per_bg(t, AUG_W), per_bg(ntile, LANES, kt),
                per_bg(t, AUG_W), per_bg(ntile, LANES, kt)]
    out_spec = pl.BlockSpec((None, Q_BLOCK, GROUP * HEAD_DIM), lambda b, g, i: (b, i, g))
    scratch = [pltpu.VMEM((AUG_W, ncol), BF16), pltpu.VMEM((AUG_W, ncol), BF16),
               pltpu.VMEM((1, ncol), F32), pltpu.VMEM((LANES, ncol), F32)]
    return pl.pallas_call(_attn_prompt_t_kernel, grid=(nb, N_KV, t // Q_BLOCK),
                          in_specs=in_specs, out_specs=out_spec,
                          out_shape=jax.ShapeDtypeStruct((nb, t, N_HEADS * HEAD_DIM), BF16),
                          scratch_shapes=scratch, compiler_params=_params(3), name="attn_prompt")(
        qt, gates_t, qc_t, mt, kc, vct, ks, vst, kw, vwt)


Q_PAD = 8
ROWS_G = GROUP * Q_PAD
ROWS_S = N_KV * ROWS_G


def _attn_sample_kernel(pt_ref, *refs, pps, past, wbuf, n_new):
    pages = refs[:pps]
    (qa_ref, qbd_ref, g_ref, slope_ref, kc_ref, vc_ref, ms_ref, e_ref, win_ref, ksn_ref, kwn_ref,
     osw_ref, oc_ref, bias_ref, m_ref, acc_ref, ow_ref) = refs[pps:]
    t = pl.program_id(1)
    nchunks = pl.num_programs(1)
    nc = kc_ref.shape[1]
    ns = past // SEL_BLOCK + 1
    row = lax.broadcasted_iota(jnp.int32, (ROWS_S, 1), 0)
    qpos = past + (row & (Q_PAD - 1))
    qposf = qpos.astype(F32)
    slope = slope_ref[...]
    qbd = qbd_ref[...]

    @pl.when(t == 0)
    def _():
        nsp = ms_ref.shape[1]
        rg = lax.broadcasted_iota(jnp.int32, (ROWS_G, 1), 0)
        qpos_g = past + (rg & (Q_PAD - 1))
        r8 = lax.broadcasted_iota(jnp.int32, (Q_PAD, 1), 0)
        qpos8 = past + r8
        si = lax.broadcasted_iota(jnp.int32, (Q_PAD, nsp), 1)
        for g in range(N_KV):
            s = _dot_nt(qa_ref[g], kc_ref[g])
            ci = lax.broadcasted_iota(jnp.int32, (1, nc), 1)
            c_valid = ci * CMP_STRIDE + (CMP_BLOCK - 1) <= qpos_g
            s = jnp.where(c_valid, s, NEG)
            e = jnp.exp(s - jnp.max(s, axis=1, keepdims=True))
            p = jnp.where(c_valid, e * (1.0 / jnp.sum(e, axis=1, keepdims=True)), 0.0)
            oc_ref[g * ROWS_G:(g + 1) * ROWS_G, :] = _dot(p.astype(BF16), vc_ref[g])
            psum = p[0:Q_PAD]
            for r in range(1, GROUP):
                psum = psum + p[r * Q_PAD:(r + 1) * Q_PAD]
            ms = ms_ref[...]
            imp = sum(_dot(part, ms) for part in _split3(psum))
            cur = lax.shift_right_logical(qpos8, 6)
            s_valid = (si * SEL_BLOCK <= qpos8) & (si < ns)
            forced = (si == 0) | (si == cur) | (si == cur - 1)
            score = jnp.where(s_valid, imp + jnp.where(forced, FORCE_BONUS, 0.0), NEG)
            rank = jnp.zeros((Q_PAD, nsp), jnp.int32)
            for sp in range(ns):
                other = score[:, sp:sp + 1]
                beats = (other > score) | ((other == score) & (si > sp))
                rank = rank + beats.astype(jnp.int32)
            bias_ref[g] = jnp.where((rank < min(N_SEL, ns)) & (si < ns), 0.0, NEG).astype(BF16)

        kw_t = win_ref[0:KV_W, :].astype(BF16)
        vw_t = win_ref[KV_W:2 * KV_W, :].astype(BF16)
        kpos = past - wbuf + lax.broadcasted_iota(jnp.int32, (1, wbuf), 1)
        s1 = _dot(qbd, kw_t) - slope * (qposf - kpos.astype(F32))
        s1 = jnp.where((kpos <= qpos) & (qpos - kpos < WINDOW) & (kpos >= 0), s1, NEG)
        kn = kwn_ref[:, 0:KV_W].astype(BF16)
        vn = kwn_ref[:, KV_W:2 * KV_W].astype(BF16)
        li = lax.broadcasted_iota(jnp.int32, (1, LANES), 1)
        kposn = past + li
        s2 = _dot_nt(qbd, kn) - slope * (qposf - kposn.astype(F32))
        s2 = jnp.where((li < n_new) & (kposn <= qpos) & (qpos - kposn < WINDOW), s2, NEG)
        m = jnp.maximum(jnp.max(s1, axis=1, keepdims=True), jnp.max(s2, axis=1, keepdims=True))
        p1 = jnp.exp(s1 - m)
        p2 = jnp.exp(s2 - m)
        den = jnp.sum(p1, axis=1, keepdims=True) + jnp.sum(p2, axis=1, keepdims=True)
        inv = 1.0 / den
        ow_ref[...] = (_dot_nt((p1 * inv).astype(BF16), vw_t) + _dot((p2 * inv).astype(BF16), vn))
        m_ref[...] = jnp.full(m_ref.shape, NEG, F32)
        acc_ref[...] = jnp.zeros(acc_ref.shape, F32)

    def block_bias(emat):
        rows = []
        for g in range(N_KV):
            bt = _dot(bias_ref[g], emat)
            rows += [bt] * GROUP
        return jnp.concatenate(rows, axis=0)

    def update(scores, pv_fns):
        m_old = m_ref[...]
        m_new = functools.reduce(jnp.maximum, [m_old] + [jnp.max(s, axis=1, keepdims=True) for s in scores])
        alpha = jnp.exp(m_old - m_new)
        ps = [jnp.exp(s - m_new) for s in scores]
        ones = functools.reduce(jnp.add, [jnp.sum(p, axis=1, keepdims=True) for p in ps])
        pv = functools.reduce(jnp.add, [f(p.astype(BF16)) for f, p in zip(pv_fns, ps)])
        acc_ref[:, 0:KV_W] = alpha * acc_ref[:, 0:KV_W] + pv
        acc_ref[:, KV_W:KV_W + LANES] = alpha * acc_ref[:, KV_W:KV_W + LANES] + ones
        m_ref[...] = m_new

    bias_chunk = block_bias(e_ref[...])
    li = lax.broadcasted_iota(jnp.int32, (1, PAGE), 1)
    scores, pv_fns = [], []
    for k in range(pps):
        pg = pages[k]
        kpos = (t * pps + k) * PAGE + li
        s = _dot(qbd, pg[0:KV_W, :].astype(BF16)) - slope * (qposf - kpos.astype(F32))
        scores.append(s + bias_chunk[:, k * PAGE:(k + 1) * PAGE])
        pv_fns.append(lambda p, pg=pg: _dot_nt(p, pg[KV_W:2 * KV_W, :].astype(BF16)))
    update(scores, pv_fns)

    @pl.when(t == nchunks - 1)
    def _():
        nsp = ms_ref.shape[1]
        kposn = past + li
        e_new = (lax.broadcasted_iota(jnp.int32, (nsp, PAGE), 0) == past // SEL_BLOCK).astype(BF16)
        s = _dot_nt(qbd, ksn_ref[:, 0:KV_W].astype(BF16)) - slope * (qposf - kposn.astype(F32)) + block_bias(e_new)
        s = jnp.where((li < n_new) & (kposn <= qpos), s, NEG)
        update([s], [lambda p: _dot(p, ksn_ref[:, KV_W:2 * KV_W].astype(BF16))])
        o_s = acc_ref[:, 0:KV_W] * (1.0 / acc_ref[:, KV_W:KV_W + 1])
        gts = g_ref[...]
        osw_ref[...] = gts[:, 1:2] * o_s + gts[:, 2:3] * ow_ref[...]
        oc_ref[...] = gts[:, 0:1] * oc_ref[...]


def _attn_sample(page_table, pages, qa, qbd, gates, slope_rows, kc, vc, ms, emat, win, ksn, kwn, *, pps, n_new):
    nb, npages = page_table.shape
    nchunks = npages // pps
    past = npages * PAGE
    wbuf = win.shape[2]
    nc = kc.shape[2]
    nsp = ms.shape[1]

    def page_spec(k):
        return pl.BlockSpec((None, 2 * KV_W, PAGE), lambda b, t, pt: (pt[b, t * pps + k], 0, 0))

    per_b = lambda *shape: pl.BlockSpec((None,) + shape, lambda b, t, pt: (b,) + (0,) * len(shape))
    cs = lambda *shape: pl.BlockSpec(shape, lambda b, t, pt: (0,) * len(shape))
    in_specs = [page_spec(k) for k in range(pps)] + [
        per_b(N_KV, ROWS_G, AUG_W), per_b(ROWS_S, KV_W), per_b(ROWS_S, 3), cs(ROWS_S, 1),
        per_b(N_KV, nc, AUG_W), per_b(N_KV, nc, LANES), cs(nc, nsp),
        pl.BlockSpec((nsp, pps * PAGE), lambda b, t, pt: (0, t)),
        per_b(2 * KV_W, wbuf), per_b(LANES, 2 * KV_W), per_b(LANES, 2 * KV_W)]
    out_specs = [per_b(ROWS_S, KV_W), per_b(ROWS_S, LANES)]
    out_shape = [jax.ShapeDtypeStruct((nb, ROWS_S, KV_W), F32), jax.ShapeDtypeStruct((nb, ROWS_S, LANES), F32)]
    scratch = [pltpu.VMEM((N_KV, Q_PAD, nsp), BF16), pltpu.VMEM((ROWS_S, 1), F32),
               pltpu.VMEM((ROWS_S, KV_W + LANES), F32), pltpu.VMEM((ROWS_S, KV_W), F32)]
    gs = pltpu.PrefetchScalarGridSpec(num_scalar_prefetch=1, grid=(nb, nchunks), in_specs=in_specs,
                                      out_specs=out_specs, scratch_shapes=scratch)
    return pl.pallas_call(functools.partial(_attn_sample_kernel, pps=pps, past=past, wbuf=wbuf, n_new=n_new), grid_spec=gs,
                          out_shape=out_shape, compiler_params=_params(2), name="attn_sample")(
        page_table, *([pages] * pps), qa, qbd, gates, slope_rows, kc, vc, ms, emat, win, ksn, kwn)


def _outproj_kernel(x_ref, a_ref, o_ref, nw_ref, sc_ref, sh_ref, g1_ref, wm_ref, woc_ref, won_ref, wo_ref, x1_ref):
    x = x_ref[...]
    h = _mod_norm(x, nw_ref[...], sc_ref[...], sh_ref[...]).astype(BF16)
    mg = jax.nn.sigmoid(_dot(h, wm_ref[...]))
    y_a = _dot(a_ref[...], woc_ref[...])
    y_b = _dot(o_ref[...], won_ref[...])
    mix = (mg[:, 0:D_MODEL] * y_a + mg[:, D_MODEL:2 * D_MODEL] * y_b).astype(BF16)
    x1_ref[...] = x + g1_ref[...] * _dot(mix, wo_ref[...])


def _outproj(x2d, a, o, mod, nw, w_merge, w_oc, w_on, w_o, *, tm):
    n = x2d.shape[0]
    tok = pl.BlockSpec((tm, D_MODEL), lambda i: (i, 0))
    sq = _const_spec((D_MODEL, D_MODEL))
    return pl.pallas_call(
        _outproj_kernel, grid=(n // tm,),
        in_specs=[tok, tok, tok, _const_spec((1, D_MODEL)), mod.spec(1), mod.spec(0), mod.spec(2),
                  _const_spec((D_MODEL, 2 * D_MODEL)), sq, sq, sq],
        out_specs=tok, out_shape=jax.ShapeDtypeStruct((n, D_MODEL), F32),
        compiler_params=_params(1), name="outproj",
    )(x2d, a, o, nw, mod.rows, mod.rows, mod.rows, w_merge, w_oc, w_on, w_o)


FF_CHUNK = D_FF // 2


def _ffn_kernel(x_ref, nw_ref, sc_ref, sh_ref, g2_ref, nf_ref, wg_ref, wu_ref, wd_ref, y_ref):
    x = x_ref[...]
    h = _mod_norm(x, nw_ref[...], sc_ref[...], sh_ref[...]).astype(BF16)
    acc = jnp.zeros(x.shape, F32)
    for c in range(D_FF // FF_CHUNK):
        sl = slice(c * FF_CHUNK, (c + 1) * FF_CHUNK)
        gate = _dot(h, wg_ref[:, sl])
        up = _dot(h, wu_ref[:, sl])
        act = (gate * jax.nn.sigmoid(gate) * up).astype(BF16)
        acc = acc + _dot(act, wd_ref[sl, :])
    x2 = x + g2_ref[...] * acc
    inv = lax.rsqrt(jnp.mean(x2 * x2, axis=-1, keepdims=True) + EPS)
    y_ref[...] = (x2 * inv) * nf_ref[...]


def _ffn(x1, mod, nw2, nf, w_gate, w_up, w_down, *, tm):
    n = x1.shape[0]
    tok = pl.BlockSpec((tm, D_MODEL), lambda i: (i, 0))
    vec = _const_spec((1, D_MODEL))
    return pl.pallas_call(
        _ffn_kernel, grid=(n // tm,),
        in_specs=[tok, vec, mod.spec(4), mod.spec(3), mod.spec(5), vec,
                  _const_spec((D_MODEL, D_FF)), _const_spec((D_MODEL, D_FF)), _const_spec((D_FF, D_MODEL))],
        out_specs=tok, out_shape=jax.ShapeDtypeStruct((n, D_MODEL), F32),
        compiler_params=_params(1), name="ffn",
    )(x1, nw2, mod.rows, mod.rows, mod.rows, nf, w_gate, w_up, w_down)


def _slopes():
    return 2.0 ** (-8.0 * jnp.arange(1, N_HEADS + 1, dtype=F32) / N_HEADS)


def _slope_lanes(slopes):
    parts = _split3(slopes)
    cols = jnp.stack([parts[0], parts[0], parts[1], parts[1], parts[2], parts[2]], axis=1)
    return jnp.pad(cols, ((0, 0), (0, LANES - 6)))


def _pos_lanes(pos_hi, pos_lo):
    cols = jnp.stack([pos_hi, pos_lo] * 3, axis=1).astype(F32)
    return jnp.pad(cols, ((0, 0), (0, LANES - 6))).astype(BF16)


def _token_consts(t):
    pos = jnp.arange(t, dtype=jnp.int32)
    onehot = (pos[:, None] // SEL_BLOCK == jnp.arange(HEAD_DIM, dtype=jnp.int32)[None, :]).astype(F32)
    clo = jnp.concatenate([jnp.zeros((t, HEAD_DIM), F32), onehot], axis=1)
    chi = _pos_lanes((pos // SEL_BLOCK) * SEL_BLOCK, pos % SEL_BLOCK)
    return clo, chi


def _cmp_consts(ncb):
    ci = jnp.arange(ncb, dtype=jnp.int32) * CMP_STRIDE
    return _pos_lanes((ci // SEL_BLOCK) * SEL_BLOCK, ci % SEL_BLOCK)


def _ones_lane():
    return (jnp.arange(LANES) == HEAD_DIM).astype(F32).reshape(1, LANES)


def _imp_matrix(nc, ns_pad):
    c = jnp.arange(nc, dtype=jnp.int32)[:, None]
    s = jnp.arange(ns_pad, dtype=jnp.int32)[None, :]
    per = SEL_BLOCK // CMP_STRIDE
    return ((c // per == s) | ((c % per == per - 1) & (c // per == s - 1))).astype(BF16)


def _prep_weights(w_in, w_phi1, w_phi2, pe_cmp):
    wb = w_in.astype(BF16)
    w_conv_in = wb[:, _C_CONV:_C_QKV]
    w_qkv = jnp.pad(wb[:, _C_QKV:_C_MERGE], ((0, 0), (0, _QKV_COLS_PAD - _QKV_COLS)))
    w_merge = wb[:, _C_MERGE:]
    half = CMP_STRIDE * HEAD_DIM
    w1cat = jnp.concatenate([w_phi1[:, :half], w_phi1[:, half:]], axis=2).astype(BF16)
    w2pad = jnp.pad(w_phi2, ((0, 0), (0, 0), (0, LANES - HEAD_DIM))).astype(BF16)
    pe2 = pe_cmp.reshape(2, 2, half)
    return w_conv_in, w_qkv, w_merge, w1cat, w2pad, pe2


def _prompt_layer(x, mod_p, wts):
    (nw1, nw2, nf, w_conv_in, w_qkv, w_merge, w1cat, w2pad, pe2, w_conv, b_conv, w_oc, w_on, w_o,
     w_gate, w_up, w_down, slopes) = wts
    nb, t, _ = x.shape
    tm = min(512, t)
    x2d = x.reshape(nb * t, D_MODEL)
    mod = _Mod(mod_p, False, t, tm)
    a, tail = _conv_path(x2d, mod, nw1, w_conv_in, w_conv, b_conv, tm=tm, seq_len=t)
    clo, chi = _token_consts(t)
    cv = _ones_lane()
    qt, kvc, kvs, kvw, gts, ksa, vst, kwa, vwt = _qkv_path(x2d, mod, nw1, w_qkv, tm=tm, seq_len=t, consts=(clo, chi))
    npages = t // PAGE
    pt = jnp.broadcast_to(jnp.arange(npages, dtype=jnp.int32), (nb, npages))
    nc = t // CMP_STRIDE
    ns = t // SEL_BLOCK
    nqb = t // Q_BLOCK
    kca, vct = _compress(kvc, pt, pe2, w1cat, w2pad, _cmp_consts(nc), cv,
                         pps=min(16, npages), transpose_v=True, paged=False)
    gates_t = gts[:, :3 * N_HEADS].reshape(nb, nqb, Q_BLOCK, N_KV, GROUP, 3).transpose(0, 3, 1, 5, 4, 2)
    gates_t = gates_t.reshape(nb, N_KV, nqb, 3, GROUP * Q_BLOCK)
    qc_t = _slope_lanes(slopes)[:, :QC_ROWS].reshape(N_KV, GROUP, QC_ROWS).transpose(0, 2, 1)
    qc_t = jnp.repeat(qc_t, Q_BLOCK, axis=2)
    mt = _imp_matrix(nc, ns).T
    o = _attn_prompt_t(qt, gates_t, qc_t, mt, kca, vct, ksa, vst, kwa, vwt)
    x1 = _outproj(x2d, a, o.reshape(nb * t, D_MODEL), mod, nw1, w_merge, w_oc, w_on, w_o, tm=tm)
    y = _ffn(x1, mod, nw2, nf, w_gate, w_up, w_down, tm=tm)
    keep = min(WINDOW, t)
    rows_of = lambda a: a.reshape(nb, 2, N_KV, HEAD_DIM, t).transpose(0, 4, 1, 2, 3)
    state = (rows_of(kvc), rows_of(kvs),
             kvw.reshape(nb, t, 2 * KV_W)[:, t - keep:].reshape(nb, keep, 2, N_KV, HEAD_DIM),
             tail[:, 8 - 2:, :])
    return y.reshape(nb, t, D_MODEL), state


def _sample_layer(x, mod_s, wts, cache_cmp, cache_sel, cache_win, state_conv, page_table):
    (nw1, nw2, nf, w_conv_in, w_qkv, w_merge, w1cat, w2pad, pe2, w_conv, b_conv, w_oc, w_on, w_o,
     w_gate, w_up, w_down, slopes) = wts
    nb, s, _ = x.shape
    n = nb * s
    x2d = x.reshape(n, D_MODEL)
    mod = _Mod(mod_s, True, s, n)
    tpos = jnp.arange(s)
    p1 = jnp.broadcast_to(state_conv[:, 1:2, :], (nb, s, D_MODEL)).reshape(n, D_MODEL)
    p2 = state_conv[:, jnp.minimum(tpos, 1), :].reshape(n, D_MODEL)
    a, u = _conv_path(x2d, mod, nw1, w_conv_in, w_conv, b_conv, tm=n, seq_len=s, prev=(p1, p2))
    qpad, kvc, kvs, kvw, gts = _qkv_path(x2d, mod, nw1, w_qkv, tm=n, seq_len=s)

    npages = page_table.shape[1]
    past = npages * PAGE
    cv = _ones_lane()
    nc = past // CMP_STRIDE
    token_minor = lambda c: c.transpose(0, 2, 3, 4, 1).reshape(c.shape[0], 2 * KV_W, c.shape[1])
    kca, vca = _compress(token_minor(cache_cmp), page_table, pe2, w1cat, w2pad, _cmp_consts(nc), cv,
                         pps=min(16, npages), transpose_v=False, paged=True)

    qh = qpad.reshape(nb, s, N_KV, GROUP, LANES).transpose(0, 2, 3, 1, 4)
    qh = jnp.pad(qh, ((0, 0), (0, 0), (0, 0), (0, Q_PAD - s), (0, 0)))
    sl = jnp.broadcast_to(_slope_lanes(slopes).reshape(1, N_KV, GROUP, 1, LANES), qh.shape)
    qa = jnp.concatenate([qh, sl], axis=-1).reshape(nb, N_KV, ROWS_G, AUG_W)
    eye = jnp.eye(N_KV, dtype=BF16)
    qbd = (qh[..., None, :HEAD_DIM] * eye[None, :, None, None, :, None]).reshape(nb, ROWS_S, KV_W)
    gates = gts[:, :3 * N_HEADS].reshape(nb, s, N_KV, GROUP, 3).transpose(0, 2, 3, 1, 4)
    gates = jnp.pad(gates, ((0, 0), (0, 0), (0, 0), (0, Q_PAD - s), (0, 0))).reshape(nb, ROWS_S, 3)
    slope_rows = jnp.repeat(slopes, Q_PAD).reshape(ROWS_S, 1)
    ns = past // SEL_BLOCK + 1
    nsp = -(-ns // LANES) * LANES
    ms = _imp_matrix(nc, nsp)
    tok = jnp.arange(past, dtype=jnp.int32)
    emat = (jnp.arange(nsp, dtype=jnp.int32)[:, None] == tok[None, :] // SEL_BLOCK).astype(BF16)
    pad_rows = lambda r: jnp.pad(r.reshape(nb, s, 2 * KV_W), ((0, 0), (0, LANES - s), (0, 0)))
    osw, ocg = _attn_sample(page_table, token_minor(cache_sel), qa, qbd, gates, slope_rows, kca, vca,
                            ms, emat, token_minor(cache_win), pad_rows(kvs), pad_rows(kvw),
                            pps=min(16, npages), n_new=s)
    osw = osw.reshape(nb, N_KV, GROUP, Q_PAD, N_KV, HEAD_DIM)
    o_sw = jnp.einsum('bgrqgd->bqgrd', osw)
    o_c = ocg.reshape(nb, N_KV, GROUP, Q_PAD, LANES)[..., :HEAD_DIM].transpose(0, 3, 1, 2, 4)
    o = (o_sw + o_c)[:, :s].reshape(n, N_HEADS * HEAD_DIM).astype(BF16)

    x1 = _outproj(x2d, a, o, mod, nw1, w_merge, w_oc, w_on, w_o, tm=n)
    y = _ffn(x1, mod, nw2, nf, w_gate, w_up, w_down, tm=n)
    kv5 = lambda r: r.reshape(nb, s, 2, N_KV, HEAD_DIM)
    win = jnp.concatenate([cache_win, kv5(kvw)], axis=1)[:, s:]
    state = (kv5(kvc), kv5(kvs), win, u.reshape(nb, s, D_MODEL)[:, s - 2:])
    return y.reshape(nb, s, D_MODEL), state


def kernel(x_prompt, x_sample, c_prompt, c_sample, cache_cmp, cache_sel, cache_win, state_conv, page_table,
           w_ada, b_ada, norm1, w_in, w_conv, b_conv, w_out_conv, pe_cmp, w_phi1, w_phi2, w_o_nsa, w_out,
           norm2, w_gate, w_up, w_down, norm_f):
    depth = w_ada.shape[0]
    assert depth == 1, "single-layer trunk"
    nbp, nbs = c_prompt.shape[0], c_sample.shape[0]
    slopes = _slopes()
    l = 0
    c_all = jnp.concatenate([c_prompt, c_sample], axis=0)
    c_all = jnp.pad(c_all, ((0, -c_all.shape[0] % 8), (0, 0)))
    mod = _ada(c_all, w_ada[l], b_ada[l])
    w_conv_in, w_qkv, w_merge, w1cat, w2pad, pe2 = _prep_weights(w_in[l], w_phi1[l], w_phi2[l], pe_cmp[l])
    row = lambda v: v.reshape(1, -1)
    wts = (row(norm1[l]), row(norm2[l]), row(norm_f), w_conv_in, w_qkv, w_merge, w1cat, w2pad, pe2,
           w_conv[l], row(b_conv[l]), w_out_conv[l].astype(BF16), w_o_nsa[l].astype(BF16), w_out[l].astype(BF16),
           w_gate[l].astype(BF16), w_up[l].astype(BF16), w_down[l].astype(BF16), slopes)
    yp, st_p = _prompt_layer(x_prompt, mod[:nbp], wts)
    ys, st_s = _sample_layer(x_sample, mod[nbp:nbp + nbs], wts, cache_cmp[l], cache_sel[l], cache_win[l],
                             state_conv[l], page_table)
    return (yp, ys, st_p[0][None], st_p[1][None], st_p[2][None], st_p[3][None],
            st_s[0][None], st_s[1][None], st_s[2][None], st_s[3][None])
```

```python
import functools

import jax
import jax.numpy as jnp
import numpy as np
from jax import lax
from jax.experimental import pallas as pl
from jax.experimental.pallas import tpu as pltpu

F32 = jnp.float32
BF16 = jnp.bfloat16

D_MODEL = 1024
N_HEADS = 16
HEAD_DIM = 64
N_KV = 4
GROUP = N_HEADS // N_KV
KV_W = N_KV * HEAD_DIM
CMP_BLOCK = 32
CMP_STRIDE = 16
SEL_BLOCK = 64
N_SEL = 16
WINDOW = 512
D_PHI = 2 * HEAD_DIM
Q_BLOCK = 256
PAGE = 128
D_FF = ((8 * D_MODEL // 3 + 255) // 256) * 256
EPS = 1e-6
NEG = -1e30
FORCE_BONUS = 1e3

LANES = 128
AUG_W = 2 * LANES
HALVES_PER_PAGE = PAGE // CMP_STRIDE
ROW_TILES = 2 * KV_W // LANES
VMEM_LIMIT = 56 * 1024 * 1024
KEY_TILE = 2 * LANES
QC_ROWS = 16
SEL_TILES_PER_ITER = 4

_C_CONV = 0
_C_QKV = 3 * D_MODEL
_C_GATE = _C_QKV + N_HEADS * HEAD_DIM + 6 * KV_W
_C_MERGE = _C_GATE + 3 * N_HEADS
_QKV_COLS = _C_MERGE - _C_QKV
_QKV_COLS_PAD = -(-_QKV_COLS // LANES) * LANES


def _dot(a, b):
    return jnp.dot(a, b, preferred_element_type=F32)


def _dot_nt(a, b):
    return lax.dot_general(a, b, (((1,), (1,)), ((), ())), preferred_element_type=F32)


def _params(n_axes):
    return pltpu.CompilerParams(dimension_semantics=("arbitrary",) * n_axes, vmem_limit_bytes=VMEM_LIMIT)


def _const_spec(shape):
    return pl.BlockSpec(shape, lambda *_: (0,) * len(shape))


def _mod_norm(x, nw, sc, sh):
    inv = lax.rsqrt(jnp.mean(x * x, axis=-1, keepdims=True) + EPS)
    return (x * inv) * nw * (1.0 + sc) + sh


def _split3(x):
    a = x.astype(BF16)
    r = x - a.astype(F32)
    b = r.astype(BF16)
    c = (r - b.astype(F32)).astype(BF16)
    return a, b, c


def _ada_kernel(c_ref, w_ref, b_ref, o_ref):
    c = c_ref[...]
    s = c * jax.nn.sigmoid(c)
    o_ref[...] = jnp.dot(s, w_ref[...], preferred_element_type=F32, precision=lax.Precision.HIGHEST) + b_ref[...]


def _ada(c, w_ada, b_ada):
    n = c.shape[0]
    tn = 1536
    return pl.pallas_call(
        _ada_kernel,
        grid=(6 * D_MODEL // tn,),
        in_specs=[_const_spec((n, D_MODEL)),
                  pl.BlockSpec((D_MODEL, tn), lambda j: (0, j)),
                  pl.BlockSpec((1, tn), lambda j: (0, j))],
        out_specs=pl.BlockSpec((n, tn), lambda j: (0, j)),
        out_shape=jax.ShapeDtypeStruct((n, 6 * D_MODEL), F32),
        compiler_params=_params(1),
        name="ada",
    )(c, w_ada, b_ada.reshape(1, -1))


class _Mod:
    def __init__(self, mod, per_token, seq_len, tm):
        self.per_token = per_token
        if per_token:
            self.rows = jnp.repeat(mod, seq_len, axis=0)
        else:
            self.rows = mod.reshape(mod.shape[0], 1, 6 * D_MODEL)
        self.tiles_per_seq = None if per_token else seq_len // tm
        self.tm = tm

    def spec(self, k):
        if self.per_token:
            return pl.BlockSpec((self.tm, D_MODEL), lambda i: (i, k))
        tps = self.tiles_per_seq
        return pl.BlockSpec((None, 1, D_MODEL), lambda i: (i // tps, 0, k))


def _conv_kernel(*refs, carry_rows, seq_len):
    if carry_rows:
        (x_ref, nw_ref, sc_ref, sh_ref, w_ref, wc_ref, bc_ref, a_ref, tail_ref, carry_ref) = refs
    else:
        (x_ref, nw_ref, sc_ref, sh_ref, w_ref, wc_ref, bc_ref, p1_ref, p2_ref, a_ref, tail_ref) = refs
    tm = x_ref.shape[0]
    h = _mod_norm(x_ref[...], nw_ref[...], sc_ref[...], sh_ref[...]).astype(BF16)
    z = _dot(h, w_ref[...])
    bg = z[:, 0:D_MODEL]
    u = z[:, D_MODEL:2 * D_MODEL] * z[:, 2 * D_MODEL:3 * D_MODEL]
    row = lax.broadcasted_iota(jnp.int32, (tm, 1), 0)
    u1 = pltpu.roll(u, 1, 0)
    u2 = pltpu.roll(u, 2, 0)
    if carry_rows:
        @pl.when(pl.program_id(0) % carry_rows == 0)
        def _():
            carry_ref[...] = jnp.zeros_like(carry_ref)
        c0 = carry_ref[0:1, :]
        c1 = carry_ref[1:2, :]
        u1 = jnp.where(row == 0, c1, u1)
        u2 = jnp.where(row == 0, c0, jnp.where(row == 1, c1, u2))
        carry_ref[0:2, :] = u[tm - 2:tm, :]
        tail_ref[...] = u[tm - 8:tm, :]
    else:
        pos = lax.rem(row, seq_len)
        u1 = jnp.where(pos >= 1, u1, p1_ref[...])
        u2 = jnp.where(pos >= 2, u2, p2_ref[...])
        tail_ref[...] = u
    v = bc_ref[...] + wc_ref[0:1, :] * u2 + wc_ref[1:2, :] * u1 + wc_ref[2:3, :] * u
    a_ref[...] = (bg * v).astype(BF16)


def _conv_path(x2d, mod, nw, w_conv_in, w_conv, b_conv, *, tm, seq_len, prev=None):
    n = x2d.shape[0]
    tok = pl.BlockSpec((tm, D_MODEL), lambda i: (i, 0))
    in_specs = [tok, _const_spec((1, D_MODEL)), mod.spec(1), mod.spec(0),
                _const_spec((D_MODEL, 3 * D_MODEL)), _const_spec((3, D_MODEL)), _const_spec((1, D_MODEL))]
    args = [x2d, nw, mod.rows, mod.rows, w_conv_in, w_conv, b_conv]
    if prev is None:
        tps = seq_len // tm
        out_specs = [tok, pl.BlockSpec((None, 8, D_MODEL), lambda i: (i // tps, 0, 0))]
        out_shape = [jax.ShapeDtypeStruct((n, D_MODEL), BF16), jax.ShapeDtypeStruct((n // seq_len, 8, D_MODEL), F32)]
        scratch = [pltpu.VMEM((8, D_MODEL), F32)]
        kern = functools.partial(_conv_kernel, carry_rows=tps, seq_len=seq_len)
    else:
        in_specs += [tok, tok]
        args += list(prev)
        out_specs = [tok, tok]
        out_shape = [jax.ShapeDtypeStruct((n, D_MODEL), BF16), jax.ShapeDtypeStruct((n, D_MODEL), F32)]
        scratch = []
        kern = functools.partial(_conv_kernel, carry_rows=0, seq_len=seq_len)
    return pl.pallas_call(kern, grid=(n // tm,), in_specs=in_specs, out_specs=out_specs, out_shape=out_shape,
                          scratch_shapes=scratch, compiler_params=_params(1), name="conv_path")(*args)


def _qkv_kernel(*refs, aug):
    if aug:
        (x_ref, nw_ref, sc_ref, sh_ref, w_ref, clo_ref, chi_ref,
         q_ref, kvc_ref, kvs_ref, kvw_ref, g_ref, ksa_ref, vsa_ref, kwa_ref, vwa_ref) = refs
    else:
        (x_ref, nw_ref, sc_ref, sh_ref, w_ref, q_ref, kvc_ref, kvs_ref, kvw_ref, g_ref) = refs
    tm = x_ref.shape[0]
    h = _mod_norm(x_ref[...], nw_ref[...], sc_ref[...], sh_ref[...]).astype(BF16)
    z = _dot(h, w_ref[...])
    low = lax.broadcasted_iota(jnp.int32, (tm, LANES), 1) < HEAD_DIM
    nq = N_HEADS * HEAD_DIM
    for c in range(N_HEADS // 2):
        t = z[:, c * LANES:(c + 1) * LANES] * (HEAD_DIM ** -0.5)
        if aug:
            tt = t.T.astype(BF16)
            q_ref[2 * c] = tt[0:HEAD_DIM]
            q_ref[2 * c + 1] = tt[HEAD_DIM:2 * HEAD_DIM]
        else:
            q_ref[:, (2 * c) * LANES:(2 * c + 1) * LANES] = jnp.where(low, t, 0.0).astype(BF16)
            q_ref[:, (2 * c + 1) * LANES:(2 * c + 2) * LANES] = jnp.where(low, pltpu.roll(t, HEAD_DIM, 1), 0.0).astype(BF16)
    if aug:
        kvc_ref[...] = z[:, nq:nq + 2 * KV_W].T
        kvs_ref[...] = z[:, nq + 2 * KV_W:nq + 4 * KV_W].T
    else:
        kvc_ref[...] = z[:, nq:nq + 2 * KV_W]
        kvs_ref[...] = z[:, nq + 2 * KV_W:nq + 4 * KV_W]
    kvw_ref[...] = z[:, nq + 4 * KV_W:nq + 6 * KV_W]
    g_ref[...] = jax.nn.sigmoid(z[:, nq + 6 * KV_W:nq + 6 * KV_W + LANES])
    if aug:
        chi = chi_ref[...]
        ones_row = (lax.broadcasted_iota(jnp.int32, (LANES - HEAD_DIM, KEY_TILE), 0) == 0).astype(BF16)
        for br, (ka_ref, va_ref) in enumerate(((ksa_ref, vsa_ref), (kwa_ref, vwa_ref))):
            kbase = nq + 2 * KV_W * (br + 1)
            clo = clo_ref[...] if br == 0 else 0.0
            for g in range(N_KV):
                kt = z[:, kbase + (g // 2) * LANES:kbase + (g // 2 + 1) * LANES]
                if g % 2:
                    kt = pltpu.roll(kt, HEAD_DIM, 1)
                ka_ref[g, :, 0:LANES] = jnp.where(low, kt, clo).astype(BF16)
                ka_ref[g, :, LANES:AUG_W] = chi
            for c in range(N_KV // 2):
                vt = z[:, kbase + KV_W + c * LANES:kbase + KV_W + (c + 1) * LANES].T.astype(BF16)
                for gg in range(2):
                    for j in range(tm // KEY_TILE):
                        va_ref[2 * c + gg, j, 0:HEAD_DIM, :] = vt[gg * HEAD_DIM:(gg + 1) * HEAD_DIM, j * KEY_TILE:(j + 1) * KEY_TILE]
                        va_ref[2 * c + gg, j, HEAD_DIM:LANES, :] = ones_row


def _qkv_path(x2d, mod, nw, w_qkv, *, tm, seq_len, consts=None):
    n = x2d.shape[0]
    aug = consts is not None
    tok = lambda w: pl.BlockSpec((tm, w), lambda i: (i, 0))
    in_specs = [tok(D_MODEL), _const_spec((1, D_MODEL)), mod.spec(1), mod.spec(0),
                _const_spec((D_MODEL, _QKV_COLS_PAD))]
    args = [x2d, nw, mod.rows, mod.rows, w_qkv]
    out_specs = [tok(N_HEADS * LANES), tok(2 * KV_W), tok(2 * KV_W), tok(2 * KV_W), tok(LANES)]
    out_shape = [jax.ShapeDtypeStruct((n, N_HEADS * LANES), BF16)] + \
                [jax.ShapeDtypeStruct((n, 2 * KV_W), F32)] * 3 + [jax.ShapeDtypeStruct((n, LANES), F32)]
    if aug:
        tps = seq_len // tm
        nb = n // seq_len
        pos = lambda w: pl.BlockSpec((tm, w), lambda i: (i % tps, 0))
        in_specs += [pos(LANES), pos(LANES)]
        args += list(consts)
        out_specs[0] = pl.BlockSpec((None, N_HEADS, HEAD_DIM, tm), lambda i: (i // tps, 0, 0, i % tps))
        out_shape[0] = jax.ShapeDtypeStruct((nb, N_HEADS, HEAD_DIM, seq_len), BF16)
        for k in (1, 2):
            out_specs[k] = pl.BlockSpec((None, 2 * KV_W, tm), lambda i: (i // tps, 0, i % tps))
            out_shape[k] = jax.ShapeDtypeStruct((nb, 2 * KV_W, seq_len), F32)
        ka = pl.BlockSpec((None, N_KV, tm, AUG_W), lambda i: (i // tps, 0, i % tps, 0))
        va = pl.BlockSpec((None, N_KV, tm // KEY_TILE, LANES, KEY_TILE), lambda i: (i // tps, 0, i % tps, 0, 0))
        out_specs += [ka, va, ka, va]
        ka_s = jax.ShapeDtypeStruct((nb, N_KV, seq_len, AUG_W), BF16)
        va_s = jax.ShapeDtypeStruct((nb, N_KV, seq_len // KEY_TILE, LANES, KEY_TILE), BF16)
        out_shape += [ka_s, va_s, ka_s, va_s]
    return pl.pallas_call(functools.partial(_qkv_kernel, aug=aug), grid=(n // tm,), in_specs=in_specs,
                          out_specs=out_specs, out_shape=out_shape, compiler_params=_params(1), name="qkv_path")(*args)


def _block_major_loader(ref, k, c):
    def load(p, nrows):
        return ref[k, c, p * HALVES_PER_PAGE:p * HALVES_PER_PAGE + nrows, :]
    return load


def _gather_pair(load, nrows):
    low = lax.broadcasted_iota(jnp.int32, (nrows, LANES), 1) < HEAD_DIM
    even, odd = [], []
    for qq in range(CMP_STRIDE // 2):
        a = load(2 * qq, nrows)
        b = load(2 * qq + 1, nrows)
        even.append(jnp.where(low, a, pltpu.roll(b, HEAD_DIM, 1)))
        odd.append(jnp.where(low, pltpu.roll(a, HEAD_DIM, 1), b))
    return jnp.concatenate(even, axis=1), jnp.concatenate(odd, axis=1)


def _compress_kernel(pt_ref, *refs, pps, transpose_v):
    pages = refs[:pps]
    nxt_ref, pe_ref, w1_ref, w2_ref, chi_ref, cv_ref, perm_ref, kc_ref, vc_ref, xt_ref = refs[pps:]
    perm = perm_ref[...]
    for k, pg in enumerate(list(pages) + [nxt_ref]):
        for c in range(ROW_TILES):
            xt_ref[k, c] = _dot_nt(perm, pg[c * LANES:(c + 1) * LANES, :].astype(BF16))
    loaders = lambda c: [_block_major_loader(xt_ref, k, c) for k in range(pps)]
    look_loader = lambda c: _block_major_loader(xt_ref, pps, c)
    t = pl.program_id(1)
    last = t == pl.num_programs(1) - 1
    nhb = pps * HALVES_PER_PAGE
    n = N_KV * nhb
    row = lax.broadcasted_iota(jnp.int32, (n, 1), 0)
    chi = chi_ref[...]
    for kv in range(2):
        by_group, look = [], []
        for cc in range(N_KV // 2):
            c = kv * (N_KV // 2) + cc
            pairs = [_gather_pair(ld, HALVES_PER_PAGE) for ld in loaders(c)]
            by_group += [[ev for ev, _ in pairs], [od for _, od in pairs]]
            look += list(_gather_pair(look_loader(c), 1))
        parts = [x for group in by_group for x in group]
        extra = jnp.concatenate([pe_ref[kv], jnp.zeros((2, CMP_STRIDE * HEAD_DIM), F32)] + look, axis=0)
        xmat = jnp.concatenate(parts + [extra], axis=0).astype(BF16)
        hab = _dot(xmat, w1_ref[kv])
        ha = hab[0:n, 0:D_PHI]
        hb = hab[0:n, D_PHI:2 * D_PHI]
        pbias = hab[n:n + 1, 0:D_PHI] + hab[n + 1:n + 2, D_PHI:2 * D_PHI]
        hbn = pltpu.roll(hb, n - 1, 0)
        for g in range(N_KV):
            la = jnp.where(last, 0.0, hab[n + 4 + g:n + 5 + g, D_PHI:2 * D_PHI])
            hbn = jnp.where(row == g * nhb + nhb - 1, la, hbn)
        act = jax.nn.gelu(ha + hbn + pbias).astype(BF16)
        out = _dot(act, w2_ref[kv])
        for g in range(N_KV):
            blk = out[g * nhb:(g + 1) * nhb, :]
            if kv == 0:
                kc_ref[g, :, 0:LANES] = blk.astype(BF16)
                kc_ref[g, :, LANES:AUG_W] = chi
            elif transpose_v:
                vc_ref[g, :, :] = (blk + cv_ref[...]).T.astype(BF16)
            else:
                vc_ref[g, :, :] = (blk + cv_ref[...]).astype(BF16)


def _compress(pages, page_table, pe2, w1cat, w2pad, chi_c, cv, *, pps, transpose_v, paged):
    nb, npages = page_table.shape
    nchunks = npages // pps
    nhb = pps * HALVES_PER_PAGE
    ncb = npages * HALVES_PER_PAGE
    where = (lambda b, j: (j, 0, 0)) if paged else (lambda b, j: (b, 0, j))

    def page_spec(k):
        return pl.BlockSpec((None, 2 * KV_W, PAGE), lambda b, t, pt: where(b, pt[b, t * pps + k]))

    nxt_spec = pl.BlockSpec((None, 2 * KV_W, PAGE),
                            lambda b, t, pt: where(b, pt[b, jnp.minimum((t + 1) * pps, npages - 1)]))
    scratch = [pltpu.VMEM((pps + 1, ROW_TILES, PAGE, LANES), F32)]
    r = jnp.arange(PAGE, dtype=jnp.int32)
    perm = (r[None, :] == (CMP_STRIDE * (r % HALVES_PER_PAGE) + r // HALVES_PER_PAGE)[:, None]).astype(BF16)
    cs = lambda shape: pl.BlockSpec(shape, lambda b, t, pt: (0,) * len(shape))
    in_specs = [page_spec(k) for k in range(pps)] + [
        nxt_spec, cs((2, 2, CMP_STRIDE * HEAD_DIM)), cs((2, CMP_STRIDE * HEAD_DIM, 2 * D_PHI)),
        cs((2, D_PHI, LANES)), pl.BlockSpec((nhb, LANES), lambda b, t, pt: (t, 0)), cs((1, LANES)),
        cs((PAGE, PAGE))]
    if transpose_v:
        v_spec = pl.BlockSpec((None, N_KV, LANES, nhb), lambda b, t, pt: (b, 0, 0, t))
        v_shape = jax.ShapeDtypeStruct((nb, N_KV, LANES, ncb), BF16)
    else:
        v_spec = pl.BlockSpec((None, N_KV, nhb, LANES), lambda b, t, pt: (b, 0, t, 0))
        v_shape = jax.ShapeDtypeStruct((nb, N_KV, ncb, LANES), BF16)
    out_specs = [pl.BlockSpec((None, N_KV, nhb, AUG_W), lambda b, t, pt: (b, 0, t, 0)), v_spec]
    out_shape = [jax.ShapeDtypeStruct((nb, N_KV, ncb, AUG_W), BF16), v_shape]
    gs = pltpu.PrefetchScalarGridSpec(num_scalar_prefetch=1, grid=(nb, nchunks), in_specs=in_specs, out_specs=out_specs,
                                      scratch_shapes=scratch)
    return pl.pallas_call(functools.partial(_compress_kernel, pps=pps, transpose_v=transpose_v),
                          grid_spec=gs, out_shape=out_shape,
                          compiler_params=_params(2), name="compress")(
        page_table, *([pages] * (pps + 1)), pe2, w1cat, w2pad, chi_c, cv, perm)


REMOVED = -3e38


def _top_k_mask(score, index, k, axis):
    work = score
    selected = jnp.zeros(score.shape, jnp.bool_)
    for _ in range(k):
        best = jnp.max(work, axis=axis, keepdims=True)
        first = jnp.min(jnp.where(work == best, index, score.shape[axis]), axis=axis, keepdims=True)
        hit = index == first
        selected = selected | hit
        work = jnp.where(hit, REMOVED, work)
    return selected


def _attn_prompt_t_kernel(q_ref, g_ref, qc_ref, mt_ref, kc_ref, vc_ref, ks_ref, vs_ref, kw_ref, vw_ref, o_ref,
                          qa_ref, qw_ref, m_ref, acc_ref):
    nc = kc_ref.shape[0]
    ns = mt_ref.shape[0]
    kt = vs_ref.shape[2]
    ncol = GROUP * Q_BLOCK
    qb = pl.program_id(2)
    q0 = qb * Q_BLOCK
    qpos = q0 + (lax.broadcasted_iota(jnp.int32, (1, ncol), 1) & (Q_BLOCK - 1))

    for ref in (qw_ref, qa_ref):
        for r in range(GROUP):
            ref[0:HEAD_DIM, r * Q_BLOCK:(r + 1) * Q_BLOCK] = q_ref[r]
        ref[HEAD_DIM:2 * HEAD_DIM, :] = jnp.zeros((HEAD_DIM, ncol), BF16)
        ref[2 * HEAD_DIM:2 * HEAD_DIM + QC_ROWS, :] = qc_ref[...]
        ref[2 * HEAD_DIM + QC_ROWS:AUG_W, :] = jnp.zeros((AUG_W - 2 * HEAD_DIM - QC_ROWS, ncol), BF16)

    s = _dot(kc_ref[...], qw_ref[...])
    ci = lax.broadcasted_iota(jnp.int32, (nc, 1), 0)
    c_valid = ci * CMP_STRIDE + (CMP_BLOCK - 1) <= qpos
    s = jnp.where(c_valid, s, NEG)
    e = jnp.exp(s - jnp.max(s, axis=0, keepdims=True))
    p = jnp.where(c_valid, e * (1.0 / jnp.sum(e, axis=0, keepdims=True)), 0.0)
    o_c = _dot(vc_ref[...], p.astype(BF16))[0:HEAD_DIM]

    psum = p[:, 0:Q_BLOCK]
    for r in range(1, GROUP):
        psum = psum + p[:, r * Q_BLOCK:(r + 1) * Q_BLOCK]
    mt = mt_ref[...]
    imp = sum(_dot(mt, part) for part in _split3(psum))
    si = lax.broadcasted_iota(jnp.int32, (ns, Q_BLOCK), 0)
    qpos_t = q0 + lax.broadcasted_iota(jnp.int32, (ns, Q_BLOCK), 1)
    cur = lax.shift_right_logical(qpos_t, 6)
    s_valid = si * SEL_BLOCK <= qpos_t
    forced = (si == 0) | (si == cur) | (si == cur - 1)
    score = jnp.where(s_valid, imp + jnp.where(forced, FORCE_BONUS, 0.0), NEG)
    selected = _top_k_mask(score, si, min(N_SEL, ns), axis=0)
    bias_t = jnp.where(selected, 0.0, NEG).astype(BF16)
    qa_ref[HEAD_DIM:HEAD_DIM + ns, :] = jnp.concatenate([bias_t] * GROUP, axis=1)

    t_hi = lax.div(q0, kt)
    key_iota = lax.broadcasted_iota(jnp.int32, (kt, 1), 0)

    n_back = WINDOW // kt
    s_w, v_w = [], []
    for j in range(n_back + 1):
        tw = t_hi - n_back + j
        tc = jnp.maximum(tw, 0)
        kpos = tw * kt + key_iota
        s = _dot(kw_ref[pl.ds(pl.multiple_of(tc * kt, kt), kt), :], qw_ref[...])
        if j == 0:
            oldest = jnp.where(tw >= 0, qpos - WINDOW, jnp.iinfo(jnp.int32).max)
            s = jnp.where(kpos > oldest, s, NEG)
        elif j == n_back:
            s = jnp.where(kpos <= qpos, s, NEG)
        else:
            s = jnp.where(tw >= 0, s, NEG)
        s_w.append(s)
        v_w.append(vw_ref[tc])
    m_w = functools.reduce(jnp.maximum, [jnp.max(s, axis=0, keepdims=True) for s in s_w])
    acc_w = sum(_dot(v, jnp.exp(s - m_w).astype(BF16)) for s, v in zip(s_w, v_w))
    o_w = acc_w[0:HEAD_DIM] * (1.0 / acc_w[HEAD_DIM:HEAD_DIM + 1])

    m_ref[...] = jnp.full(m_ref.shape, NEG, F32)
    acc_ref[...] = jnp.zeros(acc_ref.shape, F32)

    def sel_update(runs, causal_last):
        qa = qa_ref[...]
        ss, vts = [], []
        for ri, (t0, n) in enumerate(runs):
            k0 = pl.multiple_of(t0 * kt, kt)
            s = _dot(ks_ref[pl.ds(k0, n * kt), :], qa)
            for i in range(n):
                si_ = s[i * kt:(i + 1) * kt]
                if causal_last and ri == len(runs) - 1 and i == n - 1:
                    si_ = jnp.where(k0 + i * kt + key_iota <= qpos, si_, NEG)
                ss.append(si_)
                vts.append(vs_ref[t0 + i])
        m_old = m_ref[...]
        m_new = functools.reduce(jnp.maximum, [m_old] + [jnp.max(s, axis=0, keepdims=True) for s in ss])
        pv = sum(_dot(vt, jnp.exp(s - m_new).astype(BF16)) for vt, s in zip(vts, ss))
        acc_ref[...] = jnp.exp(m_old - m_new) * acc_ref[...] + pv
        m_ref[...] = m_new

    per_tile = kt // SEL_BLOCK
    lo_blk = jnp.min(jnp.where(selected & (si >= per_tile), si, ns))
    start = jnp.clip(lax.div(lo_blk, per_tile), 1, jnp.maximum(t_hi, 1))
    n_plain = jnp.maximum(t_hi - start, 0)
    n_group = lax.div(n_plain, SEL_TILES_PER_ITER)

    def body(i, carry):
        sel_update([(start + i * SEL_TILES_PER_ITER, SEL_TILES_PER_ITER)], False)
        return carry

    lax.fori_loop(0, n_group, body, 0)
    rem = n_plain - n_group * SEL_TILES_PER_ITER
    for left in range(SEL_TILES_PER_ITER):
        @pl.when((rem == left) & (t_hi >= 1))
        def _():
            sel_update([(0, 1), (t_hi - left, left + 1)], True)

    @pl.when(t_hi == 0)
    def _():
        sel_update([(0, 1)], True)

    acc = acc_ref[...]
    o_s = acc[0:HEAD_DIM] * (1.0 / acc[HEAD_DIM:HEAD_DIM + 1])

    gts = g_ref[...]
    o = gts[0:1] * o_c + gts[1:2] * o_s + gts[2:3] * o_w
    for c in range(GROUP // 2):
        pair = jnp.concatenate([o[:, (2 * c) * Q_BLOCK:(2 * c + 1) * Q_BLOCK],
                                o[:, (2 * c + 1) * Q_BLOCK:(2 * c + 2) * Q_BLOCK]], axis=0)
        o_ref[:, c * LANES:(c + 1) * LANES] = pair.T.astype(BF16)


def _attn_prompt_t(qt, gates_t, qc_t, mt, kc, vct, ks, vst, kw, vwt):
    nb, t = qt.shape[0], qt.shape[3]
    nc = kc.shape[2]
    ns = mt.shape[0]
    ntile, kt = vst.shape[2], vst.shape[4]
    ncol = GROUP * Q_BLOCK
    per_bg = lambda *shape: pl.BlockSpec((None, None) + shape, lambda b, g, i: (b, g) + (0,) * len(shape))
    in_specs = [pl.BlockSpec((None, GROUP, HEAD_DIM, Q_BLOCK), lambda b, g, i: (b, g, 0, i)),
                pl.BlockSpec((None, None, None, 3, ncol), lambda b, g, i: (b, g, i, 0, 0)),
                pl.BlockSpec((None, QC_ROWS, ncol), lambda b, g, i: (g, 0, 0)),
                pl.BlockSpec((ns, nc), lambda b, g, i: (0, 0)),
                per_bg(nc, AUG_W), per_bg(LANES, nc), per_bg(t, AUG_W), per_bg(ntile, LANES, kt),
                per_bg(t, AUG_W), per_bg(ntile, LANES, kt)]
    out_spec = pl.BlockSpec((None, Q_BLOCK, GROUP * HEAD_DIM), lambda b, g, i: (b, i, g))
    scratch = [pltpu.VMEM((AUG_W, ncol), BF16), pltpu.VMEM((AUG_W, ncol), BF16),
               pltpu.VMEM((1, ncol), F32), pltpu.VMEM((LANES, ncol), F32)]
    return pl.pallas_call(_attn_prompt_t_kernel, grid=(nb, N_KV, t // Q_BLOCK),
                          in_specs=in_specs, out_specs=out_spec,
                          out_shape=jax.ShapeDtypeStruct((nb, t, N_HEADS * HEAD_DIM), BF16),
                          scratch_shapes=scratch, compiler_params=_params(3), name="attn_prompt")(
        qt, gates_t, qc_t, mt, kc, vct, ks, vst, kw, vwt)


Q_PAD = 8
ROWS_G = GROUP * Q_PAD
ROWS_S = N_KV * ROWS_G


def _attn_sample_kernel(pt_ref, *refs, pps, past, wbuf, n_new):
    pages = refs[:pps]
    (qa_ref, qbd_ref, g_ref, slope_ref, kc_ref, vc_ref, ms_ref, e_ref, win_ref, ksn_ref, kwn_ref,
     osw_ref, oc_ref, bias_ref, m_ref, acc_ref, ow_ref) = refs[pps:]
    t = pl.program_id(1)
    nchunks = pl.num_programs(1)
    nc = kc_ref.shape[1]
    ns = past // SEL_BLOCK + 1
    row = lax.broadcasted_iota(jnp.int32, (ROWS_S, 1), 0)
    qpos = past + (row & (Q_PAD - 1))
    qposf = qpos.astype(F32)
    slope = slope_ref[...]
    qbd = qbd_ref[...]

    @pl.when(t == 0)
    def _():
        nsp = ms_ref.shape[1]
        rg = lax.broadcasted_iota(jnp.int32, (ROWS_G, 1), 0)
        qpos_g = past + (rg & (Q_PAD - 1))
        r8 = lax.broadcasted_iota(jnp.int32, (Q_PAD, 1), 0)
        qpos8 = past + r8
        si = lax.broadcasted_iota(jnp.int32, (Q_PAD, nsp), 1)
        for g in range(N_KV):
            s = _dot_nt(qa_ref[g], kc_ref[g])
            ci = lax.broadcasted_iota(jnp.int32, (1, nc), 1)
            c_valid = ci * CMP_STRIDE + (CMP_BLOCK - 1) <= qpos_g
            s = jnp.where(c_valid, s, NEG)
            e = jnp.exp(s - jnp.max(s, axis=1, keepdims=True))
            p = jnp.where(c_valid, e * (1.0 / jnp.sum(e, axis=1, keepdims=True)), 0.0)
            oc_ref[g * ROWS_G:(g + 1) * ROWS_G, :] = _dot(p.astype(BF16), vc_ref[g])
            psum = p[0:Q_PAD]
            for r in range(1, GROUP):
                psum = psum + p[r * Q_PAD:(r + 1) * Q_PAD]
            ms = ms_ref[...]
            imp = sum(_dot(part, ms) for part in _split3(psum))
            cur = lax.shift_right_logical(qpos8, 6)
            s_valid = (si * SEL_BLOCK <= qpos8) & (si < ns)
            forced = (si == 0) | (si == cur) | (si == cur - 1)
            score = jnp.where(s_valid, imp + jnp.where(forced, FORCE_BONUS, 0.0), NEG)
            selected = _top_k_mask(jnp.where(si < ns, score, REMOVED), si, min(N_SEL, ns), axis=1)
            bias_ref[g] = jnp.where(selected & (si < ns), 0.0, NEG).astype(BF16)

        kw_t = win_ref[0:KV_W, :].astype(BF16)
        vw_t = win_ref[KV_W:2 * KV_W, :].astype(BF16)
        kpos = past - wbuf + lax.broadcasted_iota(jnp.int32, (1, wbuf), 1)
        s1 = _dot(qbd, kw_t) - slope * (qposf - kpos.astype(F32))
        s1 = jnp.where((kpos <= qpos) & (qpos - kpos < WINDOW) & (kpos >= 0), s1, NEG)
        kn = kwn_ref[:, 0:KV_W].astype(BF16)
        vn = kwn_ref[:, KV_W:2 * KV_W].astype(BF16)
        li = lax.broadcasted_iota(jnp.int32, (1, LANES), 1)
        kposn = past + li
        s2 = _dot_nt(qbd, kn) - slope * (qposf - kposn.astype(F32))
        s2 = jnp.where((li < n_new) & (kposn <= qpos) & (qpos - kposn < WINDOW), s2, NEG)
        m = jnp.maximum(jnp.max(s1, axis=1, keepdims=True), jnp.max(s2, axis=1, keepdims=True))
        p1 = jnp.exp(s1 - m)
        p2 = jnp.exp(s2 - m)
        den = jnp.sum(p1, axis=1, keepdims=True) + jnp.sum(p2, axis=1, keepdims=True)
        inv = 1.0 / den
        ow_ref[...] = (_dot_nt((p1 * inv).astype(BF16), vw_t) + _dot((p2 * inv).astype(BF16), vn))
        m_ref[...] = jnp.full(m_ref.shape, NEG, F32)
        acc_ref[...] = jnp.zeros(acc_ref.shape, F32)

    def block_bias(emat):
        rows = []
        for g in range(N_KV):
            bt = _dot(bias_ref[g], emat)
            rows += [bt] * GROUP
        return jnp.concatenate(rows, axis=0)

    def update(scores, pv_fns):
        m_old = m_ref[...]
        m_new = functools.reduce(jnp.maximum, [m_old] + [jnp.max(s, axis=1, keepdims=True) for s in scores])
        alpha = jnp.exp(m_old - m_new)
        ps = [jnp.exp(s - m_new) for s in scores]
        ones = functools.reduce(jnp.add, [jnp.sum(p, axis=1, keepdims=True) for p in ps])
        pv = functools.reduce(jnp.add, [f(p.astype(BF16)) for f, p in zip(pv_fns, ps)])
        acc_ref[:, 0:KV_W] = alpha * acc_ref[:, 0:KV_W] + pv
        acc_ref[:, KV_W:KV_W + LANES] = alpha * acc_ref[:, KV_W:KV_W + LANES] + ones
        m_ref[...] = m_new

    bias_chunk = block_bias(e_ref[...])
    li = lax.broadcasted_iota(jnp.int32, (1, PAGE), 1)
    scores, pv_fns = [], []
    for k in range(pps):
        pg = pages[k]
        kpos = (t * pps + k) * PAGE + li
        s = _dot(qbd, pg[0:KV_W, :].astype(BF16)) - slope * (qposf - kpos.astype(F32))
        scores.append(s + bias_chunk[:, k * PAGE:(k + 1) * PAGE])
        pv_fns.append(lambda p, pg=pg: _dot_nt(p, pg[KV_W:2 * KV_W, :].astype(BF16)))
    update(scores, pv_fns)

    @pl.when(t == nchunks - 1)
    def _():
        nsp = ms_ref.shape[1]
        kposn = past + li
        e_new = (lax.broadcasted_iota(jnp.int32, (nsp, PAGE), 0) == past // SEL_BLOCK).astype(BF16)
        s = _dot_nt(qbd, ksn_ref[:, 0:KV_W].astype(BF16)) - slope * (qposf - kposn.astype(F32)) + block_bias(e_new)
        s = jnp.where((li < n_new) & (kposn <= qpos), s, NEG)
        update([s], [lambda p: _dot(p, ksn_ref[:, KV_W:2 * KV_W].astype(BF16))])
        o_s = acc_ref[:, 0:KV_W] * (1.0 / acc_ref[:, KV_W:KV_W + 1])
        gts = g_ref[...]
        osw_ref[...] = gts[:, 1:2] * o_s + gts[:, 2:3] * ow_ref[...]
        oc_ref[...] = gts[:, 0:1] * oc_ref[...]


def _attn_sample(page_table, pages, qa, qbd, gates, slope_rows, kc, vc, ms, emat, win, ksn, kwn, *, pps, n_new):
    nb, npages = page_table.shape
    nchunks = npages // pps
    past = npages * PAGE
    wbuf = win.shape[2]
    nc = kc.shape[2]
    nsp = ms.shape[1]

    def page_spec(k):
        return pl.BlockSpec((None, 2 * KV_W, PAGE), lambda b, t, pt: (pt[b, t * pps + k], 0, 0))

    per_b = lambda *shape: pl.BlockSpec((None,) + shape, lambda b, t, pt: (b,) + (0,) * len(shape))
    cs = lambda *shape: pl.BlockSpec(shape, lambda b, t, pt: (0,) * len(shape))
    in_specs = [page_spec(k) for k in range(pps)] + [
        per_b(N_KV, ROWS_G, AUG_W), per_b(ROWS_S, KV_W), per_b(ROWS_S, 3), cs(ROWS_S, 1),
        per_b(N_KV, nc, AUG_W), per_b(N_KV, nc, LANES), cs(nc, nsp),
        pl.BlockSpec((nsp, pps * PAGE), lambda b, t, pt: (0, t)),
        per_b(2 * KV_W, wbuf), per_b(LANES, 2 * KV_W), per_b(LANES, 2 * KV_W)]
    out_specs = [per_b(ROWS_S, KV_W), per_b(ROWS_S, LANES)]
    out_shape = [jax.ShapeDtypeStruct((nb, ROWS_S, KV_W), F32), jax.ShapeDtypeStruct((nb, ROWS_S, LANES), F32)]
    scratch = [pltpu.VMEM((N_KV, Q_PAD, nsp), BF16), pltpu.VMEM((ROWS_S, 1), F32),
               pltpu.VMEM((ROWS_S, KV_W + LANES), F32), pltpu.VMEM((ROWS_S, KV_W), F32)]
    gs = pltpu.PrefetchScalarGridSpec(num_scalar_prefetch=1, grid=(nb, nchunks), in_specs=in_specs,
                                      out_specs=out_specs, scratch_shapes=scratch)
    return pl.pallas_call(functools.partial(_attn_sample_kernel, pps=pps, past=past, wbuf=wbuf, n_new=n_new), grid_spec=gs,
                          out_shape=out_shape, compiler_params=_params(2), name="attn_sample")(
        page_table, *([pages] * pps), qa, qbd, gates, slope_rows, kc, vc, ms, emat, win, ksn, kwn)


def _outproj_kernel(x_ref, a_ref, o_ref, nw_ref, sc_ref, sh_ref, g1_ref, wm_ref, woc_ref, won_ref, wo_ref, x1_ref):
    x = x_ref[...]
    h = _mod_norm(x, nw_ref[...], sc_ref[...], sh_ref[...]).astype(BF16)
    mg = jax.nn.sigmoid(_dot(h, wm_ref[...]))
    y_a = _dot(a_ref[...], woc_ref[...])
    y_b = _dot(o_ref[...], won_ref[...])
    mix = (mg[:, 0:D_MODEL] * y_a + mg[:, D_MODEL:2 * D_MODEL] * y_b).astype(BF16)
    x1_ref[...] = x + g1_ref[...] * _dot(mix, wo_ref[...])


def _outproj(x2d, a, o, mod, nw, w_merge, w_oc, w_on, w_o, *, tm):
    n = x2d.shape[0]
    tok = pl.BlockSpec((tm, D_MODEL), lambda i: (i, 0))
    sq = _const_spec((D_MODEL, D_MODEL))
    return pl.pallas_call(
        _outproj_kernel, grid=(n // tm,),
        in_specs=[tok, tok, tok, _const_spec((1, D_MODEL)), mod.spec(1), mod.spec(0), mod.spec(2),
                  _const_spec((D_MODEL, 2 * D_MODEL)), sq, sq, sq],
        out_specs=tok, out_shape=jax.ShapeDtypeStruct((n, D_MODEL), F32),
        compiler_params=_params(1), name="outproj",
    )(x2d, a, o, nw, mod.rows, mod.rows, mod.rows, w_merge, w_oc, w_on, w_o)


FF_CHUNK = D_FF // 2


def _ffn_kernel(x_ref, nw_ref, sc_ref, sh_ref, g2_ref, nf_ref, wg_ref, wu_ref, wd_ref, y_ref):
    x = x_ref[...]
    h = _mod_norm(x, nw_ref[...], sc_ref[...], sh_ref[...]).astype(BF16)
    acc = jnp.zeros(x.shape, F32)
    for c in range(D_FF // FF_CHUNK):
        sl = slice(c * FF_CHUNK, (c + 1) * FF_CHUNK)
        gate = _dot(h, wg_ref[:, sl])
        up = _dot(h, wu_ref[:, sl])
        act = (gate * jax.nn.sigmoid(gate) * up).astype(BF16)
        acc = acc + _dot(act, wd_ref[sl, :])
    x2 = x + g2_ref[...] * acc
    inv = lax.rsqrt(jnp.mean(x2 * x2, axis=-1, keepdims=True) + EPS)
    y_ref[...] = (x2 * inv) * nf_ref[...]


def _ffn(x1, mod, nw2, nf, w_gate, w_up, w_down, *, tm):
    n = x1.shape[0]
    tok = pl.BlockSpec((tm, D_MODEL), lambda i: (i, 0))
    vec = _const_spec((1, D_MODEL))
    return pl.pallas_call(
        _ffn_kernel, grid=(n // tm,),
        in_specs=[tok, vec, mod.spec(4), mod.spec(3), mod.spec(5), vec,
                  _const_spec((D_MODEL, D_FF)), _const_spec((D_MODEL, D_FF)), _const_spec((D_FF, D_MODEL))],
        out_specs=tok, out_shape=jax.ShapeDtypeStruct((n, D_MODEL), F32),
        compiler_params=_params(1), name="ffn",
    )(x1, nw2, mod.rows, mod.rows, mod.rows, nf, w_gate, w_up, w_down)


def _slopes():
    return 2.0 ** (-8.0 * jnp.arange(1, N_HEADS + 1, dtype=F32) / N_HEADS)


def _slope_lanes(slopes):
    parts = _split3(slopes)
    cols = jnp.stack([parts[0], parts[0], parts[1], parts[1], parts[2], parts[2]], axis=1)
    return jnp.pad(cols, ((0, 0), (0, LANES - 6)))


def _pos_lanes(pos_hi, pos_lo):
    cols = jnp.stack([pos_hi, pos_lo] * 3, axis=1).astype(F32)
    return jnp.pad(cols, ((0, 0), (0, LANES - 6))).astype(BF16)


def _token_consts(t):
    pos = jnp.arange(t, dtype=jnp.int32)
    onehot = (pos[:, None] // SEL_BLOCK == jnp.arange(HEAD_DIM, dtype=jnp.int32)[None, :]).astype(F32)
    clo = jnp.concatenate([jnp.zeros((t, HEAD_DIM), F32), onehot], axis=1)
    chi = _pos_lanes((pos // SEL_BLOCK) * SEL_BLOCK, pos % SEL_BLOCK)
    return clo, chi


def _cmp_consts(ncb):
    ci = jnp.arange(ncb, dtype=jnp.int32) * CMP_STRIDE
    return _pos_lanes((ci // SEL_BLOCK) * SEL_BLOCK, ci % SEL_BLOCK)


def _ones_lane():
    return (jnp.arange(LANES) == HEAD_DIM).astype(F32).reshape(1, LANES)


def _imp_matrix(nc, ns_pad):
    c = jnp.arange(nc, dtype=jnp.int32)[:, None]
    s = jnp.arange(ns_pad, dtype=jnp.int32)[None, :]
    per = SEL_BLOCK // CMP_STRIDE
    return ((c // per == s) | ((c % per == per - 1) & (c // per == s - 1))).astype(BF16)


def _prep_weights(w_in, w_phi1, w_phi2, pe_cmp):
    wb = w_in.astype(BF16)
    w_conv_in = wb[:, _C_CONV:_C_QKV]
    w_qkv = jnp.pad(wb[:, _C_QKV:_C_MERGE], ((0, 0), (0, _QKV_COLS_PAD - _QKV_COLS)))
    w_merge = wb[:, _C_MERGE:]
    half = CMP_STRIDE * HEAD_DIM
    w1cat = jnp.concatenate([w_phi1[:, :half], w_phi1[:, half:]], axis=2).astype(BF16)
    w2pad = jnp.pad(w_phi2, ((0, 0), (0, 0), (0, LANES - HEAD_DIM))).astype(BF16)
    pe2 = pe_cmp.reshape(2, 2, half)
    return w_conv_in, w_qkv, w_merge, w1cat, w2pad, pe2


def _prompt_layer(x, mod_p, wts):
    (nw1, nw2, nf, w_conv_in, w_qkv, w_merge, w1cat, w2pad, pe2, w_conv, b_conv, w_oc, w_on, w_o,
     w_gate, w_up, w_down, slopes) = wts
    nb, t, _ = x.shape
    tm = min(512, t)
    x2d = x.reshape(nb * t, D_MODEL)
    mod = _Mod(mod_p, False, t, tm)
    a, tail = _conv_path(x2d, mod, nw1, w_conv_in, w_conv, b_conv, tm=tm, seq_len=t)
    clo, chi = _token_consts(t)
    cv = _ones_lane()
    qt, kvc, kvs, kvw, gts, ksa, vst, kwa, vwt = _qkv_path(x2d, mod, nw1, w_qkv, tm=tm, seq_len=t, consts=(clo, chi))
    npages = t // PAGE
    pt = jnp.broadcast_to(jnp.arange(npages, dtype=jnp.int32), (nb, npages))
    nc = t // CMP_STRIDE
    ns = t // SEL_BLOCK
    nqb = t // Q_BLOCK
    kca, vct = _compress(kvc, pt, pe2, w1cat, w2pad, _cmp_consts(nc), cv,
                         pps=min(16, npages), transpose_v=True, paged=False)
    gates_t = gts[:, :3 * N_HEADS].reshape(nb, nqb, Q_BLOCK, N_KV, GROUP, 3).transpose(0, 3, 1, 5, 4, 2)
    gates_t = gates_t.reshape(nb, N_KV, nqb, 3, GROUP * Q_BLOCK)
    qc_t = _slope_lanes(slopes)[:, :QC_ROWS].reshape(N_KV, GROUP, QC_ROWS).transpose(0, 2, 1)
    qc_t = jnp.repeat(qc_t, Q_BLOCK, axis=2)
    mt = _imp_matrix(nc, ns).T
    o = _attn_prompt_t(qt, gates_t, qc_t, mt, kca, vct, ksa, vst, kwa, vwt)
    x1 = _outproj(x2d, a, o.reshape(nb * t, D_MODEL), mod, nw1, w_merge, w_oc, w_on, w_o, tm=tm)
    y = _ffn(x1, mod, nw2, nf, w_gate, w_up, w_down, tm=tm)
    keep = min(WINDOW, t)
    rows_of = lambda a: a.reshape(nb, 2, N_KV, HEAD_DIM, t).transpose(0, 4, 1, 2, 3)
    state = (rows_of(kvc), rows_of(kvs),
             kvw.reshape(nb, t, 2 * KV_W)[:, t - keep:].reshape(nb, keep, 2, N_KV, HEAD_DIM),
             tail[:, 8 - 2:, :])
    return y.reshape(nb, t, D_MODEL), state


def _sample_layer(x, mod_s, wts, cache_cmp, cache_sel, cache_win, state_conv, page_table):
    (nw1, nw2, nf, w_conv_in, w_qkv, w_merge, w1cat, w2pad, pe2, w_conv, b_conv, w_oc, w_on, w_o,
     w_gate, w_up, w_down, slopes) = wts
    nb, s, _ = x.shape
    n = nb * s
    x2d = x.reshape(n, D_MODEL)
    mod = _Mod(mod_s, True, s, n)
    tpos = jnp.arange(s)
    p1 = jnp.broadcast_to(state_conv[:, 1:2, :], (nb, s, D_MODEL)).reshape(n, D_MODEL)
    p2 = state_conv[:, jnp.minimum(tpos, 1), :].reshape(n, D_MODEL)
    a, u = _conv_path(x2d, mod, nw1, w_conv_in, w_conv, b_conv, tm=n, seq_len=s, prev=(p1, p2))
    qpad, kvc, kvs, kvw, gts = _qkv_path(x2d, mod, nw1, w_qkv, tm=n, seq_len=s)

    npages = page_table.shape[1]
    past = npages * PAGE
    cv = _ones_lane()
    nc = past // CMP_STRIDE
    token_minor = lambda c: c.transpose(0, 2, 3, 4, 1).reshape(c.shape[0], 2 * KV_W, c.shape[1])
    kca, vca = _compress(token_minor(cache_cmp), page_table, pe2, w1cat, w2pad, _cmp_consts(nc), cv,
                         pps=min(16, npages), transpose_v=False, paged=True)

    qh = qpad.reshape(nb, s, N_KV, GROUP, LANES).transpose(0, 2, 3, 1, 4)
    qh = jnp.pad(qh, ((0, 0), (0, 0), (0, 0), (0, Q_PAD - s), (0, 0)))
    sl = jnp.broadcast_to(_slope_lanes(slopes).reshape(1, N_KV, GROUP, 1, LANES), qh.shape)
    qa = jnp.concatenate([qh, sl], axis=-1).reshape(nb, N_KV, ROWS_G, AUG_W)
    eye = jnp.eye(N_KV, dtype=BF16)
    qbd = (qh[..., None, :HEAD_DIM] * eye[None, :, None, None, :, None]).reshape(nb, ROWS_S, KV_W)
    gates = gts[:, :3 * N_HEADS].reshape(nb, s, N_KV, GROUP, 3).transpose(0, 2, 3, 1, 4)
    gates = jnp.pad(gates, ((0, 0), (0, 0), (0, 0), (0, Q_PAD - s), (0, 0))).reshape(nb, ROWS_S, 3)
    slope_rows = jnp.repeat(slopes, Q_PAD).reshape(ROWS_S, 1)
    ns = past // SEL_BLOCK + 1
    nsp = -(-ns // LANES) * LANES
    ms = _imp_matrix(nc, nsp)
    tok = jnp.arange(past, dtype=jnp.int32)
    emat = (jnp.arange(nsp, dtype=jnp.int32)[:, None] == tok[None, :] // SEL_BLOCK).astype(BF16)
    pad_rows = lambda r: jnp.pad(r.reshape(nb, s, 2 * KV_W), ((0, 0), (0, LANES - s), (0, 0)))
    osw, ocg = _attn_sample(page_table, token_minor(cache_sel), qa, qbd, gates, slope_rows, kca, vca,
                            ms, emat, token_minor(cache_win), pad_rows(kvs), pad_rows(kvw),
                            pps=min(16, npages), n_new=s)
    osw = osw.reshape(nb, N_KV, GROUP, Q_PAD, N_KV, HEAD_DIM)
    o_sw = jnp.einsum('bgrqgd->bqgrd', osw)
    o_c = ocg.reshape(nb, N_KV, GROUP, Q_PAD, LANES)[..., :HEAD_DIM].transpose(0, 3, 1, 2, 4)
    o = (o_sw + o_c)[:, :s].reshape(n, N_HEADS * HEAD_DIM).astype(BF16)

    x1 = _outproj(x2d, a, o, mod, nw1, w_merge, w_oc, w_on, w_o, tm=n)
    y = _ffn(x1, mod, nw2, nf, w_gate, w_up, w_down, tm=n)
    kv5 = lambda r: r.reshape(nb, s, 2, N_KV, HEAD_DIM)
    win = jnp.concatenate([cache_win, kv5(kvw)], axis=1)[:, s:]
    state = (kv5(kvc), kv5(kvs), win, u.reshape(nb, s, D_MODEL)[:, s - 2:])
    return y.reshape(nb, s, D_MODEL), state


def kernel(x_prompt, x_sample, c_prompt, c_sample, cache_cmp, cache_sel, cache_win, state_conv, page_table,
           w_ada, b_ada, norm1, w_in, w_conv, b_conv, w_out_conv, pe_cmp, w_phi1, w_phi2, w_o_nsa, w_out,
           norm2, w_gate, w_up, w_down, norm_f):
    depth = w_ada.shape[0]
    assert depth == 1, "single-layer trunk"
    nbp, nbs = c_prompt.shape[0], c_sample.shape[0]
    slopes = _slopes()
    l = 0
    c_all = jnp.concatenate([c_prompt, c_sample], axis=0)
    c_all = jnp.pad(c_all, ((0, -c_all.shape[0] % 8), (0, 0)))
    mod = _ada(c_all, w_ada[l], b_ada[l])
    w_conv_in, w_qkv, w_merge, w1cat, w2pad, pe2 = _prep_weights(w_in[l], w_phi1[l], w_phi2[l], pe_cmp[l])
    row = lambda v: v.reshape(1, -1)
    wts = (row(norm1[l]), row(norm2[l]), row(norm_f), w_conv_in, w_qkv, w_merge, w1cat, w2pad, pe2,
           w_conv[l], row(b_conv[l]), w_out_conv[l].astype(BF16), w_o_nsa[l].astype(BF16), w_out[l].astype(BF16),
           w_gate[l].astype(BF16), w_up[l].astype(BF16), w_down[l].astype(BF16), slopes)
    yp, st_p = _prompt_layer(x_prompt, mod[:nbp], wts)
    ys, st_s = _sample_layer(x_sample, mod[nbp:nbp + nbs], wts, cache_cmp[l], cache_sel[l], cache_win[l],
                             state_conv[l], page_table)
    return (yp, ys, st_p[0][None], st_p[1][None], st_p[2][None], st_p[3][None],
            st_s[0][None], st_s[1][None], st_s[2][None], st_s[3][None])
```

```python
import functools

import jax
import jax.numpy as jnp
import numpy as np
from jax import lax
from jax.experimental import pallas as pl
from jax.experimental.pallas import tpu as pltpu

F32 = jnp.float32
BF16 = jnp.bfloat16

D_MODEL = 1024
N_HEADS = 16
HEAD_DIM = 64
N_KV = 4
GROUP = N_HEADS // N_KV
KV_W = N_KV * HEAD_DIM
CMP_BLOCK = 32
CMP_STRIDE = 16
SEL_BLOCK = 64
N_SEL = 16
WINDOW = 512
D_PHI = 2 * HEAD_DIM
Q_BLOCK = 256
PAGE = 128
D_FF = ((8 * D_MODEL // 3 + 255) // 256) * 256
EPS = 1e-6
NEG = -1e30
FORCE_BONUS = 1e3

LANES = 128
AUG_W = 2 * LANES
HALVES_PER_PAGE = PAGE // CMP_STRIDE
ROW_TILES = 2 * KV_W // LANES
VMEM_LIMIT = 56 * 1024 * 1024
KEY_TILE = 2 * LANES
VT_ROWS = HEAD_DIM + 16
QC_ROWS = 16
SEL_TILES_PER_ITER = 4

_C_CONV = 0
_C_QKV = 3 * D_MODEL
_C_GATE = _C_QKV + N_HEADS * HEAD_DIM + 6 * KV_W
_C_MERGE = _C_GATE + 3 * N_HEADS
_QKV_COLS = _C_MERGE - _C_QKV
_QKV_COLS_PAD = -(-_QKV_COLS // LANES) * LANES


def _dot(a, b):
    return jnp.dot(a, b, preferred_element_type=F32)


def _dot_nt(a, b):
    return lax.dot_general(a, b, (((1,), (1,)), ((), ())), preferred_element_type=F32)


def _params(n_axes):
    return pltpu.CompilerParams(dimension_semantics=("arbitrary",) * n_axes, vmem_limit_bytes=VMEM_LIMIT)


def _const_spec(shape):
    return pl.BlockSpec(shape, lambda *_: (0,) * len(shape))


def _mod_norm(x, nw, sc, sh):
    inv = lax.rsqrt(jnp.mean(x * x, axis=-1, keepdims=True) + EPS)
    return (x * inv) * nw * (1.0 + sc) + sh


def _split3(x):
    a = x.astype(BF16)
    r = x - a.astype(F32)
    b = r.astype(BF16)
    c = (r - b.astype(F32)).astype(BF16)
    return a, b, c


def _ada_kernel(c_ref, w_ref, b_ref, o_ref):
    c = c_ref[...]
    s = c * jax.nn.sigmoid(c)
    o_ref[...] = jnp.dot(s, w_ref[...], preferred_element_type=F32, precision=lax.Precision.HIGHEST) + b_ref[...]


def _ada(c, w_ada, b_ada):
    n = c.shape[0]
    tn = 1536
    return pl.pallas_call(
        _ada_kernel,
        grid=(6 * D_MODEL // tn,),
        in_specs=[_const_spec((n, D_MODEL)),
                  pl.BlockSpec((D_MODEL, tn), lambda j: (0, j)),
                  pl.BlockSpec((1, tn), lambda j: (0, j))],
        out_specs=pl.BlockSpec((n, tn), lambda j: (0, j)),
        out_shape=jax.ShapeDtypeStruct((n, 6 * D_MODEL), F32),
        compiler_params=_params(1),
        name="ada",
    )(c, w_ada, b_ada.reshape(1, -1))


class _Mod:
    def __init__(self, mod, per_token, seq_len, tm):
        self.per_token = per_token
        if per_token:
            self.rows = jnp.repeat(mod, seq_len, axis=0)
        else:
            self.rows = mod.reshape(mod.shape[0], 1, 6 * D_MODEL)
        self.tiles_per_seq = None if per_token else seq_len // tm
        self.tm = tm

    def spec(self, k):
        if self.per_token:
            return pl.BlockSpec((self.tm, D_MODEL), lambda i: (i, k))
        tps = self.tiles_per_seq
        return pl.BlockSpec((None, 1, D_MODEL), lambda i: (i // tps, 0, k))


def _conv_kernel(*refs, carry_rows, seq_len):
    if carry_rows:
        (x_ref, nw_ref, sc_ref, sh_ref, w_ref, wc_ref, bc_ref, a_ref, tail_ref, carry_ref) = refs
    else:
        (x_ref, nw_ref, sc_ref, sh_ref, w_ref, wc_ref, bc_ref, p1_ref, p2_ref, a_ref, tail_ref) = refs
    tm = x_ref.shape[0]
    h = _mod_norm(x_ref[...], nw_ref[...], sc_ref[...], sh_ref[...]).astype(BF16)
    z = _dot(h, w_ref[...])
    bg = z[:, 0:D_MODEL]
    u = z[:, D_MODEL:2 * D_MODEL] * z[:, 2 * D_MODEL:3 * D_MODEL]
    row = lax.broadcasted_iota(jnp.int32, (tm, 1), 0)
    u1 = pltpu.roll(u, 1, 0)
    u2 = pltpu.roll(u, 2, 0)
    if carry_rows:
        @pl.when(pl.program_id(0) % carry_rows == 0)
        def _():
            carry_ref[...] = jnp.zeros_like(carry_ref)
        c0 = carry_ref[0:1, :]
        c1 = carry_ref[1:2, :]
        u1 = jnp.where(row == 0, c1, u1)
        u2 = jnp.where(row == 0, c0, jnp.where(row == 1, c1, u2))
        carry_ref[0:2, :] = u[tm - 2:tm, :]
        tail_ref[...] = u[tm - 8:tm, :]
    else:
        pos = lax.rem(row, seq_len)
        u1 = jnp.where(pos >= 1, u1, p1_ref[...])
        u2 = jnp.where(pos >= 2, u2, p2_ref[...])
        tail_ref[...] = u
    v = bc_ref[...] + wc_ref[0:1, :] * u2 + wc_ref[1:2, :] * u1 + wc_ref[2:3, :] * u
    a_ref[...] = (bg * v).astype(BF16)


def _conv_path(x2d, mod, nw, w_conv_in, w_conv, b_conv, *, tm, seq_len, prev=None):
    n = x2d.shape[0]
    tok = pl.BlockSpec((tm, D_MODEL), lambda i: (i, 0))
    in_specs = [tok, _const_spec((1, D_MODEL)), mod.spec(1), mod.spec(0),
                _const_spec((D_MODEL, 3 * D_MODEL)), _const_spec((3, D_MODEL)), _const_spec((1, D_MODEL))]
    args = [x2d, nw, mod.rows, mod.rows, w_conv_in, w_conv, b_conv]
    if prev is None:
        tps = seq_len // tm
        out_specs = [tok, pl.BlockSpec((None, 8, D_MODEL), lambda i: (i // tps, 0, 0))]
        out_shape = [jax.ShapeDtypeStruct((n, D_MODEL), BF16), jax.ShapeDtypeStruct((n // seq_len, 8, D_MODEL), F32)]
        scratch = [pltpu.VMEM((8, D_MODEL), F32)]
        kern = functools.partial(_conv_kernel, carry_rows=tps, seq_len=seq_len)
    else:
        in_specs += [tok, tok]
        args += list(prev)
        out_specs = [tok, tok]
        out_shape = [jax.ShapeDtypeStruct((n, D_MODEL), BF16), jax.ShapeDtypeStruct((n, D_MODEL), F32)]
        scratch = []
        kern = functools.partial(_conv_kernel, carry_rows=0, seq_len=seq_len)
    return pl.pallas_call(kern, grid=(n // tm,), in_specs=in_specs, out_specs=out_specs, out_shape=out_shape,
                          scratch_shapes=scratch, compiler_params=_params(1), name="conv_path")(*args)


def _qkv_kernel(*refs, aug):
    if aug:
        (x_ref, nw_ref, sc_ref, sh_ref, w_ref, clo_ref, chi_ref,
         q_ref, kvc_ref, kvs_ref, kvw_ref, g_ref, ksa_ref, vsa_ref, kwa_ref, vwa_ref) = refs
    else:
        (x_ref, nw_ref, sc_ref, sh_ref, w_ref, q_ref, kvc_ref, kvs_ref, kvw_ref, g_ref) = refs
    tm = x_ref.shape[0]
    h = _mod_norm(x_ref[...], nw_ref[...], sc_ref[...], sh_ref[...]).astype(BF16)
    z = _dot(h, w_ref[...])
    low = lax.broadcasted_iota(jnp.int32, (tm, LANES), 1) < HEAD_DIM
    nq = N_HEADS * HEAD_DIM
    for c in range(N_HEADS // 2):
        t = z[:, c * LANES:(c + 1) * LANES] * (HEAD_DIM ** -0.5)
        if aug:
            tt = t.T.astype(BF16)
            q_ref[2 * c] = tt[0:HEAD_DIM]
            q_ref[2 * c + 1] = tt[HEAD_DIM:2 * HEAD_DIM]
        else:
            q_ref[:, (2 * c) * LANES:(2 * c + 1) * LANES] = jnp.where(low, t, 0.0).astype(BF16)
            q_ref[:, (2 * c + 1) * LANES:(2 * c + 2) * LANES] = jnp.where(low, pltpu.roll(t, HEAD_DIM, 1), 0.0).astype(BF16)
    if aug:
        kvc_ref[...] = z[:, nq:nq + 2 * KV_W].T
        kvs_ref[...] = z[:, nq + 2 * KV_W:nq + 4 * KV_W].T
    else:
        kvc_ref[...] = z[:, nq:nq + 2 * KV_W]
        kvs_ref[...] = z[:, nq + 2 * KV_W:nq + 4 * KV_W]
    kvw_ref[...] = z[:, nq + 4 * KV_W:nq + 6 * KV_W]
    g_ref[...] = jax.nn.sigmoid(z[:, nq + 6 * KV_W:nq + 6 * KV_W + LANES])
    if aug:
        chi = chi_ref[...]
        ones_row = (lax.broadcasted_iota(jnp.int32, (VT_ROWS - HEAD_DIM, KEY_TILE), 0) == 0).astype(BF16)
        for br, (ka_ref, va_ref) in enumerate(((ksa_ref, vsa_ref), (kwa_ref, vwa_ref))):
            kbase = nq + 2 * KV_W * (br + 1)
            clo = clo_ref[...] if br == 0 else 0.0
            for g in range(N_KV):
                kt = z[:, kbase + (g // 2) * LANES:kbase + (g // 2 + 1) * LANES]
                if g % 2:
                    kt = pltpu.roll(kt, HEAD_DIM, 1)
                ka_ref[g, :, 0:LANES] = jnp.where(low, kt, clo).astype(BF16)
                ka_ref[g, :, LANES:AUG_W] = chi
            for c in range(N_KV // 2):
                vt = z[:, kbase + KV_W + c * LANES:kbase + KV_W + (c + 1) * LANES].T.astype(BF16)
                for gg in range(2):
                    for j in range(tm // KEY_TILE):
                        va_ref[2 * c + gg, j, 0:HEAD_DIM, :] = vt[gg * HEAD_DIM:(gg + 1) * HEAD_DIM, j * KEY_TILE:(j + 1) * KEY_TILE]
                        va_ref[2 * c + gg, j, HEAD_DIM:VT_ROWS, :] = ones_row


def _qkv_path(x2d, mod, nw, w_qkv, *, tm, seq_len, consts=None):
    n = x2d.shape[0]
    aug = consts is not None
    tok = lambda w: pl.BlockSpec((tm, w), lambda i: (i, 0))
    in_specs = [tok(D_MODEL), _const_spec((1, D_MODEL)), mod.spec(1), mod.spec(0),
                _const_spec((D_MODEL, _QKV_COLS_PAD))]
    args = [x2d, nw, mod.rows, mod.rows, w_qkv]
    out_specs = [tok(N_HEADS * LANES), tok(2 * KV_W), tok(2 * KV_W), tok(2 * KV_W), tok(LANES)]
    out_shape = [jax.ShapeDtypeStruct((n, N_HEADS * LANES), BF16)] + \
                [jax.ShapeDtypeStruct((n, 2 * KV_W), F32)] * 3 + [jax.ShapeDtypeStruct((n, LANES), F32)]
    if aug:
        tps = seq_len // tm
        nb = n // seq_len
        pos = lambda w: pl.BlockSpec((tm, w), lambda i: (i % tps, 0))
        in_specs += [pos(LANES), pos(LANES)]
        args += list(consts)
        out_specs[0] = pl.BlockSpec((None, N_HEADS, HEAD_DIM, tm), lambda i: (i // tps, 0, 0, i % tps))
        out_shape[0] = jax.ShapeDtypeStruct((nb, N_HEADS, HEAD_DIM, seq_len), BF16)
        for k in (1, 2):
            out_specs[k] = pl.BlockSpec((None, 2 * KV_W, tm), lambda i: (i // tps, 0, i % tps))
            out_shape[k] = jax.ShapeDtypeStruct((nb, 2 * KV_W, seq_len), F32)
        ka = pl.BlockSpec((None, N_KV, tm, AUG_W), lambda i: (i // tps, 0, i % tps, 0))
        va = pl.BlockSpec((None, N_KV, tm // KEY_TILE, VT_ROWS, KEY_TILE), lambda i: (i // tps, 0, i % tps, 0, 0))
        out_specs += [ka, va, ka, va]
        ka_s = jax.ShapeDtypeStruct((nb, N_KV, seq_len, AUG_W), BF16)
        va_s = jax.ShapeDtypeStruct((nb, N_KV, seq_len // KEY_TILE, VT_ROWS, KEY_TILE), BF16)
        out_shape += [ka_s, va_s, ka_s, va_s]
    return pl.pallas_call(functools.partial(_qkv_kernel, aug=aug), grid=(n // tm,), in_specs=in_specs,
                          out_specs=out_specs, out_shape=out_shape, compiler_params=_params(1), name="qkv_path")(*args)


def _block_major_loader(ref, k, c):
    def load(p, nrows):
        return ref[k, c, p * HALVES_PER_PAGE:p * HALVES_PER_PAGE + nrows, :]
    return load


def _gather_pair(load, nrows):
    low = lax.broadcasted_iota(jnp.int32, (nrows, LANES), 1) < HEAD_DIM
    even, odd = [], []
    for qq in range(CMP_STRIDE // 2):
        a = load(2 * qq, nrows)
        b = load(2 * qq + 1, nrows)
        even.append(jnp.where(low, a, pltpu.roll(b, HEAD_DIM, 1)))
        odd.append(jnp.where(low, pltpu.roll(a, HEAD_DIM, 1), b))
    return jnp.concatenate(even, axis=1), jnp.concatenate(odd, axis=1)


def _compress_kernel(pt_ref, *refs, pps, transpose_v):
    pages = refs[:pps]
    nxt_ref, pe_ref, w1_ref, w2_ref, chi_ref, cv_ref, perm_ref, kc_ref, vc_ref, xt_ref = refs[pps:]
    perm = perm_ref[...]
    for k, pg in enumerate(list(pages) + [nxt_ref]):
        for c in range(ROW_TILES):
            xt_ref[k, c] = _dot_nt(perm, pg[c * LANES:(c + 1) * LANES, :].astype(BF16))
    loaders = lambda c: [_block_major_loader(xt_ref, k, c) for k in range(pps)]
    look_loader = lambda c: _block_major_loader(xt_ref, pps, c)
    t = pl.program_id(1)
    last = t == pl.num_programs(1) - 1
    nhb = pps * HALVES_PER_PAGE
    n = N_KV * nhb
    row = lax.broadcasted_iota(jnp.int32, (n, 1), 0)
    chi = chi_ref[...]
    for kv in range(2):
        by_group, look = [], []
        for cc in range(N_KV // 2):
            c = kv * (N_KV // 2) + cc
            pairs = [_gather_pair(ld, HALVES_PER_PAGE) for ld in loaders(c)]
            by_group += [[ev for ev, _ in pairs], [od for _, od in pairs]]
            look += list(_gather_pair(look_loader(c), 1))
        parts = [x for group in by_group for x in group]
        extra = jnp.concatenate([pe_ref[kv], jnp.zeros((2, CMP_STRIDE * HEAD_DIM), F32)] + look, axis=0)
        xmat = jnp.concatenate(parts + [extra], axis=0).astype(BF16)
        hab = _dot(xmat, w1_ref[kv])
        ha = hab[0:n, 0:D_PHI]
        hb = hab[0:n, D_PHI:2 * D_PHI]
        pbias = hab[n:n + 1, 0:D_PHI] + hab[n + 1:n + 2, D_PHI:2 * D_PHI]
        hbn = pltpu.roll(hb, n - 1, 0)
        for g in range(N_KV):
            la = jnp.where(last, 0.0, hab[n + 4 + g:n + 5 + g, D_PHI:2 * D_PHI])
            hbn = jnp.where(row == g * nhb + nhb - 1, la, hbn)
        act = jax.nn.gelu(ha + hbn + pbias).astype(BF16)
        out = _dot(act, w2_ref[kv])
        for g in range(N_KV):
            blk = out[g * nhb:(g + 1) * nhb, :]
            if kv == 0:
                kc_ref[g, :, 0:LANES] = blk.astype(BF16)
                kc_ref[g, :, LANES:AUG_W] = chi
            elif transpose_v:
                vc_ref[g, :, :] = (blk + cv_ref[...]).T.astype(BF16)
            else:
                vc_ref[g, :, :] = (blk + cv_ref[...]).astype(BF16)


def _compress(pages, page_table, pe2, w1cat, w2pad, chi_c, cv, *, pps, transpose_v, paged):
    nb, npages = page_table.shape
    nchunks = npages // pps
    nhb = pps * HALVES_PER_PAGE
    ncb = npages * HALVES_PER_PAGE
    where = (lambda b, j: (j, 0, 0)) if paged else (lambda b, j: (b, 0, j))

    def page_spec(k):
        return pl.BlockSpec((None, 2 * KV_W, PAGE), lambda b, t, pt: where(b, pt[b, t * pps + k]))

    nxt_spec = pl.BlockSpec((None, 2 * KV_W, PAGE),
                            lambda b, t, pt: where(b, pt[b, jnp.minimum((t + 1) * pps, npages - 1)]))
    scratch = [pltpu.VMEM((pps + 1, ROW_TILES, PAGE, LANES), F32)]
    r = jnp.arange(PAGE, dtype=jnp.int32)
    perm = (r[None, :] == (CMP_STRIDE * (r % HALVES_PER_PAGE) + r // HALVES_PER_PAGE)[:, None]).astype(BF16)
    cs = lambda shape: pl.BlockSpec(shape, lambda b, t, pt: (0,) * len(shape))
    in_specs = [page_spec(k) for k in range(pps)] + [
        nxt_spec, cs((2, 2, CMP_STRIDE * HEAD_DIM)), cs((2, CMP_STRIDE * HEAD_DIM, 2 * D_PHI)),
        cs((2, D_PHI, LANES)), pl.BlockSpec((nhb, LANES), lambda b, t, pt: (t, 0)), cs((1, LANES)),
        cs((PAGE, PAGE))]
    if transpose_v:
        v_spec = pl.BlockSpec((None, N_KV, LANES, nhb), lambda b, t, pt: (b, 0, 0, t))
        v_shape = jax.ShapeDtypeStruct((nb, N_KV, LANES, ncb), BF16)
    else:
        v_spec = pl.BlockSpec((None, N_KV, nhb, LANES), lambda b, t, pt: (b, 0, t, 0))
        v_shape = jax.ShapeDtypeStruct((nb, N_KV, ncb, LANES), BF16)
    out_specs = [pl.BlockSpec((None, N_KV, nhb, AUG_W), lambda b, t, pt: (b, 0, t, 0)), v_spec]
    out_shape = [jax.ShapeDtypeStruct((nb, N_KV, ncb, AUG_W), BF16), v_shape]
    gs = pltpu.PrefetchScalarGridSpec(num_scalar_prefetch=1, grid=(nb, nchunks), in_specs=in_specs, out_specs=out_specs,
                                      scratch_shapes=scratch)
    return pl.pallas_call(functools.partial(_compress_kernel, pps=pps, transpose_v=transpose_v),
                          grid_spec=gs, out_shape=out_shape,
                          compiler_params=_params(2), name="compress")(
        page_table, *([pages] * (pps + 1)), pe2, w1cat, w2pad, chi_c, cv, perm)


REMOVED = -3e38


def _top_k_mask(score, index, k, axis):
    work = score
    selected = jnp.zeros(score.shape, jnp.bool_)
    for _ in range(k):
        best = jnp.max(work, axis=axis, keepdims=True)
        first = jnp.min(jnp.where(work == best, index, score.shape[axis]), axis=axis, keepdims=True)
        hit = index == first
        selected = selected | hit
        work = jnp.where(hit, REMOVED, work)
    return selected


def _attn_prompt_t_kernel(q_ref, g_ref, qc_ref, mt_ref, kc_ref, vc_ref, ks_ref, vs_ref, kw_ref, vw_ref, o_ref,
                          qa_ref, qw_ref, m_ref, acc_ref):
    nc = kc_ref.shape[0]
    ns = mt_ref.shape[0]
    kt = vs_ref.shape[2]
    ncol = GROUP * Q_BLOCK
    qb = pl.program_id(2)
    q0 = qb * Q_BLOCK
    qpos = q0 + (lax.broadcasted_iota(jnp.int32, (1, ncol), 1) & (Q_BLOCK - 1))

    for ref in (qw_ref, qa_ref):
        for r in range(GROUP):
            ref[0:HEAD_DIM, r * Q_BLOCK:(r + 1) * Q_BLOCK] = q_ref[r]
        ref[HEAD_DIM:2 * HEAD_DIM, :] = jnp.zeros((HEAD_DIM, ncol), BF16)
        ref[2 * HEAD_DIM:2 * HEAD_DIM + QC_ROWS, :] = qc_ref[...]
        ref[2 * HEAD_DIM + QC_ROWS:AUG_W, :] = jnp.zeros((AUG_W - 2 * HEAD_DIM - QC_ROWS, ncol), BF16)

    s = _dot(kc_ref[...], qw_ref[...])
    ci = lax.broadcasted_iota(jnp.int32, (nc, 1), 0)
    c_valid = ci * CMP_STRIDE + (CMP_BLOCK - 1) <= qpos
    s = jnp.where(c_valid, s, NEG)
    e = jnp.exp(s - jnp.max(s, axis=0, keepdims=True))
    p = jnp.where(c_valid, e * (1.0 / jnp.sum(e, axis=0, keepdims=True)), 0.0)
    o_c = _dot(vc_ref[...], p.astype(BF16))[0:HEAD_DIM]

    psum = p[:, 0:Q_BLOCK]
    for r in range(1, GROUP):
        psum = psum + p[:, r * Q_BLOCK:(r + 1) * Q_BLOCK]
    mt = mt_ref[...]
    imp = sum(_dot(mt, part) for part in _split3(psum))
    si = lax.broadcasted_iota(jnp.int32, (ns, Q_BLOCK), 0)
    qpos_t = q0 + lax.broadcasted_iota(jnp.int32, (ns, Q_BLOCK), 1)
    cur = lax.shift_right_logical(qpos_t, 6)
    s_valid = si * SEL_BLOCK <= qpos_t
    forced = (si == 0) | (si == cur) | (si == cur - 1)
    score = jnp.where(s_valid, imp + jnp.where(forced, FORCE_BONUS, 0.0), NEG)
    selected = _top_k_mask(score, si, min(N_SEL, ns), axis=0)
    bias_t = jnp.where(selected, 0.0, NEG).astype(BF16)
    qa_ref[HEAD_DIM:HEAD_DIM + ns, :] = jnp.concatenate([bias_t] * GROUP, axis=1)

    t_hi = lax.div(q0, kt)
    key_iota = lax.broadcasted_iota(jnp.int32, (kt, 1), 0)

    n_back = WINDOW // kt
    s_w, v_w = [], []
    for j in range(n_back + 1):
        tw = t_hi - n_back + j
        tc = jnp.maximum(tw, 0)
        kpos = tw * kt + key_iota
        s = _dot(kw_ref[pl.ds(pl.multiple_of(tc * kt, kt), kt), :], qw_ref[...])
        if j == 0:
            oldest = jnp.where(tw >= 0, qpos - WINDOW, jnp.iinfo(jnp.int32).max)
            s = jnp.where(kpos > oldest, s, NEG)
        elif j == n_back:
            s = jnp.where(kpos <= qpos, s, NEG)
        else:
            s = jnp.where(tw >= 0, s, NEG)
        s_w.append(s)
        v_w.append(vw_ref[tc])
    m_w = functools.reduce(jnp.maximum, [jnp.max(s, axis=0, keepdims=True) for s in s_w])
    acc_w = sum(_dot(v, jnp.exp(s - m_w).astype(BF16)) for s, v in zip(s_w, v_w))
    o_w = acc_w[0:HEAD_DIM] * (1.0 / acc_w[HEAD_DIM:HEAD_DIM + 1])

    m_ref[...] = jnp.full(m_ref.shape, NEG, F32)
    acc_ref[...] = jnp.zeros(acc_ref.shape, F32)

    def sel_update(runs, causal_last):
        qa = qa_ref[...]
        ss, vts = [], []
        for ri, (t0, n) in enumerate(runs):
            k0 = pl.multiple_of(t0 * kt, kt)
            s = _dot(ks_ref[pl.ds(k0, n * kt), :], qa)
            for i in range(n):
                si_ = s[i * kt:(i + 1) * kt]
                if causal_last and ri == len(runs) - 1 and i == n - 1:
                    si_ = jnp.where(k0 + i * kt + key_iota <= qpos, si_, NEG)
                ss.append(si_)
                vts.append(vs_ref[t0 + i])
        m_old = m_ref[...]
        m_new = functools.reduce(jnp.maximum, [m_old] + [jnp.max(s, axis=0, keepdims=True) for s in ss])
        pv = sum(_dot(vt, jnp.exp(s - m_new).astype(BF16)) for vt, s in zip(vts, ss))
        acc_ref[...] = jnp.exp(m_old - m_new) * acc_ref[...] + pv
        m_ref[...] = m_new

    per_tile = kt // SEL_BLOCK
    lo_blk = jnp.min(jnp.where(selected & (si >= per_tile), si, ns))
    start = jnp.clip(lax.div(lo_blk, per_tile), 1, jnp.maximum(t_hi, 1))
    n_plain = jnp.maximum(t_hi - start, 0)
    n_group = lax.div(n_plain, SEL_TILES_PER_ITER)

    def body(i, carry):
        sel_update([(start + i * SEL_TILES_PER_ITER, SEL_TILES_PER_ITER)], False)
        return carry

    lax.fori_loop(0, n_group, body, 0)
    rem = n_plain - n_group * SEL_TILES_PER_ITER
    for left in range(SEL_TILES_PER_ITER):
        @pl.when((rem == left) & (t_hi >= 1))
        def _():
            sel_update([(0, 1), (t_hi - left, left + 1)], True)

    @pl.when(t_hi == 0)
    def _():
        sel_update([(0, 1)], True)

    acc = acc_ref[...]
    o_s = acc[0:HEAD_DIM] * (1.0 / acc[HEAD_DIM:HEAD_DIM + 1])

    gts = g_ref[...]
    o = gts[0:1] * o_c + gts[1:2] * o_s + gts[2:3] * o_w
    for c in range(GROUP // 2):
        pair = jnp.concatenate([o[:, (2 * c) * Q_BLOCK:(2 * c + 1) * Q_BLOCK],
                                o[:, (2 * c + 1) * Q_BLOCK:(2 * c + 2) * Q_BLOCK]], axis=0)
        o_ref[:, c * LANES:(c + 1) * LANES] = pair.T.astype(BF16)


def _attn_prompt_t(qt, gates_t, qc_t, mt, kc, vct, ks, vst, kw, vwt):
    nb, t = qt.shape[0], qt.shape[3]
    nc = kc.shape[2]
    ns = mt.shape[0]
    ntile, kt = vst.shape[2], vst.shape[4]
    ncol = GROUP * Q_BLOCK
    per_bg = lambda *shape: pl.BlockSpec((None, None) + shape, lambda b, g, i: (b, g) + (0,) * len(shape))
    in_specs = [pl.BlockSpec((None, GROUP, HEAD_DIM, Q_BLOCK), lambda b, g, i: (b, g, 0, i)),
                pl.BlockSpec((None, None, None, 3, ncol), lambda b, g, i: (b, g, i, 0, 0)),
                pl.BlockSpec((None, QC_ROWS, ncol), lambda b, g, i: (g, 0, 0)),
                pl.BlockSpec((ns, nc), lambda b, g, i: (0, 0)),
                per_bg(nc, AUG_W), per_bg(LANES, nc), per_bg(t, AUG_W), per_bg(ntile, VT_ROWS, kt),
                per_bg(t, AUG_W), per_bg(ntile, VT_ROWS, kt)]
    out_spec = pl.BlockSpec((None, Q_BLOCK, GROUP * HEAD_DIM), lambda b, g, i: (b, i, g))
    scratch = [pltpu.VMEM((AUG_W, ncol), BF16), pltpu.VMEM((AUG_W, ncol), BF16),
               pltpu.VMEM((1, ncol), F32), pltpu.VMEM((VT_ROWS, ncol), F32)]
    return pl.pallas_call(_attn_prompt_t_kernel, grid=(nb, N_KV, t // Q_BLOCK),
                          in_specs=in_specs, out_specs=out_spec,
                          out_shape=jax.ShapeDtypeStruct((nb, t, N_HEADS * HEAD_DIM), BF16),
                          scratch_shapes=scratch, compiler_params=_params(3), name="attn_prompt")(
        qt, gates_t, qc_t, mt, kc, vct, ks, vst, kw, vwt)


Q_PAD = 8
ROWS_G = GROUP * Q_PAD
ROWS_S = N_KV * ROWS_G


def _attn_sample_kernel(pt_ref, *refs, pps, past, wbuf, n_new):
    pages = refs[:pps]
    (qa_ref, qbd_ref, g_ref, slope_ref, kc_ref, vc_ref, ms_ref, e_ref, win_ref, ksn_ref, kwn_ref,
     osw_ref, oc_ref, bias_ref, m_ref, acc_ref, ow_ref) = refs[pps:]
    t = pl.program_id(1)
    nchunks = pl.num_programs(1)
    nc = kc_ref.shape[1]
    ns = past // SEL_BLOCK + 1
    row = lax.broadcasted_iota(jnp.int32, (ROWS_S, 1), 0)
    qpos = past + (row & (Q_PAD - 1))
    qposf = qpos.astype(F32)
    slope = slope_ref[...]
    qbd = qbd_ref[...]

    @pl.when(t == 0)
    def _():
        nsp = ms_ref.shape[1]
        rg = lax.broadcasted_iota(jnp.int32, (ROWS_G, 1), 0)
        qpos_g = past + (rg & (Q_PAD - 1))
        r8 = lax.broadcasted_iota(jnp.int32, (Q_PAD, 1), 0)
        qpos8 = past + r8
        si = lax.broadcasted_iota(jnp.int32, (Q_PAD, nsp), 1)
        scores = []
        for g in range(N_KV):
            s = _dot_nt(qa_ref[g], kc_ref[g])
            ci = lax.broadcasted_iota(jnp.int32, (1, nc), 1)
            c_valid = ci * CMP_STRIDE + (CMP_BLOCK - 1) <= qpos_g
            s = jnp.where(c_valid, s, NEG)
            e = jnp.exp(s - jnp.max(s, axis=1, keepdims=True))
            p = jnp.where(c_valid, e * (1.0 / jnp.sum(e, axis=1, keepdims=True)), 0.0)
            oc_ref[g * ROWS_G:(g + 1) * ROWS_G, :] = _dot(p.astype(BF16), vc_ref[g])
            psum = p[0:Q_PAD]
            for r in range(1, GROUP):
                psum = psum + p[r * Q_PAD:(r + 1) * Q_PAD]
            ms = ms_ref[...]
            imp = sum(_dot(part, ms) for part in _split3(psum))
            cur = lax.shift_right_logical(qpos8, 6)
            s_valid = (si * SEL_BLOCK <= qpos8) & (si < ns)
            forced = (si == 0) | (si == cur) | (si == cur - 1)
            score = jnp.where(s_valid, imp + jnp.where(forced, FORCE_BONUS, 0.0), NEG)
            scores.append(score)
        score = jnp.concatenate(scores, axis=0)
        si4 = lax.broadcasted_iota(jnp.int32, score.shape, 1)
        rank = jnp.zeros(score.shape, jnp.int32)
        for sp in range(ns):
            other = score[:, sp:sp + 1]
            beats = (other > score) | ((other == score) & (si4 > sp))
            rank = rank + beats.astype(jnp.int32)
        bias_all = jnp.where((rank < min(N_SEL, ns)) & (si4 < ns), 0.0, NEG)
        for g in range(N_KV):
            bias_ref[g] = bias_all[g * Q_PAD:(g + 1) * Q_PAD].astype(BF16)

        kw_t = win_ref[0:KV_W, :].astype(BF16)
        vw_t = win_ref[KV_W:2 * KV_W, :].astype(BF16)
        kpos = past - wbuf + lax.broadcasted_iota(jnp.int32, (1, wbuf), 1)
        s1 = _dot(qbd, kw_t) - slope * (qposf - kpos.astype(F32))
        s1 = jnp.where((kpos <= qpos) & (qpos - kpos < WINDOW) & (kpos >= 0), s1, NEG)
        kn = kwn_ref[:, 0:KV_W].astype(BF16)
        vn = kwn_ref[:, KV_W:2 * KV_W].astype(BF16)
        li = lax.broadcasted_iota(jnp.int32, (1, LANES), 1)
        kposn = past + li
        s2 = _dot_nt(qbd, kn) - slope * (qposf - kposn.astype(F32))
        s2 = jnp.where((li < n_new) & (kposn <= qpos) & (qpos - kposn < WINDOW), s2, NEG)
        m = jnp.maximum(jnp.max(s1, axis=1, keepdims=True), jnp.max(s2, axis=1, keepdims=True))
        p1 = jnp.exp(s1 - m)
        p2 = jnp.exp(s2 - m)
        den = jnp.sum(p1, axis=1, keepdims=True) + jnp.sum(p2, axis=1, keepdims=True)
        inv = 1.0 / den
        ow_ref[...] = (_dot_nt((p1 * inv).astype(BF16), vw_t) + _dot((p2 * inv).astype(BF16), vn))
        m_ref[...] = jnp.full(m_ref.shape, NEG, F32)
        acc_ref[...] = jnp.zeros(acc_ref.shape, F32)

    def block_bias(emat):
        rows = []
        for g in range(N_KV):
            bt = _dot(bias_ref[g], emat)
            rows += [bt] * GROUP
        return jnp.concatenate(rows, axis=0)

    def update(scores, pv_fns):
        m_old = m_ref[...]
        m_new = functools.reduce(jnp.maximum, [m_old] + [jnp.max(s, axis=1, keepdims=True) for s in scores])
        alpha = jnp.exp(m_old - m_new)
        ps = [jnp.exp(s - m_new) for s in scores]
        ones = functools.reduce(jnp.add, [jnp.sum(p, axis=1, keepdims=True) for p in ps])
        pv = functools.reduce(jnp.add, [f(p.astype(BF16)) for f, p in zip(pv_fns, ps)])
        acc_ref[:, 0:KV_W] = alpha * acc_ref[:, 0:KV_W] + pv
        acc_ref[:, KV_W:KV_W + LANES] = alpha * acc_ref[:, KV_W:KV_W + LANES] + ones
        m_ref[...] = m_new

    bias_chunk = block_bias(e_ref[...])
    li = lax.broadcasted_iota(jnp.int32, (1, PAGE), 1)
    scores, pv_fns = [], []
    for k in range(pps):
        pg = pages[k]
        kpos = (t * pps + k) * PAGE + li
        s = _dot(qbd, pg[0:KV_W, :].astype(BF16)) - slope * (qposf - kpos.astype(F32))
        scores.append(s + bias_chunk[:, k * PAGE:(k + 1) * PAGE])
        pv_fns.append(lambda p, pg=pg: _dot_nt(p, pg[KV_W:2 * KV_W, :].astype(BF16)))
    update(scores, pv_fns)

    @pl.when(t == nchunks - 1)
    def _():
        nsp = ms_ref.shape[1]
        kposn = past + li
        e_new = (lax.broadcasted_iota(jnp.int32, (nsp, PAGE), 0) == past // SEL_BLOCK).astype(BF16)
        s = _dot_nt(qbd, ksn_ref[:, 0:KV_W].astype(BF16)) - slope * (qposf - kposn.astype(F32)) + block_bias(e_new)
        s = jnp.where((li < n_new) & (kposn <= qpos), s, NEG)
        update([s], [lambda p: _dot(p, ksn_ref[:, KV_W:2 * KV_W].astype(BF16))])
        o_s = acc_ref[:, 0:KV_W] * (1.0 / acc_ref[:, KV_W:KV_W + 1])
        gts = g_ref[...]
        osw_ref[...] = gts[:, 1:2] * o_s + gts[:, 2:3] * ow_ref[...]
        oc_ref[...] = gts[:, 0:1] * oc_ref[...]


def _attn_sample(page_table, pages, qa, qbd, gates, slope_rows, kc, vc, ms, emat, win, ksn, kwn, *, pps, n_new):
    nb, npages = page_table.shape
    nchunks = npages // pps
    past = npages * PAGE
    wbuf = win.shape[2]
    nc = kc.shape[2]
    nsp = ms.shape[1]

    def page_spec(k):
        return pl.BlockSpec((None, 2 * KV_W, PAGE), lambda b, t, pt: (pt[b, t * pps + k], 0, 0))

    per_b = lambda *shape: pl.BlockSpec((None,) + shape, lambda b, t, pt: (b,) + (0,) * len(shape))
    cs = lambda *shape: pl.BlockSpec(shape, lambda b, t, pt: (0,) * len(shape))
    in_specs = [page_spec(k) for k in range(pps)] + [
        per_b(N_KV, ROWS_G, AUG_W), per_b(ROWS_S, KV_W), per_b(ROWS_S, 3), cs(ROWS_S, 1),
        per_b(N_KV, nc, AUG_W), per_b(N_KV, nc, LANES), cs(nc, nsp),
        pl.BlockSpec((nsp, pps * PAGE), lambda b, t, pt: (0, t)),
        per_b(2 * KV_W, wbuf), per_b(LANES, 2 * KV_W), per_b(LANES, 2 * KV_W)]
    out_specs = [per_b(ROWS_S, KV_W), per_b(ROWS_S, LANES)]
    out_shape = [jax.ShapeDtypeStruct((nb, ROWS_S, KV_W), F32), jax.ShapeDtypeStruct((nb, ROWS_S, LANES), F32)]
    scratch = [pltpu.VMEM((N_KV, Q_PAD, nsp), BF16), pltpu.VMEM((ROWS_S, 1), F32),
               pltpu.VMEM((ROWS_S, KV_W + LANES), F32), pltpu.VMEM((ROWS_S, KV_W), F32)]
    gs = pltpu.PrefetchScalarGridSpec(num_scalar_prefetch=1, grid=(nb, nchunks), in_specs=in_specs,
                                      out_specs=out_specs, scratch_shapes=scratch)
    return pl.pallas_call(functools.partial(_attn_sample_kernel, pps=pps, past=past, wbuf=wbuf, n_new=n_new), grid_spec=gs,
                          out_shape=out_shape, compiler_params=_params(2), name="attn_sample")(
        page_table, *([pages] * pps), qa, qbd, gates, slope_rows, kc, vc, ms, emat, win, ksn, kwn)


def _outproj_kernel(x_ref, a_ref, o_ref, nw_ref, sc_ref, sh_ref, g1_ref, wm_ref, woc_ref, won_ref, wo_ref, x1_ref):
    x = x_ref[...]
    h = _mod_norm(x, nw_ref[...], sc_ref[...], sh_ref[...]).astype(BF16)
    mg = jax.nn.sigmoid(_dot(h, wm_ref[...]))
    y_a = _dot(a_ref[...], woc_ref[...])
    y_b = _dot(o_ref[...], won_ref[...])
    mix = (mg[:, 0:D_MODEL] * y_a + mg[:, D_MODEL:2 * D_MODEL] * y_b).astype(BF16)
    x1_ref[...] = x + g1_ref[...] * _dot(mix, wo_ref[...])


def _outproj(x2d, a, o, mod, nw, w_merge, w_oc, w_on, w_o, *, tm):
    n = x2d.shape[0]
    tok = pl.BlockSpec((tm, D_MODEL), lambda i: (i, 0))
    sq = _const_spec((D_MODEL, D_MODEL))
    return pl.pallas_call(
        _outproj_kernel, grid=(n // tm,),
        in_specs=[tok, tok, tok, _const_spec((1, D_MODEL)), mod.spec(1), mod.spec(0), mod.spec(2),
                  _const_spec((D_MODEL, 2 * D_MODEL)), sq, sq, sq],
        out_specs=tok, out_shape=jax.ShapeDtypeStruct((n, D_MODEL), F32),
        compiler_params=_params(1), name="outproj",
    )(x2d, a, o, nw, mod.rows, mod.rows, mod.rows, w_merge, w_oc, w_on, w_o)


FF_CHUNK = D_FF // 2


def _ffn_kernel(x_ref, nw_ref, sc_ref, sh_ref, g2_ref, nf_ref, wg_ref, wu_ref, wd_ref, y_ref):
    x = x_ref[...]
    h = _mod_norm(x, nw_ref[...], sc_ref[...], sh_ref[...]).astype(BF16)
    acc = jnp.zeros(x.shape, F32)
    for c in range(D_FF // FF_CHUNK):
        sl = slice(c * FF_CHUNK, (c + 1) * FF_CHUNK)
        gate = _dot(h, wg_ref[:, sl])
        up = _dot(h, wu_ref[:, sl])
        act = (gate * jax.nn.sigmoid(gate) * up).astype(BF16)
        acc = acc + _dot(act, wd_ref[sl, :])
    x2 = x + g2_ref[...] * acc
    inv = lax.rsqrt(jnp.mean(x2 * x2, axis=-1, keepdims=True) + EPS)
    y_ref[...] = (x2 * inv) * nf_ref[...]


def _ffn(x1, mod, nw2, nf, w_gate, w_up, w_down, *, tm):
    n = x1.shape[0]
    tok = pl.BlockSpec((tm, D_MODEL), lambda i: (i, 0))
    vec = _const_spec((1, D_MODEL))
    return pl.pallas_call(
        _ffn_kernel, grid=(n // tm,),
        in_specs=[tok, vec, mod.spec(4), mod.spec(3), mod.spec(5), vec,
                  _const_spec((D_MODEL, D_FF)), _const_spec((D_MODEL, D_FF)), _const_spec((D_FF, D_MODEL))],
        out_specs=tok, out_shape=jax.ShapeDtypeStruct((n, D_MODEL), F32),
        compiler_params=_params(1), name="ffn",
    )(x1, nw2, mod.rows, mod.rows, mod.rows, nf, w_gate, w_up, w_down)


def _slopes():
    return 2.0 ** (-8.0 * jnp.arange(1, N_HEADS + 1, dtype=F32) / N_HEADS)


def _slope_lanes(slopes):
    parts = _split3(slopes)
    cols = jnp.stack([parts[0], parts[0], parts[1], parts[1], parts[2], parts[2]], axis=1)
    return jnp.pad(cols, ((0, 0), (0, LANES - 6)))


def _pos_lanes(pos_hi, pos_lo):
    cols = jnp.stack([pos_hi, pos_lo] * 3, axis=1).astype(F32)
    return jnp.pad(cols, ((0, 0), (0, LANES - 6))).astype(BF16)


def _token_consts(t):
    pos = jnp.arange(t, dtype=jnp.int32)
    onehot = (pos[:, None] // SEL_BLOCK == jnp.arange(HEAD_DIM, dtype=jnp.int32)[None, :]).astype(F32)
    clo = jnp.concatenate([jnp.zeros((t, HEAD_DIM), F32), onehot], axis=1)
    chi = _pos_lanes((pos // SEL_BLOCK) * SEL_BLOCK, pos % SEL_BLOCK)
    return clo, chi


def _cmp_consts(ncb):
    ci = jnp.arange(ncb, dtype=jnp.int32) * CMP_STRIDE
    return _pos_lanes((ci // SEL_BLOCK) * SEL_BLOCK, ci % SEL_BLOCK)


def _ones_lane():
    return (jnp.arange(LANES) == HEAD_DIM).astype(F32).reshape(1, LANES)


def _imp_matrix(nc, ns_pad):
    c = jnp.arange(nc, dtype=jnp.int32)[:, None]
    s = jnp.arange(ns_pad, dtype=jnp.int32)[None, :]
    per = SEL_BLOCK // CMP_STRIDE
    return ((c // per == s) | ((c % per == per - 1) & (c // per == s - 1))).astype(BF16)


def _prep_weights(w_in, w_phi1, w_phi2, pe_cmp):
    wb = w_in.astype(BF16)
    w_conv_in = wb[:, _C_CONV:_C_QKV]
    w_qkv = jnp.pad(wb[:, _C_QKV:_C_MERGE], ((0, 0), (0, _QKV_COLS_PAD - _QKV_COLS)))
    w_merge = wb[:, _C_MERGE:]
    half = CMP_STRIDE * HEAD_DIM
    w1cat = jnp.concatenate([w_phi1[:, :half], w_phi1[:, half:]], axis=2).astype(BF16)
    w2pad = jnp.pad(w_phi2, ((0, 0), (0, 0), (0, LANES - HEAD_DIM))).astype(BF16)
    pe2 = pe_cmp.reshape(2, 2, half)
    return w_conv_in, w_qkv, w_merge, w1cat, w2pad, pe2


def _prompt_layer(x, mod_p, wts):
    (nw1, nw2, nf, w_conv_in, w_qkv, w_merge, w1cat, w2pad, pe2, w_conv, b_conv, w_oc, w_on, w_o,
     w_gate, w_up, w_down, slopes) = wts
    nb, t, _ = x.shape
    tm = min(512, t)
    x2d = x.reshape(nb * t, D_MODEL)
    mod = _Mod(mod_p, False, t, tm)
    a, tail = _conv_path(x2d, mod, nw1, w_conv_in, w_conv, b_conv, tm=tm, seq_len=t)
    clo, chi = _token_consts(t)
    cv = _ones_lane()
    qt, kvc, kvs, kvw, gts, ksa, vst, kwa, vwt = _qkv_path(x2d, mod, nw1, w_qkv, tm=tm, seq_len=t, consts=(clo, chi))
    npages = t // PAGE
    pt = jnp.broadcast_to(jnp.arange(npages, dtype=jnp.int32), (nb, npages))
    nc = t // CMP_STRIDE
    ns = t // SEL_BLOCK
    nqb = t // Q_BLOCK
    kca, vct = _compress(kvc, pt, pe2, w1cat, w2pad, _cmp_consts(nc), cv,
                         pps=min(16, npages), transpose_v=True, paged=False)
    gates_t = gts[:, :3 * N_HEADS].reshape(nb, nqb, Q_BLOCK, N_KV, GROUP, 3).transpose(0, 3, 1, 5, 4, 2)
    gates_t = gates_t.reshape(nb, N_KV, nqb, 3, GROUP * Q_BLOCK)
    qc_t = _slope_lanes(slopes)[:, :QC_ROWS].reshape(N_KV, GROUP, QC_ROWS).transpose(0, 2, 1)
    qc_t = jnp.repeat(qc_t, Q_BLOCK, axis=2)
    mt = _imp_matrix(nc, ns).T
    o = _attn_prompt_t(qt, gates_t, qc_t, mt, kca, vct, ksa, vst, kwa, vwt)
    x1 = _outproj(x2d, a, o.reshape(nb * t, D_MODEL), mod, nw1, w_merge, w_oc, w_on, w_o, tm=tm)
    y = _ffn(x1, mod, nw2, nf, w_gate, w_up, w_down, tm=tm)
    keep = min(WINDOW, t)
    rows_of = lambda a: a.reshape(nb, 2, N_KV, HEAD_DIM, t).transpose(0, 4, 1, 2, 3)
    state = (rows_of(kvc), rows_of(kvs),
             kvw.reshape(nb, t, 2 * KV_W)[:, t - keep:].reshape(nb, keep, 2, N_KV, HEAD_DIM),
             tail[:, 8 - 2:, :])
    return y.reshape(nb, t, D_MODEL), state


def _sample_layer(x, mod_s, wts, cache_cmp, cache_sel, cache_win, state_conv, page_table):
    (nw1, nw2, nf, w_conv_in, w_qkv, w_merge, w1cat, w2pad, pe2, w_conv, b_conv, w_oc, w_on, w_o,
     w_gate, w_up, w_down, slopes) = wts
    nb, s, _ = x.shape
    n = nb * s
    x2d = x.reshape(n, D_MODEL)
    mod = _Mod(mod_s, True, s, n)
    tpos = jnp.arange(s)
    p1 = jnp.broadcast_to(state_conv[:, 1:2, :], (nb, s, D_MODEL)).reshape(n, D_MODEL)
    p2 = state_conv[:, jnp.minimum(tpos, 1), :].reshape(n, D_MODEL)
    a, u = _conv_path(x2d, mod, nw1, w_conv_in, w_conv, b_conv, tm=n, seq_len=s, prev=(p1, p2))
    qpad, kvc, kvs, kvw, gts = _qkv_path(x2d, mod, nw1, w_qkv, tm=n, seq_len=s)

    npages = page_table.shape[1]
    past = npages * PAGE
    cv = _ones_lane()
    nc = past // CMP_STRIDE
    token_minor = lambda c: c.transpose(0, 2, 3, 4, 1).reshape(c.shape[0], 2 * KV_W, c.shape[1])
    kca, vca = _compress(token_minor(cache_cmp), page_table, pe2, w1cat, w2pad, _cmp_consts(nc), cv,
                         pps=min(16, npages), transpose_v=False, paged=True)

    qh = qpad.reshape(nb, s, N_KV, GROUP, LANES).transpose(0, 2, 3, 1, 4)
    qh = jnp.pad(qh, ((0, 0), (0, 0), (0, 0), (0, Q_PAD - s), (0, 0)))
    sl = jnp.broadcast_to(_slope_lanes(slopes).reshape(1, N_KV, GROUP, 1, LANES), qh.shape)
    qa = jnp.concatenate([qh, sl], axis=-1).reshape(nb, N_KV, ROWS_G, AUG_W)
    eye = jnp.eye(N_KV, dtype=BF16)
    qbd = (qh[..., None, :HEAD_DIM] * eye[None, :, None, None, :, None]).reshape(nb, ROWS_S, KV_W)
    gates = gts[:, :3 * N_HEADS].reshape(nb, s, N_KV, GROUP, 3).transpose(0, 2, 3, 1, 4)
    gates = jnp.pad(gates, ((0, 0), (0, 0), (0, 0), (0, Q_PAD - s), (0, 0))).reshape(nb, ROWS_S, 3)
    slope_rows = jnp.repeat(slopes, Q_PAD).reshape(ROWS_S, 1)
    ns = past // SEL_BLOCK + 1
    nsp = -(-ns // LANES) * LANES
    ms = _imp_matrix(nc, nsp)
    tok = jnp.arange(past, dtype=jnp.int32)
    emat = (jnp.arange(nsp, dtype=jnp.int32)[:, None] == tok[None, :] // SEL_BLOCK).astype(BF16)
    pad_rows = lambda r: jnp.pad(r.reshape(nb, s, 2 * KV_W), ((0, 0), (0, LANES - s), (0, 0)))
    osw, ocg = _attn_sample(page_table, token_minor(cache_sel), qa, qbd, gates, slope_rows, kca, vca,
                            ms, emat, token_minor(cache_win), pad_rows(kvs), pad_rows(kvw),
                            pps=min(16, npages), n_new=s)
    osw = osw.reshape(nb, N_KV, GROUP, Q_PAD, N_KV, HEAD_DIM)
    o_sw = jnp.einsum('bgrqgd->bqgrd', osw)
    o_c = ocg.reshape(nb, N_KV, GROUP, Q_PAD, LANES)[..., :HEAD_DIM].transpose(0, 3, 1, 2, 4)
    o = (o_sw + o_c)[:, :s].reshape(n, N_HEADS * HEAD_DIM).astype(BF16)

    x1 = _outproj(x2d, a, o, mod, nw1, w_merge, w_oc, w_on, w_o, tm=n)
    y = _ffn(x1, mod, nw2, nf, w_gate, w_up, w_down, tm=n)
    kv5 = lambda r: r.reshape(nb, s, 2, N_KV, HEAD_DIM)
    win = jnp.concatenate([cache_win, kv5(kvw)], axis=1)[:, s:]
    state = (kv5(kvc), kv5(kvs), win, u.reshape(nb, s, D_MODEL)[:, s - 2:])
    return y.reshape(nb, s, D_MODEL), state


def kernel(x_prompt, x_sample, c_prompt, c_sample, cache_cmp, cache_sel, cache_win, state_conv, page_table,
           w_ada, b_ada, norm1, w_in, w_conv, b_conv, w_out_conv, pe_cmp, w_phi1, w_phi2, w_o_nsa, w_out,
           norm2, w_gate, w_up, w_down, norm_f):
    depth = w_ada.shape[0]
    assert depth == 1, "single-layer trunk"
    nbp, nbs = c_prompt.shape[0], c_sample.shape[0]
    slopes = _slopes()
    l = 0
    c_all = jnp.concatenate([c_prompt, c_sample], axis=0)
    c_all = jnp.pad(c_all, ((0, -c_all.shape[0] % 8), (0, 0)))
    mod = _ada(c_all, w_ada[l], b_ada[l])
    w_conv_in, w_qkv, w_merge, w1cat, w2pad, pe2 = _prep_weights(w_in[l], w_phi1[l], w_phi2[l], pe_cmp[l])
    row = lambda v: v.reshape(1, -1)
    wts = (row(norm1[l]), row(norm2[l]), row(norm_f), w_conv_in, w_qkv, w_merge, w1cat, w2pad, pe2,
           w_conv[l], row(b_conv[l]), w_out_conv[l].astype(BF16), w_o_nsa[l].astype(BF16), w_out[l].astype(BF16),
           w_gate[l].astype(BF16), w_up[l].astype(BF16), w_down[l].astype(BF16), slopes)
    yp, st_p = _prompt_layer(x_prompt, mod[:nbp], wts)
    ys, st_s = _sample_layer(x_sample, mod[nbp:nbp + nbs], wts, cache_cmp[l], cache_sel[l], cache_win[l],
                             state_conv[l], page_table)
    return (yp, ys, st_p[0][None], st_p[1][None], st_p[2][None], st_p[3][None],
            st_s[0][None], st_s[1][None], st_s[2][None], st_s[3][None])
```

```python
import functools

import jax
import jax.numpy as jnp
import numpy as np
from jax import lax
from jax.experimental import pallas as pl
from jax.experimental.pallas import tpu as pltpu

F32 = jnp.float32
BF16 = jnp.bfloat16

D_MODEL = 1024
N_HEADS = 16
HEAD_DIM = 64
N_KV = 4
GROUP = N_HEADS // N_KV
KV_W = N_KV * HEAD_DIM
CMP_BLOCK = 32
CMP_STRIDE = 16
SEL_BLOCK = 64
N_SEL = 16
WINDOW = 512
D_PHI = 2 * HEAD_DIM
Q_BLOCK = 256
PAGE = 128
D_FF = ((8 * D_MODEL // 3 + 255) // 256) * 256
EPS = 1e-6
NEG = -1e30
FORCE_BONUS = 1e3

LANES = 128
AUG_W = 2 * LANES
HALVES_PER_PAGE = PAGE // CMP_STRIDE
ROW_TILES = 2 * KV_W // LANES
VMEM_LIMIT = 56 * 1024 * 1024
KEY_TILE = 2 * LANES
VT_ROWS = HEAD_DIM + 16
QC_ROWS = 16
SEL_TILES_PER_ITER = 4

_C_CONV = 0
_C_QKV = 3 * D_MODEL
_C_GATE = _C_QKV + N_HEADS * HEAD_DIM + 6 * KV_W
_C_MERGE = _C_GATE + 3 * N_HEADS
_QKV_COLS = _C_MERGE - _C_QKV
_QKV_COLS_PAD = -(-_QKV_COLS // LANES) * LANES


def _dot(a, b):
    return jnp.dot(a, b, preferred_element_type=F32)


def _dot_nt(a, b):
    return lax.dot_general(a, b, (((1,), (1,)), ((), ())), preferred_element_type=F32)


def _params(n_axes):
    return pltpu.CompilerParams(dimension_semantics=("arbitrary",) * n_axes, vmem_limit_bytes=VMEM_LIMIT)


def _const_spec(shape):
    return pl.BlockSpec(shape, lambda *_: (0,) * len(shape))


def _mod_norm(x, nw, sc, sh):
    inv = lax.rsqrt(jnp.mean(x * x, axis=-1, keepdims=True) + EPS)
    return (x * inv) * nw * (1.0 + sc) + sh


def _split3(x):
    a = x.astype(BF16)
    r = x - a.astype(F32)
    b = r.astype(BF16)
    c = (r - b.astype(F32)).astype(BF16)
    return a, b, c


def _ada_kernel(c_ref, w_ref, b_ref, o_ref):
    c = c_ref[...]
    s = c * jax.nn.sigmoid(c)
    o_ref[...] = jnp.dot(s, w_ref[...], preferred_element_type=F32, precision=lax.Precision.HIGHEST) + b_ref[...]


def _ada(c, w_ada, b_ada):
    n = c.shape[0]
    tn = 1536
    return pl.pallas_call(
        _ada_kernel,
        grid=(6 * D_MODEL // tn,),
        in_specs=[_const_spec((n, D_MODEL)),
                  pl.BlockSpec((D_MODEL, tn), lambda j: (0, j)),
                  pl.BlockSpec((1, tn), lambda j: (0, j))],
        out_specs=pl.BlockSpec((n, tn), lambda j: (0, j)),
        out_shape=jax.ShapeDtypeStruct((n, 6 * D_MODEL), F32),
        compiler_params=_params(1),
        name="ada",
    )(c, w_ada, b_ada.reshape(1, -1))


class _Mod:
    def __init__(self, mod, per_token, seq_len, tm):
        self.per_token = per_token
        if per_token:
            self.rows = jnp.repeat(mod, seq_len, axis=0)
        else:
            self.rows = mod.reshape(mod.shape[0], 1, 6 * D_MODEL)
        self.tiles_per_seq = None if per_token else seq_len // tm
        self.tm = tm

    def spec(self, k):
        if self.per_token:
            return pl.BlockSpec((self.tm, D_MODEL), lambda i: (i, k))
        tps = self.tiles_per_seq
        return pl.BlockSpec((None, 1, D_MODEL), lambda i: (i // tps, 0, k))


def _conv_kernel(*refs, carry_rows, seq_len):
    if carry_rows:
        (x_ref, nw_ref, sc_ref, sh_ref, w_ref, wc_ref, bc_ref, a_ref, tail_ref, carry_ref) = refs
    else:
        (x_ref, nw_ref, sc_ref, sh_ref, w_ref, wc_ref, bc_ref, p1_ref, p2_ref, a_ref, tail_ref) = refs
    tm = x_ref.shape[0]
    h = _mod_norm(x_ref[...], nw_ref[...], sc_ref[...], sh_ref[...]).astype(BF16)
    z = _dot(h, w_ref[...])
    bg = z[:, 0:D_MODEL]
    u = z[:, D_MODEL:2 * D_MODEL] * z[:, 2 * D_MODEL:3 * D_MODEL]
    row = lax.broadcasted_iota(jnp.int32, (tm, 1), 0)
    u1 = pltpu.roll(u, 1, 0)
    u2 = pltpu.roll(u, 2, 0)
    if carry_rows:
        @pl.when(pl.program_id(0) % carry_rows == 0)
        def _():
            carry_ref[...] = jnp.zeros_like(carry_ref)
        c0 = carry_ref[0:1, :]
        c1 = carry_ref[1:2, :]
        u1 = jnp.where(row == 0, c1, u1)
        u2 = jnp.where(row == 0, c0, jnp.where(row == 1, c1, u2))
        carry_ref[0:2, :] = u[tm - 2:tm, :]
        tail_ref[...] = u[tm - 8:tm, :]
    else:
        pos = lax.rem(row, seq_len)
        u1 = jnp.where(pos >= 1, u1, p1_ref[...])
        u2 = jnp.where(pos >= 2, u2, p2_ref[...])
        tail_ref[...] = u
    v = bc_ref[...] + wc_ref[0:1, :] * u2 + wc_ref[1:2, :] * u1 + wc_ref[2:3, :] * u
    a_ref[...] = (bg * v).astype(BF16)


def _conv_path(x2d, mod, nw, w_conv_in, w_conv, b_conv, *, tm, seq_len, prev=None):
    n = x2d.shape[0]
    tok = pl.BlockSpec((tm, D_MODEL), lambda i: (i, 0))
    in_specs = [tok, _const_spec((1, D_MODEL)), mod.spec(1), mod.spec(0),
                _const_spec((D_MODEL, 3 * D_MODEL)), _const_spec((3, D_MODEL)), _const_spec((1, D_MODEL))]
    args = [x2d, nw, mod.rows, mod.rows, w_conv_in, w_conv, b_conv]
    if prev is None:
        tps = seq_len // tm
        out_specs = [tok, pl.BlockSpec((None, 8, D_MODEL), lambda i: (i // tps, 0, 0))]
        out_shape = [jax.ShapeDtypeStruct((n, D_MODEL), BF16), jax.ShapeDtypeStruct((n // seq_len, 8, D_MODEL), F32)]
        scratch = [pltpu.VMEM((8, D_MODEL), F32)]
        kern = functools.partial(_conv_kernel, carry_rows=tps, seq_len=seq_len)
    else:
        in_specs += [tok, tok]
        args += list(prev)
        out_specs = [tok, tok]
        out_shape = [jax.ShapeDtypeStruct((n, D_MODEL), BF16), jax.ShapeDtypeStruct((n, D_MODEL), F32)]
        scratch = []
        kern = functools.partial(_conv_kernel, carry_rows=0, seq_len=seq_len)
    return pl.pallas_call(kern, grid=(n // tm,), in_specs=in_specs, out_specs=out_specs, out_shape=out_shape,
                          scratch_shapes=scratch, compiler_params=_params(1), name="conv_path")(*args)


def _qkv_kernel(*refs, aug):
    if aug:
        (x_ref, nw_ref, sc_ref, sh_ref, w_ref, clo_ref, chi_ref,
         q_ref, kvc_ref, kvs_ref, kvw_ref, g_ref, ksa_ref, vsa_ref, kwa_ref, vwa_ref) = refs
    else:
        (x_ref, nw_ref, sc_ref, sh_ref, w_ref, q_ref, kvc_ref, kvs_ref, kvw_ref, g_ref) = refs
    tm = x_ref.shape[0]
    h = _mod_norm(x_ref[...], nw_ref[...], sc_ref[...], sh_ref[...]).astype(BF16)
    z = _dot(h, w_ref[...])
    low = lax.broadcasted_iota(jnp.int32, (tm, LANES), 1) < HEAD_DIM
    nq = N_HEADS * HEAD_DIM
    for c in range(N_HEADS // 2):
        t = z[:, c * LANES:(c + 1) * LANES] * (HEAD_DIM ** -0.5)
        if aug:
            tt = t.T.astype(BF16)
            q_ref[2 * c] = tt[0:HEAD_DIM]
            q_ref[2 * c + 1] = tt[HEAD_DIM:2 * HEAD_DIM]
        else:
            q_ref[:, (2 * c) * LANES:(2 * c + 1) * LANES] = jnp.where(low, t, 0.0).astype(BF16)
            q_ref[:, (2 * c + 1) * LANES:(2 * c + 2) * LANES] = jnp.where(low, pltpu.roll(t, HEAD_DIM, 1), 0.0).astype(BF16)
    if aug:
        kvc_ref[...] = z[:, nq:nq + 2 * KV_W].T
        kvs_ref[...] = z[:, nq + 2 * KV_W:nq + 4 * KV_W].T
    else:
        kvc_ref[...] = z[:, nq:nq + 2 * KV_W]
        kvs_ref[...] = z[:, nq + 2 * KV_W:nq + 4 * KV_W]
    kvw_ref[...] = z[:, nq + 4 * KV_W:nq + 6 * KV_W]
    g_ref[...] = jax.nn.sigmoid(z[:, nq + 6 * KV_W:nq + 6 * KV_W + LANES])
    if aug:
        chi = chi_ref[...]
        ones_row = (lax.broadcasted_iota(jnp.int32, (VT_ROWS - HEAD_DIM, KEY_TILE), 0) == 0).astype(BF16)
        for br, (ka_ref, va_ref) in enumerate(((ksa_ref, vsa_ref), (kwa_ref, vwa_ref))):
            kbase = nq + 2 * KV_W * (br + 1)
            clo = clo_ref[...] if br == 0 else 0.0
            for g in range(N_KV):
                kt = z[:, kbase + (g // 2) * LANES:kbase + (g // 2 + 1) * LANES]
                if g % 2:
                    kt = pltpu.roll(kt, HEAD_DIM, 1)
                ka_ref[g, :, 0:LANES] = jnp.where(low, kt, clo).astype(BF16)
                ka_ref[g, :, LANES:AUG_W] = chi
            for c in range(N_KV // 2):
                vt = z[:, kbase + KV_W + c * LANES:kbase + KV_W + (c + 1) * LANES].T.astype(BF16)
                for gg in range(2):
                    for j in range(tm // KEY_TILE):
                        va_ref[2 * c + gg, j, 0:HEAD_DIM, :] = vt[gg * HEAD_DIM:(gg + 1) * HEAD_DIM, j * KEY_TILE:(j + 1) * KEY_TILE]
                        va_ref[2 * c + gg, j, HEAD_DIM:VT_ROWS, :] = ones_row


def _qkv_path(x2d, mod, nw, w_qkv, *, tm, seq_len, consts=None):
    n = x2d.shape[0]
    aug = consts is not None
    tok = lambda w: pl.BlockSpec((tm, w), lambda i: (i, 0))
    in_specs = [tok(D_MODEL), _const_spec((1, D_MODEL)), mod.spec(1), mod.spec(0),
                _const_spec((D_MODEL, _QKV_COLS_PAD))]
    args = [x2d, nw, mod.rows, mod.rows, w_qkv]
    out_specs = [tok(N_HEADS * LANES), tok(2 * KV_W), tok(2 * KV_W), tok(2 * KV_W), tok(LANES)]
    out_shape = [jax.ShapeDtypeStruct((n, N_HEADS * LANES), BF16)] + \
                [jax.ShapeDtypeStruct((n, 2 * KV_W), F32)] * 3 + [jax.ShapeDtypeStruct((n, LANES), F32)]
    if aug:
        tps = seq_len // tm
        nb = n // seq_len
        pos = lambda w: pl.BlockSpec((tm, w), lambda i: (i % tps, 0))
        in_specs += [pos(LANES), pos(LANES)]
        args += list(consts)
        out_specs[0] = pl.BlockSpec((None, N_HEADS, HEAD_DIM, tm), lambda i: (i // tps, 0, 0, i % tps))
        out_shape[0] = jax.ShapeDtypeStruct((nb, N_HEADS, HEAD_DIM, seq_len), BF16)
        for k in (1, 2):
            out_specs[k] = pl.BlockSpec((None, 2 * KV_W, tm), lambda i: (i // tps, 0, i % tps))
            out_shape[k] = jax.ShapeDtypeStruct((nb, 2 * KV_W, seq_len), F32)
        ka = pl.BlockSpec((None, N_KV, tm, AUG_W), lambda i: (i // tps, 0, i % tps, 0))
        va = pl.BlockSpec((None, N_KV, tm // KEY_TILE, VT_ROWS, KEY_TILE), lambda i: (i // tps, 0, i % tps, 0, 0))
        out_specs += [ka, va, ka, va]
        ka_s = jax.ShapeDtypeStruct((nb, N_KV, seq_len, AUG_W), BF16)
        va_s = jax.ShapeDtypeStruct((nb, N_KV, seq_len // KEY_TILE, VT_ROWS, KEY_TILE), BF16)
        out_shape += [ka_s, va_s, ka_s, va_s]
    return pl.pallas_call(functools.partial(_qkv_kernel, aug=aug), grid=(n // tm,), in_specs=in_specs,
                          out_specs=out_specs, out_shape=out_shape, compiler_params=_params(1), name="qkv_path")(*args)


def _block_major_loader(ref, k, c):
    def load(p, nrows):
        return ref[k, c, p * HALVES_PER_PAGE:p * HALVES_PER_PAGE + nrows, :]
    return load


def _gather_pair(load, nrows):
    low = lax.broadcasted_iota(jnp.int32, (nrows, LANES), 1) < HEAD_DIM
    even, odd = [], []
    for qq in range(CMP_STRIDE // 2):
        a = load(2 * qq, nrows)
        b = load(2 * qq + 1, nrows)
        even.append(jnp.where(low, a, pltpu.roll(b, HEAD_DIM, 1)))
        odd.append(jnp.where(low, pltpu.roll(a, HEAD_DIM, 1), b))
    return jnp.concatenate(even, axis=1), jnp.concatenate(odd, axis=1)


def _compress_kernel(pt_ref, *refs, pps, transpose_v):
    pages = refs[:pps]
    nxt_ref, pe_ref, w1_ref, w2_ref, chi_ref, cv_ref, perm_ref, kc_ref, vc_ref, xt_ref = refs[pps:]
    perm = perm_ref[...]
    for k, pg in enumerate(list(pages) + [nxt_ref]):
        for c in range(ROW_TILES):
            xt_ref[k, c] = _dot_nt(perm, pg[c * LANES:(c + 1) * LANES, :].astype(BF16))
    loaders = lambda c: [_block_major_loader(xt_ref, k, c) for k in range(pps)]
    look_loader = lambda c: _block_major_loader(xt_ref, pps, c)
    t = pl.program_id(1)
    last = t == pl.num_programs(1) - 1
    nhb = pps * HALVES_PER_PAGE
    n = N_KV * nhb
    row = lax.broadcasted_iota(jnp.int32, (n, 1), 0)
    chi = chi_ref[...]
    for kv in range(2):
        by_group, look = [], []
        for cc in range(N_KV // 2):
            c = kv * (N_KV // 2) + cc
            pairs = [_gather_pair(ld, HALVES_PER_PAGE) for ld in loaders(c)]
            by_group += [[ev for ev, _ in pairs], [od for _, od in pairs]]
            look += list(_gather_pair(look_loader(c), 1))
        parts = [x for group in by_group for x in group]
        extra = jnp.concatenate([pe_ref[kv], jnp.zeros((2, CMP_STRIDE * HEAD_DIM), F32)] + look, axis=0)
        xmat = jnp.concatenate(parts + [extra], axis=0).astype(BF16)
        hab = _dot(xmat, w1_ref[kv])
        ha = hab[0:n, 0:D_PHI]
        hb = hab[0:n, D_PHI:2 * D_PHI]
        pbias = hab[n:n + 1, 0:D_PHI] + hab[n + 1:n + 2, D_PHI:2 * D_PHI]
        hbn = pltpu.roll(hb, n - 1, 0)
        for g in range(N_KV):
            la = jnp.where(last, 0.0, hab[n + 4 + g:n + 5 + g, D_PHI:2 * D_PHI])
            hbn = jnp.where(row == g * nhb + nhb - 1, la, hbn)
        act = jax.nn.gelu(ha + hbn + pbias).astype(BF16)
        out = _dot(act, w2_ref[kv])
        for g in range(N_KV):
            blk = out[g * nhb:(g + 1) * nhb, :]
            if kv == 0:
                kc_ref[g, :, 0:LANES] = blk.astype(BF16)
                kc_ref[g, :, LANES:AUG_W] = chi
            elif transpose_v:
                vc_ref[g, :, :] = (blk + cv_ref[...]).T.astype(BF16)
            else:
                vc_ref[g, :, :] = (blk + cv_ref[...]).astype(BF16)


def _compress(pages, page_table, pe2, w1cat, w2pad, chi_c, cv, *, pps, transpose_v, paged):
    nb, npages = page_table.shape
    nchunks = npages // pps
    nhb = pps * HALVES_PER_PAGE
    ncb = npages * HALVES_PER_PAGE
    where = (lambda b, j: (j, 0, 0)) if paged else (lambda b, j: (b, 0, j))

    def page_spec(k):
        return pl.BlockSpec((None, 2 * KV_W, PAGE), lambda b, t, pt: where(b, pt[b, t * pps + k]))

    nxt_spec = pl.BlockSpec((None, 2 * KV_W, PAGE),
                            lambda b, t, pt: where(b, pt[b, jnp.minimum((t + 1) * pps, npages - 1)]))
    scratch = [pltpu.VMEM((pps + 1, ROW_TILES, PAGE, LANES), F32)]
    r = np.arange(PAGE, dtype=np.int32)
    perm = (r[None, :] == (CMP_STRIDE * (r % HALVES_PER_PAGE) + r // HALVES_PER_PAGE)[:, None]).astype(BF16)
    cs = lambda shape: pl.BlockSpec(shape, lambda b, t, pt: (0,) * len(shape))
    in_specs = [page_spec(k) for k in range(pps)] + [
        nxt_spec, cs((2, 2, CMP_STRIDE * HEAD_DIM)), cs((2, CMP_STRIDE * HEAD_DIM, 2 * D_PHI)),
        cs((2, D_PHI, LANES)), pl.BlockSpec((nhb, LANES), lambda b, t, pt: (t, 0)), cs((1, LANES)),
        cs((PAGE, PAGE))]
    if transpose_v:
        v_spec = pl.BlockSpec((None, N_KV, LANES, nhb), lambda b, t, pt: (b, 0, 0, t))
        v_shape = jax.ShapeDtypeStruct((nb, N_KV, LANES, ncb), BF16)
    else:
        v_spec = pl.BlockSpec((None, N_KV, nhb, LANES), lambda b, t, pt: (b, 0, t, 0))
        v_shape = jax.ShapeDtypeStruct((nb, N_KV, ncb, LANES), BF16)
    out_specs = [pl.BlockSpec((None, N_KV, nhb, AUG_W), lambda b, t, pt: (b, 0, t, 0)), v_spec]
    out_shape = [jax.ShapeDtypeStruct((nb, N_KV, ncb, AUG_W), BF16), v_shape]
    gs = pltpu.PrefetchScalarGridSpec(num_scalar_prefetch=1, grid=(nb, nchunks), in_specs=in_specs, out_specs=out_specs,
                                      scratch_shapes=scratch)
    return pl.pallas_call(functools.partial(_compress_kernel, pps=pps, transpose_v=transpose_v),
                          grid_spec=gs, out_shape=out_shape,
                          compiler_params=_params(2), name="compress")(
        page_table, *([pages] * (pps + 1)), pe2, w1cat, w2pad, chi_c, cv, perm)


REMOVED = -3e38


def _top_k_mask(score, index, k, axis):
    work = score
    selected = jnp.zeros(score.shape, jnp.bool_)
    for _ in range(k):
        best = jnp.max(work, axis=axis, keepdims=True)
        first = jnp.min(jnp.where(work == best, index, score.shape[axis]), axis=axis, keepdims=True)
        hit = index == first
        selected = selected | hit
        work = jnp.where(hit, REMOVED, work)
    return selected


def _attn_prompt_t_kernel(q_ref, g_ref, qc_ref, mt_ref, kc_ref, vc_ref, ks_ref, vs_ref, kw_ref, vw_ref, o_ref,
                          qa_ref, qw_ref, m_ref, acc_ref):
    nc = kc_ref.shape[0]
    ns = mt_ref.shape[0]
    kt = vs_ref.shape[2]
    ncol = GROUP * Q_BLOCK
    qb = pl.program_id(2)
    q0 = qb * Q_BLOCK
    qpos = q0 + (lax.broadcasted_iota(jnp.int32, (1, ncol), 1) & (Q_BLOCK - 1))

    for ref in (qw_ref, qa_ref):
        for r in range(GROUP):
            ref[0:HEAD_DIM, r * Q_BLOCK:(r + 1) * Q_BLOCK] = q_ref[r]
        ref[HEAD_DIM:2 * HEAD_DIM, :] = jnp.zeros((HEAD_DIM, ncol), BF16)
        ref[2 * HEAD_DIM:2 * HEAD_DIM + QC_ROWS, :] = qc_ref[...]
        ref[2 * HEAD_DIM + QC_ROWS:AUG_W, :] = jnp.zeros((AUG_W - 2 * HEAD_DIM - QC_ROWS, ncol), BF16)

    s = _dot(kc_ref[...], qw_ref[...])
    ci = lax.broadcasted_iota(jnp.int32, (nc, 1), 0)
    last_valid = lax.shift_right_arithmetic(qpos - (CMP_BLOCK - 1), CMP_STRIDE.bit_length() - 1)
    c_valid = ci <= last_valid
    s = jnp.where(c_valid, s, NEG)
    e = jnp.exp(s - jnp.max(s, axis=0, keepdims=True))
    inv = jnp.where(last_valid >= 0, 1.0 / jnp.sum(e, axis=0, keepdims=True), 0.0)
    p = e * inv
    o_c = _dot(vc_ref[...], p.astype(BF16))[0:HEAD_DIM]

    psum = p[:, 0:Q_BLOCK]
    for r in range(1, GROUP):
        psum = psum + p[:, r * Q_BLOCK:(r + 1) * Q_BLOCK]
    mt = mt_ref[...]
    imp = sum(_dot(mt, part) for part in _split3(psum))
    si = lax.broadcasted_iota(jnp.int32, (ns, Q_BLOCK), 0)
    qpos_t = q0 + lax.broadcasted_iota(jnp.int32, (ns, Q_BLOCK), 1)
    cur = lax.shift_right_logical(qpos_t, 6)
    s_valid = si * SEL_BLOCK <= qpos_t
    forced = (si == 0) | (si == cur) | (si == cur - 1)
    score = jnp.where(s_valid, imp + jnp.where(forced, FORCE_BONUS, 0.0), NEG)
    selected = _top_k_mask(score, si, min(N_SEL, ns), axis=0)
    bias_t = jnp.where(selected, 0.0, NEG).astype(BF16)
    qa_ref[HEAD_DIM:HEAD_DIM + ns, :] = jnp.concatenate([bias_t] * GROUP, axis=1)

    t_hi = lax.div(q0, kt)
    key_iota = lax.broadcasted_iota(jnp.int32, (kt, 1), 0)

    n_back = WINDOW // kt
    s_w, v_w, live_w = [], [], []
    for j in range(n_back + 1):
        tw = t_hi - n_back + j
        tc = jnp.maximum(tw, 0)
        kpos = tw * kt + key_iota
        s = _dot(kw_ref[pl.ds(pl.multiple_of(tc * kt, kt), kt), :], qw_ref[...])
        live = None
        if j == 0:
            oldest = jnp.where(tw >= 0, qpos - WINDOW, jnp.iinfo(jnp.int32).max)
            s = jnp.where(kpos > oldest, s, NEG)
        elif j == n_back:
            s = jnp.where(kpos <= qpos, s, NEG)
        else:
            live = tw >= 0
        s_w.append(s)
        v_w.append(vw_ref[tc])
        live_w.append(live)
    tile_max = [jnp.max(s, axis=0, keepdims=True) for s in s_w]
    tile_max = [mx if live is None else jnp.where(live, mx, NEG) for mx, live in zip(tile_max, live_w)]
    m_w = functools.reduce(jnp.maximum, tile_max)
    shift_w = [m_w if live is None else jnp.where(live, m_w, -NEG) for live in live_w]
    acc_w = sum(_dot(v, jnp.exp(s - sh).astype(BF16)) for s, v, sh in zip(s_w, v_w, shift_w))
    o_w = acc_w[0:HEAD_DIM] * (1.0 / acc_w[HEAD_DIM:HEAD_DIM + 1])

    m_ref[...] = jnp.full(m_ref.shape, NEG, F32)
    acc_ref[...] = jnp.zeros(acc_ref.shape, F32)

    def sel_update(runs, causal_last):
        qa = qa_ref[...]
        ss, vts = [], []
        for ri, (t0, n) in enumerate(runs):
            k0 = pl.multiple_of(t0 * kt, kt)
            s = _dot(ks_ref[pl.ds(k0, n * kt), :], qa)
            for i in range(n):
                si_ = s[i * kt:(i + 1) * kt]
                if causal_last and ri == len(runs) - 1 and i == n - 1:
                    si_ = jnp.where(k0 + i * kt + key_iota <= qpos, si_, NEG)
                ss.append(si_)
                vts.append(vs_ref[t0 + i])
        m_old = m_ref[...]
        m_new = functools.reduce(jnp.maximum, [m_old] + [jnp.max(s, axis=0, keepdims=True) for s in ss])
        pv = sum(_dot(vt, jnp.exp(s - m_new).astype(BF16)) for vt, s in zip(vts, ss))
        acc_ref[...] = jnp.exp(m_old - m_new) * acc_ref[...] + pv
        m_ref[...] = m_new

    per_tile = kt // SEL_BLOCK
    lo_blk = jnp.min(jnp.where(selected & (si >= per_tile), si, ns))
    start = jnp.clip(lax.div(lo_blk, per_tile), 1, jnp.maximum(t_hi, 1))
    n_plain = jnp.maximum(t_hi - start, 0)
    n_group = lax.div(n_plain, SEL_TILES_PER_ITER)

    def body(i, carry):
        sel_update([(start + i * SEL_TILES_PER_ITER, SEL_TILES_PER_ITER)], False)
        return carry

    lax.fori_loop(0, n_group, body, 0)
    rem = n_plain - n_group * SEL_TILES_PER_ITER
    for left in range(SEL_TILES_PER_ITER):
        @pl.when((rem == left) & (t_hi >= 1))
        def _():
            sel_update([(0, 1), (t_hi - left, left + 1)], True)

    @pl.when(t_hi == 0)
    def _():
        sel_update([(0, 1)], True)

    acc = acc_ref[...]
    o_s = acc[0:HEAD_DIM] * (1.0 / acc[HEAD_DIM:HEAD_DIM + 1])

    gts = g_ref[...]
    o = gts[0:1] * o_c + gts[1:2] * o_s + gts[2:3] * o_w
    for c in range(GROUP // 2):
        pair = jnp.concatenate([o[:, (2 * c) * Q_BLOCK:(2 * c + 1) * Q_BLOCK],
                                o[:, (2 * c + 1) * Q_BLOCK:(2 * c + 2) * Q_BLOCK]], axis=0)
        o_ref[:, c * LANES:(c + 1) * LANES] = pair.T.astype(BF16)


def _attn_prompt_t(qt, gates_t, qc_t, mt, kc, vct, ks, vst, kw, vwt):
    nb, t = qt.shape[0], qt.shape[3]
    nc = kc.shape[2]
    ns = mt.shape[0]
    ntile, kt = vst.shape[2], vst.shape[4]
    ncol = GROUP * Q_BLOCK
    per_bg = lambda *shape: pl.BlockSpec((None, None) + shape, lambda b, g, i: (b, g) + (0,) * len(shape))
    in_specs = [pl.BlockSpec((None, GROUP, HEAD_DIM, Q_BLOCK), lambda b, g, i: (b, g, 0, i)),
                pl.BlockSpec((None, None, None, 3, ncol), lambda b, g, i: (b, g, i, 0, 0)),
                pl.BlockSpec((None, QC_ROWS, ncol), lambda b, g, i: (g, 0, 0)),
                pl.BlockSpec((ns, nc), lambda b, g, i: (0, 0)),
                per_bg(nc, AUG_W), per_bg(LANES, nc), per_bg(t, AUG_W), per_bg(ntile, VT_ROWS, kt),
                per_bg(t, AUG_W), per_bg(ntile, VT_ROWS, kt)]
    out_spec = pl.BlockSpec((None, Q_BLOCK, GROUP * HEAD_DIM), lambda b, g, i: (b, i, g))
    scratch = [pltpu.VMEM((AUG_W, ncol), BF16), pltpu.VMEM((AUG_W, ncol), BF16),
               pltpu.VMEM((1, ncol), F32), pltpu.VMEM((VT_ROWS, ncol), F32)]
    return pl.pallas_call(_attn_prompt_t_kernel, grid=(nb, N_KV, t // Q_BLOCK),
                          in_specs=in_specs, out_specs=out_spec,
                          out_shape=jax.ShapeDtypeStruct((nb, t, N_HEADS * HEAD_DIM), BF16),
                          scratch_shapes=scratch, compiler_params=_params(3), name="attn_prompt")(
        qt, gates_t, qc_t, mt, kc, vct, ks, vst, kw, vwt)


Q_PAD = 8
ROWS_G = GROUP * Q_PAD
ROWS_S = N_KV * ROWS_G


def _attn_sample_kernel(pt_ref, *refs, pps, past, wbuf, n_new):
    pages = refs[:pps]
    (qa_ref, qbd_ref, g_ref, slope_ref, kc_ref, vc_ref, ms_ref, e_ref, win_ref, ksn_ref, kwn_ref,
     osw_ref, oc_ref, bias_ref, m_ref, acc_ref, ow_ref) = refs[pps:]
    t = pl.program_id(1)
    nchunks = pl.num_programs(1)
    nc = kc_ref.shape[1]
    ns = past // SEL_BLOCK + 1
    row = lax.broadcasted_iota(jnp.int32, (ROWS_S, 1), 0)
    qpos = past + (row & (Q_PAD - 1))
    qposf = qpos.astype(F32)
    slope = slope_ref[...]
    qbd = qbd_ref[...]

    @pl.when(t == 0)
    def _():
        nsp = ms_ref.shape[1]
        rg = lax.broadcasted_iota(jnp.int32, (ROWS_G, 1), 0)
        qpos_g = past + (rg & (Q_PAD - 1))
        r8 = lax.broadcasted_iota(jnp.int32, (Q_PAD, 1), 0)
        qpos8 = past + r8
        si = lax.broadcasted_iota(jnp.int32, (Q_PAD, nsp), 1)
        scores = []
        for g in range(N_KV):
            s = _dot_nt(qa_ref[g], kc_ref[g])
            ci = lax.broadcasted_iota(jnp.int32, (1, nc), 1)
            c_valid = ci * CMP_STRIDE + (CMP_BLOCK - 1) <= qpos_g
            s = jnp.where(c_valid, s, NEG)
            e = jnp.exp(s - jnp.max(s, axis=1, keepdims=True))
            p = jnp.where(c_valid, e * (1.0 / jnp.sum(e, axis=1, keepdims=True)), 0.0)
            oc_ref[g * ROWS_G:(g + 1) * ROWS_G, :] = _dot(p.astype(BF16), vc_ref[g])
            psum = p[0:Q_PAD]
            for r in range(1, GROUP):
                psum = psum + p[r * Q_PAD:(r + 1) * Q_PAD]
            ms = ms_ref[...]
            imp = sum(_dot(part, ms) for part in _split3(psum))
            cur = lax.shift_right_logical(qpos8, 6)
            s_valid = (si * SEL_BLOCK <= qpos8) & (si < ns)
            forced = (si == 0) | (si == cur) | (si == cur - 1)
            score = jnp.where(s_valid, imp + jnp.where(forced, FORCE_BONUS, 0.0), NEG)
            scores.append(score)
        score = jnp.concatenate(scores, axis=0)
        si4 = lax.broadcasted_iota(jnp.int32, score.shape, 1)
        rank = jnp.zeros(score.shape, jnp.int32)
        for sp in range(ns):
            other = score[:, sp:sp + 1]
            beats = (other > score) | ((other == score) & (si4 > sp))
            rank = rank + beats.astype(jnp.int32)
        bias_all = jnp.where((rank < min(N_SEL, ns)) & (si4 < ns), 0.0, NEG)
        for g in range(N_KV):
            bias_ref[g] = bias_all[g * Q_PAD:(g + 1) * Q_PAD].astype(BF16)

        kw_t = win_ref[0:KV_W, :].astype(BF16)
        vw_t = win_ref[KV_W:2 * KV_W, :].astype(BF16)
        kpos = past - wbuf + lax.broadcasted_iota(jnp.int32, (1, wbuf), 1)
        s1 = _dot(qbd, kw_t) - slope * (qposf - kpos.astype(F32))
        s1 = jnp.where((kpos <= qpos) & (qpos - kpos < WINDOW) & (kpos >= 0), s1, NEG)
        kn = kwn_ref[:, 0:KV_W].astype(BF16)
        vn = kwn_ref[:, KV_W:2 * KV_W].astype(BF16)
        li = lax.broadcasted_iota(jnp.int32, (1, LANES), 1)
        kposn = past + li
        s2 = _dot_nt(qbd, kn) - slope * (qposf - kposn.astype(F32))
        s2 = jnp.where((li < n_new) & (kposn <= qpos) & (qpos - kposn < WINDOW), s2, NEG)
        m = jnp.maximum(jnp.max(s1, axis=1, keepdims=True), jnp.max(s2, axis=1, keepdims=True))
        p1 = jnp.exp(s1 - m)
        p2 = jnp.exp(s2 - m)
        den = jnp.sum(p1, axis=1, keepdims=True) + jnp.sum(p2, axis=1, keepdims=True)
        inv = 1.0 / den
        ow_ref[...] = (_dot_nt((p1 * inv).astype(BF16), vw_t) + _dot((p2 * inv).astype(BF16), vn))
        m_ref[...] = jnp.full(m_ref.shape, NEG, F32)
        acc_ref[...] = jnp.zeros(acc_ref.shape, F32)

    def block_bias(emat):
        rows = []
        for g in range(N_KV):
            bt = _dot(bias_ref[g], emat)
            rows += [bt] * GROUP
        return jnp.concatenate(rows, axis=0)

    def update(scores, pv_fns):
        m_old = m_ref[...]
        m_new = functools.reduce(jnp.maximum, [m_old] + [jnp.max(s, axis=1, keepdims=True) for s in scores])
        alpha = jnp.exp(m_old - m_new)
        ps = [jnp.exp(s - m_new) for s in scores]
        ones = functools.reduce(jnp.add, [jnp.sum(p, axis=1, keepdims=True) for p in ps])
        pv = functools.reduce(jnp.add, [f(p.astype(BF16)) for f, p in zip(pv_fns, ps)])
        acc_ref[:, 0:KV_W] = alpha * acc_ref[:, 0:KV_W] + pv
        acc_ref[:, KV_W:KV_W + LANES] = alpha * acc_ref[:, KV_W:KV_W + LANES] + ones
        m_ref[...] = m_new

    bias_chunk = block_bias(e_ref[...])
    li = lax.broadcasted_iota(jnp.int32, (1, PAGE), 1)
    scores, pv_fns = [], []
    for k in range(pps):
        pg = pages[k]
        kpos = (t * pps + k) * PAGE + li
        s = _dot(qbd, pg[0:KV_W, :].astype(BF16)) - slope * (qposf - kpos.astype(F32))
        scores.append(s + bias_chunk[:, k * PAGE:(k + 1) * PAGE])
        pv_fns.append(lambda p, pg=pg: _dot_nt(p, pg[KV_W:2 * KV_W, :].astype(BF16)))
    update(scores, pv_fns)

    @pl.when(t == nchunks - 1)
    def _():
        nsp = ms_ref.shape[1]
        kposn = past + li
        e_new = (lax.broadcasted_iota(jnp.int32, (nsp, PAGE), 0) == past // SEL_BLOCK).astype(BF16)
        s = _dot_nt(qbd, ksn_ref[:, 0:KV_W].astype(BF16)) - slope * (qposf - kposn.astype(F32)) + block_bias(e_new)
        s = jnp.where((li < n_new) & (kposn <= qpos), s, NEG)
        update([s], [lambda p: _dot(p, ksn_ref[:, KV_W:2 * KV_W].astype(BF16))])
        o_s = acc_ref[:, 0:KV_W] * (1.0 / acc_ref[:, KV_W:KV_W + 1])
        gts = g_ref[...]
        osw_ref[...] = gts[:, 1:2] * o_s + gts[:, 2:3] * ow_ref[...]
        oc_ref[...] = gts[:, 0:1] * oc_ref[...]


def _attn_sample(page_table, pages, qa, qbd, gates, slope_rows, kc, vc, ms, emat, win, ksn, kwn, *, pps, n_new):
    nb, npages = page_table.shape
    nchunks = npages // pps
    past = npages * PAGE
    wbuf = win.shape[2]
    nc = kc.shape[2]
    nsp = ms.shape[1]

    def page_spec(k):
        return pl.BlockSpec((None, 2 * KV_W, PAGE), lambda b, t, pt: (pt[b, t * pps + k], 0, 0))

    per_b = lambda *shape: pl.BlockSpec((None,) + shape, lambda b, t, pt: (b,) + (0,) * len(shape))
    cs = lambda *shape: pl.BlockSpec(shape, lambda b, t, pt: (0,) * len(shape))
    in_specs = [page_spec(k) for k in range(pps)] + [
        per_b(N_KV, ROWS_G, AUG_W), per_b(ROWS_S, KV_W), per_b(ROWS_S, 3), cs(ROWS_S, 1),
        per_b(N_KV, nc, AUG_W), per_b(N_KV, nc, LANES), cs(nc, nsp),
        pl.BlockSpec((nsp, pps * PAGE), lambda b, t, pt: (0, t)),
        per_b(2 * KV_W, wbuf), per_b(LANES, 2 * KV_W), per_b(LANES, 2 * KV_W)]
    out_specs = [per_b(ROWS_S, KV_W), per_b(ROWS_S, LANES)]
    out_shape = [jax.ShapeDtypeStruct((nb, ROWS_S, KV_W), F32), jax.ShapeDtypeStruct((nb, ROWS_S, LANES), F32)]
    scratch = [pltpu.VMEM((N_KV, Q_PAD, nsp), BF16), pltpu.VMEM((ROWS_S, 1), F32),
               pltpu.VMEM((ROWS_S, KV_W + LANES), F32), pltpu.VMEM((ROWS_S, KV_W), F32)]
    gs = pltpu.PrefetchScalarGridSpec(num_scalar_prefetch=1, grid=(nb, nchunks), in_specs=in_specs,
                                      out_specs=out_specs, scratch_shapes=scratch)
    return pl.pallas_call(functools.partial(_attn_sample_kernel, pps=pps, past=past, wbuf=wbuf, n_new=n_new), grid_spec=gs,
                          out_shape=out_shape, compiler_params=_params(2), name="attn_sample")(
        page_table, *([pages] * pps), qa, qbd, gates, slope_rows, kc, vc, ms, emat, win, ksn, kwn)


def _outproj_kernel(x_ref, a_ref, o_ref, nw_ref, sc_ref, sh_ref, g1_ref, wm_ref, woc_ref, won_ref, wo_ref, x1_ref):
    x = x_ref[...]
    h = _mod_norm(x, nw_ref[...], sc_ref[...], sh_ref[...]).astype(BF16)
    mg = jax.nn.sigmoid(_dot(h, wm_ref[...]))
    y_a = _dot(a_ref[...], woc_ref[...])
    y_b = _dot(o_ref[...], won_ref[...])
    mix = (mg[:, 0:D_MODEL] * y_a + mg[:, D_MODEL:2 * D_MODEL] * y_b).astype(BF16)
    x1_ref[...] = x + g1_ref[...] * _dot(mix, wo_ref[...])


def _outproj(x2d, a, o, mod, nw, w_merge, w_oc, w_on, w_o, *, tm):
    n = x2d.shape[0]
    tok = pl.BlockSpec((tm, D_MODEL), lambda i: (i, 0))
    sq = _const_spec((D_MODEL, D_MODEL))
    return pl.pallas_call(
        _outproj_kernel, grid=(n // tm,),
        in_specs=[tok, tok, tok, _const_spec((1, D_MODEL)), mod.spec(1), mod.spec(0), mod.spec(2),
                  _const_spec((D_MODEL, 2 * D_MODEL)), sq, sq, sq],
        out_specs=tok, out_shape=jax.ShapeDtypeStruct((n, D_MODEL), F32),
        compiler_params=_params(1), name="outproj",
    )(x2d, a, o, nw, mod.rows, mod.rows, mod.rows, w_merge, w_oc, w_on, w_o)


FF_CHUNK = D_FF // 2


def _ffn_kernel(x_ref, nw_ref, sc_ref, sh_ref, g2_ref, nf_ref, wg_ref, wu_ref, wd_ref, y_ref):
    x = x_ref[...]
    h = _mod_norm(x, nw_ref[...], sc_ref[...], sh_ref[...]).astype(BF16)
    acc = jnp.zeros(x.shape, F32)
    for c in range(D_FF // FF_CHUNK):
        sl = slice(c * FF_CHUNK, (c + 1) * FF_CHUNK)
        gate = _dot(h, wg_ref[:, sl])
        up = _dot(h, wu_ref[:, sl])
        act = (gate * jax.nn.sigmoid(gate) * up).astype(BF16)
        acc = acc + _dot(act, wd_ref[sl, :])
    x2 = x + g2_ref[...] * acc
    inv = lax.rsqrt(jnp.mean(x2 * x2, axis=-1, keepdims=True) + EPS)
    y_ref[...] = (x2 * inv) * nf_ref[...]


def _ffn(x1, mod, nw2, nf, w_gate, w_up, w_down, *, tm):
    n = x1.shape[0]
    tok = pl.BlockSpec((tm, D_MODEL), lambda i: (i, 0))
    vec = _const_spec((1, D_MODEL))
    return pl.pallas_call(
        _ffn_kernel, grid=(n // tm,),
        in_specs=[tok, vec, mod.spec(4), mod.spec(3), mod.spec(5), vec,
                  _const_spec((D_MODEL, D_FF)), _const_spec((D_MODEL, D_FF)), _const_spec((D_FF, D_MODEL))],
        out_specs=tok, out_shape=jax.ShapeDtypeStruct((n, D_MODEL), F32),
        compiler_params=_params(1), name="ffn",
    )(x1, nw2, mod.rows, mod.rows, mod.rows, nf, w_gate, w_up, w_down)


def _slopes():
    return 2.0 ** (-8.0 * jnp.arange(1, N_HEADS + 1, dtype=F32) / N_HEADS)


def _slope_lanes(slopes):
    parts = _split3(slopes)
    cols = jnp.stack([parts[0], parts[0], parts[1], parts[1], parts[2], parts[2]], axis=1)
    return jnp.pad(cols, ((0, 0), (0, LANES - 6)))


def _pos_lanes(pos_hi, pos_lo):
    cols = np.stack([pos_hi, pos_lo] * 3, axis=1).astype(np.float32)
    return np.pad(cols, ((0, 0), (0, LANES - 6))).astype(BF16)


def _token_consts(t):
    pos = np.arange(t, dtype=np.int32)
    onehot = (pos[:, None] // SEL_BLOCK == np.arange(HEAD_DIM, dtype=np.int32)[None, :]).astype(np.float32)
    clo = np.concatenate([np.zeros((t, HEAD_DIM), np.float32), onehot], axis=1)
    chi = _pos_lanes((pos // SEL_BLOCK) * SEL_BLOCK, pos % SEL_BLOCK)
    return clo, chi


def _cmp_consts(ncb):
    ci = np.arange(ncb, dtype=np.int32) * CMP_STRIDE
    return _pos_lanes((ci // SEL_BLOCK) * SEL_BLOCK, ci % SEL_BLOCK)


def _ones_lane():
    return (np.arange(LANES) == HEAD_DIM).astype(np.float32).reshape(1, LANES)


def _imp_matrix(nc, ns_pad):
    c = np.arange(nc, dtype=np.int32)[:, None]
    s = np.arange(ns_pad, dtype=np.int32)[None, :]
    per = SEL_BLOCK // CMP_STRIDE
    return ((c // per == s) | ((c % per == per - 1) & (c // per == s - 1))).astype(BF16)


def _prep_weights(w_in, w_phi1, w_phi2, pe_cmp):
    wb = w_in.astype(BF16)
    w_conv_in = wb[:, _C_CONV:_C_QKV]
    w_qkv = jnp.pad(wb[:, _C_QKV:_C_MERGE], ((0, 0), (0, _QKV_COLS_PAD - _QKV_COLS)))
    w_merge = wb[:, _C_MERGE:]
    half = CMP_STRIDE * HEAD_DIM
    w1cat = jnp.concatenate([w_phi1[:, :half], w_phi1[:, half:]], axis=2).astype(BF16)
    w2pad = jnp.pad(w_phi2, ((0, 0), (0, 0), (0, LANES - HEAD_DIM))).astype(BF16)
    pe2 = pe_cmp.reshape(2, 2, half)
    return w_conv_in, w_qkv, w_merge, w1cat, w2pad, pe2


def _prompt_layer(x, mod_p, wts):
    (nw1, nw2, nf, w_conv_in, w_qkv, w_merge, w1cat, w2pad, pe2, w_conv, b_conv, w_oc, w_on, w_o,
     w_gate, w_up, w_down, slopes) = wts
    nb, t, _ = x.shape
    tm = min(512, t)
    x2d = x.reshape(nb * t, D_MODEL)
    mod = _Mod(mod_p, False, t, tm)
    a, tail = _conv_path(x2d, mod, nw1, w_conv_in, w_conv, b_conv, tm=tm, seq_len=t)
    clo, chi = _token_consts(t)
    cv = _ones_lane()
    qt, kvc, kvs, kvw, gts, ksa, vst, kwa, vwt = _qkv_path(x2d, mod, nw1, w_qkv, tm=tm, seq_len=t, consts=(clo, chi))
    npages = t // PAGE
    pt = np.broadcast_to(np.arange(npages, dtype=np.int32), (nb, npages))
    nc = t // CMP_STRIDE
    ns = t // SEL_BLOCK
    nqb = t // Q_BLOCK
    kca, vct = _compress(kvc, pt, pe2, w1cat, w2pad, _cmp_consts(nc), cv,
                         pps=min(16, npages), transpose_v=True, paged=False)
    gates_t = gts[:, :3 * N_HEADS].reshape(nb, nqb, Q_BLOCK, N_KV, GROUP, 3).transpose(0, 3, 1, 5, 4, 2)
    gates_t = gates_t.reshape(nb, N_KV, nqb, 3, GROUP * Q_BLOCK)
    qc_t = _slope_lanes(slopes)[:, :QC_ROWS].reshape(N_KV, GROUP, QC_ROWS).transpose(0, 2, 1)
    qc_t = jnp.repeat(qc_t, Q_BLOCK, axis=2)
    mt = _imp_matrix(nc, ns).T
    o = _attn_prompt_t(qt, gates_t, qc_t, mt, kca, vct, ksa, vst, kwa, vwt)
    x1 = _outproj(x2d, a, o.reshape(nb * t, D_MODEL), mod, nw1, w_merge, w_oc, w_on, w_o, tm=tm)
    y = _ffn(x1, mod, nw2, nf, w_gate, w_up, w_down, tm=tm)
    keep = min(WINDOW, t)
    rows_of = lambda a: a.reshape(nb, 2, N_KV, HEAD_DIM, t).transpose(0, 4, 1, 2, 3)
    state = (rows_of(kvc), rows_of(kvs),
             kvw.reshape(nb, t, 2 * KV_W)[:, t - keep:].reshape(nb, keep, 2, N_KV, HEAD_DIM),
             tail[:, 8 - 2:, :])
    return y.reshape(nb, t, D_MODEL), state


def _sample_layer(x, mod_s, wts, cache_cmp, cache_sel, cache_win, state_conv, page_table):
    (nw1, nw2, nf, w_conv_in, w_qkv, w_merge, w1cat, w2pad, pe2, w_conv, b_conv, w_oc, w_on, w_o,
     w_gate, w_up, w_down, slopes) = wts
    nb, s, _ = x.shape
    n = nb * s
    x2d = x.reshape(n, D_MODEL)
    mod = _Mod(mod_s, True, s, n)
    tpos = jnp.arange(s)
    p1 = jnp.broadcast_to(state_conv[:, 1:2, :], (nb, s, D_MODEL)).reshape(n, D_MODEL)
    p2 = state_conv[:, jnp.minimum(tpos, 1), :].reshape(n, D_MODEL)
    a, u = _conv_path(x2d, mod, nw1, w_conv_in, w_conv, b_conv, tm=n, seq_len=s, prev=(p1, p2))
    qpad, kvc, kvs, kvw, gts = _qkv_path(x2d, mod, nw1, w_qkv, tm=n, seq_len=s)

    npages = page_table.shape[1]
    past = npages * PAGE
    cv = _ones_lane()
    nc = past // CMP_STRIDE
    token_minor = lambda c: c.transpose(0, 2, 3, 4, 1).reshape(c.shape[0], 2 * KV_W, c.shape[1])
    kca, vca = _compress(token_minor(cache_cmp), page_table, pe2, w1cat, w2pad, _cmp_consts(nc), cv,
                         pps=min(16, npages), transpose_v=False, paged=True)

    qh = qpad.reshape(nb, s, N_KV, GROUP, LANES).transpose(0, 2, 3, 1, 4)
    qh = jnp.pad(qh, ((0, 0), (0, 0), (0, 0), (0, Q_PAD - s), (0, 0)))
    sl = jnp.broadcast_to(_slope_lanes(slopes).reshape(1, N_KV, GROUP, 1, LANES), qh.shape)
    qa = jnp.concatenate([qh, sl], axis=-1).reshape(nb, N_KV, ROWS_G, AUG_W)
    eye = jnp.eye(N_KV, dtype=BF16)
    qbd = (qh[..., None, :HEAD_DIM] * eye[None, :, None, None, :, None]).reshape(nb, ROWS_S, KV_W)
    gates = gts[:, :3 * N_HEADS].reshape(nb, s, N_KV, GROUP, 3).transpose(0, 2, 3, 1, 4)
    gates = jnp.pad(gates, ((0, 0), (0, 0), (0, 0), (0, Q_PAD - s), (0, 0))).reshape(nb, ROWS_S, 3)
    slope_rows = jnp.repeat(slopes, Q_PAD).reshape(ROWS_S, 1)
    ns = past // SEL_BLOCK + 1
    nsp = -(-ns // LANES) * LANES
    ms = _imp_matrix(nc, nsp)
    tok = np.arange(past, dtype=np.int32)
    emat = (np.arange(nsp, dtype=np.int32)[:, None] == tok[None, :] // SEL_BLOCK).astype(BF16)
    pad_rows = lambda r: jnp.pad(r.reshape(nb, s, 2 * KV_W), ((0, 0), (0, LANES - s), (0, 0)))
    osw, ocg = _attn_sample(page_table, token_minor(cache_sel), qa, qbd, gates, slope_rows, kca, vca,
                            ms, emat, token_minor(cache_win), pad_rows(kvs), pad_rows(kvw),
                            pps=min(16, npages), n_new=s)
    osw = osw.reshape(nb, N_KV, GROUP, Q_PAD, N_KV, HEAD_DIM)
    o_sw = jnp.einsum('bgrqgd->bqgrd', osw)
    o_c = ocg.reshape(nb, N_KV, GROUP, Q_PAD, LANES)[..., :HEAD_DIM].transpose(0, 3, 1, 2, 4)
    o = (o_sw + o_c)[:, :s].reshape(n, N_HEADS * HEAD_DIM).astype(BF16)

    x1 = _outproj(x2d, a, o, mod, nw1, w_merge, w_oc, w_on, w_o, tm=n)
    y = _ffn(x1, mod, nw2, nf, w_gate, w_up, w_down, tm=n)
    kv5 = lambda r: r.reshape(nb, s, 2, N_KV, HEAD_DIM)
    win = jnp.concatenate([cache_win, kv5(kvw)], axis=1)[:, s:]
    state = (kv5(kvc), kv5(kvs), win, u.reshape(nb, s, D_MODEL)[:, s - 2:])
    return y.reshape(nb, s, D_MODEL), state


def kernel(x_prompt, x_sample, c_prompt, c_sample, cache_cmp, cache_sel, cache_win, state_conv, page_table,
           w_ada, b_ada, norm1, w_in, w_conv, b_conv, w_out_conv, pe_cmp, w_phi1, w_phi2, w_o_nsa, w_out,
           norm2, w_gate, w_up, w_down, norm_f):
    depth = w_ada.shape[0]
    assert depth == 1, "single-layer trunk"
    nbp, nbs = c_prompt.shape[0], c_sample.shape[0]
    slopes = _slopes()
    l = 0
    c_all = jnp.concatenate([c_prompt, c_sample], axis=0)
    c_all = jnp.pad(c_all, ((0, -c_all.shape[0] % 8), (0, 0)))
    mod = _ada(c_all, w_ada[l], b_ada[l])
    w_conv_in, w_qkv, w_merge, w1cat, w2pad, pe2 = _prep_weights(w_in[l], w_phi1[l], w_phi2[l], pe_cmp[l])
    row = lambda v: v.reshape(1, -1)
    wts = (row(norm1[l]), row(norm2[l]), row(norm_f), w_conv_in, w_qkv, w_merge, w1cat, w2pad, pe2,
           w_conv[l], row(b_conv[l]), w_out_conv[l].astype(BF16), w_o_nsa[l].astype(BF16), w_out[l].astype(BF16),
           w_gate[l].astype(BF16), w_up[l].astype(BF16), w_down[l].astype(BF16), slopes)
    yp, st_p = _prompt_layer(x_prompt, mod[:nbp], wts)
    ys, st_s = _sample_layer(x_sample, mod[nbp:nbp + nbs], wts, cache_cmp[l], cache_sel[l], cache_win[l],
                             state_conv[l], page_table)
    return (yp, ys, st_p[0][None], st_p[1][None], st_p[2][None], st_p[3][None],
            st_s[0][None], st_s[1][None], st_s[2][None], st_s[3][None])
```

```python
import functools

import jax
import jax.numpy as jnp
import numpy as np
from jax import lax
from jax.experimental import pallas as pl
from jax.experimental.pallas import tpu as pltpu

F32 = jnp.float32
BF16 = jnp.bfloat16

D_MODEL = 1024
N_HEADS = 16
HEAD_DIM = 64
N_KV = 4
GROUP = N_HEADS // N_KV
KV_W = N_KV * HEAD_DIM
CMP_BLOCK = 32
CMP_STRIDE = 16
SEL_BLOCK = 64
N_SEL = 16
WINDOW = 512
D_PHI = 2 * HEAD_DIM
Q_BLOCK = 256
PAGE = 128
D_FF = ((8 * D_MODEL // 3 + 255) // 256) * 256
EPS = 1e-6
NEG = -1e30
FORCE_BONUS = 1e3

LANES = 128
AUG_W = 2 * LANES
HALVES_PER_PAGE = PAGE // CMP_STRIDE
ROW_TILES = 2 * KV_W // LANES
VMEM_LIMIT = 56 * 1024 * 1024
KEY_TILE = 2 * LANES
VT_ROWS = HEAD_DIM + 16
QC_ROWS = 16
SEL_TILES_PER_ITER = 8

_C_CONV = 0
_C_QKV = 3 * D_MODEL
_C_GATE = _C_QKV + N_HEADS * HEAD_DIM + 6 * KV_W
_C_MERGE = _C_GATE + 3 * N_HEADS
_QKV_COLS = _C_MERGE - _C_QKV
_QKV_COLS_PAD = -(-_QKV_COLS // LANES) * LANES


def _dot(a, b):
    return jnp.dot(a, b, preferred_element_type=F32)


def _dot_nt(a, b):
    return lax.dot_general(a, b, (((1,), (1,)), ((), ())), preferred_element_type=F32)


def _params(n_axes):
    return pltpu.CompilerParams(dimension_semantics=("arbitrary",) * n_axes, vmem_limit_bytes=VMEM_LIMIT)


def _const_spec(shape):
    return pl.BlockSpec(shape, lambda *_: (0,) * len(shape))


def _mod_norm(x, nw, sc, sh):
    inv = lax.rsqrt(jnp.mean(x * x, axis=-1, keepdims=True) + EPS)
    return (x * inv) * nw * (1.0 + sc) + sh


def _split3(x):
    a = x.astype(BF16)
    r = x - a.astype(F32)
    b = r.astype(BF16)
    c = (r - b.astype(F32)).astype(BF16)
    return a, b, c


def _ada_kernel(c_ref, w_ref, b_ref, o_ref):
    c = c_ref[...]
    s = c * jax.nn.sigmoid(c)
    o_ref[...] = jnp.dot(s, w_ref[...], preferred_element_type=F32, precision=lax.Precision.HIGHEST) + b_ref[...]


def _ada(c, w_ada, b_ada):
    n = c.shape[0]
    tn = 1536
    return pl.pallas_call(
        _ada_kernel,
        grid=(6 * D_MODEL // tn,),
        in_specs=[_const_spec((n, D_MODEL)),
                  pl.BlockSpec((D_MODEL, tn), lambda j: (0, j)),
                  pl.BlockSpec((1, tn), lambda j: (0, j))],
        out_specs=pl.BlockSpec((n, tn), lambda j: (0, j)),
        out_shape=jax.ShapeDtypeStruct((n, 6 * D_MODEL), F32),
        compiler_params=_params(1),
        name="ada",
    )(c, w_ada, b_ada.reshape(1, -1))


class _Mod:
    def __init__(self, mod, per_token, seq_len, tm):
        self.per_token = per_token
        if per_token:
            self.rows = jnp.repeat(mod, seq_len, axis=0)
        else:
            self.rows = mod.reshape(mod.shape[0], 1, 6 * D_MODEL)
        self.tiles_per_seq = None if per_token else seq_len // tm
        self.tm = tm

    def spec(self, k):
        if self.per_token:
            return pl.BlockSpec((self.tm, D_MODEL), lambda i: (i, k))
        tps = self.tiles_per_seq
        return pl.BlockSpec((None, 1, D_MODEL), lambda i: (i // tps, 0, k))


def _conv_kernel(*refs, carry_rows, seq_len):
    if carry_rows:
        (x_ref, nw_ref, sc_ref, sh_ref, w_ref, wc_ref, bc_ref, a_ref, tail_ref, carry_ref) = refs
    else:
        (x_ref, nw_ref, sc_ref, sh_ref, w_ref, wc_ref, bc_ref, p1_ref, p2_ref, a_ref, tail_ref) = refs
    tm = x_ref.shape[0]
    h = _mod_norm(x_ref[...], nw_ref[...], sc_ref[...], sh_ref[...]).astype(BF16)
    z = _dot(h, w_ref[...])
    bg = z[:, 0:D_MODEL]
    u = z[:, D_MODEL:2 * D_MODEL] * z[:, 2 * D_MODEL:3 * D_MODEL]
    row = lax.broadcasted_iota(jnp.int32, (tm, 1), 0)
    u1 = pltpu.roll(u, 1, 0)
    u2 = pltpu.roll(u, 2, 0)
    if carry_rows:
        @pl.when(pl.program_id(0) % carry_rows == 0)
        def _():
            carry_ref[...] = jnp.zeros_like(carry_ref)
        c0 = carry_ref[0:1, :]
        c1 = carry_ref[1:2, :]
        u1 = jnp.where(row == 0, c1, u1)
        u2 = jnp.where(row == 0, c0, jnp.where(row == 1, c1, u2))
        carry_ref[0:2, :] = u[tm - 2:tm, :]
        tail_ref[...] = u[tm - 8:tm, :]
    else:
        pos = lax.rem(row, seq_len)
        u1 = jnp.where(pos >= 1, u1, p1_ref[...])
        u2 = jnp.where(pos >= 2, u2, p2_ref[...])
        tail_ref[...] = u
    v = bc_ref[...] + wc_ref[0:1, :] * u2 + wc_ref[1:2, :] * u1 + wc_ref[2:3, :] * u
    a_ref[...] = (bg * v).astype(BF16)


def _conv_path(x2d, mod, nw, w_conv_in, w_conv, b_conv, *, tm, seq_len, prev=None):
    n = x2d.shape[0]
    tok = pl.BlockSpec((tm, D_MODEL), lambda i: (i, 0))
    in_specs = [tok, _const_spec((1, D_MODEL)), mod.spec(1), mod.spec(0),
                _const_spec((D_MODEL, 3 * D_MODEL)), _const_spec((3, D_MODEL)), _const_spec((1, D_MODEL))]
    args = [x2d, nw, mod.rows, mod.rows, w_conv_in, w_conv, b_conv]
    if prev is None:
        tps = seq_len // tm
        out_specs = [tok, pl.BlockSpec((None, 8, D_MODEL), lambda i: (i // tps, 0, 0))]
        out_shape = [jax.ShapeDtypeStruct((n, D_MODEL), BF16), jax.ShapeDtypeStruct((n // seq_len, 8, D_MODEL), F32)]
        scratch = [pltpu.VMEM((8, D_MODEL), F32)]
        kern = functools.partial(_conv_kernel, carry_rows=tps, seq_len=seq_len)
    else:
        in_specs += [tok, tok]
        args += list(prev)
        out_specs = [tok, tok]
        out_shape = [jax.ShapeDtypeStruct((n, D_MODEL), BF16), jax.ShapeDtypeStruct((n, D_MODEL), F32)]
        scratch = []
        kern = functools.partial(_conv_kernel, carry_rows=0, seq_len=seq_len)
    return pl.pallas_call(kern, grid=(n // tm,), in_specs=in_specs, out_specs=out_specs, out_shape=out_shape,
                          scratch_shapes=scratch, compiler_params=_params(1), name="conv_path")(*args)


def _qkv_kernel(*refs, aug):
    if aug:
        (x_ref, nw_ref, sc_ref, sh_ref, w_ref, clo_ref, chi_ref,
         q_ref, kvc_ref, kvs_ref, kvw_ref, g_ref, ksa_ref, vsa_ref, kwa_ref, vwa_ref) = refs
    else:
        (x_ref, nw_ref, sc_ref, sh_ref, w_ref, q_ref, kvc_ref, kvs_ref, kvw_ref, g_ref) = refs
    tm = x_ref.shape[0]
    h = _mod_norm(x_ref[...], nw_ref[...], sc_ref[...], sh_ref[...]).astype(BF16)
    z = _dot(h, w_ref[...])
    low = lax.broadcasted_iota(jnp.int32, (tm, LANES), 1) < HEAD_DIM
    nq = N_HEADS * HEAD_DIM
    for c in range(N_HEADS // 2):
        t = z[:, c * LANES:(c + 1) * LANES] * (HEAD_DIM ** -0.5)
        if aug:
            tt = t.T.astype(BF16)
            q_ref[2 * c] = tt[0:HEAD_DIM]
            q_ref[2 * c + 1] = tt[HEAD_DIM:2 * HEAD_DIM]
        else:
            q_ref[:, (2 * c) * LANES:(2 * c + 1) * LANES] = jnp.where(low, t, 0.0).astype(BF16)
            q_ref[:, (2 * c + 1) * LANES:(2 * c + 2) * LANES] = jnp.where(low, pltpu.roll(t, HEAD_DIM, 1), 0.0).astype(BF16)
    if aug:
        kvc_ref[...] = z[:, nq:nq + 2 * KV_W].T
        kvs_ref[...] = z[:, nq + 2 * KV_W:nq + 4 * KV_W].T
    else:
        kvc_ref[...] = z[:, nq:nq + 2 * KV_W]
        kvs_ref[...] = z[:, nq + 2 * KV_W:nq + 4 * KV_W]
    kvw_ref[...] = z[:, nq + 4 * KV_W:nq + 6 * KV_W]
    g_ref[...] = jax.nn.sigmoid(z[:, nq + 6 * KV_W:nq + 6 * KV_W + LANES])
    if aug:
        chi = chi_ref[...]
        ones_row = (lax.broadcasted_iota(jnp.int32, (VT_ROWS - HEAD_DIM, KEY_TILE), 0) == 0).astype(BF16)
        for br, (ka_ref, va_ref) in enumerate(((ksa_ref, vsa_ref), (kwa_ref, vwa_ref))):
            kbase = nq + 2 * KV_W * (br + 1)
            clo = clo_ref[...] if br == 0 else 0.0
            for g in range(N_KV):
                kt = z[:, kbase + (g // 2) * LANES:kbase + (g // 2 + 1) * LANES]
                if g % 2:
                    kt = pltpu.roll(kt, HEAD_DIM, 1)
                ka_ref[g, :, 0:LANES] = jnp.where(low, kt, clo).astype(BF16)
                ka_ref[g, :, LANES:AUG_W] = chi
            for c in range(N_KV // 2):
                vt = z[:, kbase + KV_W + c * LANES:kbase + KV_W + (c + 1) * LANES].T.astype(BF16)
                for gg in range(2):
                    for j in range(tm // KEY_TILE):
                        va_ref[2 * c + gg, j, 0:HEAD_DIM, :] = vt[gg * HEAD_DIM:(gg + 1) * HEAD_DIM, j * KEY_TILE:(j + 1) * KEY_TILE]
                        va_ref[2 * c + gg, j, HEAD_DIM:VT_ROWS, :] = ones_row


def _qkv_path(x2d, mod, nw, w_qkv, *, tm, seq_len, consts=None):
    n = x2d.shape[0]
    aug = consts is not None
    tok = lambda w: pl.BlockSpec((tm, w), lambda i: (i, 0))
    in_specs = [tok(D_MODEL), _const_spec((1, D_MODEL)), mod.spec(1), mod.spec(0),
                _const_spec((D_MODEL, _QKV_COLS_PAD))]
    args = [x2d, nw, mod.rows, mod.rows, w_qkv]
    out_specs = [tok(N_HEADS * LANES), tok(2 * KV_W), tok(2 * KV_W), tok(2 * KV_W), tok(LANES)]
    out_shape = [jax.ShapeDtypeStruct((n, N_HEADS * LANES), BF16)] + \
                [jax.ShapeDtypeStruct((n, 2 * KV_W), F32)] * 3 + [jax.ShapeDtypeStruct((n, LANES), F32)]
    if aug:
        tps = seq_len // tm
        nb = n // seq_len
        pos = lambda w: pl.BlockSpec((tm, w), lambda i: (i % tps, 0))
        in_specs += [pos(LANES), pos(LANES)]
        args += list(consts)
        out_specs[0] = pl.BlockSpec((None, N_HEADS, HEAD_DIM, tm), lambda i: (i // tps, 0, 0, i % tps))
        out_shape[0] = jax.ShapeDtypeStruct((nb, N_HEADS, HEAD_DIM, seq_len), BF16)
        for k in (1, 2):
            out_specs[k] = pl.BlockSpec((None, 2 * KV_W, tm), lambda i: (i // tps, 0, i % tps))
            out_shape[k] = jax.ShapeDtypeStruct((nb, 2 * KV_W, seq_len), F32)
        ka = pl.BlockSpec((None, N_KV, tm, AUG_W), lambda i: (i // tps, 0, i % tps, 0))
        va = pl.BlockSpec((None, N_KV, tm // KEY_TILE, VT_ROWS, KEY_TILE), lambda i: (i // tps, 0, i % tps, 0, 0))
        out_specs += [ka, va, ka, va]
        ka_s = jax.ShapeDtypeStruct((nb, N_KV, seq_len, AUG_W), BF16)
        va_s = jax.ShapeDtypeStruct((nb, N_KV, seq_len // KEY_TILE, VT_ROWS, KEY_TILE), BF16)
        out_shape += [ka_s, va_s, ka_s, va_s]
    return pl.pallas_call(functools.partial(_qkv_kernel, aug=aug), grid=(n // tm,), in_specs=in_specs,
                          out_specs=out_specs, out_shape=out_shape, compiler_params=_params(1), name="qkv_path")(*args)


def _block_major_loader(ref, k, c):
    def load(p, nrows):
        return ref[k, c, p * HALVES_PER_PAGE:p * HALVES_PER_PAGE + nrows, :]
    return load


def _gather_pair(load, nrows):
    low = lax.broadcasted_iota(jnp.int32, (nrows, LANES), 1) < HEAD_DIM
    even, odd = [], []
    for qq in range(CMP_STRIDE // 2):
        a = load(2 * qq, nrows)
        b = load(2 * qq + 1, nrows)
        even.append(jnp.where(low, a, pltpu.roll(b, HEAD_DIM, 1)))
        odd.append(jnp.where(low, pltpu.roll(a, HEAD_DIM, 1), b))
    return jnp.concatenate(even, axis=1), jnp.concatenate(odd, axis=1)


def _compress_kernel(pt_ref, *refs, pps, transpose_v):
    pages = refs[:pps]
    nxt_ref, pe_ref, w1_ref, w2_ref, chi_ref, cv_ref, perm_ref, kc_ref, vc_ref, xt_ref = refs[pps:]
    perm = perm_ref[...]
    for k, pg in enumerate(list(pages) + [nxt_ref]):
        for c in range(ROW_TILES):
            xt_ref[k, c] = _dot_nt(perm, pg[c * LANES:(c + 1) * LANES, :].astype(BF16))
    loaders = lambda c: [_block_major_loader(xt_ref, k, c) for k in range(pps)]
    look_loader = lambda c: _block_major_loader(xt_ref, pps, c)
    t = pl.program_id(1)
    last = t == pl.num_programs(1) - 1
    nhb = pps * HALVES_PER_PAGE
    n = N_KV * nhb
    row = lax.broadcasted_iota(jnp.int32, (n, 1), 0)
    chi = chi_ref[...]
    for kv in range(2):
        by_group, look = [], []
        for cc in range(N_KV // 2):
            c = kv * (N_KV // 2) + cc
            pairs = [_gather_pair(ld, HALVES_PER_PAGE) for ld in loaders(c)]
            by_group += [[ev for ev, _ in pairs], [od for _, od in pairs]]
            look += list(_gather_pair(look_loader(c), 1))
        parts = [x for group in by_group for x in group]
        extra = jnp.concatenate([pe_ref[kv], jnp.zeros((2, CMP_STRIDE * HEAD_DIM), F32)] + look, axis=0)
        xmat = jnp.concatenate(parts + [extra], axis=0).astype(BF16)
        hab = _dot(xmat, w1_ref[kv])
        ha = hab[0:n, 0:D_PHI]
        hb = hab[0:n, D_PHI:2 * D_PHI]
        pbias = hab[n:n + 1, 0:D_PHI] + hab[n + 1:n + 2, D_PHI:2 * D_PHI]
        hbn = pltpu.roll(hb, n - 1, 0)
        for g in range(N_KV):
            la = jnp.where(last, 0.0, hab[n + 4 + g:n + 5 + g, D_PHI:2 * D_PHI])
            hbn = jnp.where(row == g * nhb + nhb - 1, la, hbn)
        act = jax.nn.gelu(ha + hbn + pbias).astype(BF16)
        out = _dot(act, w2_ref[kv])
        for g in range(N_KV):
            blk = out[g * nhb:(g + 1) * nhb, :]
            if kv == 0:
                kc_ref[g, :, 0:LANES] = blk.astype(BF16)
                kc_ref[g, :, LANES:AUG_W] = chi
            elif transpose_v:
                vc_ref[g, :, :] = (blk + cv_ref[...]).T.astype(BF16)
            else:
                vc_ref[g, :, :] = (blk + cv_ref[...]).astype(BF16)


def _compress(pages, page_table, pe2, w1cat, w2pad, chi_c, cv, *, pps, transpose_v, paged):
    nb, npages = page_table.shape
    nchunks = npages // pps
    nhb = pps * HALVES_PER_PAGE
    ncb = npages * HALVES_PER_PAGE
    where = (lambda b, j: (j, 0, 0)) if paged else (lambda b, j: (b, 0, j))

    def page_spec(k):
        return pl.BlockSpec((None, 2 * KV_W, PAGE), lambda b, t, pt: where(b, pt[b, t * pps + k]))

    nxt_spec = pl.BlockSpec((None, 2 * KV_W, PAGE),
                            lambda b, t, pt: where(b, pt[b, jnp.minimum((t + 1) * pps, npages - 1)]))
    scratch = [pltpu.VMEM((pps + 1, ROW_TILES, PAGE, LANES), F32)]
    r = np.arange(PAGE, dtype=np.int32)
    perm = (r[None, :] == (CMP_STRIDE * (r % HALVES_PER_PAGE) + r // HALVES_PER_PAGE)[:, None]).astype(BF16)
    cs = lambda shape: pl.BlockSpec(shape, lambda b, t, pt: (0,) * len(shape))
    in_specs = [page_spec(k) for k in range(pps)] + [
        nxt_spec, cs((2, 2, CMP_STRIDE * HEAD_DIM)), cs((2, CMP_STRIDE * HEAD_DIM, 2 * D_PHI)),
        cs((2, D_PHI, LANES)), pl.BlockSpec((nhb, LANES), lambda b, t, pt: (t, 0)), cs((1, LANES)),
        cs((PAGE, PAGE))]
    if transpose_v:
        v_spec = pl.BlockSpec((None, N_KV, LANES, nhb), lambda b, t, pt: (b, 0, 0, t))
        v_shape = jax.ShapeDtypeStruct((nb, N_KV, LANES, ncb), BF16)
    else:
        v_spec = pl.BlockSpec((None, N_KV, nhb, LANES), lambda b, t, pt: (b, 0, t, 0))
        v_shape = jax.ShapeDtypeStruct((nb, N_KV, ncb, LANES), BF16)
    out_specs = [pl.BlockSpec((None, N_KV, nhb, AUG_W), lambda b, t, pt: (b, 0, t, 0)), v_spec]
    out_shape = [jax.ShapeDtypeStruct((nb, N_KV, ncb, AUG_W), BF16), v_shape]
    gs = pltpu.PrefetchScalarGridSpec(num_scalar_prefetch=1, grid=(nb, nchunks), in_specs=in_specs, out_specs=out_specs,
                                      scratch_shapes=scratch)
    return pl.pallas_call(functools.partial(_compress_kernel, pps=pps, transpose_v=transpose_v),
                          grid_spec=gs, out_shape=out_shape,
                          compiler_params=_params(2), name="compress")(
        page_table, *([pages] * (pps + 1)), pe2, w1cat, w2pad, chi_c, cv, perm)


REMOVED = -3e38


def _top_k_mask(score, index, k, axis):
    work = score
    selected = jnp.zeros(score.shape, jnp.bool_)
    for _ in range(k):
        best = jnp.max(work, axis=axis, keepdims=True)
        first = jnp.min(jnp.where(work == best, index, score.shape[axis]), axis=axis, keepdims=True)
        hit = index == first
        selected = selected | hit
        work = jnp.where(hit, REMOVED, work)
    return selected


def _attn_prompt_t_kernel(q_ref, g_ref, qc_ref, mt_ref, kc_ref, vc_ref, ks_ref, vs_ref, kw_ref, vw_ref, o_ref,
                          qa_ref, qw_ref, m_ref, acc_ref):
    nc = kc_ref.shape[0]
    ns = mt_ref.shape[0]
    kt = vs_ref.shape[2]
    ncol = GROUP * Q_BLOCK
    qb = pl.program_id(2)
    q0 = qb * Q_BLOCK
    qpos = q0 + (lax.broadcasted_iota(jnp.int32, (1, ncol), 1) & (Q_BLOCK - 1))

    for ref in (qw_ref, qa_ref):
        for r in range(GROUP):
            ref[0:HEAD_DIM, r * Q_BLOCK:(r + 1) * Q_BLOCK] = q_ref[r]
        ref[HEAD_DIM:2 * HEAD_DIM, :] = jnp.zeros((HEAD_DIM, ncol), BF16)
        ref[2 * HEAD_DIM:2 * HEAD_DIM + QC_ROWS, :] = qc_ref[...]
        ref[2 * HEAD_DIM + QC_ROWS:AUG_W, :] = jnp.zeros((AUG_W - 2 * HEAD_DIM - QC_ROWS, ncol), BF16)

    s = _dot(kc_ref[...], qw_ref[...])
    ci = lax.broadcasted_iota(jnp.int32, (nc, 1), 0)
    last_valid = lax.shift_right_arithmetic(qpos - (CMP_BLOCK - 1), CMP_STRIDE.bit_length() - 1)
    c_valid = ci <= last_valid
    s = jnp.where(c_valid, s, NEG)
    e = jnp.exp(s - jnp.max(s, axis=0, keepdims=True))
    inv = jnp.where(last_valid >= 0, 1.0 / jnp.sum(e, axis=0, keepdims=True), 0.0)
    p = e * inv
    o_c = _dot(vc_ref[...], p.astype(BF16))[0:HEAD_DIM]

    psum = p[:, 0:Q_BLOCK]
    for r in range(1, GROUP):
        psum = psum + p[:, r * Q_BLOCK:(r + 1) * Q_BLOCK]
    mt = mt_ref[...]
    imp = sum(_dot(mt, part) for part in _split3(psum))
    si = lax.broadcasted_iota(jnp.int32, (ns, Q_BLOCK), 0)
    qpos_t = q0 + lax.broadcasted_iota(jnp.int32, (ns, Q_BLOCK), 1)
    cur = lax.shift_right_logical(qpos_t, 6)
    s_valid = si * SEL_BLOCK <= qpos_t
    forced = (si == 0) | (si == cur) | (si == cur - 1)
    score = jnp.where(s_valid, imp + jnp.where(forced, FORCE_BONUS, 0.0), NEG)
    selected = _top_k_mask(score, si, min(N_SEL, ns), axis=0)
    bias_t = jnp.where(selected, 0.0, NEG).astype(BF16)
    qa_ref[HEAD_DIM:HEAD_DIM + ns, :] = jnp.concatenate([bias_t] * GROUP, axis=1)

    t_hi = lax.div(q0, kt)
    key_iota = lax.broadcasted_iota(jnp.int32, (kt, 1), 0)

    n_back = WINDOW // kt
    s_w, v_w, live_w = [], [], []
    for j in range(n_back + 1):
        tw = t_hi - n_back + j
        tc = jnp.maximum(tw, 0)
        kpos = tw * kt + key_iota
        s = _dot(kw_ref[pl.ds(pl.multiple_of(tc * kt, kt), kt), :], qw_ref[...])
        live = None
        if j == 0:
            oldest = jnp.where(tw >= 0, qpos - WINDOW, jnp.iinfo(jnp.int32).max)
            s = jnp.where(kpos > oldest, s, NEG)
        elif j == n_back:
            s = jnp.where(kpos <= qpos, s, NEG)
        else:
            live = tw >= 0
        s_w.append(s)
        v_w.append(vw_ref[tc])
        live_w.append(live)
    tile_max = [jnp.max(s, axis=0, keepdims=True) for s in s_w]
    tile_max = [mx if live is None else jnp.where(live, mx, NEG) for mx, live in zip(tile_max, live_w)]
    m_w = functools.reduce(jnp.maximum, tile_max)
    shift_w = [m_w if live is None else jnp.where(live, m_w, -NEG) for live in live_w]
    acc_w = sum(_dot(v, jnp.exp(s - sh).astype(BF16)) for s, v, sh in zip(s_w, v_w, shift_w))
    o_w = acc_w[0:HEAD_DIM] * (1.0 / acc_w[HEAD_DIM:HEAD_DIM + 1])

    m_ref[...] = jnp.full(m_ref.shape, NEG, F32)
    acc_ref[...] = jnp.zeros(acc_ref.shape, F32)

    def sel_update(runs, causal_last):
        qa = qa_ref[...]
        ss, vts = [], []
        for ri, (t0, n) in enumerate(runs):
            k0 = pl.multiple_of(t0 * kt, kt)
            s = _dot(ks_ref[pl.ds(k0, n * kt), :], qa)
            for i in range(n):
                si_ = s[i * kt:(i + 1) * kt]
                if causal_last and ri == len(runs) - 1 and i == n - 1:
                    si_ = jnp.where(k0 + i * kt + key_iota <= qpos, si_, NEG)
                ss.append(si_)
                vts.append(vs_ref[t0 + i])
        m_old = m_ref[...]
        m_new = functools.reduce(jnp.maximum, [m_old] + [jnp.max(s, axis=0, keepdims=True) for s in ss])
        pv = sum(_dot(vt, jnp.exp(s - m_new).astype(BF16)) for vt, s in zip(vts, ss))
        acc_ref[...] = jnp.exp(m_old - m_new) * acc_ref[...] + pv
        m_ref[...] = m_new

    per_tile = kt // SEL_BLOCK
    lo_blk = jnp.min(jnp.where(selected & (si >= per_tile), si, ns))
    start = jnp.clip(lax.div(lo_blk, per_tile), 1, jnp.maximum(t_hi, 1))
    n_plain = jnp.maximum(t_hi - start, 0)
    n_group = lax.div(n_plain, SEL_TILES_PER_ITER)

    def body(i, carry):
        sel_update([(start + i * SEL_TILES_PER_ITER, SEL_TILES_PER_ITER)], False)
        return carry

    lax.fori_loop(0, n_group, body, 0)
    rem = n_plain - n_group * SEL_TILES_PER_ITER
    for left in range(SEL_TILES_PER_ITER):
        @pl.when((rem == left) & (t_hi >= 1))
        def _():
            sel_update([(0, 1), (t_hi - left, left + 1)], True)

    @pl.when(t_hi == 0)
    def _():
        sel_update([(0, 1)], True)

    acc = acc_ref[...]
    o_s = acc[0:HEAD_DIM] * (1.0 / acc[HEAD_DIM:HEAD_DIM + 1])

    gts = g_ref[...]
    o = gts[0:1] * o_c + gts[1:2] * o_s + gts[2:3] * o_w
    for c in range(GROUP // 2):
        pair = jnp.concatenate([o[:, (2 * c) * Q_BLOCK:(2 * c + 1) * Q_BLOCK],
                                o[:, (2 * c + 1) * Q_BLOCK:(2 * c + 2) * Q_BLOCK]], axis=0)
        o_ref[:, c * LANES:(c + 1) * LANES] = pair.T.astype(BF16)


def _attn_prompt_t(qt, gates_t, qc_t, mt, kc, vct, ks, vst, kw, vwt):
    nb, t = qt.shape[0], qt.shape[3]
    nc = kc.shape[2]
    ns = mt.shape[0]
    ntile, kt = vst.shape[2], vst.shape[4]
    ncol = GROUP * Q_BLOCK
    per_bg = lambda *shape: pl.BlockSpec((None, None) + shape, lambda b, g, i: (b, g) + (0,) * len(shape))
    in_specs = [pl.BlockSpec((None, GROUP, HEAD_DIM, Q_BLOCK), lambda b, g, i: (b, g, 0, i)),
                pl.BlockSpec((None, None, None, 3, ncol), lambda b, g, i: (b, g, i, 0, 0)),
                pl.BlockSpec((None, QC_ROWS, ncol), lambda b, g, i: (g, 0, 0)),
                pl.BlockSpec((ns, nc), lambda b, g, i: (0, 0)),
                per_bg(nc, AUG_W), per_bg(LANES, nc), per_bg(t, AUG_W), per_bg(ntile, VT_ROWS, kt),
                per_bg(t, AUG_W), per_bg(ntile, VT_ROWS, kt)]
    out_spec = pl.BlockSpec((None, Q_BLOCK, GROUP * HEAD_DIM), lambda b, g, i: (b, i, g))
    scratch = [pltpu.VMEM((AUG_W, ncol), BF16), pltpu.VMEM((AUG_W, ncol), BF16),
               pltpu.VMEM((1, ncol), F32), pltpu.VMEM((VT_ROWS, ncol), F32)]
    return pl.pallas_call(_attn_prompt_t_kernel, grid=(nb, N_KV, t // Q_BLOCK),
                          in_specs=in_specs, out_specs=out_spec,
                          out_shape=jax.ShapeDtypeStruct((nb, t, N_HEADS * HEAD_DIM), BF16),
                          scratch_shapes=scratch, compiler_params=_params(3), name="attn_prompt")(
        qt, gates_t, qc_t, mt, kc, vct, ks, vst, kw, vwt)


Q_PAD = 8
ROWS_G = GROUP * Q_PAD
ROWS_S = N_KV * ROWS_G


def _attn_sample_kernel(pt_ref, *refs, pps, past, wbuf, n_new):
    pages = refs[:pps]
    (qa_ref, qbd_ref, g_ref, slope_ref, kc_ref, vc_ref, ms_ref, e_ref, win_ref, ksn_ref, kwn_ref,
     osw_ref, oc_ref, bias_ref, m_ref, acc_ref, ow_ref) = refs[pps:]
    t = pl.program_id(1)
    nchunks = pl.num_programs(1)
    nc = kc_ref.shape[1]
    ns = past // SEL_BLOCK + 1
    row = lax.broadcasted_iota(jnp.int32, (ROWS_S, 1), 0)
    qpos = past + (row & (Q_PAD - 1))
    qposf = qpos.astype(F32)
    slope = slope_ref[...]
    qbd = qbd_ref[...]

    @pl.when(t == 0)
    def _():
        nsp = ms_ref.shape[1]
        rg = lax.broadcasted_iota(jnp.int32, (ROWS_G, 1), 0)
        qpos_g = past + (rg & (Q_PAD - 1))
        r8 = lax.broadcasted_iota(jnp.int32, (Q_PAD, 1), 0)
        qpos8 = past + r8
        si = lax.broadcasted_iota(jnp.int32, (Q_PAD, nsp), 1)
        scores = []
        for g in range(N_KV):
            s = _dot_nt(qa_ref[g], kc_ref[g])
            ci = lax.broadcasted_iota(jnp.int32, (1, nc), 1)
            c_valid = ci * CMP_STRIDE + (CMP_BLOCK - 1) <= qpos_g
            s = jnp.where(c_valid, s, NEG)
            e = jnp.exp(s - jnp.max(s, axis=1, keepdims=True))
            p = jnp.where(c_valid, e * (1.0 / jnp.sum(e, axis=1, keepdims=True)), 0.0)
            oc_ref[g * ROWS_G:(g + 1) * ROWS_G, :] = _dot(p.astype(BF16), vc_ref[g])
            psum = p[0:Q_PAD]
            for r in range(1, GROUP):
                psum = psum + p[r * Q_PAD:(r + 1) * Q_PAD]
            ms = ms_ref[...]
            imp = sum(_dot(part, ms) for part in _split3(psum))
            cur = lax.shift_right_logical(qpos8, 6)
            s_valid = (si * SEL_BLOCK <= qpos8) & (si < ns)
            forced = (si == 0) | (si == cur) | (si == cur - 1)
            score = jnp.where(s_valid, imp + jnp.where(forced, FORCE_BONUS, 0.0), NEG)
            scores.append(score)
        score = jnp.concatenate(scores, axis=0)
        si4 = lax.broadcasted_iota(jnp.int32, score.shape, 1)
        rank = jnp.zeros(score.shape, jnp.int32)
        for sp in range(ns):
            other = score[:, sp:sp + 1]
            beats = (other > score) | ((other == score) & (si4 > sp))
            rank = rank + beats.astype(jnp.int32)
        bias_all = jnp.where((rank < min(N_SEL, ns)) & (si4 < ns), 0.0, NEG)
        for g in range(N_KV):
            bias_ref[g] = bias_all[g * Q_PAD:(g + 1) * Q_PAD].astype(BF16)

        kw_t = win_ref[0:KV_W, :].astype(BF16)
        vw_t = win_ref[KV_W:2 * KV_W, :].astype(BF16)
        kpos = past - wbuf + lax.broadcasted_iota(jnp.int32, (1, wbuf), 1)
        s1 = _dot(qbd, kw_t) - slope * (qposf - kpos.astype(F32))
        s1 = jnp.where((kpos <= qpos) & (qpos - kpos < WINDOW) & (kpos >= 0), s1, NEG)
        kn = kwn_ref[:, 0:KV_W].astype(BF16)
        vn = kwn_ref[:, KV_W:2 * KV_W].astype(BF16)
        li = lax.broadcasted_iota(jnp.int32, (1, LANES), 1)
        kposn = past + li
        s2 = _dot_nt(qbd, kn) - slope * (qposf - kposn.astype(F32))
        s2 = jnp.where((li < n_new) & (kposn <= qpos) & (qpos - kposn < WINDOW), s2, NEG)
        m = jnp.maximum(jnp.max(s1, axis=1, keepdims=True), jnp.max(s2, axis=1, keepdims=True))
        p1 = jnp.exp(s1 - m)
        p2 = jnp.exp(s2 - m)
        den = jnp.sum(p1, axis=1, keepdims=True) + jnp.sum(p2, axis=1, keepdims=True)
        inv = 1.0 / den
        ow_ref[...] = (_dot_nt((p1 * inv).astype(BF16), vw_t) + _dot((p2 * inv).astype(BF16), vn))
        m_ref[...] = jnp.full(m_ref.shape, NEG, F32)
        acc_ref[...] = jnp.zeros(acc_ref.shape, F32)

    def block_bias(emat):
        rows = []
        for g in range(N_KV):
            bt = _dot(bias_ref[g], emat)
            rows += [bt] * GROUP
        return jnp.concatenate(rows, axis=0)

    def update(scores, pv_fns):
        m_old = m_ref[...]
        m_new = functools.reduce(jnp.maximum, [m_old] + [jnp.max(s, axis=1, keepdims=True) for s in scores])
        alpha = jnp.exp(m_old - m_new)
        ps = [jnp.exp(s - m_new) for s in scores]
        ones = functools.reduce(jnp.add, [jnp.sum(p, axis=1, keepdims=True) for p in ps])
        pv = functools.reduce(jnp.add, [f(p.astype(BF16)) for f, p in zip(pv_fns, ps)])
        acc_ref[:, 0:KV_W] = alpha * acc_ref[:, 0:KV_W] + pv
        acc_ref[:, KV_W:KV_W + LANES] = alpha * acc_ref[:, KV_W:KV_W + LANES] + ones
        m_ref[...] = m_new

    bias_chunk = block_bias(e_ref[...])
    li = lax.broadcasted_iota(jnp.int32, (1, PAGE), 1)
    scores, pv_fns = [], []
    for k in range(pps):
        pg = pages[k]
        kpos = (t * pps + k) * PAGE + li
        s = _dot(qbd, pg[0:KV_W, :].astype(BF16)) - slope * (qposf - kpos.astype(F32))
        scores.append(s + bias_chunk[:, k * PAGE:(k + 1) * PAGE])
        pv_fns.append(lambda p, pg=pg: _dot_nt(p, pg[KV_W:2 * KV_W, :].astype(BF16)))
    update(scores, pv_fns)

    @pl.when(t == nchunks - 1)
    def _():
        nsp = ms_ref.shape[1]
        kposn = past + li
        e_new = (lax.broadcasted_iota(jnp.int32, (nsp, PAGE), 0) == past // SEL_BLOCK).astype(BF16)
        s = _dot_nt(qbd, ksn_ref[:, 0:KV_W].astype(BF16)) - slope * (qposf - kposn.astype(F32)) + block_bias(e_new)
        s = jnp.where((li < n_new) & (kposn <= qpos), s, NEG)
        update([s], [lambda p: _dot(p, ksn_ref[:, KV_W:2 * KV_W].astype(BF16))])
        o_s = acc_ref[:, 0:KV_W] * (1.0 / acc_ref[:, KV_W:KV_W + 1])
        gts = g_ref[...]
        osw_ref[...] = gts[:, 1:2] * o_s + gts[:, 2:3] * ow_ref[...]
        oc_ref[...] = gts[:, 0:1] * oc_ref[...]


def _attn_sample(page_table, pages, qa, qbd, gates, slope_rows, kc, vc, ms, emat, win, ksn, kwn, *, pps, n_new):
    nb, npages = page_table.shape
    nchunks = npages // pps
    past = npages * PAGE
    wbuf = win.shape[2]
    nc = kc.shape[2]
    nsp = ms.shape[1]

    def page_spec(k):
        return pl.BlockSpec((None, 2 * KV_W, PAGE), lambda b, t, pt: (pt[b, t * pps + k], 0, 0))

    per_b = lambda *shape: pl.BlockSpec((None,) + shape, lambda b, t, pt: (b,) + (0,) * len(shape))
    cs = lambda *shape: pl.BlockSpec(shape, lambda b, t, pt: (0,) * len(shape))
    in_specs = [page_spec(k) for k in range(pps)] + [
        per_b(N_KV, ROWS_G, AUG_W), per_b(ROWS_S, KV_W), per_b(ROWS_S, 3), cs(ROWS_S, 1),
        per_b(N_KV, nc, AUG_W), per_b(N_KV, nc, LANES), cs(nc, nsp),
        pl.BlockSpec((nsp, pps * PAGE), lambda b, t, pt: (0, t)),
        per_b(2 * KV_W, wbuf), per_b(LANES, 2 * KV_W), per_b(LANES, 2 * KV_W)]
    out_specs = [per_b(ROWS_S, KV_W), per_b(ROWS_S, LANES)]
    out_shape = [jax.ShapeDtypeStruct((nb, ROWS_S, KV_W), F32), jax.ShapeDtypeStruct((nb, ROWS_S, LANES), F32)]
    scratch = [pltpu.VMEM((N_KV, Q_PAD, nsp), BF16), pltpu.VMEM((ROWS_S, 1), F32),
               pltpu.VMEM((ROWS_S, KV_W + LANES), F32), pltpu.VMEM((ROWS_S, KV_W), F32)]
    gs = pltpu.PrefetchScalarGridSpec(num_scalar_prefetch=1, grid=(nb, nchunks), in_specs=in_specs,
                                      out_specs=out_specs, scratch_shapes=scratch)
    return pl.pallas_call(functools.partial(_attn_sample_kernel, pps=pps, past=past, wbuf=wbuf, n_new=n_new), grid_spec=gs,
                          out_shape=out_shape, compiler_params=_params(2), name="attn_sample")(
        page_table, *([pages] * pps), qa, qbd, gates, slope_rows, kc, vc, ms, emat, win, ksn, kwn)


def _outproj_kernel(x_ref, a_ref, o_ref, nw_ref, sc_ref, sh_ref, g1_ref, wm_ref, woc_ref, won_ref, wo_ref, x1_ref):
    x = x_ref[...]
    h = _mod_norm(x, nw_ref[...], sc_ref[...], sh_ref[...]).astype(BF16)
    mg = jax.nn.sigmoid(_dot(h, wm_ref[...]))
    y_a = _dot(a_ref[...], woc_ref[...])
    y_b = _dot(o_ref[...], won_ref[...])
    mix = (mg[:, 0:D_MODEL] * y_a + mg[:, D_MODEL:2 * D_MODEL] * y_b).astype(BF16)
    x1_ref[...] = x + g1_ref[...] * _dot(mix, wo_ref[...])


def _outproj(x2d, a, o, mod, nw, w_merge, w_oc, w_on, w_o, *, tm):
    n = x2d.shape[0]
    tok = pl.BlockSpec((tm, D_MODEL), lambda i: (i, 0))
    sq = _const_spec((D_MODEL, D_MODEL))
    return pl.pallas_call(
        _outproj_kernel, grid=(n // tm,),
        in_specs=[tok, tok, tok, _const_spec((1, D_MODEL)), mod.spec(1), mod.spec(0), mod.spec(2),
                  _const_spec((D_MODEL, 2 * D_MODEL)), sq, sq, sq],
        out_specs=tok, out_shape=jax.ShapeDtypeStruct((n, D_MODEL), F32),
        compiler_params=_params(1), name="outproj",
    )(x2d, a, o, nw, mod.rows, mod.rows, mod.rows, w_merge, w_oc, w_on, w_o)


FF_CHUNK = D_FF // 2


def _ffn_kernel(x_ref, nw_ref, sc_ref, sh_ref, g2_ref, nf_ref, wg_ref, wu_ref, wd_ref, y_ref):
    x = x_ref[...]
    h = _mod_norm(x, nw_ref[...], sc_ref[...], sh_ref[...]).astype(BF16)
    acc = jnp.zeros(x.shape, F32)
    for c in range(D_FF // FF_CHUNK):
        sl = slice(c * FF_CHUNK, (c + 1) * FF_CHUNK)
        gate = _dot(h, wg_ref[:, sl])
        up = _dot(h, wu_ref[:, sl])
        act = (gate * jax.nn.sigmoid(gate) * up).astype(BF16)
        acc = acc + _dot(act, wd_ref[sl, :])
    x2 = x + g2_ref[...] * acc
    inv = lax.rsqrt(jnp.mean(x2 * x2, axis=-1, keepdims=True) + EPS)
    y_ref[...] = (x2 * inv) * nf_ref[...]


def _ffn(x1, mod, nw2, nf, w_gate, w_up, w_down, *, tm):
    n = x1.shape[0]
    tok = pl.BlockSpec((tm, D_MODEL), lambda i: (i, 0))
    vec = _const_spec((1, D_MODEL))
    return pl.pallas_call(
        _ffn_kernel, grid=(n // tm,),
        in_specs=[tok, vec, mod.spec(4), mod.spec(3), mod.spec(5), vec,
                  _const_spec((D_MODEL, D_FF)), _const_spec((D_MODEL, D_FF)), _const_spec((D_FF, D_MODEL))],
        out_specs=tok, out_shape=jax.ShapeDtypeStruct((n, D_MODEL), F32),
        compiler_params=_params(1), name="ffn",
    )(x1, nw2, mod.rows, mod.rows, mod.rows, nf, w_gate, w_up, w_down)


def _slopes():
    return 2.0 ** (-8.0 * jnp.arange(1, N_HEADS + 1, dtype=F32) / N_HEADS)


def _slope_lanes(slopes):
    parts = _split3(slopes)
    cols = jnp.stack([parts[0], parts[0], parts[1], parts[1], parts[2], parts[2]], axis=1)
    return jnp.pad(cols, ((0, 0), (0, LANES - 6)))


def _pos_lanes(pos_hi, pos_lo):
    cols = np.stack([pos_hi, pos_lo] * 3, axis=1).astype(np.float32)
    return np.pad(cols, ((0, 0), (0, LANES - 6))).astype(BF16)


def _token_consts(t):
    pos = np.arange(t, dtype=np.int32)
    onehot = (pos[:, None] // SEL_BLOCK == np.arange(HEAD_DIM, dtype=np.int32)[None, :]).astype(np.float32)
    clo = np.concatenate([np.zeros((t, HEAD_DIM), np.float32), onehot], axis=1)
    chi = _pos_lanes((pos // SEL_BLOCK) * SEL_BLOCK, pos % SEL_BLOCK)
    return clo, chi


def _cmp_consts(ncb):
    ci = np.arange(ncb, dtype=np.int32) * CMP_STRIDE
    return _pos_lanes((ci // SEL_BLOCK) * SEL_BLOCK, ci % SEL_BLOCK)


def _ones_lane():
    return (np.arange(LANES) == HEAD_DIM).astype(np.float32).reshape(1, LANES)


def _imp_matrix(nc, ns_pad):
    c = np.arange(nc, dtype=np.int32)[:, None]
    s = np.arange(ns_pad, dtype=np.int32)[None, :]
    per = SEL_BLOCK // CMP_STRIDE
    return ((c // per == s) | ((c % per == per - 1) & (c // per == s - 1))).astype(BF16)


def _prep_weights(w_in, w_phi1, w_phi2, pe_cmp):
    wb = w_in.astype(BF16)
    w_conv_in = wb[:, _C_CONV:_C_QKV]
    w_qkv = jnp.pad(wb[:, _C_QKV:_C_MERGE], ((0, 0), (0, _QKV_COLS_PAD - _QKV_COLS)))
    w_merge = wb[:, _C_MERGE:]
    half = CMP_STRIDE * HEAD_DIM
    w1cat = jnp.concatenate([w_phi1[:, :half], w_phi1[:, half:]], axis=2).astype(BF16)
    w2pad = jnp.pad(w_phi2, ((0, 0), (0, 0), (0, LANES - HEAD_DIM))).astype(BF16)
    pe2 = pe_cmp.reshape(2, 2, half)
    return w_conv_in, w_qkv, w_merge, w1cat, w2pad, pe2


def _prompt_layer(x, mod_p, wts):
    (nw1, nw2, nf, w_conv_in, w_qkv, w_merge, w1cat, w2pad, pe2, w_conv, b_conv, w_oc, w_on, w_o,
     w_gate, w_up, w_down, slopes) = wts
    nb, t, _ = x.shape
    tm = min(512, t)
    x2d = x.reshape(nb * t, D_MODEL)
    mod = _Mod(mod_p, False, t, tm)
    a, tail = _conv_path(x2d, mod, nw1, w_conv_in, w_conv, b_conv, tm=tm, seq_len=t)
    clo, chi = _token_consts(t)
    cv = _ones_lane()
    qt, kvc, kvs, kvw, gts, ksa, vst, kwa, vwt = _qkv_path(x2d, mod, nw1, w_qkv, tm=tm, seq_len=t, consts=(clo, chi))
    npages = t // PAGE
    pt = np.broadcast_to(np.arange(npages, dtype=np.int32), (nb, npages))
    nc = t // CMP_STRIDE
    ns = t // SEL_BLOCK
    nqb = t // Q_BLOCK
    kca, vct = _compress(kvc, pt, pe2, w1cat, w2pad, _cmp_consts(nc), cv,
                         pps=min(16, npages), transpose_v=True, paged=False)
    gates_t = gts[:, :3 * N_HEADS].reshape(nb, nqb, Q_BLOCK, N_KV, GROUP, 3).transpose(0, 3, 1, 5, 4, 2)
    gates_t = gates_t.reshape(nb, N_KV, nqb, 3, GROUP * Q_BLOCK)
    qc_t = _slope_lanes(slopes)[:, :QC_ROWS].reshape(N_KV, GROUP, QC_ROWS).transpose(0, 2, 1)
    qc_t = jnp.repeat(qc_t, Q_BLOCK, axis=2)
    mt = _imp_matrix(nc, ns).T
    o = _attn_prompt_t(qt, gates_t, qc_t, mt, kca, vct, ksa, vst, kwa, vwt)
    x1 = _outproj(x2d, a, o.reshape(nb * t, D_MODEL), mod, nw1, w_merge, w_oc, w_on, w_o, tm=tm)
    y = _ffn(x1, mod, nw2, nf, w_gate, w_up, w_down, tm=tm)
    keep = min(WINDOW, t)
    rows_of = lambda a: a.reshape(nb, 2, N_KV, HEAD_DIM, t).transpose(0, 4, 1, 2, 3)
    state = (rows_of(kvc), rows_of(kvs),
             kvw.reshape(nb, t, 2 * KV_W)[:, t - keep:].reshape(nb, keep, 2, N_KV, HEAD_DIM),
             tail[:, 8 - 2:, :])
    return y.reshape(nb, t, D_MODEL), state


def _sample_layer(x, mod_s, wts, cache_cmp, cache_sel, cache_win, state_conv, page_table):
    (nw1, nw2, nf, w_conv_in, w_qkv, w_merge, w1cat, w2pad, pe2, w_conv, b_conv, w_oc, w_on, w_o,
     w_gate, w_up, w_down, slopes) = wts
    nb, s, _ = x.shape
    n = nb * s
    x2d = x.reshape(n, D_MODEL)
    mod = _Mod(mod_s, True, s, n)
    tpos = jnp.arange(s)
    p1 = jnp.broadcast_to(state_conv[:, 1:2, :], (nb, s, D_MODEL)).reshape(n, D_MODEL)
    p2 = state_conv[:, jnp.minimum(tpos, 1), :].reshape(n, D_MODEL)
    a, u = _conv_path(x2d, mod, nw1, w_conv_in, w_conv, b_conv, tm=n, seq_len=s, prev=(p1, p2))
    qpad, kvc, kvs, kvw, gts = _qkv_path(x2d, mod, nw1, w_qkv, tm=n, seq_len=s)

    npages = page_table.shape[1]
    past = npages * PAGE
    cv = _ones_lane()
    nc = past // CMP_STRIDE
    token_minor = lambda c: c.transpose(0, 2, 3, 4, 1).reshape(c.shape[0], 2 * KV_W, c.shape[1])
    kca, vca = _compress(token_minor(cache_cmp), page_table, pe2, w1cat, w2pad, _cmp_consts(nc), cv,
                         pps=min(16, npages), transpose_v=False, paged=True)

    qh = qpad.reshape(nb, s, N_KV, GROUP, LANES).transpose(0, 2, 3, 1, 4)
    qh = jnp.pad(qh, ((0, 0), (0, 0), (0, 0), (0, Q_PAD - s), (0, 0)))
    sl = jnp.broadcast_to(_slope_lanes(slopes).reshape(1, N_KV, GROUP, 1, LANES), qh.shape)
    qa = jnp.concatenate([qh, sl], axis=-1).reshape(nb, N_KV, ROWS_G, AUG_W)
    eye = jnp.eye(N_KV, dtype=BF16)
    qbd = (qh[..., None, :HEAD_DIM] * eye[None, :, None, None, :, None]).reshape(nb, ROWS_S, KV_W)
    gates = gts[:, :3 * N_HEADS].reshape(nb, s, N_KV, GROUP, 3).transpose(0, 2, 3, 1, 4)
    gates = jnp.pad(gates, ((0, 0), (0, 0), (0, 0), (0, Q_PAD - s), (0, 0))).reshape(nb, ROWS_S, 3)
    slope_rows = jnp.repeat(slopes, Q_PAD).reshape(ROWS_S, 1)
    ns = past // SEL_BLOCK + 1
    nsp = -(-ns // LANES) * LANES
    ms = _imp_matrix(nc, nsp)
    tok = np.arange(past, dtype=np.int32)
    emat = (np.arange(nsp, dtype=np.int32)[:, None] == tok[None, :] // SEL_BLOCK).astype(BF16)
    pad_rows = lambda r: jnp.pad(r.reshape(nb, s, 2 * KV_W), ((0, 0), (0, LANES - s), (0, 0)))
    osw, ocg = _attn_sample(page_table, token_minor(cache_sel), qa, qbd, gates, slope_rows, kca, vca,
                            ms, emat, token_minor(cache_win), pad_rows(kvs), pad_rows(kvw),
                            pps=min(32, npages), n_new=s)
    osw = osw.reshape(nb, N_KV, GROUP, Q_PAD, N_KV, HEAD_DIM)
    o_sw = jnp.einsum('bgrqgd->bqgrd', osw)
    o_c = ocg.reshape(nb, N_KV, GROUP, Q_PAD, LANES)[..., :HEAD_DIM].transpose(0, 3, 1, 2, 4)
    o = (o_sw + o_c)[:, :s].reshape(n, N_HEADS * HEAD_DIM).astype(BF16)

    x1 = _outproj(x2d, a, o, mod, nw1, w_merge, w_oc, w_on, w_o, tm=n)
    y = _ffn(x1, mod, nw2, nf, w_gate, w_up, w_down, tm=n)
    kv5 = lambda r: r.reshape(nb, s, 2, N_KV, HEAD_DIM)
    win = jnp.concatenate([cache_win, kv5(kvw)], axis=1)[:, s:]
    state = (kv5(kvc), kv5(kvs), win, u.reshape(nb, s, D_MODEL)[:, s - 2:])
    return y.reshape(nb, s, D_MODEL), state


def kernel(x_prompt, x_sample, c_prompt, c_sample, cache_cmp, cache_sel, cache_win, state_conv, page_table,
           w_ada, b_ada, norm1, w_in, w_conv, b_conv, w_out_conv, pe_cmp, w_phi1, w_phi2, w_o_nsa, w_out,
           norm2, w_gate, w_up, w_down, norm_f):
    depth = w_ada.shape[0]
    assert depth == 1, "single-layer trunk"
    nbp, nbs = c_prompt.shape[0], c_sample.shape[0]
    slopes = _slopes()
    l = 0
    c_all = jnp.concatenate([c_prompt, c_sample], axis=0)
    c_all = jnp.pad(c_all, ((0, -c_all.shape[0] % 8), (0, 0)))
    mod = _ada(c_all, w_ada[l], b_ada[l])
    w_conv_in, w_qkv, w_merge, w1cat, w2pad, pe2 = _prep_weights(w_in[l], w_phi1[l], w_phi2[l], pe_cmp[l])
    row = lambda v: v.reshape(1, -1)
    wts = (row(norm1[l]), row(norm2[l]), row(norm_f), w_conv_in, w_qkv, w_merge, w1cat, w2pad, pe2,
           w_conv[l], row(b_conv[l]), w_out_conv[l].astype(BF16), w_o_nsa[l].astype(BF16), w_out[l].astype(BF16),
           w_gate[l].astype(BF16), w_up[l].astype(BF16), w_down[l].astype(BF16), slopes)
    yp, st_p = _prompt_layer(x_prompt, mod[:nbp], wts)
    ys, st_s = _sample_layer(x_sample, mod[nbp:nbp + nbs], wts, cache_cmp[l], cache_sel[l], cache_win[l],
                             state_conv[l], page_table)
    return (yp, ys, st_p[0][None], st_p[1][None], st_p[2][None], st_p[3][None],
            st_s[0][None], st_s[1][None], st_s[2][None], st_s[3][None])
```

```python
import functools

import jax
import jax.numpy as jnp
import numpy as np
from jax import lax
from jax.experimental import pallas as pl
from jax.experimental.pallas import tpu as pltpu

F32 = jnp.float32
BF16 = jnp.bfloat16

D_MODEL = 1024
N_HEADS = 16
HEAD_DIM = 64
N_KV = 4
GROUP = N_HEADS // N_KV
KV_W = N_KV * HEAD_DIM
CMP_BLOCK = 32
CMP_STRIDE = 16
SEL_BLOCK = 64
N_SEL = 16
WINDOW = 512
D_PHI = 2 * HEAD_DIM
Q_BLOCK = 256
PAGE = 128
D_FF = ((8 * D_MODEL // 3 + 255) // 256) * 256
EPS = 1e-6
NEG = -1e30
FORCE_BONUS = 1e3

LANES = 128
AUG_W = 2 * LANES
HALVES_PER_PAGE = PAGE // CMP_STRIDE
ROW_TILES = 2 * KV_W // LANES
VMEM_LIMIT = 56 * 1024 * 1024
KEY_TILE = 2 * LANES
VT_ROWS = HEAD_DIM + 16
QC_ROWS = 16
SEL_TILES_PER_ITER = 8

_C_CONV = 0
_C_QKV = 3 * D_MODEL
_C_GATE = _C_QKV + N_HEADS * HEAD_DIM + 6 * KV_W
_C_MERGE = _C_GATE + 3 * N_HEADS
_QKV_COLS = _C_MERGE - _C_QKV
_QKV_COLS_PAD = -(-_QKV_COLS // LANES) * LANES


def _dot(a, b):
    return jnp.dot(a, b, preferred_element_type=F32)


def _dot_nt(a, b):
    return lax.dot_general(a, b, (((1,), (1,)), ((), ())), preferred_element_type=F32)


def _params(n_axes):
    return pltpu.CompilerParams(dimension_semantics=("arbitrary",) * n_axes, vmem_limit_bytes=VMEM_LIMIT)


def _const_spec(shape):
    return pl.BlockSpec(shape, lambda *_: (0,) * len(shape))


def _mod_norm(x, nw, sc, sh):
    inv = lax.rsqrt(jnp.mean(x * x, axis=-1, keepdims=True) + EPS)
    return (x * inv) * nw * (1.0 + sc) + sh


def _split3(x):
    a = x.astype(BF16)
    r = x - a.astype(F32)
    b = r.astype(BF16)
    c = (r - b.astype(F32)).astype(BF16)
    return a, b, c


def _ada_kernel(c_ref, w_ref, b_ref, o_ref):
    c = c_ref[...]
    s = c * jax.nn.sigmoid(c)
    o_ref[...] = jnp.dot(s, w_ref[...], preferred_element_type=F32, precision=lax.Precision.HIGHEST) + b_ref[...]


def _ada(c, w_ada, b_ada):
    n = c.shape[0]
    tn = 1536
    return pl.pallas_call(
        _ada_kernel,
        grid=(6 * D_MODEL // tn,),
        in_specs=[_const_spec((n, D_MODEL)),
                  pl.BlockSpec((D_MODEL, tn), lambda j: (0, j)),
                  pl.BlockSpec((1, tn), lambda j: (0, j))],
        out_specs=pl.BlockSpec((n, tn), lambda j: (0, j)),
        out_shape=jax.ShapeDtypeStruct((n, 6 * D_MODEL), F32),
        compiler_params=_params(1),
        name="ada",
    )(c, w_ada, b_ada.reshape(1, -1))


class _Mod:
    def __init__(self, mod, per_token, seq_len, tm):
        self.per_token = per_token
        if per_token:
            self.rows = jnp.repeat(mod, seq_len, axis=0)
        else:
            self.rows = mod.reshape(mod.shape[0], 1, 6 * D_MODEL)
        self.tiles_per_seq = None if per_token else seq_len // tm
        self.tm = tm

    def spec(self, k):
        if self.per_token:
            return pl.BlockSpec((self.tm, D_MODEL), lambda i: (i, k))
        tps = self.tiles_per_seq
        return pl.BlockSpec((None, 1, D_MODEL), lambda i: (i // tps, 0, k))


def _conv_kernel(*refs, carry_rows, seq_len):
    if carry_rows:
        (x_ref, nw_ref, sc_ref, sh_ref, w_ref, wc_ref, bc_ref, a_ref, tail_ref, carry_ref) = refs
    else:
        (x_ref, nw_ref, sc_ref, sh_ref, w_ref, wc_ref, bc_ref, p1_ref, p2_ref, a_ref, tail_ref) = refs
    tm = x_ref.shape[0]
    h = _mod_norm(x_ref[...], nw_ref[...], sc_ref[...], sh_ref[...]).astype(BF16)
    z = _dot(h, w_ref[...])
    bg = z[:, 0:D_MODEL]
    u = z[:, D_MODEL:2 * D_MODEL] * z[:, 2 * D_MODEL:3 * D_MODEL]
    row = lax.broadcasted_iota(jnp.int32, (tm, 1), 0)
    u1 = pltpu.roll(u, 1, 0)
    u2 = pltpu.roll(u, 2, 0)
    if carry_rows:
        @pl.when(pl.program_id(0) % carry_rows == 0)
        def _():
            carry_ref[...] = jnp.zeros_like(carry_ref)
        c0 = carry_ref[0:1, :]
        c1 = carry_ref[1:2, :]
        u1 = jnp.where(row == 0, c1, u1)
        u2 = jnp.where(row == 0, c0, jnp.where(row == 1, c1, u2))
        carry_ref[0:2, :] = u[tm - 2:tm, :]
        tail_ref[...] = u[tm - 8:tm, :]
    else:
        pos = lax.rem(row, seq_len)
        u1 = jnp.where(pos >= 1, u1, p1_ref[...])
        u2 = jnp.where(pos >= 2, u2, p2_ref[...])
        tail_ref[...] = u
    v = bc_ref[...] + wc_ref[0:1, :] * u2 + wc_ref[1:2, :] * u1 + wc_ref[2:3, :] * u
    a_ref[...] = (bg * v).astype(BF16)


def _conv_path(x2d, mod, nw, w_conv_in, w_conv, b_conv, *, tm, seq_len, prev=None):
    n = x2d.shape[0]
    tok = pl.BlockSpec((tm, D_MODEL), lambda i: (i, 0))
    in_specs = [tok, _const_spec((1, D_MODEL)), mod.spec(1), mod.spec(0),
                _const_spec((D_MODEL, 3 * D_MODEL)), _const_spec((3, D_MODEL)), _const_spec((1, D_MODEL))]
    args = [x2d, nw, mod.rows, mod.rows, w_conv_in, w_conv, b_conv]
    if prev is None:
        tps = seq_len // tm
        out_specs = [tok, pl.BlockSpec((None, 8, D_MODEL), lambda i: (i // tps, 0, 0))]
        out_shape = [jax.ShapeDtypeStruct((n, D_MODEL), BF16), jax.ShapeDtypeStruct((n // seq_len, 8, D_MODEL), F32)]
        scratch = [pltpu.VMEM((8, D_MODEL), F32)]
        kern = functools.partial(_conv_kernel, carry_rows=tps, seq_len=seq_len)
    else:
        in_specs += [tok, tok]
        args += list(prev)
        out_specs = [tok, tok]
        out_shape = [jax.ShapeDtypeStruct((n, D_MODEL), BF16), jax.ShapeDtypeStruct((n, D_MODEL), F32)]
        scratch = []
        kern = functools.partial(_conv_kernel, carry_rows=0, seq_len=seq_len)
    return pl.pallas_call(kern, grid=(n // tm,), in_specs=in_specs, out_specs=out_specs, out_shape=out_shape,
                          scratch_shapes=scratch, compiler_params=_params(1), name="conv_path")(*args)


def _qkv_kernel(*refs, aug):
    if aug:
        (x_ref, nw_ref, sc_ref, sh_ref, w_ref, clo_ref, chi_ref,
         q_ref, kvc_ref, kvs_ref, kvw_ref, g_ref, ksa_ref, vsa_ref, kwa_ref, vwa_ref) = refs
    else:
        (x_ref, nw_ref, sc_ref, sh_ref, w_ref, q_ref, kvc_ref, kvs_ref, kvw_ref, g_ref) = refs
    tm = x_ref.shape[0]
    h = _mod_norm(x_ref[...], nw_ref[...], sc_ref[...], sh_ref[...]).astype(BF16)
    z = _dot(h, w_ref[...])
    low = lax.broadcasted_iota(jnp.int32, (tm, LANES), 1) < HEAD_DIM
    nq = N_HEADS * HEAD_DIM
    for c in range(N_HEADS // 2):
        t = z[:, c * LANES:(c + 1) * LANES] * (HEAD_DIM ** -0.5)
        if aug:
            tt = t.T.astype(BF16)
            q_ref[2 * c] = tt[0:HEAD_DIM]
            q_ref[2 * c + 1] = tt[HEAD_DIM:2 * HEAD_DIM]
        else:
            q_ref[:, (2 * c) * LANES:(2 * c + 1) * LANES] = jnp.where(low, t, 0.0).astype(BF16)
            q_ref[:, (2 * c + 1) * LANES:(2 * c + 2) * LANES] = jnp.where(low, pltpu.roll(t, HEAD_DIM, 1), 0.0).astype(BF16)
    if aug:
        kvc_ref[...] = z[:, nq:nq + 2 * KV_W].T
        kvs_ref[...] = z[:, nq + 2 * KV_W:nq + 4 * KV_W].T
    else:
        kvc_ref[...] = z[:, nq:nq + 2 * KV_W]
        kvs_ref[...] = z[:, nq + 2 * KV_W:nq + 4 * KV_W]
    kvw_ref[...] = z[:, nq + 4 * KV_W:nq + 6 * KV_W]
    g_ref[...] = jax.nn.sigmoid(z[:, nq + 6 * KV_W:nq + 6 * KV_W + LANES])
    if aug:
        chi = chi_ref[...]
        ones_row = (lax.broadcasted_iota(jnp.int32, (VT_ROWS - HEAD_DIM, KEY_TILE), 0) == 0).astype(BF16)
        for br, (ka_ref, va_ref) in enumerate(((ksa_ref, vsa_ref), (kwa_ref, vwa_ref))):
            kbase = nq + 2 * KV_W * (br + 1)
            clo = clo_ref[...] if br == 0 else 0.0
            for g in range(N_KV):
                kt = z[:, kbase + (g // 2) * LANES:kbase + (g // 2 + 1) * LANES]
                if g % 2:
                    kt = pltpu.roll(kt, HEAD_DIM, 1)
                ka_ref[g, :, 0:LANES] = jnp.where(low, kt, clo).astype(BF16)
                ka_ref[g, :, LANES:AUG_W] = chi
            for c in range(N_KV // 2):
                vt = z[:, kbase + KV_W + c * LANES:kbase + KV_W + (c + 1) * LANES].T.astype(BF16)
                for gg in range(2):
                    for j in range(tm // KEY_TILE):
                        va_ref[2 * c + gg, j, 0:HEAD_DIM, :] = vt[gg * HEAD_DIM:(gg + 1) * HEAD_DIM, j * KEY_TILE:(j + 1) * KEY_TILE]
                        va_ref[2 * c + gg, j, HEAD_DIM:VT_ROWS, :] = ones_row


def _qkv_path(x2d, mod, nw, w_qkv, *, tm, seq_len, consts=None):
    n = x2d.shape[0]
    aug = consts is not None
    tok = lambda w: pl.BlockSpec((tm, w), lambda i: (i, 0))
    in_specs = [tok(D_MODEL), _const_spec((1, D_MODEL)), mod.spec(1), mod.spec(0),
                _const_spec((D_MODEL, _QKV_COLS_PAD))]
    args = [x2d, nw, mod.rows, mod.rows, w_qkv]
    out_specs = [tok(N_HEADS * LANES), tok(2 * KV_W), tok(2 * KV_W), tok(2 * KV_W), tok(LANES)]
    out_shape = [jax.ShapeDtypeStruct((n, N_HEADS * LANES), BF16)] + \
                [jax.ShapeDtypeStruct((n, 2 * KV_W), F32)] * 3 + [jax.ShapeDtypeStruct((n, LANES), F32)]
    if aug:
        tps = seq_len // tm
        nb = n // seq_len
        pos = lambda w: pl.BlockSpec((tm, w), lambda i: (i % tps, 0))
        in_specs += [pos(LANES), pos(LANES)]
        args += list(consts)
        out_specs[0] = pl.BlockSpec((None, N_HEADS, HEAD_DIM, tm), lambda i: (i // tps, 0, 0, i % tps))
        out_shape[0] = jax.ShapeDtypeStruct((nb, N_HEADS, HEAD_DIM, seq_len), BF16)
        for k in (1, 2):
            out_specs[k] = pl.BlockSpec((None, 2 * KV_W, tm), lambda i: (i // tps, 0, i % tps))
            out_shape[k] = jax.ShapeDtypeStruct((nb, 2 * KV_W, seq_len), F32)
        ka = pl.BlockSpec((None, N_KV, tm, AUG_W), lambda i: (i // tps, 0, i % tps, 0))
        va = pl.BlockSpec((None, N_KV, tm // KEY_TILE, VT_ROWS, KEY_TILE), lambda i: (i // tps, 0, i % tps, 0, 0))
        out_specs += [ka, va, ka, va]
        ka_s = jax.ShapeDtypeStruct((nb, N_KV, seq_len, AUG_W), BF16)
        va_s = jax.ShapeDtypeStruct((nb, N_KV, seq_len // KEY_TILE, VT_ROWS, KEY_TILE), BF16)
        out_shape += [ka_s, va_s, ka_s, va_s]
    return pl.pallas_call(functools.partial(_qkv_kernel, aug=aug), grid=(n // tm,), in_specs=in_specs,
                          out_specs=out_specs, out_shape=out_shape, compiler_params=_params(1), name="qkv_path")(*args)


def _block_major_loader(ref, k, c):
    def load(p, nrows):
        return ref[k, c, p * HALVES_PER_PAGE:p * HALVES_PER_PAGE + nrows, :]
    return load


def _gather_pair(load, nrows):
    low = lax.broadcasted_iota(jnp.int32, (nrows, LANES), 1) < HEAD_DIM
    even, odd = [], []
    for qq in range(CMP_STRIDE // 2):
        a = load(2 * qq, nrows)
        b = load(2 * qq + 1, nrows)
        even.append(jnp.where(low, a, pltpu.roll(b, HEAD_DIM, 1)))
        odd.append(jnp.where(low, pltpu.roll(a, HEAD_DIM, 1), b))
    return jnp.concatenate(even, axis=1), jnp.concatenate(odd, axis=1)


def _compress_kernel(pt_ref, *refs, pps, transpose_v):
    pages = refs[:pps]
    nxt_ref, pe_ref, w1_ref, w2_ref, chi_ref, cv_ref, perm_ref, kc_ref, vc_ref, xt_ref = refs[pps:]
    perm = perm_ref[...]
    for k, pg in enumerate(list(pages) + [nxt_ref]):
        for c in range(ROW_TILES):
            xt_ref[k, c] = _dot_nt(perm, pg[c * LANES:(c + 1) * LANES, :].astype(BF16))
    loaders = lambda c: [_block_major_loader(xt_ref, k, c) for k in range(pps)]
    look_loader = lambda c: _block_major_loader(xt_ref, pps, c)
    t = pl.program_id(1)
    last = t == pl.num_programs(1) - 1
    nhb = pps * HALVES_PER_PAGE
    n = N_KV * nhb
    row = lax.broadcasted_iota(jnp.int32, (n, 1), 0)
    chi = chi_ref[...]
    for kv in range(2):
        by_group, look = [], []
        for cc in range(N_KV // 2):
            c = kv * (N_KV // 2) + cc
            pairs = [_gather_pair(ld, HALVES_PER_PAGE) for ld in loaders(c)]
            by_group += [[ev for ev, _ in pairs], [od for _, od in pairs]]
            look += list(_gather_pair(look_loader(c), 1))
        parts = [x for group in by_group for x in group]
        extra = jnp.concatenate([pe_ref[kv], jnp.zeros((2, CMP_STRIDE * HEAD_DIM), F32)] + look, axis=0)
        xmat = jnp.concatenate(parts + [extra], axis=0).astype(BF16)
        hab = _dot(xmat, w1_ref[kv])
        ha = hab[0:n, 0:D_PHI]
        hb = hab[0:n, D_PHI:2 * D_PHI]
        pbias = hab[n:n + 1, 0:D_PHI] + hab[n + 1:n + 2, D_PHI:2 * D_PHI]
        hbn = pltpu.roll(hb, n - 1, 0)
        for g in range(N_KV):
            la = jnp.where(last, 0.0, hab[n + 4 + g:n + 5 + g, D_PHI:2 * D_PHI])
            hbn = jnp.where(row == g * nhb + nhb - 1, la, hbn)
        act = jax.nn.gelu(ha + hbn + pbias).astype(BF16)
        out = _dot(act, w2_ref[kv])
        for g in range(N_KV):
            blk = out[g * nhb:(g + 1) * nhb, :]
            if kv == 0:
                kc_ref[g, :, 0:LANES] = blk.astype(BF16)
                kc_ref[g, :, LANES:AUG_W] = chi
            elif transpose_v:
                vc_ref[g, :, :] = (blk + cv_ref[...]).T.astype(BF16)
            else:
                vc_ref[g, :, :] = (blk + cv_ref[...]).astype(BF16)


def _compress(pages, page_table, pe2, w1cat, w2pad, chi_c, cv, *, pps, transpose_v, paged):
    nb, npages = page_table.shape
    nchunks = npages // pps
    nhb = pps * HALVES_PER_PAGE
    ncb = npages * HALVES_PER_PAGE
    where = (lambda b, j: (j, 0, 0)) if paged else (lambda b, j: (b, 0, j))

    def page_spec(k):
        return pl.BlockSpec((None, 2 * KV_W, PAGE), lambda b, t, pt: where(b, pt[b, t * pps + k]))

    nxt_spec = pl.BlockSpec((None, 2 * KV_W, PAGE),
                            lambda b, t, pt: where(b, pt[b, jnp.minimum((t + 1) * pps, npages - 1)]))
    scratch = [pltpu.VMEM((pps + 1, ROW_TILES, PAGE, LANES), F32)]
    r = np.arange(PAGE, dtype=np.int32)
    perm = (r[None, :] == (CMP_STRIDE * (r % HALVES_PER_PAGE) + r // HALVES_PER_PAGE)[:, None]).astype(BF16)
    cs = lambda shape: pl.BlockSpec(shape, lambda b, t, pt: (0,) * len(shape))
    in_specs = [page_spec(k) for k in range(pps)] + [
        nxt_spec, cs((2, 2, CMP_STRIDE * HEAD_DIM)), cs((2, CMP_STRIDE * HEAD_DIM, 2 * D_PHI)),
        cs((2, D_PHI, LANES)), pl.BlockSpec((nhb, LANES), lambda b, t, pt: (t, 0)), cs((1, LANES)),
        cs((PAGE, PAGE))]
    if transpose_v:
        v_spec = pl.BlockSpec((None, N_KV, LANES, nhb), lambda b, t, pt: (b, 0, 0, t))
        v_shape = jax.ShapeDtypeStruct((nb, N_KV, LANES, ncb), BF16)
    else:
        v_spec = pl.BlockSpec((None, N_KV, nhb, LANES), lambda b, t, pt: (b, 0, t, 0))
        v_shape = jax.ShapeDtypeStruct((nb, N_KV, ncb, LANES), BF16)
    out_specs = [pl.BlockSpec((None, N_KV, nhb, AUG_W), lambda b, t, pt: (b, 0, t, 0)), v_spec]
    out_shape = [jax.ShapeDtypeStruct((nb, N_KV, ncb, AUG_W), BF16), v_shape]
    gs = pltpu.PrefetchScalarGridSpec(num_scalar_prefetch=1, grid=(nb, nchunks), in_specs=in_specs, out_specs=out_specs,
                                      scratch_shapes=scratch)
    return pl.pallas_call(functools.partial(_compress_kernel, pps=pps, transpose_v=transpose_v),
                          grid_spec=gs, out_shape=out_shape,
                          compiler_params=_params(2), name="compress")(
        page_table, *([pages] * (pps + 1)), pe2, w1cat, w2pad, chi_c, cv, perm)


REMOVED = -3e38


def _top_k_mask(score, index, k, axis):
    work = score
    selected = jnp.zeros(score.shape, jnp.bool_)
    for _ in range(k):
        best = jnp.max(work, axis=axis, keepdims=True)
        first = jnp.min(jnp.where(work == best, index, score.shape[axis]), axis=axis, keepdims=True)
        hit = index == first
        selected = selected | hit
        work = jnp.where(hit, REMOVED, work)
    return selected


def _attn_prompt_t_kernel(q_ref, g_ref, qc_ref, mt_ref, kc_ref, vc_ref, ks_ref, vs_ref, kw_ref, vw_ref, o_ref,
                          qa_ref, qw_ref, m_ref, acc_ref):
    nc = kc_ref.shape[0]
    ns = mt_ref.shape[0]
    kt = vs_ref.shape[2]
    ncol = GROUP * Q_BLOCK
    qb = pl.program_id(2)
    q0 = qb * Q_BLOCK
    qpos = q0 + (lax.broadcasted_iota(jnp.int32, (1, ncol), 1) & (Q_BLOCK - 1))

    for ref in (qw_ref, qa_ref):
        for r in range(GROUP):
            ref[0:HEAD_DIM, r * Q_BLOCK:(r + 1) * Q_BLOCK] = q_ref[r]
        ref[HEAD_DIM:2 * HEAD_DIM, :] = jnp.zeros((HEAD_DIM, ncol), BF16)
        ref[2 * HEAD_DIM:2 * HEAD_DIM + QC_ROWS, :] = qc_ref[...]
        ref[2 * HEAD_DIM + QC_ROWS:AUG_W, :] = jnp.zeros((AUG_W - 2 * HEAD_DIM - QC_ROWS, ncol), BF16)

    s = _dot(kc_ref[...], qw_ref[...])
    ci = lax.broadcasted_iota(jnp.int32, (nc, 1), 0)
    last_valid = lax.shift_right_arithmetic(qpos - (CMP_BLOCK - 1), CMP_STRIDE.bit_length() - 1)
    c_valid = ci <= last_valid
    s = jnp.where(c_valid, s, NEG)
    e = jnp.exp(s - jnp.max(s, axis=0, keepdims=True))
    inv = jnp.where(last_valid >= 0, 1.0 / jnp.sum(e, axis=0, keepdims=True), 0.0)
    p = e * inv
    o_c = _dot(vc_ref[...], p.astype(BF16))[0:HEAD_DIM]

    psum = p[:, 0:Q_BLOCK]
    for r in range(1, GROUP):
        psum = psum + p[:, r * Q_BLOCK:(r + 1) * Q_BLOCK]
    mt = mt_ref[...]
    imp = sum(_dot(mt, part) for part in _split3(psum))
    si = lax.broadcasted_iota(jnp.int32, (ns, Q_BLOCK), 0)
    qpos_t = q0 + lax.broadcasted_iota(jnp.int32, (ns, Q_BLOCK), 1)
    cur = lax.shift_right_logical(qpos_t, 6)
    s_valid = si * SEL_BLOCK <= qpos_t
    forced = (si == 0) | (si == cur) | (si == cur - 1)
    score = jnp.where(s_valid, imp + jnp.where(forced, FORCE_BONUS, 0.0), NEG)
    selected = _top_k_mask(score, si, min(N_SEL, ns), axis=0)
    bias_t = jnp.where(selected, 0.0, NEG).astype(BF16)
    qa_ref[HEAD_DIM:HEAD_DIM + ns, :] = jnp.concatenate([bias_t] * GROUP, axis=1)

    t_hi = lax.div(q0, kt)
    key_iota = lax.broadcasted_iota(jnp.int32, (kt, 1), 0)

    n_back = WINDOW // kt
    s_w, v_w, live_w = [], [], []
    for j in range(n_back + 1):
        tw = t_hi - n_back + j
        tc = jnp.maximum(tw, 0)
        kpos = tw * kt + key_iota
        s = _dot(kw_ref[pl.ds(pl.multiple_of(tc * kt, kt), kt), :], qw_ref[...])
        live = None
        if j == 0:
            oldest = jnp.where(tw >= 0, qpos - WINDOW, jnp.iinfo(jnp.int32).max)
            s = jnp.where(kpos > oldest, s, NEG)
        elif j == n_back:
            s = jnp.where(kpos <= qpos, s, NEG)
        else:
            live = tw >= 0
        s_w.append(s)
        v_w.append(vw_ref[tc])
        live_w.append(live)
    tile_max = [jnp.max(s, axis=0, keepdims=True) for s in s_w]
    tile_max = [mx if live is None else jnp.where(live, mx, NEG) for mx, live in zip(tile_max, live_w)]
    m_w = functools.reduce(jnp.maximum, tile_max)
    shift_w = [m_w if live is None else jnp.where(live, m_w, -NEG) for live in live_w]
    acc_w = sum(_dot(v, jnp.exp(s - sh).astype(BF16)) for s, v, sh in zip(s_w, v_w, shift_w))
    o_w = acc_w[0:HEAD_DIM] * (1.0 / acc_w[HEAD_DIM:HEAD_DIM + 1])

    m_ref[...] = jnp.full(m_ref.shape, NEG, F32)
    acc_ref[...] = jnp.zeros(acc_ref.shape, F32)

    def sel_update(runs, causal_last):
        qa = qa_ref[...]
        ss, vts = [], []
        for ri, (t0, n) in enumerate(runs):
            k0 = pl.multiple_of(t0 * kt, kt)
            s = _dot(ks_ref[pl.ds(k0, n * kt), :], qa)
            for i in range(n):
                si_ = s[i * kt:(i + 1) * kt]
                if causal_last and ri == len(runs) - 1 and i == n - 1:
                    si_ = jnp.where(k0 + i * kt + key_iota <= qpos, si_, NEG)
                ss.append(si_)
                vts.append(vs_ref[t0 + i])
        m_old = m_ref[...]
        m_new = functools.reduce(jnp.maximum, [m_old] + [jnp.max(s, axis=0, keepdims=True) for s in ss])
        pv = sum(_dot(vt, jnp.exp(s - m_new).astype(BF16)) for vt, s in zip(vts, ss))
        acc_ref[...] = jnp.exp(m_old - m_new) * acc_ref[...] + pv
        m_ref[...] = m_new

    per_tile = kt // SEL_BLOCK
    lo_blk = jnp.min(jnp.where(selected & (si >= per_tile), si, ns))
    start = jnp.clip(lax.div(lo_blk, per_tile), 1, jnp.maximum(t_hi, 1))
    n_plain = jnp.maximum(t_hi - start, 0)
    n_group = lax.div(n_plain, SEL_TILES_PER_ITER)

    def body(i, carry):
        sel_update([(start + i * SEL_TILES_PER_ITER, SEL_TILES_PER_ITER)], False)
        return carry

    lax.fori_loop(0, n_group, body, 0)
    rem = n_plain - n_group * SEL_TILES_PER_ITER
    for left in range(SEL_TILES_PER_ITER):
        @pl.when((rem == left) & (t_hi >= 1))
        def _():
            sel_update([(0, 1), (t_hi - left, left + 1)], True)

    @pl.when(t_hi == 0)
    def _():
        sel_update([(0, 1)], True)

    acc = acc_ref[...]
    o_s = acc[0:HEAD_DIM] * (1.0 / acc[HEAD_DIM:HEAD_DIM + 1])

    gts = g_ref[...]
    o = gts[0:1] * o_c + gts[1:2] * o_s + gts[2:3] * o_w
    for c in range(GROUP // 2):
        pair = jnp.concatenate([o[:, (2 * c) * Q_BLOCK:(2 * c + 1) * Q_BLOCK],
                                o[:, (2 * c + 1) * Q_BLOCK:(2 * c + 2) * Q_BLOCK]], axis=0)
        o_ref[:, c * LANES:(c + 1) * LANES] = pair.T.astype(BF16)


def _attn_prompt_t(qt, gates_t, qc_t, mt, kc, vct, ks, vst, kw, vwt):
    nb, t = qt.shape[0], qt.shape[3]
    nc = kc.shape[2]
    ns = mt.shape[0]
    ntile, kt = vst.shape[2], vst.shape[4]
    ncol = GROUP * Q_BLOCK
    per_bg = lambda *shape: pl.BlockSpec((None, None) + shape, lambda b, g, i: (b, g) + (0,) * len(shape))
    in_specs = [pl.BlockSpec((None, GROUP, HEAD_DIM, Q_BLOCK), lambda b, g, i: (b, g, 0, i)),
                pl.BlockSpec((None, None, None, 3, ncol), lambda b, g, i: (b, g, i, 0, 0)),
                pl.BlockSpec((None, QC_ROWS, ncol), lambda b, g, i: (g, 0, 0)),
                pl.BlockSpec((ns, nc), lambda b, g, i: (0, 0)),
                per_bg(nc, AUG_W), per_bg(LANES, nc), per_bg(t, AUG_W), per_bg(ntile, VT_ROWS, kt),
                per_bg(t, AUG_W), per_bg(ntile, VT_ROWS, kt)]
    out_spec = pl.BlockSpec((None, Q_BLOCK, GROUP * HEAD_DIM), lambda b, g, i: (b, i, g))
    scratch = [pltpu.VMEM((AUG_W, ncol), BF16), pltpu.VMEM((AUG_W, ncol), BF16),
               pltpu.VMEM((1, ncol), F32), pltpu.VMEM((VT_ROWS, ncol), F32)]
    return pl.pallas_call(_attn_prompt_t_kernel, grid=(nb, N_KV, t // Q_BLOCK),
                          in_specs=in_specs, out_specs=out_spec,
                          out_shape=jax.ShapeDtypeStruct((nb, t, N_HEADS * HEAD_DIM), BF16),
                          scratch_shapes=scratch, compiler_params=_params(3), name="attn_prompt")(
        qt, gates_t, qc_t, mt, kc, vct, ks, vst, kw, vwt)


Q_PAD = 8
ROWS_G = GROUP * Q_PAD
ROWS_S = N_KV * ROWS_G


def _attn_sample_kernel(pt_ref, *refs, pps, past, wbuf, n_new):
    pages = refs[:pps]
    (qa_ref, qbd_ref, g_ref, slope_ref, kc_ref, vc_ref, ms_ref, e_ref, win_ref, ksn_ref, kwn_ref,
     osw_ref, oc_ref, bias_ref, m_ref, acc_ref, ow_ref) = refs[pps:]
    t = pl.program_id(1)
    nchunks = pl.num_programs(1)
    nc = kc_ref.shape[1]
    ns = past // SEL_BLOCK + 1
    row = lax.broadcasted_iota(jnp.int32, (ROWS_S, 1), 0)
    qpos = past + (row & (Q_PAD - 1))
    qposf = qpos.astype(F32)
    slope = slope_ref[...]
    qbd = qbd_ref[...]

    @pl.when(t == 0)
    def _():
        nsp = ms_ref.shape[1]
        rg = lax.broadcasted_iota(jnp.int32, (ROWS_G, 1), 0)
        qpos_g = past + (rg & (Q_PAD - 1))
        r8 = lax.broadcasted_iota(jnp.int32, (Q_PAD, 1), 0)
        qpos8 = past + r8
        si = lax.broadcasted_iota(jnp.int32, (Q_PAD, nsp), 1)
        scores = []
        for g in range(N_KV):
            s = _dot_nt(qa_ref[g], kc_ref[g])
            ci = lax.broadcasted_iota(jnp.int32, (1, nc), 1)
            c_valid = ci * CMP_STRIDE + (CMP_BLOCK - 1) <= qpos_g
            s = jnp.where(c_valid, s, NEG)
            e = jnp.exp(s - jnp.max(s, axis=1, keepdims=True))
            p = jnp.where(c_valid, e * (1.0 / jnp.sum(e, axis=1, keepdims=True)), 0.0)
            oc_ref[g * ROWS_G:(g + 1) * ROWS_G, :] = _dot(p.astype(BF16), vc_ref[g])
            psum = p[0:Q_PAD]
            for r in range(1, GROUP):
                psum = psum + p[r * Q_PAD:(r + 1) * Q_PAD]
            ms = ms_ref[...]
            imp = sum(_dot(part, ms) for part in _split3(psum))
            cur = lax.shift_right_logical(qpos8, 6)
            s_valid = (si * SEL_BLOCK <= qpos8) & (si < ns)
            forced = (si == 0) | (si == cur) | (si == cur - 1)
            score = jnp.where(s_valid, imp + jnp.where(forced, FORCE_BONUS, 0.0), NEG)
            scores.append(score)
        score = jnp.concatenate(scores, axis=0)
        si4 = lax.broadcasted_iota(jnp.int32, score.shape, 1)
        rank = jnp.zeros(score.shape, jnp.int32)
        for sp in range(ns):
            other = score[:, sp:sp + 1]
            beats = (other > score) | ((other == score) & (si4 > sp))
            rank = rank + beats.astype(jnp.int32)
        bias_all = jnp.where((rank < min(N_SEL, ns)) & (si4 < ns), 0.0, NEG)
        for g in range(N_KV):
            bias_ref[g] = bias_all[g * Q_PAD:(g + 1) * Q_PAD].astype(BF16)

        kw_t = win_ref[0:KV_W, :].astype(BF16)
        vw_t = win_ref[KV_W:2 * KV_W, :].astype(BF16)
        kpos = past - wbuf + lax.broadcasted_iota(jnp.int32, (1, wbuf), 1)
        s1 = _dot(qbd, kw_t) - slope * (qposf - kpos.astype(F32))
        s1 = jnp.where((kpos <= qpos) & (qpos - kpos < WINDOW) & (kpos >= 0), s1, NEG)
        kn = kwn_ref[:, 0:KV_W].astype(BF16)
        vn = kwn_ref[:, KV_W:2 * KV_W].astype(BF16)
        li = lax.broadcasted_iota(jnp.int32, (1, LANES), 1)
        kposn = past + li
        s2 = _dot_nt(qbd, kn) - slope * (qposf - kposn.astype(F32))
        s2 = jnp.where((li < n_new) & (kposn <= qpos) & (qpos - kposn < WINDOW), s2, NEG)
        m = jnp.maximum(jnp.max(s1, axis=1, keepdims=True), jnp.max(s2, axis=1, keepdims=True))
        p1 = jnp.exp(s1 - m)
        p2 = jnp.exp(s2 - m)
        den = jnp.sum(p1, axis=1, keepdims=True) + jnp.sum(p2, axis=1, keepdims=True)
        inv = 1.0 / den
        ow_ref[...] = (_dot_nt((p1 * inv).astype(BF16), vw_t) + _dot((p2 * inv).astype(BF16), vn))
        m_ref[...] = jnp.full(m_ref.shape, NEG, F32)
        acc_ref[...] = jnp.zeros(acc_ref.shape, F32)

    def block_bias(emat):
        rows = []
        for g in range(N_KV):
            bt = _dot(bias_ref[g], emat)
            rows += [bt] * GROUP
        return jnp.concatenate(rows, axis=0)

    def update(scores, pv_fns):
        m_old = m_ref[...]
        m_new = functools.reduce(jnp.maximum, [m_old] + [jnp.max(s, axis=1, keepdims=True) for s in scores])
        alpha = jnp.exp(m_old - m_new)
        ps = [jnp.exp(s - m_new) for s in scores]
        ones = functools.reduce(jnp.add, [jnp.sum(p, axis=1, keepdims=True) for p in ps])
        pv = functools.reduce(jnp.add, [f(p.astype(BF16)) for f, p in zip(pv_fns, ps)])
        acc_ref[:, 0:KV_W] = alpha * acc_ref[:, 0:KV_W] + pv
        acc_ref[:, KV_W:KV_W + LANES] = alpha * acc_ref[:, KV_W:KV_W + LANES] + ones
        m_ref[...] = m_new

    bias_chunk = block_bias(e_ref[...])
    li = lax.broadcasted_iota(jnp.int32, (1, PAGE), 1)
    scores, pv_fns = [], []
    for k in range(pps):
        pg = pages[k]
        kpos = (t * pps + k) * PAGE + li
        s = _dot(qbd, pg[0:KV_W, :].astype(BF16)) - slope * (qposf - kpos.astype(F32))
        scores.append(s + bias_chunk[:, k * PAGE:(k + 1) * PAGE])
        pv_fns.append(lambda p, pg=pg: _dot_nt(p, pg[KV_W:2 * KV_W, :].astype(BF16)))
    update(scores, pv_fns)

    @pl.when(t == nchunks - 1)
    def _():
        nsp = ms_ref.shape[1]
        kposn = past + li
        e_new = (lax.broadcasted_iota(jnp.int32, (nsp, PAGE), 0) == past // SEL_BLOCK).astype(BF16)
        s = _dot_nt(qbd, ksn_ref[:, 0:KV_W].astype(BF16)) - slope * (qposf - kposn.astype(F32)) + block_bias(e_new)
        s = jnp.where((li < n_new) & (kposn <= qpos), s, NEG)
        update([s], [lambda p: _dot(p, ksn_ref[:, KV_W:2 * KV_W].astype(BF16))])
        o_s = acc_ref[:, 0:KV_W] * (1.0 / acc_ref[:, KV_W:KV_W + 1])
        gts = g_ref[...]
        osw_ref[...] = gts[:, 1:2] * o_s + gts[:, 2:3] * ow_ref[...]
        oc_ref[...] = gts[:, 0:1] * oc_ref[...]


def _attn_sample(page_table, pages, qa, qbd, gates, slope_rows, kc, vc, ms, emat, win, ksn, kwn, *, pps, n_new):
    nb, npages = page_table.shape
    nchunks = npages // pps
    past = npages * PAGE
    wbuf = win.shape[2]
    nc = kc.shape[2]
    nsp = ms.shape[1]

    def page_spec(k):
        return pl.BlockSpec((None, 2 * KV_W, PAGE), lambda b, t, pt: (pt[b, t * pps + k], 0, 0))

    per_b = lambda *shape: pl.BlockSpec((None,) + shape, lambda b, t, pt: (b,) + (0,) * len(shape))
    cs = lambda *shape: pl.BlockSpec(shape, lambda b, t, pt: (0,) * len(shape))
    in_specs = [page_spec(k) for k in range(pps)] + [
        per_b(N_KV, ROWS_G, AUG_W), per_b(ROWS_S, KV_W), per_b(ROWS_S, 3), cs(ROWS_S, 1),
        per_b(N_KV, nc, AUG_W), per_b(N_KV, nc, LANES), cs(nc, nsp),
        pl.BlockSpec((nsp, pps * PAGE), lambda b, t, pt: (0, t)),
        per_b(2 * KV_W, wbuf), per_b(LANES, 2 * KV_W), per_b(LANES, 2 * KV_W)]
    out_specs = [per_b(ROWS_S, KV_W), per_b(ROWS_S, LANES)]
    out_shape = [jax.ShapeDtypeStruct((nb, ROWS_S, KV_W), F32), jax.ShapeDtypeStruct((nb, ROWS_S, LANES), F32)]
    scratch = [pltpu.VMEM((N_KV, Q_PAD, nsp), BF16), pltpu.VMEM((ROWS_S, 1), F32),
               pltpu.VMEM((ROWS_S, KV_W + LANES), F32), pltpu.VMEM((ROWS_S, KV_W), F32)]
    gs = pltpu.PrefetchScalarGridSpec(num_scalar_prefetch=1, grid=(nb, nchunks), in_specs=in_specs,
                                      out_specs=out_specs, scratch_shapes=scratch)
    return pl.pallas_call(functools.partial(_attn_sample_kernel, pps=pps, past=past, wbuf=wbuf, n_new=n_new), grid_spec=gs,
                          out_shape=out_shape, compiler_params=_params(2), name="attn_sample")(
        page_table, *([pages] * pps), qa, qbd, gates, slope_rows, kc, vc, ms, emat, win, ksn, kwn)


def _outproj_kernel(x_ref, a_ref, o_ref, nw_ref, sc_ref, sh_ref, g1_ref, wm_ref, woc_ref, won_ref, wo_ref, x1_ref):
    x = x_ref[...]
    h = _mod_norm(x, nw_ref[...], sc_ref[...], sh_ref[...]).astype(BF16)
    mg = jax.nn.sigmoid(_dot(h, wm_ref[...]))
    y_a = _dot(a_ref[...], woc_ref[...])
    y_b = _dot(o_ref[...], won_ref[...])
    mix = (mg[:, 0:D_MODEL] * y_a + mg[:, D_MODEL:2 * D_MODEL] * y_b).astype(BF16)
    x1_ref[...] = x + g1_ref[...] * _dot(mix, wo_ref[...])


def _outproj(x2d, a, o, mod, nw, w_merge, w_oc, w_on, w_o, *, tm):
    n = x2d.shape[0]
    tok = pl.BlockSpec((tm, D_MODEL), lambda i: (i, 0))
    sq = _const_spec((D_MODEL, D_MODEL))
    return pl.pallas_call(
        _outproj_kernel, grid=(n // tm,),
        in_specs=[tok, tok, tok, _const_spec((1, D_MODEL)), mod.spec(1), mod.spec(0), mod.spec(2),
                  _const_spec((D_MODEL, 2 * D_MODEL)), sq, sq, sq],
        out_specs=tok, out_shape=jax.ShapeDtypeStruct((n, D_MODEL), F32),
        compiler_params=_params(1), name="outproj",
    )(x2d, a, o, nw, mod.rows, mod.rows, mod.rows, w_merge, w_oc, w_on, w_o)


FF_CHUNK = D_FF // 2


def _ffn_kernel(x_ref, nw_ref, sc_ref, sh_ref, g2_ref, nf_ref, wg_ref, wu_ref, wd_ref, y_ref):
    x = x_ref[...]
    h = _mod_norm(x, nw_ref[...], sc_ref[...], sh_ref[...]).astype(BF16)
    acc = jnp.zeros(x.shape, F32)
    for c in range(D_FF // FF_CHUNK):
        sl = slice(c * FF_CHUNK, (c + 1) * FF_CHUNK)
        gate = _dot(h, wg_ref[:, sl])
        up = _dot(h, wu_ref[:, sl])
        act = (gate * jax.nn.sigmoid(gate) * up).astype(BF16)
        acc = acc + _dot(act, wd_ref[sl, :])
    x2 = x + g2_ref[...] * acc
    inv = lax.rsqrt(jnp.mean(x2 * x2, axis=-1, keepdims=True) + EPS)
    y_ref[...] = (x2 * inv) * nf_ref[...]


def _ffn(x1, mod, nw2, nf, w_gate, w_up, w_down, *, tm):
    n = x1.shape[0]
    tok = pl.BlockSpec((tm, D_MODEL), lambda i: (i, 0))
    vec = _const_spec((1, D_MODEL))
    return pl.pallas_call(
        _ffn_kernel, grid=(n // tm,),
        in_specs=[tok, vec, mod.spec(4), mod.spec(3), mod.spec(5), vec,
                  _const_spec((D_MODEL, D_FF)), _const_spec((D_MODEL, D_FF)), _const_spec((D_FF, D_MODEL))],
        out_specs=tok, out_shape=jax.ShapeDtypeStruct((n, D_MODEL), F32),
        compiler_params=_params(1), name="ffn",
    )(x1, nw2, mod.rows, mod.rows, mod.rows, nf, w_gate, w_up, w_down)


def _slopes():
    return 2.0 ** (-8.0 * jnp.arange(1, N_HEADS + 1, dtype=F32) / N_HEADS)


def _slope_lanes(slopes):
    parts = _split3(slopes)
    cols = jnp.stack([parts[0], parts[0], parts[1], parts[1], parts[2], parts[2]], axis=1)
    return jnp.pad(cols, ((0, 0), (0, LANES - 6)))


def _pos_lanes(pos_hi, pos_lo):
    cols = np.stack([pos_hi, pos_lo] * 3, axis=1).astype(np.float32)
    return np.pad(cols, ((0, 0), (0, LANES - 6))).astype(BF16)


def _token_consts(t):
    pos = np.arange(t, dtype=np.int32)
    onehot = (pos[:, None] // SEL_BLOCK == np.arange(HEAD_DIM, dtype=np.int32)[None, :]).astype(np.float32)
    clo = np.concatenate([np.zeros((t, HEAD_DIM), np.float32), onehot], axis=1)
    chi = _pos_lanes((pos // SEL_BLOCK) * SEL_BLOCK, pos % SEL_BLOCK)
    return clo, chi


def _cmp_consts(ncb):
    ci = np.arange(ncb, dtype=np.int32) * CMP_STRIDE
    return _pos_lanes((ci // SEL_BLOCK) * SEL_BLOCK, ci % SEL_BLOCK)


def _ones_lane():
    return (np.arange(LANES) == HEAD_DIM).astype(np.float32).reshape(1, LANES)


def _imp_matrix(nc, ns_pad):
    c = np.arange(nc, dtype=np.int32)[:, None]
    s = np.arange(ns_pad, dtype=np.int32)[None, :]
    per = SEL_BLOCK // CMP_STRIDE
    return ((c // per == s) | ((c % per == per - 1) & (c // per == s - 1))).astype(BF16)


def _prep_weights(w_in, w_phi1, w_phi2, pe_cmp):
    wb = w_in.astype(BF16)
    w_conv_in = wb[:, _C_CONV:_C_QKV]
    w_qkv = jnp.pad(wb[:, _C_QKV:_C_MERGE], ((0, 0), (0, _QKV_COLS_PAD - _QKV_COLS)))
    w_merge = wb[:, _C_MERGE:]
    half = CMP_STRIDE * HEAD_DIM
    w1cat = jnp.concatenate([w_phi1[:, :half], w_phi1[:, half:]], axis=2).astype(BF16)
    w2pad = jnp.pad(w_phi2, ((0, 0), (0, 0), (0, LANES - HEAD_DIM))).astype(BF16)
    pe2 = pe_cmp.reshape(2, 2, half)
    return w_conv_in, w_qkv, w_merge, w1cat, w2pad, pe2


def _prompt_layer(x, mod_p, wts):
    (nw1, nw2, nf, w_conv_in, w_qkv, w_merge, w1cat, w2pad, pe2, w_conv, b_conv, w_oc, w_on, w_o,
     w_gate, w_up, w_down, slopes) = wts
    nb, t, _ = x.shape
    tm = min(512, t)
    x2d = x.reshape(nb * t, D_MODEL)
    mod = _Mod(mod_p, False, t, tm)
    a, tail = _conv_path(x2d, mod, nw1, w_conv_in, w_conv, b_conv, tm=tm, seq_len=t)
    clo, chi = _token_consts(t)
    cv = _ones_lane()
    qt, kvc, kvs, kvw, gts, ksa, vst, kwa, vwt = _qkv_path(x2d, mod, nw1, w_qkv, tm=tm, seq_len=t, consts=(clo, chi))
    npages = t // PAGE
    pt = np.broadcast_to(np.arange(npages, dtype=np.int32), (nb, npages))
    nc = t // CMP_STRIDE
    ns = t // SEL_BLOCK
    nqb = t // Q_BLOCK
    kca, vct = _compress(kvc, pt, pe2, w1cat, w2pad, _cmp_consts(nc), cv,
                         pps=min(16, npages), transpose_v=True, paged=False)
    gates_t = gts[:, :3 * N_HEADS].reshape(nb, nqb, Q_BLOCK, N_KV, GROUP, 3).transpose(0, 3, 1, 5, 4, 2)
    gates_t = gates_t.reshape(nb, N_KV, nqb, 3, GROUP * Q_BLOCK)
    qc_t = _slope_lanes(slopes)[:, :QC_ROWS].reshape(N_KV, GROUP, QC_ROWS).transpose(0, 2, 1)
    qc_t = jnp.repeat(qc_t, Q_BLOCK, axis=2)
    mt = _imp_matrix(nc, ns).T
    o = _attn_prompt_t(qt, gates_t, qc_t, mt, kca, vct, ksa, vst, kwa, vwt)
    x1 = _outproj(x2d, a, o.reshape(nb * t, D_MODEL), mod, nw1, w_merge, w_oc, w_on, w_o, tm=tm)
    y = _ffn(x1, mod, nw2, nf, w_gate, w_up, w_down, tm=tm)
    keep = min(WINDOW, t)
    rows_of = lambda a: a.reshape(nb, 2, N_KV, HEAD_DIM, t).transpose(0, 4, 1, 2, 3)
    state = (rows_of(kvc), rows_of(kvs),
             kvw.reshape(nb, t, 2 * KV_W)[:, t - keep:].reshape(nb, keep, 2, N_KV, HEAD_DIM),
             tail[:, 8 - 2:, :])
    return y.reshape(nb, t, D_MODEL), state


def _sample_layer(x, mod_s, wts, cache_cmp, cache_sel, cache_win, state_conv, page_table):
    (nw1, nw2, nf, w_conv_in, w_qkv, w_merge, w1cat, w2pad, pe2, w_conv, b_conv, w_oc, w_on, w_o,
     w_gate, w_up, w_down, slopes) = wts
    nb, s, _ = x.shape
    n = nb * s
    x2d = x.reshape(n, D_MODEL)
    mod = _Mod(mod_s, True, s, n)
    tpos = jnp.arange(s)
    p1 = jnp.broadcast_to(state_conv[:, 1:2, :], (nb, s, D_MODEL)).reshape(n, D_MODEL)
    p2 = state_conv[:, jnp.minimum(tpos, 1), :].reshape(n, D_MODEL)
    a, u = _conv_path(x2d, mod, nw1, w_conv_in, w_conv, b_conv, tm=n, seq_len=s, prev=(p1, p2))
    qpad, kvc, kvs, kvw, gts = _qkv_path(x2d, mod, nw1, w_qkv, tm=n, seq_len=s)

    npages = page_table.shape[1]
    past = npages * PAGE
    cv = _ones_lane()
    nc = past // CMP_STRIDE
    token_minor = lambda c: c.transpose(0, 2, 3, 4, 1).reshape(c.shape[0], 2 * KV_W, c.shape[1])
    kca, vca = _compress(token_minor(cache_cmp), page_table, pe2, w1cat, w2pad, _cmp_consts(nc), cv,
                         pps=min(32, npages), transpose_v=False, paged=True)

    qh = qpad.reshape(nb, s, N_KV, GROUP, LANES).transpose(0, 2, 3, 1, 4)
    qh = jnp.pad(qh, ((0, 0), (0, 0), (0, 0), (0, Q_PAD - s), (0, 0)))
    sl = jnp.broadcast_to(_slope_lanes(slopes).reshape(1, N_KV, GROUP, 1, LANES), qh.shape)
    qa = jnp.concatenate([qh, sl], axis=-1).reshape(nb, N_KV, ROWS_G, AUG_W)
    eye = jnp.eye(N_KV, dtype=BF16)
    qbd = (qh[..., None, :HEAD_DIM] * eye[None, :, None, None, :, None]).reshape(nb, ROWS_S, KV_W)
    gates = gts[:, :3 * N_HEADS].reshape(nb, s, N_KV, GROUP, 3).transpose(0, 2, 3, 1, 4)
    gates = jnp.pad(gates, ((0, 0), (0, 0), (0, 0), (0, Q_PAD - s), (0, 0))).reshape(nb, ROWS_S, 3)
    slope_rows = jnp.repeat(slopes, Q_PAD).reshape(ROWS_S, 1)
    ns = past // SEL_BLOCK + 1
    nsp = -(-ns // LANES) * LANES
    ms = _imp_matrix(nc, nsp)
    tok = np.arange(past, dtype=np.int32)
    emat = (np.arange(nsp, dtype=np.int32)[:, None] == tok[None, :] // SEL_BLOCK).astype(BF16)
    pad_rows = lambda r: jnp.pad(r.reshape(nb, s, 2 * KV_W), ((0, 0), (0, LANES - s), (0, 0)))
    osw, ocg = _attn_sample(page_table, token_minor(cache_sel), qa, qbd, gates, slope_rows, kca, vca,
                            ms, emat, token_minor(cache_win), pad_rows(kvs), pad_rows(kvw),
                            pps=min(32, npages), n_new=s)
    osw = osw.reshape(nb, N_KV, GROUP, Q_PAD, N_KV, HEAD_DIM)
    o_sw = jnp.einsum('bgrqgd->bqgrd', osw)
    o_c = ocg.reshape(nb, N_KV, GROUP, Q_PAD, LANES)[..., :HEAD_DIM].transpose(0, 3, 1, 2, 4)
    o = (o_sw + o_c)[:, :s].reshape(n, N_HEADS * HEAD_DIM).astype(BF16)

    x1 = _outproj(x2d, a, o, mod, nw1, w_merge, w_oc, w_on, w_o, tm=n)
    y = _ffn(x1, mod, nw2, nf, w_gate, w_up, w_down, tm=n)
    kv5 = lambda r: r.reshape(nb, s, 2, N_KV, HEAD_DIM)
    win = jnp.concatenate([cache_win, kv5(kvw)], axis=1)[:, s:]
    state = (kv5(kvc), kv5(kvs), win, u.reshape(nb, s, D_MODEL)[:, s - 2:])
    return y.reshape(nb, s, D_MODEL), state


def kernel(x_prompt, x_sample, c_prompt, c_sample, cache_cmp, cache_sel, cache_win, state_conv, page_table,
           w_ada, b_ada, norm1, w_in, w_conv, b_conv, w_out_conv, pe_cmp, w_phi1, w_phi2, w_o_nsa, w_out,
           norm2, w_gate, w_up, w_down, norm_f):
    depth = w_ada.shape[0]
    assert depth == 1, "single-layer trunk"
    nbp, nbs = c_prompt.shape[0], c_sample.shape[0]
    slopes = _slopes()
    l = 0
    c_all = jnp.concatenate([c_prompt, c_sample], axis=0)
    c_all = jnp.pad(c_all, ((0, -c_all.shape[0] % 8), (0, 0)))
    mod = _ada(c_all, w_ada[l], b_ada[l])
    w_conv_in, w_qkv, w_merge, w1cat, w2pad, pe2 = _prep_weights(w_in[l], w_phi1[l], w_phi2[l], pe_cmp[l])
    row = lambda v: v.reshape(1, -1)
    wts = (row(norm1[l]), row(norm2[l]), row(norm_f), w_conv_in, w_qkv, w_merge, w1cat, w2pad, pe2,
           w_conv[l], row(b_conv[l]), w_out_conv[l].astype(BF16), w_o_nsa[l].astype(BF16), w_out[l].astype(BF16),
           w_gate[l].astype(BF16), w_up[l].astype(BF16), w_down[l].astype(BF16), slopes)
    yp, st_p = _prompt_layer(x_prompt, mod[:nbp], wts)
    ys, st_s = _sample_layer(x_sample, mod[nbp:nbp + nbs], wts, cache_cmp[l], cache_sel[l], cache_win[l],
                             state_conv[l], page_table)
    return (yp, ys, st_p[0][None], st_p[1][None], st_p[2][None], st_p[3][None],
            st_s[0][None], st_s[1][None], st_s[2][None], st_s[3][None])
```

```python
import functools

import jax
import jax.numpy as jnp
import numpy as np
from jax import lax
from jax.experimental import pallas as pl
from jax.experimental.pallas import tpu as pltpu

F32 = jnp.float32
BF16 = jnp.bfloat16

D_MODEL = 1024
N_HEADS = 16
HEAD_DIM = 64
N_KV = 4
GROUP = N_HEADS // N_KV
KV_W = N_KV * HEAD_DIM
CMP_BLOCK = 32
CMP_STRIDE = 16
SEL_BLOCK = 64
N_SEL = 16
WINDOW = 512
D_PHI = 2 * HEAD_DIM
Q_BLOCK = 256
PAGE = 128
D_FF = ((8 * D_MODEL // 3 + 255) // 256) * 256
EPS = 1e-6
NEG = -1e30
FORCE_BONUS = 1e3
LOG2_E = 1.4426950408889634

LANES = 128
AUG_W = 2 * LANES
HALVES_PER_PAGE = PAGE // CMP_STRIDE
ROW_TILES = 2 * KV_W // LANES
VMEM_LIMIT = 56 * 1024 * 1024
KEY_TILE = 2 * LANES
VT_ROWS = HEAD_DIM + 16
QC_ROWS = 16
SEL_TILES_PER_ITER = 8

_C_CONV = 0
_C_QKV = 3 * D_MODEL
_C_GATE = _C_QKV + N_HEADS * HEAD_DIM + 6 * KV_W
_C_MERGE = _C_GATE + 3 * N_HEADS
_QKV_COLS = _C_MERGE - _C_QKV
_QKV_COLS_PAD = -(-_QKV_COLS // LANES) * LANES


def _dot(a, b):
    return jnp.dot(a, b, preferred_element_type=F32)


def _dot_nt(a, b):
    return lax.dot_general(a, b, (((1,), (1,)), ((), ())), preferred_element_type=F32)


def _params(n_axes):
    return pltpu.CompilerParams(dimension_semantics=("arbitrary",) * n_axes, vmem_limit_bytes=VMEM_LIMIT)


def _const_spec(shape):
    return pl.BlockSpec(shape, lambda *_: (0,) * len(shape))


def _mod_norm(x, nw, sc, sh):
    inv = lax.rsqrt(jnp.mean(x * x, axis=-1, keepdims=True) + EPS)
    return (x * inv) * nw * (1.0 + sc) + sh


def _split3(x):
    a = x.astype(BF16)
    r = x - a.astype(F32)
    b = r.astype(BF16)
    c = (r - b.astype(F32)).astype(BF16)
    return a, b, c


def _ada_kernel(c_ref, w_ref, b_ref, o_ref):
    c = c_ref[...]
    s = c * jax.nn.sigmoid(c)
    o_ref[...] = jnp.dot(s, w_ref[...], preferred_element_type=F32, precision=lax.Precision.HIGHEST) + b_ref[...]


def _ada(c, w_ada, b_ada):
    n = c.shape[0]
    tn = 1536
    return pl.pallas_call(
        _ada_kernel,
        grid=(6 * D_MODEL // tn,),
        in_specs=[_const_spec((n, D_MODEL)),
                  pl.BlockSpec((D_MODEL, tn), lambda j: (0, j)),
                  pl.BlockSpec((1, tn), lambda j: (0, j))],
        out_specs=pl.BlockSpec((n, tn), lambda j: (0, j)),
        out_shape=jax.ShapeDtypeStruct((n, 6 * D_MODEL), F32),
        compiler_params=_params(1),
        name="ada",
    )(c, w_ada, b_ada.reshape(1, -1))


class _Mod:
    def __init__(self, mod, per_token, seq_len, tm):
        self.per_token = per_token
        if per_token:
            self.rows = jnp.repeat(mod, seq_len, axis=0)
        else:
            self.rows = mod.reshape(mod.shape[0], 1, 6 * D_MODEL)
        self.tiles_per_seq = None if per_token else seq_len // tm
        self.tm = tm

    def spec(self, k):
        if self.per_token:
            return pl.BlockSpec((self.tm, D_MODEL), lambda i: (i, k))
        tps = self.tiles_per_seq
        return pl.BlockSpec((None, 1, D_MODEL), lambda i: (i // tps, 0, k))


def _conv_kernel(*refs, carry_rows, seq_len):
    if carry_rows:
        (x_ref, nw_ref, sc_ref, sh_ref, w_ref, wc_ref, bc_ref, a_ref, tail_ref, carry_ref) = refs
    else:
        (x_ref, nw_ref, sc_ref, sh_ref, w_ref, wc_ref, bc_ref, p1_ref, p2_ref, a_ref, tail_ref) = refs
    tm = x_ref.shape[0]
    h = _mod_norm(x_ref[...], nw_ref[...], sc_ref[...], sh_ref[...]).astype(BF16)
    z = _dot(h, w_ref[...])
    bg = z[:, 0:D_MODEL]
    u = z[:, D_MODEL:2 * D_MODEL] * z[:, 2 * D_MODEL:3 * D_MODEL]
    row = lax.broadcasted_iota(jnp.int32, (tm, 1), 0)
    u1 = pltpu.roll(u, 1, 0)
    u2 = pltpu.roll(u, 2, 0)
    if carry_rows:
        @pl.when(pl.program_id(0) % carry_rows == 0)
        def _():
            carry_ref[...] = jnp.zeros_like(carry_ref)
        c0 = carry_ref[0:1, :]
        c1 = carry_ref[1:2, :]
        u1 = jnp.where(row == 0, c1, u1)
        u2 = jnp.where(row == 0, c0, jnp.where(row == 1, c1, u2))
        carry_ref[0:2, :] = u[tm - 2:tm, :]
        tail_ref[...] = u[tm - 8:tm, :]
    else:
        pos = lax.rem(row, seq_len)
        u1 = jnp.where(pos >= 1, u1, p1_ref[...])
        u2 = jnp.where(pos >= 2, u2, p2_ref[...])
        tail_ref[...] = u
    v = bc_ref[...] + wc_ref[0:1, :] * u2 + wc_ref[1:2, :] * u1 + wc_ref[2:3, :] * u
    a_ref[...] = (bg * v).astype(BF16)


def _conv_path(x2d, mod, nw, w_conv_in, w_conv, b_conv, *, tm, seq_len, prev=None):
    n = x2d.shape[0]
    tok = pl.BlockSpec((tm, D_MODEL), lambda i: (i, 0))
    in_specs = [tok, _const_spec((1, D_MODEL)), mod.spec(1), mod.spec(0),
                _const_spec((D_MODEL, 3 * D_MODEL)), _const_spec((3, D_MODEL)), _const_spec((1, D_MODEL))]
    args = [x2d, nw, mod.rows, mod.rows, w_conv_in, w_conv, b_conv]
    if prev is None:
        tps = seq_len // tm
        out_specs = [tok, pl.BlockSpec((None, 8, D_MODEL), lambda i: (i // tps, 0, 0))]
        out_shape = [jax.ShapeDtypeStruct((n, D_MODEL), BF16), jax.ShapeDtypeStruct((n // seq_len, 8, D_MODEL), F32)]
        scratch = [pltpu.VMEM((8, D_MODEL), F32)]
        kern = functools.partial(_conv_kernel, carry_rows=tps, seq_len=seq_len)
    else:
        in_specs += [tok, tok]
        args += list(prev)
        out_specs = [tok, tok]
        out_shape = [jax.ShapeDtypeStruct((n, D_MODEL), BF16), jax.ShapeDtypeStruct((n, D_MODEL), F32)]
        scratch = []
        kern = functools.partial(_conv_kernel, carry_rows=0, seq_len=seq_len)
    return pl.pallas_call(kern, grid=(n // tm,), in_specs=in_specs, out_specs=out_specs, out_shape=out_shape,
                          scratch_shapes=scratch, compiler_params=_params(1), name="conv_path")(*args)


def _qkv_kernel(*refs, aug):
    if aug:
        (x_ref, nw_ref, sc_ref, sh_ref, w_ref, clo_ref, chi_ref,
         q_ref, kvc_ref, kvs_ref, kvw_ref, g_ref, ksa_ref, vsa_ref, kwa_ref, vwa_ref) = refs
    else:
        (x_ref, nw_ref, sc_ref, sh_ref, w_ref, q_ref, kvc_ref, kvs_ref, kvw_ref, g_ref) = refs
    tm = x_ref.shape[0]
    h = _mod_norm(x_ref[...], nw_ref[...], sc_ref[...], sh_ref[...]).astype(BF16)
    z = _dot(h, w_ref[...])
    low = lax.broadcasted_iota(jnp.int32, (tm, LANES), 1) < HEAD_DIM
    nq = N_HEADS * HEAD_DIM
    for c in range(N_HEADS // 2):
        t = z[:, c * LANES:(c + 1) * LANES] * (HEAD_DIM ** -0.5 * (LOG2_E if aug else 1.0))
        if aug:
            tt = t.T.astype(BF16)
            q_ref[2 * c] = tt[0:HEAD_DIM]
            q_ref[2 * c + 1] = tt[HEAD_DIM:2 * HEAD_DIM]
        else:
            q_ref[:, (2 * c) * LANES:(2 * c + 1) * LANES] = jnp.where(low, t, 0.0).astype(BF16)
            q_ref[:, (2 * c + 1) * LANES:(2 * c + 2) * LANES] = jnp.where(low, pltpu.roll(t, HEAD_DIM, 1), 0.0).astype(BF16)
    if aug:
        kvc_ref[...] = z[:, nq:nq + 2 * KV_W].T
        kvs_ref[...] = z[:, nq + 2 * KV_W:nq + 4 * KV_W].T
    else:
        kvc_ref[...] = z[:, nq:nq + 2 * KV_W]
        kvs_ref[...] = z[:, nq + 2 * KV_W:nq + 4 * KV_W]
    kvw_ref[...] = z[:, nq + 4 * KV_W:nq + 6 * KV_W]
    g_ref[...] = jax.nn.sigmoid(z[:, nq + 6 * KV_W:nq + 6 * KV_W + LANES])
    if aug:
        chi = chi_ref[...]
        ones_row = (lax.broadcasted_iota(jnp.int32, (VT_ROWS - HEAD_DIM, KEY_TILE), 0) == 0).astype(BF16)
        for br, (ka_ref, va_ref) in enumerate(((ksa_ref, vsa_ref), (kwa_ref, vwa_ref))):
            kbase = nq + 2 * KV_W * (br + 1)
            clo = clo_ref[...] if br == 0 else 0.0
            for g in range(N_KV):
                kt = z[:, kbase + (g // 2) * LANES:kbase + (g // 2 + 1) * LANES]
                if g % 2:
                    kt = pltpu.roll(kt, HEAD_DIM, 1)
                ka_ref[g, :, 0:LANES] = jnp.where(low, kt, clo).astype(BF16)
                ka_ref[g, :, LANES:AUG_W] = chi
            for c in range(N_KV // 2):
                vt = z[:, kbase + KV_W + c * LANES:kbase + KV_W + (c + 1) * LANES].T.astype(BF16)
                for gg in range(2):
                    for j in range(tm // KEY_TILE):
                        va_ref[2 * c + gg, j, 0:HEAD_DIM, :] = vt[gg * HEAD_DIM:(gg + 1) * HEAD_DIM, j * KEY_TILE:(j + 1) * KEY_TILE]
                        va_ref[2 * c + gg, j, HEAD_DIM:VT_ROWS, :] = ones_row


def _qkv_path(x2d, mod, nw, w_qkv, *, tm, seq_len, consts=None):
    n = x2d.shape[0]
    aug = consts is not None
    tok = lambda w: pl.BlockSpec((tm, w), lambda i: (i, 0))
    in_specs = [tok(D_MODEL), _const_spec((1, D_MODEL)), mod.spec(1), mod.spec(0),
                _const_spec((D_MODEL, _QKV_COLS_PAD))]
    args = [x2d, nw, mod.rows, mod.rows, w_qkv]
    out_specs = [tok(N_HEADS * LANES), tok(2 * KV_W), tok(2 * KV_W), tok(2 * KV_W), tok(LANES)]
    out_shape = [jax.ShapeDtypeStruct((n, N_HEADS * LANES), BF16)] + \
                [jax.ShapeDtypeStruct((n, 2 * KV_W), F32)] * 3 + [jax.ShapeDtypeStruct((n, LANES), F32)]
    if aug:
        tps = seq_len // tm
        nb = n // seq_len
        pos = lambda w: pl.BlockSpec((tm, w), lambda i: (i % tps, 0))
        in_specs += [pos(LANES), pos(LANES)]
        args += list(consts)
        out_specs[0] = pl.BlockSpec((None, N_HEADS, HEAD_DIM, tm), lambda i: (i // tps, 0, 0, i % tps))
        out_shape[0] = jax.ShapeDtypeStruct((nb, N_HEADS, HEAD_DIM, seq_len), BF16)
        for k in (1, 2):
            out_specs[k] = pl.BlockSpec((None, 2 * KV_W, tm), lambda i: (i // tps, 0, i % tps))
            out_shape[k] = jax.ShapeDtypeStruct((nb, 2 * KV_W, seq_len), F32)
        ka = pl.BlockSpec((None, N_KV, tm, AUG_W), lambda i: (i // tps, 0, i % tps, 0))
        va = pl.BlockSpec((None, N_KV, tm // KEY_TILE, VT_ROWS, KEY_TILE), lambda i: (i // tps, 0, i % tps, 0, 0))
        out_specs += [ka, va, ka, va]
        ka_s = jax.ShapeDtypeStruct((nb, N_KV, seq_len, AUG_W), BF16)
        va_s = jax.ShapeDtypeStruct((nb, N_KV, seq_len // KEY_TILE, VT_ROWS, KEY_TILE), BF16)
        out_shape += [ka_s, va_s, ka_s, va_s]
    return pl.pallas_call(functools.partial(_qkv_kernel, aug=aug), grid=(n // tm,), in_specs=in_specs,
                          out_specs=out_specs, out_shape=out_shape, compiler_params=_params(1), name="qkv_path")(*args)


def _block_major_loader(ref, k, c):
    def load(p, nrows):
        return ref[k, c, p * HALVES_PER_PAGE:p * HALVES_PER_PAGE + nrows, :]
    return load


def _gather_pair(load, nrows):
    low = lax.broadcasted_iota(jnp.int32, (nrows, LANES), 1) < HEAD_DIM
    even, odd = [], []
    for qq in range(CMP_STRIDE // 2):
        a = load(2 * qq, nrows)
        b = load(2 * qq + 1, nrows)
        even.append(jnp.where(low, a, pltpu.roll(b, HEAD_DIM, 1)))
        odd.append(jnp.where(low, pltpu.roll(a, HEAD_DIM, 1), b))
    return jnp.concatenate(even, axis=1), jnp.concatenate(odd, axis=1)


def _compress_kernel(pt_ref, *refs, pps, transpose_v):
    pages = refs[:pps]
    nxt_ref, pe_ref, w1_ref, w2_ref, chi_ref, cv_ref, perm_ref, kc_ref, vc_ref, xt_ref = refs[pps:]
    perm = perm_ref[...]
    for k, pg in enumerate(list(pages) + [nxt_ref]):
        for c in range(ROW_TILES):
            xt_ref[k, c] = _dot_nt(perm, pg[c * LANES:(c + 1) * LANES, :].astype(BF16))
    loaders = lambda c: [_block_major_loader(xt_ref, k, c) for k in range(pps)]
    look_loader = lambda c: _block_major_loader(xt_ref, pps, c)
    t = pl.program_id(1)
    last = t == pl.num_programs(1) - 1
    nhb = pps * HALVES_PER_PAGE
    n = N_KV * nhb
    row = lax.broadcasted_iota(jnp.int32, (n, 1), 0)
    chi = chi_ref[...]
    for kv in range(2):
        by_group, look = [], []
        for cc in range(N_KV // 2):
            c = kv * (N_KV // 2) + cc
            pairs = [_gather_pair(ld, HALVES_PER_PAGE) for ld in loaders(c)]
            by_group += [[ev for ev, _ in pairs], [od for _, od in pairs]]
            look += list(_gather_pair(look_loader(c), 1))
        parts = [x for group in by_group for x in group]
        extra = jnp.concatenate([pe_ref[kv], jnp.zeros((2, CMP_STRIDE * HEAD_DIM), F32)] + look, axis=0)
        xmat = jnp.concatenate(parts + [extra], axis=0).astype(BF16)
        hab = _dot(xmat, w1_ref[kv])
        ha = hab[0:n, 0:D_PHI]
        hb = hab[0:n, D_PHI:2 * D_PHI]
        pbias = hab[n:n + 1, 0:D_PHI] + hab[n + 1:n + 2, D_PHI:2 * D_PHI]
        hbn = pltpu.roll(hb, n - 1, 0)
        for g in range(N_KV):
            la = jnp.where(last, 0.0, hab[n + 4 + g:n + 5 + g, D_PHI:2 * D_PHI])
            hbn = jnp.where(row == g * nhb + nhb - 1, la, hbn)
        act = jax.nn.gelu(ha + hbn + pbias).astype(BF16)
        out = _dot(act, w2_ref[kv])
        for g in range(N_KV):
            blk = out[g * nhb:(g + 1) * nhb, :]
            if kv == 0:
                kc_ref[g, :, 0:LANES] = blk.astype(BF16)
                kc_ref[g, :, LANES:AUG_W] = chi
            elif transpose_v:
                vc_ref[g, :, :] = (blk + cv_ref[...]).T.astype(BF16)
            else:
                vc_ref[g, :, :] = (blk + cv_ref[...]).astype(BF16)


def _compress(pages, page_table, pe2, w1cat, w2pad, chi_c, cv, *, pps, transpose_v, paged):
    nb, npages = page_table.shape
    nchunks = npages // pps
    nhb = pps * HALVES_PER_PAGE
    ncb = npages * HALVES_PER_PAGE
    where = (lambda b, j: (j, 0, 0)) if paged else (lambda b, j: (b, 0, j))

    def page_spec(k):
        return pl.BlockSpec((None, 2 * KV_W, PAGE), lambda b, t, pt: where(b, pt[b, t * pps + k]))

    nxt_spec = pl.BlockSpec((None, 2 * KV_W, PAGE),
                            lambda b, t, pt: where(b, pt[b, jnp.minimum((t + 1) * pps, npages - 1)]))
    scratch = [pltpu.VMEM((pps + 1, ROW_TILES, PAGE, LANES), F32)]
    r = np.arange(PAGE, dtype=np.int32)
    perm = (r[None, :] == (CMP_STRIDE * (r % HALVES_PER_PAGE) + r // HALVES_PER_PAGE)[:, None]).astype(BF16)
    cs = lambda shape: pl.BlockSpec(shape, lambda b, t, pt: (0,) * len(shape))
    in_specs = [page_spec(k) for k in range(pps)] + [
        nxt_spec, cs((2, 2, CMP_STRIDE * HEAD_DIM)), cs((2, CMP_STRIDE * HEAD_DIM, 2 * D_PHI)),
        cs((2, D_PHI, LANES)), pl.BlockSpec((nhb, LANES), lambda b, t, pt: (t, 0)), cs((1, LANES)),
        cs((PAGE, PAGE))]
    if transpose_v:
        v_spec = pl.BlockSpec((None, N_KV, LANES, nhb), lambda b, t, pt: (b, 0, 0, t))
        v_shape = jax.ShapeDtypeStruct((nb, N_KV, LANES, ncb), BF16)
    else:
        v_spec = pl.BlockSpec((None, N_KV, nhb, LANES), lambda b, t, pt: (b, 0, t, 0))
        v_shape = jax.ShapeDtypeStruct((nb, N_KV, ncb, LANES), BF16)
    out_specs = [pl.BlockSpec((None, N_KV, nhb, AUG_W), lambda b, t, pt: (b, 0, t, 0)), v_spec]
    out_shape = [jax.ShapeDtypeStruct((nb, N_KV, ncb, AUG_W), BF16), v_shape]
    gs = pltpu.PrefetchScalarGridSpec(num_scalar_prefetch=1, grid=(nb, nchunks), in_specs=in_specs, out_specs=out_specs,
                                      scratch_shapes=scratch)
    return pl.pallas_call(functools.partial(_compress_kernel, pps=pps, transpose_v=transpose_v),
                          grid_spec=gs, out_shape=out_shape,
                          compiler_params=_params(2), name="compress")(
        page_table, *([pages] * (pps + 1)), pe2, w1cat, w2pad, chi_c, cv, perm)


REMOVED = -3e38


def _top_k_mask(score, index, k, axis):
    work = score
    selected = jnp.zeros(score.shape, jnp.bool_)
    for _ in range(k):
        best = jnp.max(work, axis=axis, keepdims=True)
        first = jnp.min(jnp.where(work == best, index, score.shape[axis]), axis=axis, keepdims=True)
        hit = index == first
        selected = selected | hit
        work = jnp.where(hit, REMOVED, work)
    return selected


def _attn_prompt_t_kernel(q_ref, g_ref, qc_ref, mt_ref, kc_ref, vc_ref, ks_ref, vs_ref, kw_ref, vw_ref, o_ref,
                          qa_ref, qw_ref, m_ref, acc_ref):
    nc = kc_ref.shape[0]
    ns = mt_ref.shape[0]
    kt = vs_ref.shape[2]
    ncol = GROUP * Q_BLOCK
    qb = pl.program_id(2)
    q0 = qb * Q_BLOCK
    qpos = q0 + (lax.broadcasted_iota(jnp.int32, (1, ncol), 1) & (Q_BLOCK - 1))

    for ref in (qw_ref, qa_ref):
        for r in range(GROUP):
            ref[0:HEAD_DIM, r * Q_BLOCK:(r + 1) * Q_BLOCK] = q_ref[r]
        ref[HEAD_DIM:2 * HEAD_DIM, :] = jnp.zeros((HEAD_DIM, ncol), BF16)
        ref[2 * HEAD_DIM:2 * HEAD_DIM + QC_ROWS, :] = qc_ref[...]
        ref[2 * HEAD_DIM + QC_ROWS:AUG_W, :] = jnp.zeros((AUG_W - 2 * HEAD_DIM - QC_ROWS, ncol), BF16)

    s = _dot(kc_ref[...], qw_ref[...])
    ci = lax.broadcasted_iota(jnp.int32, (nc, 1), 0)
    last_valid = lax.shift_right_arithmetic(qpos - (CMP_BLOCK - 1), CMP_STRIDE.bit_length() - 1)
    c_valid = ci <= last_valid
    s = jnp.where(c_valid, s, NEG)
    e = jnp.exp2(s - jnp.max(s, axis=0, keepdims=True))
    inv = jnp.where(last_valid >= 0, 1.0 / jnp.sum(e, axis=0, keepdims=True), 0.0)
    p = e * inv
    o_c = _dot(vc_ref[...], p.astype(BF16))[0:HEAD_DIM]

    psum = p[:, 0:Q_BLOCK]
    for r in range(1, GROUP):
        psum = psum + p[:, r * Q_BLOCK:(r + 1) * Q_BLOCK]
    mt = mt_ref[...]
    imp = sum(_dot(mt, part) for part in _split3(psum))
    si = lax.broadcasted_iota(jnp.int32, (ns, Q_BLOCK), 0)
    qpos_t = q0 + lax.broadcasted_iota(jnp.int32, (ns, Q_BLOCK), 1)
    cur = lax.shift_right_logical(qpos_t, 6)
    s_valid = si * SEL_BLOCK <= qpos_t
    forced = (si == 0) | (si == cur) | (si == cur - 1)
    score = jnp.where(s_valid, imp + jnp.where(forced, FORCE_BONUS, 0.0), NEG)
    selected = _top_k_mask(score, si, min(N_SEL, ns), axis=0)
    bias_t = jnp.where(selected, 0.0, NEG).astype(BF16)
    qa_ref[HEAD_DIM:HEAD_DIM + ns, :] = jnp.concatenate([bias_t] * GROUP, axis=1)

    t_hi = lax.div(q0, kt)
    key_iota = lax.broadcasted_iota(jnp.int32, (kt, 1), 0)

    n_back = WINDOW // kt
    s_w, v_w, live_w = [], [], []
    for j in range(n_back + 1):
        tw = t_hi - n_back + j
        tc = jnp.maximum(tw, 0)
        kpos = tw * kt + key_iota
        s = _dot(kw_ref[pl.ds(pl.multiple_of(tc * kt, kt), kt), :], qw_ref[...])
        live = None
        if j == 0:
            oldest = jnp.where(tw >= 0, qpos - WINDOW, jnp.iinfo(jnp.int32).max)
            s = jnp.where(kpos > oldest, s, NEG)
        elif j == n_back:
            s = jnp.where(kpos <= qpos, s, NEG)
        else:
            live = tw >= 0
        s_w.append(s)
        v_w.append(vw_ref[tc])
        live_w.append(live)
    tile_max = [jnp.max(s, axis=0, keepdims=True) for s in s_w]
    tile_max = [mx if live is None else jnp.where(live, mx, NEG) for mx, live in zip(tile_max, live_w)]
    m_w = functools.reduce(jnp.maximum, tile_max)
    shift_w = [m_w if live is None else jnp.where(live, m_w, -NEG) for live in live_w]
    acc_w = sum(_dot(v, jnp.exp2(s - sh).astype(BF16)) for s, v, sh in zip(s_w, v_w, shift_w))
    o_w = acc_w[0:HEAD_DIM] * (1.0 / acc_w[HEAD_DIM:HEAD_DIM + 1])

    m_ref[...] = jnp.full(m_ref.shape, NEG, F32)
    acc_ref[...] = jnp.zeros(acc_ref.shape, F32)

    def sel_update(runs, causal_last):
        qa = qa_ref[...]
        ss, vts = [], []
        for ri, (t0, n) in enumerate(runs):
            k0 = pl.multiple_of(t0 * kt, kt)
            s = _dot(ks_ref[pl.ds(k0, n * kt), :], qa)
            for i in range(n):
                si_ = s[i * kt:(i + 1) * kt]
                if causal_last and ri == len(runs) - 1 and i == n - 1:
                    si_ = jnp.where(k0 + i * kt + key_iota <= qpos, si_, NEG)
                ss.append(si_)
                vts.append(vs_ref[t0 + i])
        m_old = m_ref[...]
        m_new = functools.reduce(jnp.maximum, [m_old] + [jnp.max(s, axis=0, keepdims=True) for s in ss])
        pv = sum(_dot(vt, jnp.exp2(s - m_new).astype(BF16)) for vt, s in zip(vts, ss))
        acc_ref[...] = jnp.exp2(m_old - m_new) * acc_ref[...] + pv
        m_ref[...] = m_new

    per_tile = kt // SEL_BLOCK
    lo_blk = jnp.min(jnp.where(selected & (si >= per_tile), si, ns))
    start = jnp.clip(lax.div(lo_blk, per_tile), 1, jnp.maximum(t_hi, 1))
    n_plain = jnp.maximum(t_hi - start, 0)
    n_group = lax.div(n_plain, SEL_TILES_PER_ITER)

    def body(i, carry):
        sel_update([(start + i * SEL_TILES_PER_ITER, SEL_TILES_PER_ITER)], False)
        return carry

    lax.fori_loop(0, n_group, body, 0)
    rem = n_plain - n_group * SEL_TILES_PER_ITER
    for left in range(SEL_TILES_PER_ITER):
        @pl.when((rem == left) & (t_hi >= 1))
        def _():
            sel_update([(0, 1), (t_hi - left, left + 1)], True)

    @pl.when(t_hi == 0)
    def _():
        sel_update([(0, 1)], True)

    acc = acc_ref[...]
    o_s = acc[0:HEAD_DIM] * (1.0 / acc[HEAD_DIM:HEAD_DIM + 1])

    gts = g_ref[...]
    o = gts[0:1] * o_c + gts[1:2] * o_s + gts[2:3] * o_w
    for c in range(GROUP // 2):
        pair = jnp.concatenate([o[:, (2 * c) * Q_BLOCK:(2 * c + 1) * Q_BLOCK],
                                o[:, (2 * c + 1) * Q_BLOCK:(2 * c + 2) * Q_BLOCK]], axis=0)
        o_ref[:, c * LANES:(c + 1) * LANES] = pair.T.astype(BF16)


def _attn_prompt_t(qt, gates_t, qc_t, mt, kc, vct, ks, vst, kw, vwt):
    nb, t = qt.shape[0], qt.shape[3]
    nc = kc.shape[2]
    ns = mt.shape[0]
    ntile, kt = vst.shape[2], vst.shape[4]
    ncol = GROUP * Q_BLOCK
    per_bg = lambda *shape: pl.BlockSpec((None, None) + shape, lambda b, g, i: (b, g) + (0,) * len(shape))
    in_specs = [pl.BlockSpec((None, GROUP, HEAD_DIM, Q_BLOCK), lambda b, g, i: (b, g, 0, i)),
                pl.BlockSpec((None, None, None, 3, ncol), lambda b, g, i: (b, g, i, 0, 0)),
                pl.BlockSpec((None, QC_ROWS, ncol), lambda b, g, i: (g, 0, 0)),
                pl.BlockSpec((ns, nc), lambda b, g, i: (0, 0)),
                per_bg(nc, AUG_W), per_bg(LANES, nc), per_bg(t, AUG_W), per_bg(ntile, VT_ROWS, kt),
                per_bg(t, AUG_W), per_bg(ntile, VT_ROWS, kt)]
    out_spec = pl.BlockSpec((None, Q_BLOCK, GROUP * HEAD_DIM), lambda b, g, i: (b, i, g))
    scratch = [pltpu.VMEM((AUG_W, ncol), BF16), pltpu.VMEM((AUG_W, ncol), BF16),
               pltpu.VMEM((1, ncol), F32), pltpu.VMEM((VT_ROWS, ncol), F32)]
    return pl.pallas_call(_attn_prompt_t_kernel, grid=(nb, N_KV, t // Q_BLOCK),
                          in_specs=in_specs, out_specs=out_spec,
                          out_shape=jax.ShapeDtypeStruct((nb, t, N_HEADS * HEAD_DIM), BF16),
                          scratch_shapes=scratch, compiler_params=_params(3), name="attn_prompt")(
        qt, gates_t, qc_t, mt, kc, vct, ks, vst, kw, vwt)


Q_PAD = 8
ROWS_G = GROUP * Q_PAD
ROWS_S = N_KV * ROWS_G


def _attn_sample_kernel(pt_ref, *refs, pps, past, wbuf, n_new):
    pages = refs[:pps]
    (qa_ref, qbd_ref, g_ref, slope_ref, kc_ref, vc_ref, ms_ref, e_ref, win_ref, ksn_ref, kwn_ref,
     osw_ref, oc_ref, bias_ref, m_ref, acc_ref, ow_ref) = refs[pps:]
    t = pl.program_id(1)
    nchunks = pl.num_programs(1)
    nc = kc_ref.shape[1]
    ns = past // SEL_BLOCK + 1
    row = lax.broadcasted_iota(jnp.int32, (ROWS_S, 1), 0)
    qpos = past + (row & (Q_PAD - 1))
    qposf = qpos.astype(F32)
    slope = slope_ref[...]
    qbd = qbd_ref[...]

    @pl.when(t == 0)
    def _():
        nsp = ms_ref.shape[1]
        rg = lax.broadcasted_iota(jnp.int32, (ROWS_G, 1), 0)
        qpos_g = past + (rg & (Q_PAD - 1))
        r8 = lax.broadcasted_iota(jnp.int32, (Q_PAD, 1), 0)
        qpos8 = past + r8
        si = lax.broadcasted_iota(jnp.int32, (Q_PAD, nsp), 1)
        scores = []
        for g in range(N_KV):
            s = _dot_nt(qa_ref[g], kc_ref[g])
            ci = lax.broadcasted_iota(jnp.int32, (1, nc), 1)
            c_valid = ci * CMP_STRIDE + (CMP_BLOCK - 1) <= qpos_g
            s = jnp.where(c_valid, s, NEG)
            e = jnp.exp(s - jnp.max(s, axis=1, keepdims=True))
            p = jnp.where(c_valid, e * (1.0 / jnp.sum(e, axis=1, keepdims=True)), 0.0)
            oc_ref[g * ROWS_G:(g + 1) * ROWS_G, :] = _dot(p.astype(BF16), vc_ref[g])
            psum = p[0:Q_PAD]
            for r in range(1, GROUP):
                psum = psum + p[r * Q_PAD:(r + 1) * Q_PAD]
            ms = ms_ref[...]
            imp = sum(_dot(part, ms) for part in _split3(psum))
            cur = lax.shift_right_logical(qpos8, 6)
            s_valid = (si * SEL_BLOCK <= qpos8) & (si < ns)
            forced = (si == 0) | (si == cur) | (si == cur - 1)
            score = jnp.where(s_valid, imp + jnp.where(forced, FORCE_BONUS, 0.0), NEG)
            scores.append(score)
        score = jnp.concatenate(scores, axis=0)
        si4 = lax.broadcasted_iota(jnp.int32, score.shape, 1)
        rank = jnp.zeros(score.shape, jnp.int32)
        for sp in range(ns):
            other = score[:, sp:sp + 1]
            beats = (other > score) | ((other == score) & (si4 > sp))
            rank = rank + beats.astype(jnp.int32)
        bias_all = jnp.where((rank < min(N_SEL, ns)) & (si4 < ns), 0.0, NEG)
        for g in range(N_KV):
            bias_ref[g] = bias_all[g * Q_PAD:(g + 1) * Q_PAD].astype(BF16)

        kw_t = win_ref[0:KV_W, :].astype(BF16)
        vw_t = win_ref[KV_W:2 * KV_W, :].astype(BF16)
        kpos = past - wbuf + lax.broadcasted_iota(jnp.int32, (1, wbuf), 1)
        s1 = _dot(qbd, kw_t) - slope * (qposf - kpos.astype(F32))
        s1 = jnp.where((kpos <= qpos) & (qpos - kpos < WINDOW) & (kpos >= 0), s1, NEG)
        kn = kwn_ref[:, 0:KV_W].astype(BF16)
        vn = kwn_ref[:, KV_W:2 * KV_W].astype(BF16)
        li = lax.broadcasted_iota(jnp.int32, (1, LANES), 1)
        kposn = past + li
        s2 = _dot_nt(qbd, kn) - slope * (qposf - kposn.astype(F32))
        s2 = jnp.where((li < n_new) & (kposn <= qpos) & (qpos - kposn < WINDOW), s2, NEG)
        m = jnp.maximum(jnp.max(s1, axis=1, keepdims=True), jnp.max(s2, axis=1, keepdims=True))
        p1 = jnp.exp(s1 - m)
        p2 = jnp.exp(s2 - m)
        den = jnp.sum(p1, axis=1, keepdims=True) + jnp.sum(p2, axis=1, keepdims=True)
        inv = 1.0 / den
        ow_ref[...] = (_dot_nt((p1 * inv).astype(BF16), vw_t) + _dot((p2 * inv).astype(BF16), vn))
        m_ref[...] = jnp.full(m_ref.shape, NEG, F32)
        acc_ref[...] = jnp.zeros(acc_ref.shape, F32)

    def block_bias(emat):
        rows = []
        for g in range(N_KV):
            bt = _dot(bias_ref[g], emat)
            rows += [bt] * GROUP
        return jnp.concatenate(rows, axis=0)

    def update(scores, pv_fns):
        m_old = m_ref[...]
        m_new = functools.reduce(jnp.maximum, [m_old] + [jnp.max(s, axis=1, keepdims=True) for s in scores])
        alpha = jnp.exp(m_old - m_new)
        ps = [jnp.exp(s - m_new) for s in scores]
        ones = functools.reduce(jnp.add, [jnp.sum(p, axis=1, keepdims=True) for p in ps])
        pv = functools.reduce(jnp.add, [f(p.astype(BF16)) for f, p in zip(pv_fns, ps)])
        acc_ref[:, 0:KV_W] = alpha * acc_ref[:, 0:KV_W] + pv
        acc_ref[:, KV_W:KV_W + LANES] = alpha * acc_ref[:, KV_W:KV_W + LANES] + ones
        m_ref[...] = m_new

    bias_chunk = block_bias(e_ref[...])
    li = lax.broadcasted_iota(jnp.int32, (1, PAGE), 1)
    scores, pv_fns = [], []
    for k in range(pps):
        pg = pages[k]
        kpos = (t * pps + k) * PAGE + li
        s = _dot(qbd, pg[0:KV_W, :].astype(BF16)) - slope * (qposf - kpos.astype(F32))
        scores.append(s + bias_chunk[:, k * PAGE:(k + 1) * PAGE])
        pv_fns.append(lambda p, pg=pg: _dot_nt(p, pg[KV_W:2 * KV_W, :].astype(BF16)))
    update(scores, pv_fns)

    @pl.when(t == nchunks - 1)
    def _():
        nsp = ms_ref.shape[1]
        kposn = past + li
        e_new = (lax.broadcasted_iota(jnp.int32, (nsp, PAGE), 0) == past // SEL_BLOCK).astype(BF16)
        s = _dot_nt(qbd, ksn_ref[:, 0:KV_W].astype(BF16)) - slope * (qposf - kposn.astype(F32)) + block_bias(e_new)
        s = jnp.where((li < n_new) & (kposn <= qpos), s, NEG)
        update([s], [lambda p: _dot(p, ksn_ref[:, KV_W:2 * KV_W].astype(BF16))])
        o_s = acc_ref[:, 0:KV_W] * (1.0 / acc_ref[:, KV_W:KV_W + 1])
        gts = g_ref[...]
        osw_ref[...] = gts[:, 1:2] * o_s + gts[:, 2:3] * ow_ref[...]
        oc_ref[...] = gts[:, 0:1] * oc_ref[...]


def _attn_sample(page_table, pages, qa, qbd, gates, slope_rows, kc, vc, ms, emat, win, ksn, kwn, *, pps, n_new):
    nb, npages = page_table.shape
    nchunks = npages // pps
    past = npages * PAGE
    wbuf = win.shape[2]
    nc = kc.shape[2]
    nsp = ms.shape[1]

    def page_spec(k):
        return pl.BlockSpec((None, 2 * KV_W, PAGE), lambda b, t, pt: (pt[b, t * pps + k], 0, 0))

    per_b = lambda *shape: pl.BlockSpec((None,) + shape, lambda b, t, pt: (b,) + (0,) * len(shape))
    cs = lambda *shape: pl.BlockSpec(shape, lambda b, t, pt: (0,) * len(shape))
    in_specs = [page_spec(k) for k in range(pps)] + [
        per_b(N_KV, ROWS_G, AUG_W), per_b(ROWS_S, KV_W), per_b(ROWS_S, 3), cs(ROWS_S, 1),
        per_b(N_KV, nc, AUG_W), per_b(N_KV, nc, LANES), cs(nc, nsp),
        pl.BlockSpec((nsp, pps * PAGE), lambda b, t, pt: (0, t)),
        per_b(2 * KV_W, wbuf), per_b(LANES, 2 * KV_W), per_b(LANES, 2 * KV_W)]
    out_specs = [per_b(ROWS_S, KV_W), per_b(ROWS_S, LANES)]
    out_shape = [jax.ShapeDtypeStruct((nb, ROWS_S, KV_W), F32), jax.ShapeDtypeStruct((nb, ROWS_S, LANES), F32)]
    scratch = [pltpu.VMEM((N_KV, Q_PAD, nsp), BF16), pltpu.VMEM((ROWS_S, 1), F32),
               pltpu.VMEM((ROWS_S, KV_W + LANES), F32), pltpu.VMEM((ROWS_S, KV_W), F32)]
    gs = pltpu.PrefetchScalarGridSpec(num_scalar_prefetch=1, grid=(nb, nchunks), in_specs=in_specs,
                                      out_specs=out_specs, scratch_shapes=scratch)
    return pl.pallas_call(functools.partial(_attn_sample_kernel, pps=pps, past=past, wbuf=wbuf, n_new=n_new), grid_spec=gs,
                          out_shape=out_shape, compiler_params=_params(2), name="attn_sample")(
        page_table, *([pages] * pps), qa, qbd, gates, slope_rows, kc, vc, ms, emat, win, ksn, kwn)


def _outproj_kernel(x_ref, a_ref, o_ref, nw_ref, sc_ref, sh_ref, g1_ref, wm_ref, woc_ref, won_ref, wo_ref, x1_ref):
    x = x_ref[...]
    h = _mod_norm(x, nw_ref[...], sc_ref[...], sh_ref[...]).astype(BF16)
    mg = jax.nn.sigmoid(_dot(h, wm_ref[...]))
    y_a = _dot(a_ref[...], woc_ref[...])
    y_b = _dot(o_ref[...], won_ref[...])
    mix = (mg[:, 0:D_MODEL] * y_a + mg[:, D_MODEL:2 * D_MODEL] * y_b).astype(BF16)
    x1_ref[...] = x + g1_ref[...] * _dot(mix, wo_ref[...])


def _outproj(x2d, a, o, mod, nw, w_merge, w_oc, w_on, w_o, *, tm):
    n = x2d.shape[0]
    tok = pl.BlockSpec((tm, D_MODEL), lambda i: (i, 0))
    sq = _const_spec((D_MODEL, D_MODEL))
    return pl.pallas_call(
        _outproj_kernel, grid=(n // tm,),
        in_specs=[tok, tok, tok, _const_spec((1, D_MODEL)), mod.spec(1), mod.spec(0), mod.spec(2),
                  _const_spec((D_MODEL, 2 * D_MODEL)), sq, sq, sq],
        out_specs=tok, out_shape=jax.ShapeDtypeStruct((n, D_MODEL), F32),
        compiler_params=_params(1), name="outproj",
    )(x2d, a, o, nw, mod.rows, mod.rows, mod.rows, w_merge, w_oc, w_on, w_o)


FF_CHUNK = D_FF // 2


def _ffn_kernel(x_ref, nw_ref, sc_ref, sh_ref, g2_ref, nf_ref, wg_ref, wu_ref, wd_ref, y_ref):
    x = x_ref[...]
    h = _mod_norm(x, nw_ref[...], sc_ref[...], sh_ref[...]).astype(BF16)
    acc = jnp.zeros(x.shape, F32)
    for c in range(D_FF // FF_CHUNK):
        sl = slice(c * FF_CHUNK, (c + 1) * FF_CHUNK)
        gate = _dot(h, wg_ref[:, sl])
        up = _dot(h, wu_ref[:, sl])
        act = (gate * jax.nn.sigmoid(gate) * up).astype(BF16)
        acc = acc + _dot(act, wd_ref[sl, :])
    x2 = x + g2_ref[...] * acc
    inv = lax.rsqrt(jnp.mean(x2 * x2, axis=-1, keepdims=True) + EPS)
    y_ref[...] = (x2 * inv) * nf_ref[...]


def _ffn(x1, mod, nw2, nf, w_gate, w_up, w_down, *, tm):
    n = x1.shape[0]
    tok = pl.BlockSpec((tm, D_MODEL), lambda i: (i, 0))
    vec = _const_spec((1, D_MODEL))
    return pl.pallas_call(
        _ffn_kernel, grid=(n // tm,),
        in_specs=[tok, vec, mod.spec(4), mod.spec(3), mod.spec(5), vec,
                  _const_spec((D_MODEL, D_FF)), _const_spec((D_MODEL, D_FF)), _const_spec((D_FF, D_MODEL))],
        out_specs=tok, out_shape=jax.ShapeDtypeStruct((n, D_MODEL), F32),
        compiler_params=_params(1), name="ffn",
    )(x1, nw2, mod.rows, mod.rows, mod.rows, nf, w_gate, w_up, w_down)


def _slopes():
    return 2.0 ** (-8.0 * jnp.arange(1, N_HEADS + 1, dtype=F32) / N_HEADS)


def _slope_lanes(slopes):
    parts = _split3(slopes)
    cols = jnp.stack([parts[0], parts[0], parts[1], parts[1], parts[2], parts[2]], axis=1)
    return jnp.pad(cols, ((0, 0), (0, LANES - 6)))


def _pos_lanes(pos_hi, pos_lo):
    cols = np.stack([pos_hi, pos_lo] * 3, axis=1).astype(np.float32)
    return np.pad(cols, ((0, 0), (0, LANES - 6))).astype(BF16)


def _token_consts(t):
    pos = np.arange(t, dtype=np.int32)
    onehot = (pos[:, None] // SEL_BLOCK == np.arange(HEAD_DIM, dtype=np.int32)[None, :]).astype(np.float32)
    clo = np.concatenate([np.zeros((t, HEAD_DIM), np.float32), onehot], axis=1)
    chi = _pos_lanes((pos // SEL_BLOCK) * SEL_BLOCK, pos % SEL_BLOCK)
    return clo, chi


def _cmp_consts(ncb):
    ci = np.arange(ncb, dtype=np.int32) * CMP_STRIDE
    return _pos_lanes((ci // SEL_BLOCK) * SEL_BLOCK, ci % SEL_BLOCK)


def _ones_lane():
    return (np.arange(LANES) == HEAD_DIM).astype(np.float32).reshape(1, LANES)


def _imp_matrix(nc, ns_pad):
    c = np.arange(nc, dtype=np.int32)[:, None]
    s = np.arange(ns_pad, dtype=np.int32)[None, :]
    per = SEL_BLOCK // CMP_STRIDE
    return ((c // per == s) | ((c % per == per - 1) & (c // per == s - 1))).astype(BF16)


def _prep_weights(w_in, w_phi1, w_phi2, pe_cmp):
    wb = w_in.astype(BF16)
    w_conv_in = wb[:, _C_CONV:_C_QKV]
    w_qkv = jnp.pad(wb[:, _C_QKV:_C_MERGE], ((0, 0), (0, _QKV_COLS_PAD - _QKV_COLS)))
    w_merge = wb[:, _C_MERGE:]
    half = CMP_STRIDE * HEAD_DIM
    w1cat = jnp.concatenate([w_phi1[:, :half], w_phi1[:, half:]], axis=2).astype(BF16)
    w2pad = jnp.pad(w_phi2, ((0, 0), (0, 0), (0, LANES - HEAD_DIM))).astype(BF16)
    pe2 = pe_cmp.reshape(2, 2, half)
    return w_conv_in, w_qkv, w_merge, w1cat, w2pad, pe2


def _prompt_layer(x, mod_p, wts):
    (nw1, nw2, nf, w_conv_in, w_qkv, w_merge, w1cat, w2pad, pe2, w_conv, b_conv, w_oc, w_on, w_o,
     w_gate, w_up, w_down, slopes) = wts
    nb, t, _ = x.shape
    tm = min(512, t)
    x2d = x.reshape(nb * t, D_MODEL)
    mod = _Mod(mod_p, False, t, tm)
    a, tail = _conv_path(x2d, mod, nw1, w_conv_in, w_conv, b_conv, tm=tm, seq_len=t)
    clo, chi = _token_consts(t)
    cv = _ones_lane()
    qt, kvc, kvs, kvw, gts, ksa, vst, kwa, vwt = _qkv_path(x2d, mod, nw1, w_qkv, tm=tm, seq_len=t, consts=(clo, chi))
    npages = t // PAGE
    pt = np.broadcast_to(np.arange(npages, dtype=np.int32), (nb, npages))
    nc = t // CMP_STRIDE
    ns = t // SEL_BLOCK
    nqb = t // Q_BLOCK
    kca, vct = _compress(kvc, pt, pe2, w1cat, w2pad, _cmp_consts(nc), cv,
                         pps=min(16, npages), transpose_v=True, paged=False)
    gates_t = gts[:, :3 * N_HEADS].reshape(nb, nqb, Q_BLOCK, N_KV, GROUP, 3).transpose(0, 3, 1, 5, 4, 2)
    gates_t = gates_t.reshape(nb, N_KV, nqb, 3, GROUP * Q_BLOCK)
    qc_t = _slope_lanes(slopes * LOG2_E)[:, :QC_ROWS].reshape(N_KV, GROUP, QC_ROWS).transpose(0, 2, 1)
    qc_t = jnp.repeat(qc_t, Q_BLOCK, axis=2)
    mt = _imp_matrix(nc, ns).T
    o = _attn_prompt_t(qt, gates_t, qc_t, mt, kca, vct, ksa, vst, kwa, vwt)
    x1 = _outproj(x2d, a, o.reshape(nb * t, D_MODEL), mod, nw1, w_merge, w_oc, w_on, w_o, tm=tm)
    y = _ffn(x1, mod, nw2, nf, w_gate, w_up, w_down, tm=tm)
    keep = min(WINDOW, t)
    rows_of = lambda a: a.reshape(nb, 2, N_KV, HEAD_DIM, t).transpose(0, 4, 1, 2, 3)
    state = (rows_of(kvc), rows_of(kvs),
             kvw.reshape(nb, t, 2 * KV_W)[:, t - keep:].reshape(nb, keep, 2, N_KV, HEAD_DIM),
             tail[:, 8 - 2:, :])
    return y.reshape(nb, t, D_MODEL), state


def _sample_layer(x, mod_s, wts, cache_cmp, cache_sel, cache_win, state_conv, page_table):
    (nw1, nw2, nf, w_conv_in, w_qkv, w_merge, w1cat, w2pad, pe2, w_conv, b_conv, w_oc, w_on, w_o,
     w_gate, w_up, w_down, slopes) = wts
    nb, s, _ = x.shape
    n = nb * s
    x2d = x.reshape(n, D_MODEL)
    mod = _Mod(mod_s, True, s, n)
    tpos = jnp.arange(s)
    p1 = jnp.broadcast_to(state_conv[:, 1:2, :], (nb, s, D_MODEL)).reshape(n, D_MODEL)
    p2 = state_conv[:, jnp.minimum(tpos, 1), :].reshape(n, D_MODEL)
    a, u = _conv_path(x2d, mod, nw1, w_conv_in, w_conv, b_conv, tm=n, seq_len=s, prev=(p1, p2))
    qpad, kvc, kvs, kvw, gts = _qkv_path(x2d, mod, nw1, w_qkv, tm=n, seq_len=s)

    npages = page_table.shape[1]
    past = npages * PAGE
    cv = _ones_lane()
    nc = past // CMP_STRIDE
    token_minor = lambda c: c.transpose(0, 2, 3, 4, 1).reshape(c.shape[0], 2 * KV_W, c.shape[1])
    kca, vca = _compress(token_minor(cache_cmp), page_table, pe2, w1cat, w2pad, _cmp_consts(nc), cv,
                         pps=min(32, npages), transpose_v=False, paged=True)

    qh = qpad.reshape(nb, s, N_KV, GROUP, LANES).transpose(0, 2, 3, 1, 4)
    qh = jnp.pad(qh, ((0, 0), (0, 0), (0, 0), (0, Q_PAD - s), (0, 0)))
    sl = jnp.broadcast_to(_slope_lanes(slopes).reshape(1, N_KV, GROUP, 1, LANES), qh.shape)
    qa = jnp.concatenate([qh, sl], axis=-1).reshape(nb, N_KV, ROWS_G, AUG_W)
    eye = jnp.eye(N_KV, dtype=BF16)
    qbd = (qh[..., None, :HEAD_DIM] * eye[None, :, None, None, :, None]).reshape(nb, ROWS_S, KV_W)
    gates = gts[:, :3 * N_HEADS].reshape(nb, s, N_KV, GROUP, 3).transpose(0, 2, 3, 1, 4)
    gates = jnp.pad(gates, ((0, 0), (0, 0), (0, 0), (0, Q_PAD - s), (0, 0))).reshape(nb, ROWS_S, 3)
    slope_rows = jnp.repeat(slopes, Q_PAD).reshape(ROWS_S, 1)
    ns = past // SEL_BLOCK + 1
    nsp = -(-ns // LANES) * LANES
    ms = _imp_matrix(nc, nsp)
    tok = np.arange(past, dtype=np.int32)
    emat = (np.arange(nsp, dtype=np.int32)[:, None] == tok[None, :] // SEL_BLOCK).astype(BF16)
    pad_rows = lambda r: jnp.pad(r.reshape(nb, s, 2 * KV_W), ((0, 0), (0, LANES - s), (0, 0)))
    osw, ocg = _attn_sample(page_table, token_minor(cache_sel), qa, qbd, gates, slope_rows, kca, vca,
                            ms, emat, token_minor(cache_win), pad_rows(kvs), pad_rows(kvw),
                            pps=min(32, npages), n_new=s)
    osw = osw.reshape(nb, N_KV, GROUP, Q_PAD, N_KV, HEAD_DIM)
    o_sw = jnp.einsum('bgrqgd->bqgrd', osw)
    o_c = ocg.reshape(nb, N_KV, GROUP, Q_PAD, LANES)[..., :HEAD_DIM].transpose(0, 3, 1, 2, 4)
    o = (o_sw + o_c)[:, :s].reshape(n, N_HEADS * HEAD_DIM).astype(BF16)

    x1 = _outproj(x2d, a, o, mod, nw1, w_merge, w_oc, w_on, w_o, tm=n)
    y = _ffn(x1, mod, nw2, nf, w_gate, w_up, w_down, tm=n)
    kv5 = lambda r: r.reshape(nb, s, 2, N_KV, HEAD_DIM)
    win = jnp.concatenate([cache_win, kv5(kvw)], axis=1)[:, s:]
    state = (kv5(kvc), kv5(kvs), win, u.reshape(nb, s, D_MODEL)[:, s - 2:])
    return y.reshape(nb, s, D_MODEL), state


def kernel(x_prompt, x_sample, c_prompt, c_sample, cache_cmp, cache_sel, cache_win, state_conv, page_table,
           w_ada, b_ada, norm1, w_in, w_conv, b_conv, w_out_conv, pe_cmp, w_phi1, w_phi2, w_o_nsa, w_out,
           norm2, w_gate, w_up, w_down, norm_f):
    depth = w_ada.shape[0]
    assert depth == 1, "single-layer trunk"
    nbp, nbs = c_prompt.shape[0], c_sample.shape[0]
    slopes = _slopes()
    l = 0
    c_all = jnp.concatenate([c_prompt, c_sample], axis=0)
    c_all = jnp.pad(c_all, ((0, -c_all.shape[0] % 8), (0, 0)))
    mod = _ada(c_all, w_ada[l], b_ada[l])
    w_conv_in, w_qkv, w_merge, w1cat, w2pad, pe2 = _prep_weights(w_in[l], w_phi1[l], w_phi2[l], pe_cmp[l])
    row = lambda v: v.reshape(1, -1)
    wts = (row(norm1[l]), row(norm2[l]), row(norm_f), w_conv_in, w_qkv, w_merge, w1cat, w2pad, pe2,
           w_conv[l], row(b_conv[l]), w_out_conv[l].astype(BF16), w_o_nsa[l].astype(BF16), w_out[l].astype(BF16),
           w_gate[l].astype(BF16), w_up[l].astype(BF16), w_down[l].astype(BF16), slopes)
    yp, st_p = _prompt_layer(x_prompt, mod[:nbp], wts)
    ys, st_s = _sample_layer(x_sample, mod[nbp:nbp + nbs], wts, cache_cmp[l], cache_sel[l], cache_win[l],
                             state_conv[l], page_table)
    return (yp, ys, st_p[0][None], st_p[1][None], st_p[2][None], st_p[3][None],
            st_s[0][None], st_s[1][None], st_s[2][None], st_s[3][None])
```
